```python
import jax, jax.numpy as jnp
from jax import lax
import numpy as np

D_MODEL = 1024
BATCH = 4
SEQ = 8192
DEPTH = 2

CHUNK = 64
Q_BLOCK = 128
HEAD_DIM = 64
ATTN_HEADS = D_MODEL // 128
CONV_GROUPS = D_MODEL // 256
HGRN_HEADS = D_MODEL // 256
CONV_WIDTH = 3
ATTN_W = ATTN_HEADS * HEAD_DIM
CONV_C = CONV_GROUPS * HEAD_DIM
HGRN_W = HGRN_HEADS * HEAD_DIM
D_MIX = ATTN_W + CONV_C + HGRN_W
IN_SIZES = (ATTN_W, ATTN_W, ATTN_W, ATTN_HEADS,
            CONV_C, CONV_C, CONV_C,
            HGRN_W, HGRN_W, HGRN_W, HGRN_W)
D_IN = sum(IN_SIZES)
IN_SPLITS = [int(v) for v in np.cumsum(IN_SIZES)[:-1]]
D_FF = (7 * D_MODEL) // 2
N_EXPERTS = 8
TOP_K = 2
N_DENSE = (DEPTH + 1) // 2
N_MOE = DEPTH // 2
EPS = 1e-6
FORGET_BIAS_MEAN = 3.0
MASK_VALUE = -1e30
MASK_LOG_DECAY = -1e4
TINY = 1e-30

kernel_name = "hybrid_fox_shortconv_hgrn2_moe"


def rms(x):
    xf = x.astype(jnp.float32)
    return xf * lax.rsqrt(jnp.mean(xf * xf, axis=-1, keepdims=True) + EPS)


def rms_norm(x, g):
    return (rms(x) * g.astype(jnp.float32)).astype(x.dtype)


def swiglu(x, w_gate, w_up, w_down):
    return (jax.nn.silu(x @ w_gate) * (x @ w_up)) @ w_down


def forgetting_attention(q, k, v, f_logit, g_q, g_k):
    B, S, H, Dh = q.shape
    q = rms(q) * g_q.astype(jnp.float32)
    k = rms(k) * g_k.astype(jnp.float32)
    v = v.astype(jnp.float32)
    dcum = jnp.cumsum(jax.nn.log_sigmoid(f_logit.astype(jnp.float32)), axis=1)
    scale = np.float32(1.0 / np.sqrt(Dh))
    nb = S // Q_BLOCK
    q_blocks = q.reshape(B, nb, Q_BLOCK, H, Dh).transpose(1, 0, 3, 2, 4)
    d_blocks = dcum.reshape(B, nb, Q_BLOCK, H).transpose(1, 0, 3, 2)
    k_t = k.transpose(0, 2, 1, 3)
    v_t = v.transpose(0, 2, 1, 3)
    d_k = dcum.transpose(0, 2, 1)
    k_pos = jnp.arange(S)

    def block(args):
        q_i, d_i, i = args
        q_pos = i * Q_BLOCK + jnp.arange(Q_BLOCK)
        logits = (jnp.einsum('bhqd,bhkd->bhqk', q_i, k_t) * scale
                  + d_i[..., None] - d_k[:, :, None, :])
        mask = k_pos[None, :] <= q_pos[:, None]
        p = jax.nn.softmax(jnp.where(mask, logits, MASK_VALUE), axis=-1)
        return jnp.einsum('bhqk,bhkd->bhqd', p, v_t)

    o = lax.map(block, (q_blocks, d_blocks, jnp.arange(nb)))
    return o.transpose(1, 0, 3, 2, 4).reshape(B, S, H * Dh)


def short_conv(x_in, b_gate, c_gate, w):
    S = x_in.shape[1]
    u = (c_gate * x_in).astype(jnp.float32)
    up = jnp.pad(u, ((0, 0), (CONV_WIDTH - 1, 0), (0, 0)))
    wf = w.astype(jnp.float32)
    y = up[:, 0:S] * wf[0]
    for j in range(1, CONV_WIDTH):
        y = y + up[:, j:j + S] * wf[j]
    return b_gate.astype(jnp.float32) * y


def hgrn2(q, f_logit, i_in, lb):
    B, S, H, d = q.shape
    q = jax.nn.silu(q.astype(jnp.float32))
    z = f_logit.astype(jnp.float32)
    lb = lb.reshape(H, d).astype(jnp.float32)
    f = lb + (1.0 - lb) * jax.nn.sigmoid(z)
    log_f = jnp.log(jnp.maximum(f, TINY))
    k = (1.0 - lb) * jax.nn.sigmoid(-z)
    v = i_in.astype(jnp.float32)
    n_chunks = S // CHUNK

    def to_chunks(t):
        return t.reshape(B, n_chunks, CHUNK, H, d).transpose(1, 0, 3, 2, 4)

    causal = jnp.tril(jnp.ones((CHUNK, CHUNK), dtype=bool))

    def step(state, inp):
        q_c, k_c, v_c, l_c = inp
        c = jnp.cumsum(l_c, axis=2)
        o_inter = jnp.einsum('bhtk,bhkv->bhtv', q_c * jnp.exp(c), state)
        diff = c[:, :, :, None, :] - c[:, :, None, :, :]
        decay = jnp.exp(jnp.where(causal[:, :, None], diff, MASK_LOG_DECAY))
        a = jnp.einsum('bhtk,bhsk,bhtsk->bhts', q_c, k_c, decay)
        o = o_inter + jnp.einsum('bhts,bhsv->bhtv', a, v_c)
        c_last = c[:, :, -1:, :]
        k_dec = k_c * jnp.exp(c_last - c)
        state = (jnp.exp(c_last[:, :, 0, :])[..., None] * state
                 + jnp.einsum('bhsk,bhsv->bhkv', k_dec, v_c))
        return state, o

    state0 = jnp.zeros((B, H, d, d), jnp.float32)
    _, o = lax.scan(step, state0, (to_chunks(q), to_chunks(k), to_chunks(v), to_chunks(log_f)))
    return o.transpose(1, 0, 3, 2, 4).reshape(B, S, H * d)


def moe_swiglu(h, w_router, b_router, w_gate, w_up, w_down):
    B, S, D = h.shape
    t = h.reshape(B * S, D)
    logits = (t @ w_router).astype(jnp.float32) + b_router.astype(jnp.float32)
    top_v, top_i = lax.top_k(logits, TOP_K)
    top_w = jax.nn.softmax(top_v, axis=-1)
    gates = jnp.sum(jax.nn.one_hot(top_i, N_EXPERTS, dtype=jnp.float32) * top_w[..., None], axis=1)
    out = jnp.zeros_like(t)
    for e in range(N_EXPERTS):
        out = out + gates[:, e:e + 1].astype(t.dtype) * swiglu(t, w_gate[e], w_up[e], w_down[e])
    return out.reshape(B, S, D)


def setup_inputs(seed: int = 0) -> dict:
    key = jax.random.key(seed)
    ks = jax.random.split(key, 24)
    f32 = jnp.float32
    nrm = lambda k, shape, s: jax.random.normal(k, shape, f32) * s
    return {
        "x": nrm(ks[0], (BATCH, SEQ, D_MODEL), 1.0),
        "norm_mix": 1.0 + nrm(ks[1], (DEPTH, D_MODEL), 0.02),
        "w_in": nrm(ks[2], (DEPTH, D_MODEL, D_IN), D_MODEL ** -0.5),
        "attn_f_bias": FORGET_BIAS_MEAN + nrm(ks[3], (DEPTH, ATTN_HEADS), 0.1),
        "q_norm_gain": 1.0 + nrm(ks[4], (DEPTH, HEAD_DIM), 0.02),
        "k_norm_gain": 1.0 + nrm(ks[5], (DEPTH, HEAD_DIM), 0.02),
        "conv_w": nrm(ks[6], (DEPTH, CONV_WIDTH, CONV_C), CONV_WIDTH ** -0.5),
        "hgrn_lb_logits": nrm(ks[7], (DEPTH, HGRN_W), 0.1),
        "mix_out_gain": 1.0 + nrm(ks[8], (DEPTH, D_MIX), 0.02),
        "w_out": nrm(ks[9], (DEPTH, D_MIX, D_MODEL), D_MIX ** -0.5),
        "norm_ffn": 1.0 + nrm(ks[10], (DEPTH, D_MODEL), 0.02),
        "ffn_w_gate": nrm(ks[11], (N_DENSE, D_MODEL, D_FF), D_MODEL ** -0.5),
        "ffn_w_up": nrm(ks[12], (N_DENSE, D_MODEL, D_FF), D_MODEL ** -0.5),
        "ffn_w_down": nrm(ks[13], (N_DENSE, D_FF, D_MODEL), D_FF ** -0.5),
        "moe_router_w": nrm(ks[14], (N_MOE, D_MODEL, N_EXPERTS), D_MODEL ** -0.5),
        "moe_router_b": nrm(ks[15], (N_MOE, N_EXPERTS), 0.01),
        "moe_w_gate": nrm(ks[16], (N_MOE, N_EXPERTS, D_MODEL, D_FF), D_MODEL ** -0.5),
        "moe_w_up": nrm(ks[17], (N_MOE, N_EXPERTS, D_MODEL, D_FF), D_MODEL ** -0.5),
        "moe_w_down": nrm(ks[18], (N_MOE, N_EXPERTS, D_FF, D_MODEL), D_FF ** -0.5),
    }


def reference(x, norm_mix, w_in, attn_f_bias, q_norm_gain, k_norm_gain, conv_w,
              hgrn_lb_logits, mix_out_gain, w_out, norm_ffn, ffn_w_gate, ffn_w_up,
              ffn_w_down, moe_router_w, moe_router_b, moe_w_gate, moe_w_up, moe_w_down):
    B, S, _ = x.shape
    dt = x.dtype
    p_lb = jax.nn.softmax(hgrn_lb_logits.astype(jnp.float32), axis=0)
    lb_all = jnp.cumsum(p_lb, axis=0) - p_lb[0]
    for l in range(DEPTH):
        h = rms_norm(x, norm_mix[l])
        z = h @ w_in[l]
        (a_q, a_k, a_v, a_f, c_x, c_b, c_c,
         r_q, r_f, r_i, r_g) = jnp.split(z, IN_SPLITS, axis=-1)
        a_f = a_f + attn_f_bias[l].astype(dt)
        y_a = forgetting_attention(a_q.reshape(B, S, ATTN_HEADS, HEAD_DIM),
                                   a_k.reshape(B, S, ATTN_HEADS, HEAD_DIM),
                                   a_v.reshape(B, S, ATTN_HEADS, HEAD_DIM),
                                   a_f, q_norm_gain[l], k_norm_gain[l])
        y_c = short_conv(c_x, c_b, c_c, conv_w[l])
        y_r = hgrn2(r_q.reshape(B, S, HGRN_HEADS, HEAD_DIM),
                    r_f.reshape(B, S, HGRN_HEADS, HEAD_DIM),
                    r_i.reshape(B, S, HGRN_HEADS, HEAD_DIM), lb_all[l])
        y = jnp.concatenate([y_a, y_c, y_r], axis=-1)
        y = rms(y.reshape(B, S, D_MIX // HEAD_DIM, HEAD_DIM)).reshape(B, S, D_MIX)
        y = y * mix_out_gain[l].astype(jnp.float32)
        y = jnp.concatenate([y[..., :ATTN_W + CONV_C],
                             y[..., ATTN_W + CONV_C:] * jax.nn.silu(r_g.astype(jnp.float32))], axis=-1)
        x = x + y.astype(dt) @ w_out[l]
        h = rms_norm(x, norm_ffn[l])
        if l % 2 == 0:
            j = l // 2
            x = x + swiglu(h, ffn_w_gate[j], ffn_w_up[j], ffn_w_down[j])
        else:
            j = l // 2
            x = x + moe_swiglu(h, moe_router_w[j], moe_router_b[j],
                               moe_w_gate[j], moe_w_up[j], moe_w_down[j])
    return x
```

```python
import functools
import math

import jax
import jax.numpy as jnp
import numpy as np
from jax import lax
from jax.experimental import pallas as pl
from jax.experimental.pallas import tpu as pltpu

HEAD_DIM = 64
ATTN_HEADS = 8
CONV_C = 256
HGRN_W = 256
ATTN_W = ATTN_HEADS * HEAD_DIM
HGRN_CHUNK = 64
N_EXPERTS = 8
EPS = 1e-6
MASK_VALUE = -1e30
MASK_LOG_DECAY = -1e4
TINY = 1e-30
LOG2E = math.log2(math.e)

LANES = 128
SUBLANES = 8
VMEM_LIMIT_BYTES = 56 * 1024 * 1024

PAD_HEAD = LANES
QP_W = ATTN_HEADS * PAD_HEAD
OFF_Q = 0
OFF_K = OFF_Q + QP_W
OFF_V = OFF_K + QP_W
OFF_CX = OFF_V + ATTN_W
OFF_CB = OFF_CX + CONV_C
OFF_CC = OFF_CB + CONV_C
OFF_RQ = OFF_CC + CONV_C
OFF_RF = OFF_RQ + HGRN_W
OFF_RI = OFF_RF + HGRN_W
OFF_RG = OFF_RI + HGRN_W
OFF_AF = OFF_RG + HGRN_W
D_PACK = OFF_AF + LANES
AUG0 = HEAD_DIM


def _bf16(x):
    return x.astype(jnp.bfloat16)


def _split3(x):
    p1 = _bf16(x)
    r1 = x - p1.astype(jnp.float32)
    p2 = _bf16(r1)
    r2 = r1 - p2.astype(jnp.float32)
    return p1, p2, _bf16(r2)


def _dot(a, b):
    return jnp.dot(a, b, preferred_element_type=jnp.float32)


def _group_sum(x, bd):
    hi = _bf16(x)
    lo = _bf16(x - hi.astype(jnp.float32))
    return _dot(hi, bd) + _dot(lo, bd)


def _silu(x):
    return x * (1.0 / (1.0 + jnp.exp(-x)))


def _sigmoid(x):
    return 1.0 / (1.0 + jnp.exp(-x))


def _inproj_kernel(x_ref, g_ref, w_ref, tri_ref, fb_ref, gq_ref, gk_ref, selq_ref, selk_ref,
                   oneq_ref, onek_ref, convw_ref, gc_ref, bd_ref, lb_ref,
                   qa_ref, ka_ref, v_ref, yc_ref, rq_ref, rlf_ref, rk_ref, rv_ref, rg_ref,
                   dcarry, ucarry, *, tiles_per_seq):
    i = pl.program_id(0)

    @pl.when(i % tiles_per_seq == 0)
    def _():
        dcarry[...] = jnp.zeros_like(dcarry)
        ucarry[...] = jnp.zeros_like(ucarry)

    x = x_ref[...]
    h = x * lax.rsqrt(jnp.mean(x * x, axis=-1, keepdims=True) + EPS) * g_ref[...]
    hb = _bf16(h)

    zf = _dot(hb, w_ref[:, OFF_AF:OFF_AF + LANES]) + fb_ref[...]
    ls = jnp.minimum(zf, 0.0) - jnp.log(1.0 + jnp.exp(-jnp.abs(zf)))
    p1, p2, p3 = _split3(ls)
    loc = _dot(tri_ref[...], jnp.concatenate([p1, p2, p3], axis=1))
    d = dcarry[...] + loc[:, 0:LANES] + loc[:, LANES:2 * LANES] + loc[:, 2 * LANES:3 * LANES]
    dcarry[...] = d[d.shape[0] - 1:, :]
    e1, e2, e3 = _split3(d * LOG2E)
    ecat = jnp.concatenate([e1, e2, e3], axis=1)
    dq_part = _dot(ecat, selq_ref[...])
    dk_part = _dot(ecat, selk_ref[...])

    zq = _dot(hb, w_ref[:, OFF_Q:OFF_Q + QP_W])
    zk = _dot(hb, w_ref[:, OFF_K:OFF_K + QP_W])
    for hd in range(ATTN_HEADS):
        sl = slice(hd * PAD_HEAD, (hd + 1) * PAD_HEAD)
        qb = zq[:, sl]
        qn = qb * lax.rsqrt(jnp.sum(qb * qb, axis=-1, keepdims=True) * (1.0 / HEAD_DIM) + EPS) * gq_ref[:, sl]
        qa_ref[:, sl] = _bf16(qn + dq_part[:, sl] + oneq_ref[:, sl])
        kb = zk[:, sl]
        kn = kb * lax.rsqrt(jnp.sum(kb * kb, axis=-1, keepdims=True) * (1.0 / HEAD_DIM) + EPS) * gk_ref[:, sl]
        ka_ref[:, sl] = _bf16(kn + dk_part[:, sl] + onek_ref[:, sl])
    v_ref[...] = _bf16(_dot(hb, w_ref[:, OFF_V:OFF_V + ATTN_W]))

    cx = _dot(hb, w_ref[:, OFF_CX:OFF_CX + CONV_C])
    cb = _dot(hb, w_ref[:, OFF_CB:OFF_CB + CONV_C])
    cc = _dot(hb, w_ref[:, OFF_CC:OFF_CC + CONV_C])
    u = cc * cx
    uc = ucarry[...]
    row8 = lax.broadcasted_iota(jnp.int32, (SUBLANES, CONV_C), 0)
    r1 = pltpu.roll(u, 1, 0)
    r2 = pltpu.roll(u, 2, 0)
    top1 = jnp.where(row8 < 1, pltpu.roll(uc, 1, 0), r1[0:SUBLANES])
    top2 = jnp.where(row8 < 2, pltpu.roll(uc, 2, 0), r2[0:SUBLANES])
    u1 = jnp.concatenate([top1, r1[SUBLANES:]], axis=0)
    u2 = jnp.concatenate([top2, r2[SUBLANES:]], axis=0)
    ucarry[...] = u[u.shape[0] - SUBLANES:, :]
    yc = cb * (u2 * convw_ref[0:1, :] + u1 * convw_ref[1:2, :] + u * convw_ref[2:3, :])
    ssc = _group_sum(yc * yc, bd_ref[...])
    yc_ref[...] = _bf16(yc * lax.rsqrt(ssc * (1.0 / HEAD_DIM) + EPS) * gc_ref[...])

    lb = lb_ref[...]
    zr = _dot(hb, w_ref[:, OFF_RF:OFF_RF + HGRN_W])
    sg = _sigmoid(zr)
    f = lb + (1.0 - lb) * sg
    rlf_ref[...] = jnp.log(jnp.maximum(f, TINY))
    rk_ref[...] = (1.0 - lb) * _sigmoid(-zr)
    rq_ref[...] = _silu(_dot(hb, w_ref[:, OFF_RQ:OFF_RQ + HGRN_W]))
    rv_ref[...] = _dot(hb, w_ref[:, OFF_RI:OFF_RI + HGRN_W])
    rg_ref[...] = _silu(_dot(hb, w_ref[:, OFF_RG:OFF_RG + HGRN_W]))


def _inproj(x2, seq, g, wp, tri, fb, gq, gk, selq, selk, oneq, onek, convw, gc, bd, lb, tm):
    t, dm = x2.shape
    full = lambda a: pl.BlockSpec(a.shape, lambda i: (0,) * a.ndim)
    row = lambda w: pl.BlockSpec((tm, w), lambda i: (i, 0))
    consts = (g, wp, tri, fb, gq, gk, selq, selk, oneq, onek, convw, gc, bd, lb)
    out_shape = (
        jax.ShapeDtypeStruct((t, QP_W), jnp.bfloat16),
        jax.ShapeDtypeStruct((t, QP_W), jnp.bfloat16),
        jax.ShapeDtypeStruct((t, ATTN_W), jnp.bfloat16),
        jax.ShapeDtypeStruct((t, CONV_C), jnp.bfloat16),
    ) + tuple(jax.ShapeDtypeStruct((t, HGRN_W), jnp.float32) for _ in range(5))
    out_specs = (row(QP_W), row(QP_W), row(ATTN_W), row(CONV_C)) + tuple(row(HGRN_W) for _ in range(5))
    return pl.pallas_call(
        functools.partial(_inproj_kernel, tiles_per_seq=seq // tm),
        grid=(t // tm,),
        in_specs=[row(dm)] + [full(a) for a in consts],
        out_specs=out_specs,
        out_shape=out_shape,
        scratch_shapes=[pltpu.VMEM((1, LANES), jnp.float32), pltpu.VMEM((SUBLANES, CONV_C), jnp.float32)],
        compiler_params=pltpu.CompilerParams(dimension_semantics=("arbitrary",),
                                             vmem_limit_bytes=VMEM_LIMIT_BYTES),
        name="inproj",
    )(x2, *consts)


def _attn_kernel(qa_ref, ka_ref, vt_ref, gain_ref, out_ref, *, tq):
    i = pl.program_id(2)
    ones_rows = jnp.ones((2 * SUBLANES, tq), jnp.bfloat16)
    kpos = lax.broadcasted_iota(jnp.int32, (tq, tq), 0)
    qpos = lax.broadcasted_iota(jnp.int32, (tq, tq), 1)
    causal = kpos <= qpos
    ys = []
    for hh in range(2):
        sl = slice(hh * PAD_HEAD, (hh + 1) * PAD_HEAD)
        q = qa_ref[:, sl]

        def step(j, carry, masked):
            m, acc = carry
            k0 = pl.multiple_of(j * tq, tq)
            kblk = ka_ref[pl.ds(k0, tq), sl]
            s = lax.dot_general(kblk, q, (((1,), (1,)), ((), ())), preferred_element_type=jnp.float32)
            if masked:
                s = jnp.where(causal, s, MASK_VALUE)
            m_new = jnp.maximum(m, jnp.max(s, axis=0, keepdims=True))
            p = jnp.exp2(s - m_new)
            alpha = jnp.exp2(m - m_new)
            vaug = jnp.concatenate([vt_ref[hh, :, pl.ds(k0, tq)], ones_rows], axis=0)
            acc = acc * alpha + _dot(vaug, _bf16(p))
            return m_new, acc

        init = (jnp.full((1, tq), MASK_VALUE, jnp.float32),
                jnp.zeros((HEAD_DIM + 2 * SUBLANES, tq), jnp.float32))
        carry = lax.fori_loop(0, i, functools.partial(step, masked=False), init)
        _, acc = step(i, carry, True)
        o = acc[0:HEAD_DIM] * (1.0 / acc[HEAD_DIM:HEAD_DIM + 1])
        ms = jnp.mean(o * o, axis=0, keepdims=True)
        ys.append(o * lax.rsqrt(ms + EPS) * gain_ref[hh])
    out_ref[...] = _bf16(jnp.concatenate(ys, axis=0).T)


def _attention(qa, ka, vt, gain, tq):
    b, s, _ = qa.shape
    return pl.pallas_call(
        functools.partial(_attn_kernel, tq=tq),
        grid=(b, ATTN_HEADS // 2, s // tq),
        in_specs=[
            pl.BlockSpec((None, tq, 2 * PAD_HEAD), lambda bi, hp, i: (bi, i, hp)),
            pl.BlockSpec((None, s, 2 * PAD_HEAD), lambda bi, hp, i: (bi, 0, hp)),
            pl.BlockSpec((None, 2, HEAD_DIM, s), lambda bi, hp, i: (bi, hp, 0, 0)),
            pl.BlockSpec((2, HEAD_DIM, tq), lambda bi, hp, i: (hp, 0, 0)),
        ],
        out_specs=pl.BlockSpec((None, tq, 2 * HEAD_DIM), lambda bi, hp, i: (bi, i, hp)),
        out_shape=jax.ShapeDtypeStruct((b, s, ATTN_W), jnp.bfloat16),
        compiler_params=pltpu.CompilerParams(dimension_semantics=("arbitrary", "arbitrary", "arbitrary"),
                                             vmem_limit_bytes=VMEM_LIMIT_BYTES),
        name="fox_attention",
    )(qa, ka, vt, gain)


def _hgrn_kernel(q_ref, lf_ref, k_ref, v_ref, g_ref, tri_ref, bd_ref, gain_ref, out_ref,
                 state, c_s, k_s, v_s, *, n_chunks):
    @pl.when(pl.program_id(1) == 0)
    def _():
        state[...] = jnp.zeros_like(state)

    ch = HGRN_CHUNK
    rowid = lax.broadcasted_iota(jnp.int32, (ch, HGRN_W), 0)
    r128 = lax.broadcasted_iota(jnp.int32, (LANES, LANES), 0)
    c128 = lax.broadcasted_iota(jnp.int32, (LANES, LANES), 1)
    same_head = (r128 < HEAD_DIM) == (c128 < HEAD_DIM)
    bd = bd_ref[...]

    def chunk(ci, _):
        r0 = pl.multiple_of(ci * ch, ch)
        q = q_ref[pl.ds(r0, ch), :]
        k = k_ref[pl.ds(r0, ch), :]
        v = v_ref[pl.ds(r0, ch), :]
        p1, p2, p3 = _split3(lf_ref[pl.ds(r0, ch), :])
        cc = _dot(tri_ref[...], jnp.concatenate([p1, p2, p3], axis=1))
        c = cc[:, 0:HGRN_W] + cc[:, HGRN_W:2 * HGRN_W] + cc[:, 2 * HGRN_W:3 * HGRN_W]
        c_s[...] = c
        k_s[...] = k
        v_s[...] = v

        def intra(jj, o):
            ps = []
            vrows = []
            for uu in range(4):
                s = jj * 4 + uu
                crow = c_s[pl.ds(s, 1), :]
                krow = k_s[pl.ds(s, 1), :]
                vrows.append(v_s[pl.ds(s, 1), :])
                dec = jnp.exp(jnp.where(rowid >= s, c - crow, MASK_LOG_DECAY))
                ps.append(_bf16(q * krow * dec))
            a = _dot(jnp.concatenate(ps, axis=0), bd)
            for uu in range(4):
                o = o + a[uu * ch:(uu + 1) * ch] * vrows[uu]
            return o

        o = lax.fori_loop(0, ch // 4, intra, jnp.zeros((ch, HGRN_W), jnp.float32))

        c_last = c[ch - 1:ch, :]
        qe = q * jnp.exp(c)
        kd = k * jnp.exp(c_last - c)
        e_last = jnp.exp(c_last)
        o_inter = []
        for bb in range(HGRN_W // LANES):
            sl = slice(bb * LANES, (bb + 1) * LANES)
            st = state[bb]
            o_inter.append(_dot(_bf16(qe[:, sl]), _bf16(st)))
            ecol = jnp.broadcast_to(e_last[:, sl], (LANES, LANES)).T
            upd = lax.dot_general(_bf16(kd[:, sl]), _bf16(v[:, sl]), (((0,), (0,)), ((), ())),
                                  preferred_element_type=jnp.float32)
            state[bb] = jnp.where(same_head, ecol * st + upd, 0.0)
        o = o + jnp.concatenate(o_inter, axis=1)

        ss = _group_sum(o * o, bd)
        y = o * lax.rsqrt(ss * (1.0 / HEAD_DIM) + EPS) * gain_ref[...] * g_ref[pl.ds(r0, ch), :]
        out_ref[pl.ds(r0, ch), :] = _bf16(y)
        return 0

    lax.fori_loop(0, n_chunks, chunk, 0)


def _hgrn(rq, rlf, rk, rv, rg, tri, bd, gain, batch, seq, rows):
    t = rq.shape[0]
    steps = seq // rows
    blk = pl.BlockSpec((rows, HGRN_W), lambda b, j: (b * steps + j, 0))
    full = lambda a: pl.BlockSpec(a.shape, lambda b, j: (0,) * a.ndim)
    return pl.pallas_call(
        functools.partial(_hgrn_kernel, n_chunks=rows // HGRN_CHUNK),
        grid=(batch, steps),
        in_specs=[blk, blk, blk, blk, blk, full(tri), full(bd), full(gain)],
        out_specs=blk,
        out_shape=jax.ShapeDtypeStruct((t, HGRN_W), jnp.bfloat16),
        scratch_shapes=[pltpu.VMEM((HGRN_W // LANES, LANES, LANES), jnp.float32)]
        + [pltpu.VMEM((HGRN_CHUNK, HGRN_W), jnp.float32) for _ in range(3)],
        compiler_params=pltpu.CompilerParams(dimension_semantics=("arbitrary", "arbitrary"),
                                             vmem_limit_bytes=VMEM_LIMIT_BYTES),
        name="hgrn2",
    )(rq, rlf, rk, rv, rg, tri, bd, gain)


def _outproj_kernel(x_ref, ya_ref, yc_ref, yr_ref, w_ref, g_ref, wr_ref, br_ref,
                    xo_ref, h_ref, gates_ref, *, routed):
    xn = (x_ref[...]
          + _dot(ya_ref[...], w_ref[0:ATTN_W, :])
          + _dot(yc_ref[...], w_ref[ATTN_W:ATTN_W + CONV_C, :])
          + _dot(yr_ref[...], w_ref[ATTN_W + CONV_C:, :]))
    xo_ref[...] = xn
    h = xn * lax.rsqrt(jnp.mean(xn * xn, axis=-1, keepdims=True) + EPS) * g_ref[...]
    h_ref[...] = _bf16(h)
    if routed:
        logits = jnp.dot(h, wr_ref[...], preferred_element_type=jnp.float32,
                         precision=lax.Precision.HIGHEST) + br_ref[...]
        lane = lax.broadcasted_iota(jnp.int32, logits.shape, 1)
        logits = jnp.where(lane < N_EXPERTS, logits, MASK_VALUE)
        m1 = jnp.max(logits, axis=-1, keepdims=True)
        i1 = jnp.min(jnp.where(logits == m1, lane, LANES), axis=-1, keepdims=True)
        rest = jnp.where(lane == i1, MASK_VALUE, logits)
        m2 = jnp.max(rest, axis=-1, keepdims=True)
        i2 = jnp.min(jnp.where(rest == m2, lane, LANES), axis=-1, keepdims=True)
        e2 = jnp.exp(m2 - m1)
        w1 = 1.0 / (1.0 + e2)
        w2 = e2 * w1
        gates_ref[...] = jnp.where(lane == i1, w1, 0.0) + jnp.where(lane == i2, w2, 0.0)
    else:
        gates_ref[...] = jnp.ones_like(gates_ref)


def _outproj(x2, ya, yc, yr, wo, g, wr, br, routed, tm):
    t, dm = x2.shape
    row = lambda w: pl.BlockSpec((tm, w), lambda i: (i, 0))
    full = lambda a: pl.BlockSpec(a.shape, lambda i: (0,) * a.ndim)
    return pl.pallas_call(
        functools.partial(_outproj_kernel, routed=routed),
        grid=(t // tm,),
        in_specs=[row(dm), row(ATTN_W), row(CONV_C), row(HGRN_W), full(wo), full(g), full(wr), full(br)],
        out_specs=(row(dm), row(dm), row(LANES)),
        out_shape=(jax.ShapeDtypeStruct((t, dm), jnp.float32),
                   jax.ShapeDtypeStruct((t, dm), jnp.bfloat16),
                   jax.ShapeDtypeStruct((t, LANES), jnp.float32)),
        compiler_params=pltpu.CompilerParams(dimension_semantics=("arbitrary",),
                                             vmem_limit_bytes=VMEM_LIMIT_BYTES),
        name="outproj",
    )(x2, ya, yc, yr, wo, g, wr, br)


def _ffn_kernel(h_ref, x_ref, gates_ref, wg_ref, wu_ref, wd_ref, out_ref, acc):
    e = pl.program_id(1)
    f = pl.program_id(2)

    @pl.when((e == 0) & (f == 0))
    def _():
        acc[...] = x_ref[...]

    h = h_ref[...]
    gt = _dot(h, wg_ref[...])
    up = _dot(h, wu_ref[...])
    act = _bf16(gt * (1.0 / (1.0 + jnp.exp(-gt))) * up)
    lane = lax.broadcasted_iota(jnp.int32, gates_ref.shape, 1)
    gate = jnp.sum(jnp.where(lane == e, gates_ref[...], 0.0), axis=-1, keepdims=True)
    acc[...] += gate * _dot(act, wd_ref[...])

    @pl.when((e == pl.num_programs(1) - 1) & (f == pl.num_programs(2) - 1))
    def _():
        out_ref[...] = acc[...]


def _ffn(h, x2, gates, wg, wu, wd, tm, fc):
    t, dm = x2.shape
    ne, _, dff = wg.shape
    return pl.pallas_call(
        _ffn_kernel,
        grid=(t // tm, ne, dff // fc),
        in_specs=[
            pl.BlockSpec((tm, dm), lambda i, e, f: (i, 0)),
            pl.BlockSpec((tm, dm), lambda i, e, f: (i, 0)),
            pl.BlockSpec((tm, LANES), lambda i, e, f: (i, 0)),
            pl.BlockSpec((None, dm, fc), lambda i, e, f: (e, 0, f)),
            pl.BlockSpec((None, dm, fc), lambda i, e, f: (e, 0, f)),
            pl.BlockSpec((None, fc, dm), lambda i, e, f: (e, f, 0)),
        ],
        out_specs=pl.BlockSpec((tm, dm), lambda i, e, f: (i, 0)),
        out_shape=jax.ShapeDtypeStruct((t, dm), jnp.float32),
        scratch_shapes=[pltpu.VMEM((tm, dm), jnp.float32)],
        compiler_params=pltpu.CompilerParams(dimension_semantics=("arbitrary", "arbitrary", "arbitrary"),
                                             vmem_limit_bytes=VMEM_LIMIT_BYTES),
        name="swiglu",
    )(h, x2, gates, wg, wu, wd)


def _pad_heads(w):
    d = w.shape[0]
    w = w.reshape(d, ATTN_HEADS, HEAD_DIM)
    return jnp.pad(w, ((0, 0), (0, 0), (0, PAD_HEAD - HEAD_DIM))).reshape(d, QP_W)


def _pack_w_in(w):
    s = np.cumsum([0, ATTN_W, ATTN_W, ATTN_W, ATTN_HEADS, CONV_C, CONV_C, CONV_C, HGRN_W, HGRN_W, HGRN_W, HGRN_W])
    seg = [w[:, s[n]:s[n + 1]] for n in range(11)]
    a_q, a_k, a_v, a_f, c_x, c_b, c_c, r_q, r_f, r_i, r_g = seg
    a_f = jnp.pad(a_f, ((0, 0), (0, LANES - ATTN_HEADS)))
    return _bf16(jnp.concatenate([_pad_heads(a_q), _pad_heads(a_k), a_v, c_x, c_b, c_c, r_q, r_f, r_i, r_g, a_f],
                                 axis=1))


def _selection_constants():
    selq = np.zeros((3 * LANES, QP_W), np.float32)
    selk = np.zeros((3 * LANES, QP_W), np.float32)
    oneq = np.zeros((1, QP_W), np.float32)
    onek = np.zeros((1, QP_W), np.float32)
    for hd in range(ATTN_HEADS):
        base = hd * PAD_HEAD + AUG0
        for piece in range(3):
            selq[piece * LANES + hd, base + piece] = 1.0
            selk[piece * LANES + hd, base + 3 + piece] = -1.0
            oneq[0, base + 3 + piece] = 1.0
            onek[0, base + piece] = 1.0
    return (jnp.asarray(selq, jnp.bfloat16), jnp.asarray(selk, jnp.bfloat16),
            jnp.asarray(oneq), jnp.asarray(onek))


def _pad_gain(gain, mult):
    g = jnp.pad(gain.astype(jnp.float32) * mult, (0, PAD_HEAD - HEAD_DIM))
    return jnp.tile(g, ATTN_HEADS).reshape(1, QP_W)


def kernel(x, norm_mix, w_in, attn_f_bias, q_norm_gain, k_norm_gain, conv_w, hgrn_lb_logits, mix_out_gain, w_out,
           norm_ffn, ffn_w_gate, ffn_w_up, ffn_w_down, moe_router_w, moe_router_b, moe_w_gate, moe_w_up, moe_w_down):
    batch, seq, dm = x.shape
    depth = w_in.shape[0]
    t = batch * seq
    f32 = jnp.float32
    tm = min(512, seq)
    tq = min(256, seq)
    hg_rows = min(256, seq)
    tm_ffn = min(1024, t)
    dff = ffn_w_gate.shape[-1]
    fc = dff // 4

    p_lb = jax.nn.softmax(hgrn_lb_logits.astype(f32), axis=0)
    lb_all = jnp.cumsum(p_lb, axis=0) - p_lb[0]

    tri_m = _bf16(jnp.tril(jnp.ones((tm, tm), f32)))
    tri_c = _bf16(jnp.tril(jnp.ones((HGRN_CHUNK, HGRN_CHUNK), f32)))
    grp = np.arange(HGRN_W) // HEAD_DIM
    bd = jnp.asarray(grp[:, None] == grp[None, :], jnp.bfloat16)
    selq, selk, oneq, onek = _selection_constants()
    scale = 1.0 / math.sqrt(HEAD_DIM)

    x2 = x.reshape(t, dm)
    for l in range(depth):
        wp = _pack_w_in(w_in[l])
        fb = jnp.pad(attn_f_bias[l].astype(f32), (0, LANES - ATTN_HEADS)).reshape(1, LANES)
        gq = _pad_gain(q_norm_gain[l], scale * LOG2E)
        gk = _pad_gain(k_norm_gain[l], 1.0)
        mog = mix_out_gain[l].astype(f32)
        ga = jnp.broadcast_to(mog[:ATTN_W].reshape(ATTN_HEADS, HEAD_DIM, 1), (ATTN_HEADS, HEAD_DIM, tq))
        gc = mog[ATTN_W:ATTN_W + CONV_C].reshape(1, CONV_C)
        gr = mog[ATTN_W + CONV_C:].reshape(1, HGRN_W)

        qa, ka, v, yc, rq, rlf, rk, rv, rg = _inproj(
            x2, seq, norm_mix[l].astype(f32).reshape(1, dm), wp, tri_m, fb, gq, gk, selq, selk, oneq, onek,
            conv_w[l].astype(f32), gc, bd, lb_all[l].reshape(1, HGRN_W), tm)

        vt = v.reshape(batch, seq, ATTN_HEADS, HEAD_DIM).transpose(0, 2, 3, 1)
        ya = _attention(qa.reshape(batch, seq, QP_W), ka.reshape(batch, seq, QP_W), vt, ga, tq)
        yr = _hgrn(rq, rlf, rk, rv, rg, tri_c, bd, gr, batch, seq, hg_rows)

        routed = l % 2 == 1
        j = l // 2
        if routed:
            wr = jnp.pad(moe_router_w[j].astype(f32), ((0, 0), (0, LANES - N_EXPERTS)))
            br = jnp.pad(moe_router_b[j].astype(f32), (0, LANES - N_EXPERTS)).reshape(1, LANES)
            wg, wu, wd = _bf16(moe_w_gate[j]), _bf16(moe_w_up[j]), _bf16(moe_w_down[j])
        else:
            wr = jnp.zeros((dm, LANES), f32)
            br = jnp.zeros((1, LANES), f32)
            wg, wu, wd = _bf16(ffn_w_gate[j])[None], _bf16(ffn_w_up[j])[None], _bf16(ffn_w_down[j])[None]
        xo, h2, gates = _outproj(x2, ya.reshape(t, ATTN_W), yc, yr, _bf16(w_out[l]),
                                 norm_ffn[l].astype(f32).reshape(1, dm), wr, br, routed, tm)
        x2 = _ffn(h2, xo, gates, wg, wu, wd, tm_ffn, fc)
    return x2.reshape(batch, seq, dm)
```

```python
import functools
import math

import jax
import jax.numpy as jnp
import numpy as np
from jax import lax
from jax.experimental import pallas as pl
from jax.experimental.pallas import tpu as pltpu

HEAD_DIM = 64
ATTN_HEADS = 8
CONV_C = 256
HGRN_W = 256
ATTN_W = ATTN_HEADS * HEAD_DIM
HGRN_CHUNK = 64
N_EXPERTS = 8
EPS = 1e-6
MASK_VALUE = -1e30
MASK_LOG_DECAY = -1e4
TINY = 1e-30
LOG2E = math.log2(math.e)

LANES = 128
SUBLANES = 8
VMEM_LIMIT_BYTES = 56 * 1024 * 1024

PAD_HEAD = LANES
QP_W = ATTN_HEADS * PAD_HEAD
OFF_Q = 0
OFF_K = OFF_Q + QP_W
OFF_V = OFF_K + QP_W
OFF_CX = OFF_V + ATTN_W
OFF_CB = OFF_CX + CONV_C
OFF_CC = OFF_CB + CONV_C
OFF_RQ = OFF_CC + CONV_C
OFF_RF = OFF_RQ + HGRN_W
OFF_RI = OFF_RF + HGRN_W
OFF_RG = OFF_RI + HGRN_W
OFF_AF = OFF_RG + HGRN_W
D_PACK = OFF_AF + LANES
AUG0 = HEAD_DIM


def _bf16(x):
    return x.astype(jnp.bfloat16)


def _split3(x):
    p1 = _bf16(x)
    r1 = x - p1.astype(jnp.float32)
    p2 = _bf16(r1)
    r2 = r1 - p2.astype(jnp.float32)
    return p1, p2, _bf16(r2)


def _dot(a, b):
    return jnp.dot(a, b, preferred_element_type=jnp.float32)


def _group_sum(x, bd):
    hi = _bf16(x)
    lo = _bf16(x - hi.astype(jnp.float32))
    return _dot(hi, bd) + _dot(lo, bd)


def _silu(x):
    return x * (1.0 / (1.0 + jnp.exp(-x)))


def _sigmoid(x):
    return 1.0 / (1.0 + jnp.exp(-x))


def _inproj_kernel(x_ref, g_ref, w_ref, tri_ref, fb_ref, gq_ref, gk_ref, selq_ref, selk_ref,
                   oneq_ref, onek_ref, convw_ref, gc_ref, bd_ref, lb_ref,
                   qa_ref, ka_ref, v_ref, yc_ref, rq_ref, rlf_ref, rk_ref, rv_ref, rg_ref,
                   dcarry, ucarry, *, tiles_per_seq):
    i = pl.program_id(0)

    @pl.when(i % tiles_per_seq == 0)
    def _():
        dcarry[...] = jnp.zeros_like(dcarry)
        ucarry[...] = jnp.zeros_like(ucarry)

    x = x_ref[...]
    h = x * lax.rsqrt(jnp.mean(x * x, axis=-1, keepdims=True) + EPS) * g_ref[...]
    hb = _bf16(h)

    zf = _dot(hb, w_ref[:, OFF_AF:OFF_AF + LANES]) + fb_ref[...]
    ls = jnp.minimum(zf, 0.0) - jnp.log(1.0 + jnp.exp(-jnp.abs(zf)))
    p1, p2, p3 = _split3(ls)
    loc = _dot(tri_ref[...], jnp.concatenate([p1, p2, p3], axis=1))
    d = dcarry[...] + loc[:, 0:LANES] + loc[:, LANES:2 * LANES] + loc[:, 2 * LANES:3 * LANES]
    dcarry[...] = d[d.shape[0] - 1:, :]
    e1, e2, e3 = _split3(d * LOG2E)
    ecat = jnp.concatenate([e1, e2, e3], axis=1)
    dq_part = _dot(ecat, selq_ref[...])
    dk_part = _dot(ecat, selk_ref[...])

    zq = _dot(hb, w_ref[:, OFF_Q:OFF_Q + QP_W])
    zk = _dot(hb, w_ref[:, OFF_K:OFF_K + QP_W])
    for hd in range(ATTN_HEADS):
        sl = slice(hd * PAD_HEAD, (hd + 1) * PAD_HEAD)
        qb = zq[:, sl]
        qn = qb * lax.rsqrt(jnp.sum(qb * qb, axis=-1, keepdims=True) * (1.0 / HEAD_DIM) + EPS) * gq_ref[:, sl]
        qa_ref[:, sl] = _bf16(qn + dq_part[:, sl] + oneq_ref[:, sl])
        kb = zk[:, sl]
        kn = kb * lax.rsqrt(jnp.sum(kb * kb, axis=-1, keepdims=True) * (1.0 / HEAD_DIM) + EPS) * gk_ref[:, sl]
        ka_ref[:, sl] = _bf16(kn + dk_part[:, sl] + onek_ref[:, sl])
    v_ref[...] = _bf16(_dot(hb, w_ref[:, OFF_V:OFF_V + ATTN_W]))

    cx = _dot(hb, w_ref[:, OFF_CX:OFF_CX + CONV_C])
    cb = _dot(hb, w_ref[:, OFF_CB:OFF_CB + CONV_C])
    cc = _dot(hb, w_ref[:, OFF_CC:OFF_CC + CONV_C])
    u = cc * cx
    uc = ucarry[...]
    row8 = lax.broadcasted_iota(jnp.int32, (SUBLANES, CONV_C), 0)
    r1 = pltpu.roll(u, 1, 0)
    r2 = pltpu.roll(u, 2, 0)
    top1 = jnp.where(row8 < 1, pltpu.roll(uc, 1, 0), r1[0:SUBLANES])
    top2 = jnp.where(row8 < 2, pltpu.roll(uc, 2, 0), r2[0:SUBLANES])
    u1 = jnp.concatenate([top1, r1[SUBLANES:]], axis=0)
    u2 = jnp.concatenate([top2, r2[SUBLANES:]], axis=0)
    ucarry[...] = u[u.shape[0] - SUBLANES:, :]
    yc = cb * (u2 * convw_ref[0:1, :] + u1 * convw_ref[1:2, :] + u * convw_ref[2:3, :])
    ssc = _group_sum(yc * yc, bd_ref[...])
    yc_ref[...] = _bf16(yc * lax.rsqrt(ssc * (1.0 / HEAD_DIM) + EPS) * gc_ref[...])

    lb = lb_ref[...]
    zr = _dot(hb, w_ref[:, OFF_RF:OFF_RF + HGRN_W])
    sg = _sigmoid(zr)
    f = lb + (1.0 - lb) * sg
    rlf_ref[...] = jnp.log(jnp.maximum(f, TINY))
    rk_ref[...] = (1.0 - lb) * _sigmoid(-zr)
    rq_ref[...] = _silu(_dot(hb, w_ref[:, OFF_RQ:OFF_RQ + HGRN_W]))
    rv_ref[...] = _dot(hb, w_ref[:, OFF_RI:OFF_RI + HGRN_W])
    rg_ref[...] = _silu(_dot(hb, w_ref[:, OFF_RG:OFF_RG + HGRN_W]))


def _inproj(x2, seq, g, wp, tri, fb, gq, gk, selq, selk, oneq, onek, convw, gc, bd, lb, tm):
    t, dm = x2.shape
    full = lambda a: pl.BlockSpec(a.shape, lambda i: (0,) * a.ndim)
    row = lambda w: pl.BlockSpec((tm, w), lambda i: (i, 0))
    consts = (g, wp, tri, fb, gq, gk, selq, selk, oneq, onek, convw, gc, bd, lb)
    out_shape = (
        jax.ShapeDtypeStruct((t, QP_W), jnp.bfloat16),
        jax.ShapeDtypeStruct((t, QP_W), jnp.bfloat16),
        jax.ShapeDtypeStruct((t, ATTN_W), jnp.bfloat16),
        jax.ShapeDtypeStruct((t, CONV_C), jnp.bfloat16),
    ) + tuple(jax.ShapeDtypeStruct((t, HGRN_W), jnp.float32) for _ in range(5))
    out_specs = (row(QP_W), row(QP_W), row(ATTN_W), row(CONV_C)) + tuple(row(HGRN_W) for _ in range(5))
    return pl.pallas_call(
        functools.partial(_inproj_kernel, tiles_per_seq=seq // tm),
        grid=(t // tm,),
        in_specs=[row(dm)] + [full(a) for a in consts],
        out_specs=out_specs,
        out_shape=out_shape,
        scratch_shapes=[pltpu.VMEM((1, LANES), jnp.float32), pltpu.VMEM((SUBLANES, CONV_C), jnp.float32)],
        compiler_params=pltpu.CompilerParams(dimension_semantics=("arbitrary",),
                                             vmem_limit_bytes=VMEM_LIMIT_BYTES),
        name="inproj",
    )(x2, *consts)


def _attn_kernel(qa_ref, ka_ref, vt_ref, gain_ref, out_ref, m_s, acc_s, *, tq):
    i = pl.program_id(2)
    ones_rows = jnp.ones((2 * SUBLANES, tq), jnp.bfloat16)
    m_s[...] = jnp.full_like(m_s, MASK_VALUE)
    acc_s[...] = jnp.zeros_like(acc_s)

    def step(j, masked):
        k0 = pl.multiple_of(j * tq, tq)
        for hh in range(2):
            sl = slice(hh * PAD_HEAD, (hh + 1) * PAD_HEAD)
            s = lax.dot_general(ka_ref[pl.ds(k0, tq), sl], qa_ref[:, sl], (((1,), (1,)), ((), ())),
                                preferred_element_type=jnp.float32)
            if masked:
                kpos = lax.broadcasted_iota(jnp.int32, (tq, tq), 0)
                qpos = lax.broadcasted_iota(jnp.int32, (tq, tq), 1)
                s = jnp.where(kpos <= qpos, s, MASK_VALUE)
            m = m_s[hh]
            m_new = jnp.maximum(m, jnp.max(s, axis=0, keepdims=True))
            p = jnp.exp2(s - m_new)
            alpha = jnp.exp2(m - m_new)
            vaug = jnp.concatenate([vt_ref[hh, :, pl.ds(k0, tq)], ones_rows], axis=0)
            acc_s[hh] = acc_s[hh] * alpha + _dot(vaug, _bf16(p))
            m_s[hh] = m_new

    def body(j, _):
        step(j, False)
        return 0

    lax.fori_loop(0, i, body, 0)
    step(i, True)
    ys = []
    for hh in range(2):
        acc = acc_s[hh]
        o = acc[0:HEAD_DIM] * (1.0 / acc[HEAD_DIM:HEAD_DIM + 1])
        ms = jnp.mean(o * o, axis=0, keepdims=True)
        ys.append(o * lax.rsqrt(ms + EPS) * gain_ref[hh])
    out_ref[...] = _bf16(jnp.concatenate(ys, axis=0).T)


def _attention(qa, ka, vt, gain, tq):
    b, s, _ = qa.shape
    return pl.pallas_call(
        functools.partial(_attn_kernel, tq=tq),
        grid=(b, ATTN_HEADS // 2, s // tq),
        in_specs=[
            pl.BlockSpec((None, tq, 2 * PAD_HEAD), lambda bi, hp, i: (bi, i, hp)),
            pl.BlockSpec((None, s, 2 * PAD_HEAD), lambda bi, hp, i: (bi, 0, hp)),
            pl.BlockSpec((None, 2, HEAD_DIM, s), lambda bi, hp, i: (bi, hp, 0, 0)),
            pl.BlockSpec((2, HEAD_DIM, tq), lambda bi, hp, i: (hp, 0, 0)),
        ],
        out_specs=pl.BlockSpec((None, tq, 2 * HEAD_DIM), lambda bi, hp, i: (bi, i, hp)),
        out_shape=jax.ShapeDtypeStruct((b, s, ATTN_W), jnp.bfloat16),
        scratch_shapes=[pltpu.VMEM((2, 1, tq), jnp.float32),
                        pltpu.VMEM((2, HEAD_DIM + 2 * SUBLANES, tq), jnp.float32)],
        compiler_params=pltpu.CompilerParams(dimension_semantics=("arbitrary", "arbitrary", "arbitrary"),
                                             vmem_limit_bytes=VMEM_LIMIT_BYTES),
        name="fox_attention",
    )(qa, ka, vt, gain)


def _hgrn_kernel(q_ref, lf_ref, k_ref, v_ref, g_ref, tri_ref, bd_ref, gain_ref, out_ref,
                 state, c_s, k_s, v_s, *, n_chunks):
    @pl.when(pl.program_id(1) == 0)
    def _():
        state[...] = jnp.zeros_like(state)

    ch = HGRN_CHUNK
    rowid = lax.broadcasted_iota(jnp.int32, (ch, HGRN_W), 0)
    r128 = lax.broadcasted_iota(jnp.int32, (LANES, LANES), 0)
    c128 = lax.broadcasted_iota(jnp.int32, (LANES, LANES), 1)
    same_head = (r128 < HEAD_DIM) == (c128 < HEAD_DIM)
    bd = bd_ref[...]

    def chunk(ci, _):
        r0 = pl.multiple_of(ci * ch, ch)
        q = q_ref[pl.ds(r0, ch), :]
        k = k_ref[pl.ds(r0, ch), :]
        v = v_ref[pl.ds(r0, ch), :]
        p1, p2, p3 = _split3(lf_ref[pl.ds(r0, ch), :])
        cc = _dot(tri_ref[...], jnp.concatenate([p1, p2, p3], axis=1))
        c = cc[:, 0:HGRN_W] + cc[:, HGRN_W:2 * HGRN_W] + cc[:, 2 * HGRN_W:3 * HGRN_W]
        c_s[...] = c
        k_s[...] = k
        v_s[...] = v

        def intra(jj, o):
            ps = []
            vrows = []
            for uu in range(4):
                s = jj * 4 + uu
                crow = c_s[pl.ds(s, 1), :]
                krow = k_s[pl.ds(s, 1), :]
                vrows.append(v_s[pl.ds(s, 1), :])
                dec = jnp.exp(jnp.where(rowid >= s, c - crow, MASK_LOG_DECAY))
                ps.append(_bf16(q * krow * dec))
            a = _dot(jnp.concatenate(ps, axis=0), bd)
            for uu in range(4):
                o = o + a[uu * ch:(uu + 1) * ch] * vrows[uu]
            return o

        o = lax.fori_loop(0, ch // 4, intra, jnp.zeros((ch, HGRN_W), jnp.float32))

        c_last = c[ch - 1:ch, :]
        qe = q * jnp.exp(c)
        kd = k * jnp.exp(c_last - c)
        e_last = jnp.exp(c_last)
        o_inter = []
        for bb in range(HGRN_W // LANES):
            sl = slice(bb * LANES, (bb + 1) * LANES)
            st = state[bb]
            o_inter.append(_dot(_bf16(qe[:, sl]), _bf16(st)))
            ecol = jnp.broadcast_to(e_last[:, sl], (LANES, LANES)).T
            upd = lax.dot_general(_bf16(kd[:, sl]), _bf16(v[:, sl]), (((0,), (0,)), ((), ())),
                                  preferred_element_type=jnp.float32)
            state[bb] = jnp.where(same_head, ecol * st + upd, 0.0)
        o = o + jnp.concatenate(o_inter, axis=1)

        ss = _group_sum(o * o, bd)
        y = o * lax.rsqrt(ss * (1.0 / HEAD_DIM) + EPS) * gain_ref[...] * g_ref[pl.ds(r0, ch), :]
        out_ref[pl.ds(r0, ch), :] = _bf16(y)
        return 0

    lax.fori_loop(0, n_chunks, chunk, 0)


def _hgrn(rq, rlf, rk, rv, rg, tri, bd, gain, batch, seq, rows):
    t = rq.shape[0]
    steps = seq // rows
    blk = pl.BlockSpec((rows, HGRN_W), lambda b, j: (b * steps + j, 0))
    full = lambda a: pl.BlockSpec(a.shape, lambda b, j: (0,) * a.ndim)
    return pl.pallas_call(
        functools.partial(_hgrn_kernel, n_chunks=rows // HGRN_CHUNK),
        grid=(batch, steps),
        in_specs=[blk, blk, blk, blk, blk, full(tri), full(bd), full(gain)],
        out_specs=blk,
        out_shape=jax.ShapeDtypeStruct((t, HGRN_W), jnp.bfloat16),
        scratch_shapes=[pltpu.VMEM((HGRN_W // LANES, LANES, LANES), jnp.float32)]
        + [pltpu.VMEM((HGRN_CHUNK, HGRN_W), jnp.float32) for _ in range(3)],
        compiler_params=pltpu.CompilerParams(dimension_semantics=("arbitrary", "arbitrary"),
                                             vmem_limit_bytes=VMEM_LIMIT_BYTES),
        name="hgrn2",
    )(rq, rlf, rk, rv, rg, tri, bd, gain)


INFO_E1, INFO_E2, INFO_R1, INFO_R2, INFO_W1, INFO_W2 = range(6)


def _mix_and_norm(x_ref, ya_ref, yc_ref, yr_ref, w_ref, g_ref):
    xn = (x_ref[...]
          + _dot(ya_ref[...], w_ref[0:ATTN_W, :])
          + _dot(yc_ref[...], w_ref[ATTN_W:ATTN_W + CONV_C, :])
          + _dot(yr_ref[...], w_ref[ATTN_W + CONV_C:, :]))
    h = xn * lax.rsqrt(jnp.mean(xn * xn, axis=-1, keepdims=True) + EPS) * g_ref[...]
    return xn, h


def _outproj_dense_kernel(x_ref, ya_ref, yc_ref, yr_ref, w_ref, g_ref, xo_ref, h_ref):
    xn, h = _mix_and_norm(x_ref, ya_ref, yc_ref, yr_ref, w_ref, g_ref)
    xo_ref[...] = xn
    h_ref[...] = _bf16(h)


def _outproj_routed_kernel(x_ref, ya_ref, yc_ref, yr_ref, w_ref, g_ref, wr_ref, br_ref, tri_ref,
                           xo_ref, h_ref, info_ref, cnt_ref, cnt_s):
    @pl.when(pl.program_id(0) == 0)
    def _():
        cnt_s[...] = jnp.zeros_like(cnt_s)

    xn, h = _mix_and_norm(x_ref, ya_ref, yc_ref, yr_ref, w_ref, g_ref)
    xo_ref[...] = xn
    h_ref[...] = h
    logits = jnp.dot(h, wr_ref[...], preferred_element_type=jnp.float32,
                     precision=lax.Precision.HIGHEST) + br_ref[...]
    lane = lax.broadcasted_iota(jnp.int32, logits.shape, 1)
    logits = jnp.where(lane < N_EXPERTS, logits, MASK_VALUE)
    m1 = jnp.max(logits, axis=-1, keepdims=True)
    i1 = jnp.min(jnp.where(logits == m1, lane, LANES), axis=-1, keepdims=True)
    rest = jnp.where(lane == i1, MASK_VALUE, logits)
    m2 = jnp.max(rest, axis=-1, keepdims=True)
    i2 = jnp.min(jnp.where(rest == m2, lane, LANES), axis=-1, keepdims=True)
    e2 = jnp.exp(m2 - m1)
    w1 = 1.0 / (1.0 + e2)
    w2 = e2 * w1
    hit = (lane == i1) | (lane == i2)
    onehot = jnp.where(hit, 1.0, 0.0)
    incl = _dot(tri_ref[...], _bf16(onehot))
    rank = cnt_s[...] + incl - onehot
    r1 = jnp.sum(jnp.where(lane == i1, rank, 0.0), axis=-1, keepdims=True)
    r2 = jnp.sum(jnp.where(lane == i2, rank, 0.0), axis=-1, keepdims=True)
    cnt_new = cnt_s[...] + incl[incl.shape[0] - 1:, :]
    cnt_s[...] = cnt_new
    cnt_ref[...] = jnp.broadcast_to(cnt_new, cnt_ref.shape)
    rec = jnp.zeros(logits.shape, jnp.float32)
    for ln, val in ((INFO_E1, i1.astype(jnp.float32)), (INFO_E2, i2.astype(jnp.float32)),
                    (INFO_R1, r1), (INFO_R2, r2), (INFO_W1, w1), (INFO_W2, w2)):
        rec = jnp.where(lane == ln, val, rec)
    info_ref[...] = rec


def _outproj(x2, ya, yc, yr, wo, g, tm, router=None):
    t, dm = x2.shape
    row = lambda w: pl.BlockSpec((tm, w), lambda i: (i, 0))
    full = lambda a: pl.BlockSpec(a.shape, lambda i: (0,) * a.ndim)
    params = pltpu.CompilerParams(dimension_semantics=("arbitrary",), vmem_limit_bytes=VMEM_LIMIT_BYTES)
    base_specs = [row(dm), row(ATTN_W), row(CONV_C), row(HGRN_W), full(wo), full(g)]
    if router is None:
        return pl.pallas_call(
            _outproj_dense_kernel,
            grid=(t // tm,),
            in_specs=base_specs,
            out_specs=(row(dm), row(dm)),
            out_shape=(jax.ShapeDtypeStruct((t, dm), jnp.float32), jax.ShapeDtypeStruct((t, dm), jnp.bfloat16)),
            compiler_params=params,
            name="outproj_dense",
        )(x2, ya, yc, yr, wo, g)
    wr, br, tri = router
    return pl.pallas_call(
        _outproj_routed_kernel,
        grid=(t // tm,),
        in_specs=base_specs + [full(wr), full(br), full(tri)],
        out_specs=(row(dm), row(dm), row(LANES), pl.BlockSpec((SUBLANES, LANES), lambda i: (0, 0))),
        out_shape=(jax.ShapeDtypeStruct((t, dm), jnp.float32), jax.ShapeDtypeStruct((t, dm), jnp.float32),
                   jax.ShapeDtypeStruct((t, LANES), jnp.float32),
                   jax.ShapeDtypeStruct((SUBLANES, LANES), jnp.float32)),
        scratch_shapes=[pltpu.VMEM((1, LANES), jnp.float32)],
        compiler_params=params,
        name="outproj_routed",
    )(x2, ya, yc, yr, wo, g, wr, br, tri)


def _swiglu_step(h_ref, wg_ref, wu_ref, wd_ref):
    h = _bf16(h_ref[...])
    gt = _dot(h, wg_ref[...])
    up = _dot(h, wu_ref[...])
    act = _bf16(gt * (1.0 / (1.0 + jnp.exp(-gt))) * up)
    return _dot(act, wd_ref[...])


def _ffn_dense_kernel(h_ref, x_ref, wg_ref, wu_ref, wd_ref, out_ref, acc):
    f = pl.program_id(1)

    @pl.when(f == 0)
    def _():
        acc[...] = x_ref[...]

    acc[...] += _swiglu_step(h_ref, wg_ref, wu_ref, wd_ref)

    @pl.when(f == pl.num_programs(1) - 1)
    def _():
        out_ref[...] = acc[...]


def _ffn_dense(h, x2, wg, wu, wd, tm, fc):
    t, dm = x2.shape
    dff = wg.shape[1]
    return pl.pallas_call(
        _ffn_dense_kernel,
        grid=(t // tm, dff // fc),
        in_specs=[
            pl.BlockSpec((tm, dm), lambda i, f: (i, 0)),
            pl.BlockSpec((tm, dm), lambda i, f: (i, 0)),
            pl.BlockSpec((dm, fc), lambda i, f: (0, f)),
            pl.BlockSpec((dm, fc), lambda i, f: (0, f)),
            pl.BlockSpec((fc, dm), lambda i, f: (f, 0)),
        ],
        out_specs=pl.BlockSpec((tm, dm), lambda i, f: (i, 0)),
        out_shape=jax.ShapeDtypeStruct((t, dm), jnp.float32),
        scratch_shapes=[pltpu.VMEM((tm, dm), jnp.float32)],
        compiler_params=pltpu.CompilerParams(dimension_semantics=("arbitrary", "arbitrary"),
                                             vmem_limit_bytes=VMEM_LIMIT_BYTES),
        name="swiglu_dense",
    )(h, x2, wg, wu, wd)


def _ffn_grouped_kernel(te_ref, ta_ref, h_ref, wg_ref, wu_ref, wd_ref, out_ref, acc):
    i = pl.program_id(0)
    f = pl.program_id(1)

    @pl.when(ta_ref[i] == 1)
    def _():
        @pl.when(f == 0)
        def _():
            acc[...] = jnp.zeros_like(acc)

        acc[...] += _swiglu_step(h_ref, wg_ref, wu_ref, wd_ref)

        @pl.when(f == pl.num_programs(1) - 1)
        def _():
            out_ref[...] = acc[...]

    @pl.when((ta_ref[i] == 0) & (f == pl.num_programs(1) - 1))
    def _():
        out_ref[...] = jnp.zeros_like(out_ref)


def _ffn_grouped(tile_expert, tile_active, xs, wg, wu, wd, tm, fc):
    r, dm = xs.shape
    dff = wg.shape[2]
    nf = dff // fc
    fidx = lambda i, f, ta: f * ta[i] + (nf - 1) * (1 - ta[i])
    grid_spec = pltpu.PrefetchScalarGridSpec(
        num_scalar_prefetch=2,
        grid=(r // tm, nf),
        in_specs=[
            pl.BlockSpec((tm, dm), lambda i, f, te, ta: (i, 0)),
            pl.BlockSpec((None, dm, fc), lambda i, f, te, ta: (te[i], 0, fidx(i, f, ta))),
            pl.BlockSpec((None, dm, fc), lambda i, f, te, ta: (te[i], 0, fidx(i, f, ta))),
            pl.BlockSpec((None, fc, dm), lambda i, f, te, ta: (te[i], fidx(i, f, ta), 0)),
        ],
        out_specs=pl.BlockSpec((tm, dm), lambda i, f, te, ta: (i, 0)),
        scratch_shapes=[pltpu.VMEM((tm, dm), jnp.float32)],
    )
    return pl.pallas_call(
        _ffn_grouped_kernel,
        grid_spec=grid_spec,
        out_shape=jax.ShapeDtypeStruct((r, dm), jnp.float32),
        compiler_params=pltpu.CompilerParams(dimension_semantics=("arbitrary", "arbitrary"),
                                             vmem_limit_bytes=VMEM_LIMIT_BYTES),
        name="swiglu_grouped",
    )(tile_expert, tile_active, xs, wg, wu, wd)


def _row_copy(src, dst, sem):
    return pltpu.make_async_copy(src, dst, sem)


def _dispatch_kernel(p1_ref, p2_ref, h_ref, xs_in_ref, xs_ref, sem, *, tb):
    del xs_in_ref
    base = pl.program_id(0) * tb

    def issue(r, _):
        src = h_ref.at[pl.ds(r, 1), :]
        _row_copy(src, xs_ref.at[pl.ds(p1_ref[base + r], 1), :], sem).start()
        _row_copy(src, xs_ref.at[pl.ds(p2_ref[base + r], 1), :], sem).start()
        return 0

    lax.fori_loop(0, tb, issue, 0)

    def drain(r, _):
        src = h_ref.at[pl.ds(0, 1), :]
        dst = xs_ref.at[pl.ds(0, 1), :]
        _row_copy(src, dst, sem).wait()
        _row_copy(src, dst, sem).wait()
        return 0

    lax.fori_loop(0, tb, drain, 0)


def _dispatch(pos1, pos2, h, xs_init, tb):
    t, dm = h.shape
    grid_spec = pltpu.PrefetchScalarGridSpec(
        num_scalar_prefetch=2,
        grid=(t // tb,),
        in_specs=[pl.BlockSpec((tb, dm), lambda i, p1, p2: (i, 0)),
                  pl.BlockSpec(memory_space=pl.ANY)],
        out_specs=pl.BlockSpec(memory_space=pl.ANY),
        scratch_shapes=[pltpu.SemaphoreType.DMA(())],
    )
    return pl.pallas_call(
        functools.partial(_dispatch_kernel, tb=tb),
        grid_spec=grid_spec,
        out_shape=jax.ShapeDtypeStruct(xs_init.shape, xs_init.dtype),
        input_output_aliases={3: 0},
        compiler_params=pltpu.CompilerParams(dimension_semantics=("arbitrary",), has_side_effects=True),
        name="moe_dispatch",
    )(pos1, pos2, h, xs_init)


def _combine_kernel(p1_ref, p2_ref, y_ref, x_ref, info_ref, out_ref, buf, sem, *, tb):
    base = pl.program_id(0) * tb

    def issue(r, _):
        _row_copy(y_ref.at[pl.ds(p1_ref[base + r], 1), :], buf.at[0, pl.ds(r, 1), :], sem).start()
        _row_copy(y_ref.at[pl.ds(p2_ref[base + r], 1), :], buf.at[1, pl.ds(r, 1), :], sem).start()
        return 0

    lax.fori_loop(0, tb, issue, 0)

    def drain(r, _):
        src = y_ref.at[pl.ds(0, 1), :]
        dst = buf.at[0, pl.ds(0, 1), :]
        _row_copy(src, dst, sem).wait()
        _row_copy(src, dst, sem).wait()
        return 0

    lax.fori_loop(0, tb, drain, 0)
    info = info_ref[...]
    lane = lax.broadcasted_iota(jnp.int32, info.shape, 1)
    w1 = jnp.sum(jnp.where(lane == INFO_W1, info, 0.0), axis=-1, keepdims=True)
    w2 = jnp.sum(jnp.where(lane == INFO_W2, info, 0.0), axis=-1, keepdims=True)
    out_ref[...] = x_ref[...] + w1 * buf[0] + w2 * buf[1]


def _combine(pos1, pos2, y, x2, info, tb):
    t, dm = x2.shape
    grid_spec = pltpu.PrefetchScalarGridSpec(
        num_scalar_prefetch=2,
        grid=(t // tb,),
        in_specs=[pl.BlockSpec(memory_space=pl.ANY),
                  pl.BlockSpec((tb, dm), lambda i, p1, p2: (i, 0)),
                  pl.BlockSpec((tb, LANES), lambda i, p1, p2: (i, 0))],
        out_specs=pl.BlockSpec((tb, dm), lambda i, p1, p2: (i, 0)),
        scratch_shapes=[pltpu.VMEM((2, tb, dm), jnp.float32), pltpu.SemaphoreType.DMA(())],
    )
    return pl.pallas_call(
        functools.partial(_combine_kernel, tb=tb),
        grid_spec=grid_spec,
        out_shape=jax.ShapeDtypeStruct((t, dm), jnp.float32),
        compiler_params=pltpu.CompilerParams(dimension_semantics=("arbitrary",)),
        name="moe_combine",
    )(pos1, pos2, y, x2, info)


def _moe_routing_tables(info, counts, tm, n_tiles):
    cnt = counts[0, :N_EXPERTS].astype(jnp.int32)
    padded = ((cnt + tm - 1) // tm) * tm
    ends = jnp.cumsum(padded)
    offsets = ends - padded
    e1 = info[:, INFO_E1].astype(jnp.int32)
    e2 = info[:, INFO_E2].astype(jnp.int32)
    pos1 = offsets[e1] + info[:, INFO_R1].astype(jnp.int32)
    pos2 = offsets[e2] + info[:, INFO_R2].astype(jnp.int32)
    start = jnp.arange(n_tiles, dtype=jnp.int32) * tm
    tile_expert = jnp.minimum(jnp.sum((start[:, None] >= ends[None, :]).astype(jnp.int32), axis=1), N_EXPERTS - 1)
    tile_active = (start < ends[-1]).astype(jnp.int32)
    return pos1, pos2, tile_expert, tile_active


def _pad_heads(w):
    d = w.shape[0]
    w = w.reshape(d, ATTN_HEADS, HEAD_DIM)
    return jnp.pad(w, ((0, 0), (0, 0), (0, PAD_HEAD - HEAD_DIM))).reshape(d, QP_W)


def _pack_w_in(w):
    s = np.cumsum([0, ATTN_W, ATTN_W, ATTN_W, ATTN_HEADS, CONV_C, CONV_C, CONV_C, HGRN_W, HGRN_W, HGRN_W, HGRN_W])
    seg = [w[:, s[n]:s[n + 1]] for n in range(11)]
    a_q, a_k, a_v, a_f, c_x, c_b, c_c, r_q, r_f, r_i, r_g = seg
    a_f = jnp.pad(a_f, ((0, 0), (0, LANES - ATTN_HEADS)))
    return _bf16(jnp.concatenate([_pad_heads(a_q), _pad_heads(a_k), a_v, c_x, c_b, c_c, r_q, r_f, r_i, r_g, a_f],
                                 axis=1))


def _selection_constants():
    selq = np.zeros((3 * LANES, QP_W), np.float32)
    selk = np.zeros((3 * LANES, QP_W), np.float32)
    oneq = np.zeros((1, QP_W), np.float32)
    onek = np.zeros((1, QP_W), np.float32)
    for hd in range(ATTN_HEADS):
        base = hd * PAD_HEAD + AUG0
        for piece in range(3):
            selq[piece * LANES + hd, base + piece] = 1.0
            selk[piece * LANES + hd, base + 3 + piece] = -1.0
            oneq[0, base + 3 + piece] = 1.0
            onek[0, base + piece] = 1.0
    return (jnp.asarray(selq, jnp.bfloat16), jnp.asarray(selk, jnp.bfloat16),
            jnp.asarray(oneq), jnp.asarray(onek))


def _pad_gain(gain, mult):
    g = jnp.pad(gain.astype(jnp.float32) * mult, (0, PAD_HEAD - HEAD_DIM))
    return jnp.tile(g, ATTN_HEADS).reshape(1, QP_W)


def kernel(x, norm_mix, w_in, attn_f_bias, q_norm_gain, k_norm_gain, conv_w, hgrn_lb_logits, mix_out_gain, w_out,
           norm_ffn, ffn_w_gate, ffn_w_up, ffn_w_down, moe_router_w, moe_router_b, moe_w_gate, moe_w_up, moe_w_down):
    batch, seq, dm = x.shape
    depth = w_in.shape[0]
    t = batch * seq
    f32 = jnp.float32
    tm = min(512, seq)
    tq = min(512, seq)
    hg_rows = min(256, seq)
    tm_ffn = min(1024, t)
    tb_moe = min(256, t)
    dff = ffn_w_gate.shape[-1]
    fc = dff // 4

    p_lb = jax.nn.softmax(hgrn_lb_logits.astype(f32), axis=0)
    lb_all = jnp.cumsum(p_lb, axis=0) - p_lb[0]

    tri_m = _bf16(jnp.tril(jnp.ones((tm, tm), f32)))
    tri_c = _bf16(jnp.tril(jnp.ones((HGRN_CHUNK, HGRN_CHUNK), f32)))
    grp = np.arange(HGRN_W) // HEAD_DIM
    bd = jnp.asarray(grp[:, None] == grp[None, :], jnp.bfloat16)
    selq, selk, oneq, onek = _selection_constants()
    scale = 1.0 / math.sqrt(HEAD_DIM)

    x2 = x.reshape(t, dm)
    for l in range(depth):
        wp = _pack_w_in(w_in[l])
        fb = jnp.pad(attn_f_bias[l].astype(f32), (0, LANES - ATTN_HEADS)).reshape(1, LANES)
        gq = _pad_gain(q_norm_gain[l], scale * LOG2E)
        gk = _pad_gain(k_norm_gain[l], 1.0)
        mog = mix_out_gain[l].astype(f32)
        ga = jnp.broadcast_to(mog[:ATTN_W].reshape(ATTN_HEADS, HEAD_DIM, 1), (ATTN_HEADS, HEAD_DIM, tq))
        gc = mog[ATTN_W:ATTN_W + CONV_C].reshape(1, CONV_C)
        gr = mog[ATTN_W + CONV_C:].reshape(1, HGRN_W)

        qa, ka, v, yc, rq, rlf, rk, rv, rg = _inproj(
            x2, seq, norm_mix[l].astype(f32).reshape(1, dm), wp, tri_m, fb, gq, gk, selq, selk, oneq, onek,
            conv_w[l].astype(f32), gc, bd, lb_all[l].reshape(1, HGRN_W), tm)

        vt = v.reshape(batch, seq, ATTN_HEADS, HEAD_DIM).transpose(0, 2, 3, 1)
        ya = _attention(qa.reshape(batch, seq, QP_W), ka.reshape(batch, seq, QP_W), vt, ga, tq)
        yr = _hgrn(rq, rlf, rk, rv, rg, tri_c, bd, gr, batch, seq, hg_rows)

        j = l // 2
        wo = _bf16(w_out[l])
        gf = norm_ffn[l].astype(f32).reshape(1, dm)
        if l % 2 == 0:
            xo, h2 = _outproj(x2, ya.reshape(t, ATTN_W), yc, yr, wo, gf, tm)
            x2 = _ffn_dense(h2, xo, _bf16(ffn_w_gate[j]), _bf16(ffn_w_up[j]), _bf16(ffn_w_down[j]), tm_ffn, fc)
        else:
            wr = jnp.pad(moe_router_w[j].astype(f32), ((0, 0), (0, LANES - N_EXPERTS)))
            br = jnp.pad(moe_router_b[j].astype(f32), (0, LANES - N_EXPERTS)).reshape(1, LANES)
            xo, h2, info, counts = _outproj(x2, ya.reshape(t, ATTN_W), yc, yr, wo, gf, tm, router=(wr, br, tri_m))
            n_tiles = (2 * t) // tm_ffn + N_EXPERTS
            pos1, pos2, tile_expert, tile_active = _moe_routing_tables(info, counts, tm_ffn, n_tiles)
            xs = _dispatch(pos1, pos2, h2, jnp.zeros((n_tiles * tm_ffn, dm), f32), tb_moe)
            ys = _ffn_grouped(tile_expert, tile_active, xs,
                              _bf16(moe_w_gate[j]), _bf16(moe_w_up[j]), _bf16(moe_w_down[j]), tm_ffn, fc)
            x2 = _combine(pos1, pos2, ys, xo, info, tb_moe)
    return x2.reshape(batch, seq, dm)
```

```python
import functools
import math

import jax
import jax.numpy as jnp
import numpy as np
from jax import lax
from jax.experimental import pallas as pl
from jax.experimental.pallas import tpu as pltpu

HEAD_DIM = 64
ATTN_HEADS = 8
CONV_C = 256
HGRN_W = 256
ATTN_W = ATTN_HEADS * HEAD_DIM
HGRN_CHUNK = 64
HGRN_SUB = 16
N_EXPERTS = 8
EPS = 1e-6
MASK_VALUE = -1e30
MASK_LOG_DECAY = -1e4
TINY = 1e-30
LOG2E = math.log2(math.e)

LANES = 128
SUBLANES = 8
VMEM_LIMIT_BYTES = 56 * 1024 * 1024

PAD_HEAD = LANES
QP_W = ATTN_HEADS * PAD_HEAD
OFF_Q = 0
OFF_K = OFF_Q + QP_W
OFF_V = OFF_K + QP_W
OFF_CX = OFF_V + ATTN_W
OFF_CB = OFF_CX + CONV_C
OFF_CC = OFF_CB + CONV_C
OFF_RQ = OFF_CC + CONV_C
OFF_RF = OFF_RQ + HGRN_W
OFF_RI = OFF_RF + HGRN_W
OFF_RG = OFF_RI + HGRN_W
OFF_AF = OFF_RG + HGRN_W
D_PACK = OFF_AF + LANES
AUG0 = HEAD_DIM


def _bf16(x):
    return x.astype(jnp.bfloat16)


def _split3(x):
    p1 = _bf16(x)
    r1 = x - p1.astype(jnp.float32)
    p2 = _bf16(r1)
    r2 = r1 - p2.astype(jnp.float32)
    return p1, p2, _bf16(r2)


def _dot(a, b):
    return jnp.dot(a, b, preferred_element_type=jnp.float32)


def _group_sum(x, bd):
    hi = _bf16(x)
    lo = _bf16(x - hi.astype(jnp.float32))
    return _dot(hi, bd) + _dot(lo, bd)


def _silu(x):
    return x * (1.0 / (1.0 + jnp.exp(-x)))


def _sigmoid(x):
    return 1.0 / (1.0 + jnp.exp(-x))


def _inproj_kernel(x_ref, g_ref, w_ref, tri_ref, fb_ref, gq_ref, gk_ref, selq_ref, selk_ref,
                   oneq_ref, onek_ref, convw_ref, gc_ref, bd_ref, lb_ref,
                   qa_ref, ka_ref, v_ref, yc_ref, rq_ref, rlf_ref, rk_ref, rv_ref, rg_ref,
                   dcarry, ucarry, *, tiles_per_seq):
    i = pl.program_id(0)

    @pl.when(i % tiles_per_seq == 0)
    def _():
        dcarry[...] = jnp.zeros_like(dcarry)
        ucarry[...] = jnp.zeros_like(ucarry)

    x = x_ref[...]
    h = x * lax.rsqrt(jnp.mean(x * x, axis=-1, keepdims=True) + EPS) * g_ref[...]
    hb = _bf16(h)

    zf = _dot(hb, w_ref[:, OFF_AF:OFF_AF + LANES]) + fb_ref[...]
    ls = jnp.minimum(zf, 0.0) - jnp.log(1.0 + jnp.exp(-jnp.abs(zf)))
    p1, p2, p3 = _split3(ls)
    loc = _dot(tri_ref[...], jnp.concatenate([p1, p2, p3], axis=1))
    d = dcarry[...] + loc[:, 0:LANES] + loc[:, LANES:2 * LANES] + loc[:, 2 * LANES:3 * LANES]
    dcarry[...] = d[d.shape[0] - 1:, :]
    e1, e2, e3 = _split3(d * LOG2E)
    ecat = jnp.concatenate([e1, e2, e3], axis=1)
    dq_part = _dot(ecat, selq_ref[...])
    dk_part = _dot(ecat, selk_ref[...])

    zq = _dot(hb, w_ref[:, OFF_Q:OFF_Q + QP_W])
    zk = _dot(hb, w_ref[:, OFF_K:OFF_K + QP_W])
    for hd in range(ATTN_HEADS):
        sl = slice(hd * PAD_HEAD, (hd + 1) * PAD_HEAD)
        qb = zq[:, sl]
        qn = qb * lax.rsqrt(jnp.sum(qb * qb, axis=-1, keepdims=True) * (1.0 / HEAD_DIM) + EPS) * gq_ref[:, sl]
        qa_ref[:, sl] = _bf16(qn + dq_part[:, sl] + oneq_ref[:, sl])
        kb = zk[:, sl]
        kn = kb * lax.rsqrt(jnp.sum(kb * kb, axis=-1, keepdims=True) * (1.0 / HEAD_DIM) + EPS) * gk_ref[:, sl]
        ka_ref[:, sl] = _bf16(kn + dk_part[:, sl] + onek_ref[:, sl])
    v_ref[...] = _bf16(_dot(hb, w_ref[:, OFF_V:OFF_V + ATTN_W]))

    cx = _dot(hb, w_ref[:, OFF_CX:OFF_CX + CONV_C])
    cb = _dot(hb, w_ref[:, OFF_CB:OFF_CB + CONV_C])
    cc = _dot(hb, w_ref[:, OFF_CC:OFF_CC + CONV_C])
    u = cc * cx
    uc = ucarry[...]
    row8 = lax.broadcasted_iota(jnp.int32, (SUBLANES, CONV_C), 0)
    r1 = pltpu.roll(u, 1, 0)
    r2 = pltpu.roll(u, 2, 0)
    top1 = jnp.where(row8 < 1, pltpu.roll(uc, 1, 0), r1[0:SUBLANES])
    top2 = jnp.where(row8 < 2, pltpu.roll(uc, 2, 0), r2[0:SUBLANES])
    u1 = jnp.concatenate([top1, r1[SUBLANES:]], axis=0)
    u2 = jnp.concatenate([top2, r2[SUBLANES:]], axis=0)
    ucarry[...] = u[u.shape[0] - SUBLANES:, :]
    yc = cb * (u2 * convw_ref[0:1, :] + u1 * convw_ref[1:2, :] + u * convw_ref[2:3, :])
    ssc = _group_sum(yc * yc, bd_ref[...])
    yc_ref[...] = _bf16(yc * lax.rsqrt(ssc * (1.0 / HEAD_DIM) + EPS) * gc_ref[...])

    lb = lb_ref[...]
    zr = _dot(hb, w_ref[:, OFF_RF:OFF_RF + HGRN_W])
    sg = _sigmoid(zr)
    f = lb + (1.0 - lb) * sg
    rlf_ref[...] = jnp.log(jnp.maximum(f, TINY))
    rk_ref[...] = (1.0 - lb) * _sigmoid(-zr)
    rq_ref[...] = _silu(_dot(hb, w_ref[:, OFF_RQ:OFF_RQ + HGRN_W]))
    rv_ref[...] = _dot(hb, w_ref[:, OFF_RI:OFF_RI + HGRN_W])
    rg_ref[...] = _silu(_dot(hb, w_ref[:, OFF_RG:OFF_RG + HGRN_W]))


def _inproj(x2, seq, g, wp, tri, fb, gq, gk, selq, selk, oneq, onek, convw, gc, bd, lb, tm):
    t, dm = x2.shape
    full = lambda a: pl.BlockSpec(a.shape, lambda i: (0,) * a.ndim)
    row = lambda w: pl.BlockSpec((tm, w), lambda i: (i, 0))
    consts = (g, wp, tri, fb, gq, gk, selq, selk, oneq, onek, convw, gc, bd, lb)
    out_shape = (
        jax.ShapeDtypeStruct((t, QP_W), jnp.bfloat16),
        jax.ShapeDtypeStruct((t, QP_W), jnp.bfloat16),
        jax.ShapeDtypeStruct((t, ATTN_W), jnp.bfloat16),
        jax.ShapeDtypeStruct((t, CONV_C), jnp.bfloat16),
    ) + tuple(jax.ShapeDtypeStruct((t, HGRN_W), jnp.float32) for _ in range(5))
    out_specs = (row(QP_W), row(QP_W), row(ATTN_W), row(CONV_C)) + tuple(row(HGRN_W) for _ in range(5))
    return pl.pallas_call(
        functools.partial(_inproj_kernel, tiles_per_seq=seq // tm),
        grid=(t // tm,),
        in_specs=[row(dm)] + [full(a) for a in consts],
        out_specs=out_specs,
        out_shape=out_shape,
        scratch_shapes=[pltpu.VMEM((1, LANES), jnp.float32), pltpu.VMEM((SUBLANES, CONV_C), jnp.float32)],
        compiler_params=pltpu.CompilerParams(dimension_semantics=("arbitrary",),
                                             vmem_limit_bytes=VMEM_LIMIT_BYTES),
        name="inproj",
    )(x2, *consts)


def _attn_kernel(qa_ref, ka_ref, vt_ref, gain_ref, out_ref, m_s, acc_s, s_a, s_b, *, tq):
    i = pl.program_id(2)
    ones_rows = jnp.ones((2 * SUBLANES, tq), jnp.bfloat16)
    m_s[...] = jnp.full_like(m_s, MASK_VALUE)
    acc_s[...] = jnp.zeros_like(acc_s)

    def scores(j, s_buf):
        k0 = pl.multiple_of(j * tq, tq)
        for hh in range(2):
            sl = slice(hh * PAD_HEAD, (hh + 1) * PAD_HEAD)
            s_buf[hh] = lax.dot_general(ka_ref[pl.ds(k0, tq), sl], qa_ref[:, sl],
                                        (((1,), (1,)), ((), ())), preferred_element_type=jnp.float32)

    def consume(j, s_buf, masked):
        k0 = pl.multiple_of(j * tq, tq)
        for hh in range(2):
            s = s_buf[hh]
            if masked:
                kpos = lax.broadcasted_iota(jnp.int32, (tq, tq), 0)
                qpos = lax.broadcasted_iota(jnp.int32, (tq, tq), 1)
                s = jnp.where(kpos <= qpos, s, MASK_VALUE)
            m = m_s[hh]
            m_new = jnp.maximum(m, jnp.max(s, axis=0, keepdims=True))
            p = jnp.exp2(s - m_new)
            alpha = jnp.exp2(m - m_new)
            vaug = jnp.concatenate([vt_ref[hh, :, pl.ds(k0, tq)], ones_rows], axis=0)
            acc_s[hh] = acc_s[hh] * alpha + _dot(vaug, _bf16(p))
            m_s[hh] = m_new

    scores(0, s_a)

    def body(jj, _):
        j = 2 * jj
        scores(j + 1, s_b)
        consume(j, s_a, False)
        scores(j + 2, s_a)
        consume(j + 1, s_b, False)
        return 0

    lax.fori_loop(0, i // 2, body, 0)

    @pl.when(i % 2 == 1)
    def _():
        scores(i, s_b)
        consume(i - 1, s_a, False)
        consume(i, s_b, True)

    @pl.when(i % 2 == 0)
    def _():
        consume(i, s_a, True)
    ys = []
    for hh in range(2):
        acc = acc_s[hh]
        o = acc[0:HEAD_DIM] * (1.0 / acc[HEAD_DIM:HEAD_DIM + 1])
        ms = jnp.mean(o * o, axis=0, keepdims=True)
        ys.append(o * lax.rsqrt(ms + EPS) * gain_ref[hh])
    out_ref[...] = _bf16(jnp.concatenate(ys, axis=0).T)


def _attention(qa, ka, vt, gain, tq):
    b, s, _ = qa.shape
    return pl.pallas_call(
        functools.partial(_attn_kernel, tq=tq),
        grid=(b, ATTN_HEADS // 2, s // tq),
        in_specs=[
            pl.BlockSpec((None, tq, 2 * PAD_HEAD), lambda bi, hp, i: (bi, i, hp)),
            pl.BlockSpec((None, s, 2 * PAD_HEAD), lambda bi, hp, i: (bi, 0, hp)),
            pl.BlockSpec((None, 2, HEAD_DIM, s), lambda bi, hp, i: (bi, hp, 0, 0)),
            pl.BlockSpec((2, HEAD_DIM, tq), lambda bi, hp, i: (hp, 0, 0)),
        ],
        out_specs=pl.BlockSpec((None, tq, 2 * HEAD_DIM), lambda bi, hp, i: (bi, i, hp)),
        out_shape=jax.ShapeDtypeStruct((b, s, ATTN_W), jnp.bfloat16),
        scratch_shapes=[pltpu.VMEM((2, 1, tq), jnp.float32),
                        pltpu.VMEM((2, HEAD_DIM + 2 * SUBLANES, tq), jnp.float32),
                        pltpu.VMEM((2, tq, tq), jnp.float32),
                        pltpu.VMEM((2, tq, tq), jnp.float32)],
        compiler_params=pltpu.CompilerParams(dimension_semantics=("arbitrary", "arbitrary", "arbitrary"),
                                             vmem_limit_bytes=VMEM_LIMIT_BYTES),
        name="fox_attention",
    )(qa, ka, vt, gain)


def _hgrn_kernel(q_ref, lf_ref, k_ref, v_ref, g_ref, tri_ref, bd_ref, gain_ref, out_ref,
                 state, c_s, k_s, v_s, *, n_chunks):
    @pl.when(pl.program_id(1) == 0)
    def _():
        state[...] = jnp.zeros_like(state)

    ch = HGRN_CHUNK
    sb = HGRN_SUB
    nsb = ch // sb
    row_sb = lax.broadcasted_iota(jnp.int32, (sb, HGRN_W), 0)
    r128 = lax.broadcasted_iota(jnp.int32, (LANES, LANES), 0)
    c128 = lax.broadcasted_iota(jnp.int32, (LANES, LANES), 1)
    same_head = (r128 < HEAD_DIM) == (c128 < HEAD_DIM)
    t64 = lax.broadcasted_iota(jnp.int32, (ch, LANES), 0) // sb
    s64 = (lax.broadcasted_iota(jnp.int32, (ch, LANES), 1) % HEAD_DIM) // sb
    level2 = ((t64 == 1) & (s64 == 0)) | ((t64 == 3) & (s64 == 2))
    lane_head0 = lax.broadcasted_iota(jnp.int32, (ch, LANES), 1) < HEAD_DIM
    bd = bd_ref[...]
    zeros_sb = jnp.zeros((sb, HGRN_W), jnp.float32)

    def chunk(ci, _):
        r0 = pl.multiple_of(ci * ch, ch)
        q = q_ref[pl.ds(r0, ch), :]
        k = k_ref[pl.ds(r0, ch), :]
        v = v_ref[pl.ds(r0, ch), :]
        p1, p2, p3 = _split3(lf_ref[pl.ds(r0, ch), :])
        cc = _dot(tri_ref[...], jnp.concatenate([p1, p2, p3], axis=1))
        c = cc[:, 0:HGRN_W] + cc[:, HGRN_W:2 * HGRN_W] + cc[:, 2 * HGRN_W:3 * HGRN_W]
        c_s[...] = c
        k_s[...] = k
        v_s[...] = v
        blk = lambda a, n: a[n * sb:(n + 1) * sb]

        ps = []
        for n in range(nsb):
            cn, qn = blk(c, n), blk(q, n)
            for s in range(sb):
                r = n * sb + s
                dec = jnp.exp(jnp.where(row_sb >= s, cn - c_s[r:r + 1, :], MASK_LOG_DECAY))
                ps.append(_bf16(qn * k_s[r:r + 1, :] * dec))
        a_d = _dot(jnp.concatenate(ps, axis=0), bd)
        o_parts = []
        for n in range(nsb):
            acc = jnp.zeros((sb, HGRN_W), jnp.float32)
            for s in range(sb):
                r = n * sb + s
                acc = acc + a_d[r * sb:(r + 1) * sb] * v_s[r:r + 1, :]
            o_parts.append(acc)
        o = jnp.concatenate(o_parts, axis=0)

        ref1 = c[2 * sb - 1:2 * sb]
        ref2a = c[sb - 1:sb]
        ref2b = c[3 * sb - 1:3 * sb]
        hi, lo = slice(2 * sb, 4 * sb), slice(0, 2 * sb)
        zeros_half = jnp.zeros((2 * sb, HGRN_W), jnp.float32)
        q1 = jnp.concatenate([zeros_half, q[hi] * jnp.exp(c[hi] - ref1)], axis=0)
        k1 = jnp.concatenate([k[lo] * jnp.exp(ref1 - c[lo]), zeros_half], axis=0)
        q2 = jnp.concatenate([zeros_sb, blk(q, 1) * jnp.exp(blk(c, 1) - ref2a),
                              zeros_sb, blk(q, 3) * jnp.exp(blk(c, 3) - ref2b)], axis=0)
        k2 = jnp.concatenate([blk(k, 0) * jnp.exp(ref2a - blk(c, 0)), zeros_sb,
                              blk(k, 2) * jnp.exp(ref2b - blk(c, 2)), zeros_sb], axis=0)

        c_last = c[ch - 1:ch, :]
        qe = q * jnp.exp(c)
        kd = k * jnp.exp(c_last - c)
        e_last = jnp.exp(c_last)
        nt = (((1,), (1,)), ((), ()))
        o_off = []
        for bb in range(HGRN_W // LANES):
            sl = slice(bb * LANES, (bb + 1) * LANES)
            st = state[bb]
            stack2 = lambda a: jnp.concatenate([jnp.where(lane_head0, a[:, sl], 0.0),
                                                jnp.where(lane_head0, 0.0, a[:, sl])], axis=0)
            a1 = lax.dot_general(_bf16(q1[:, sl]), _bf16(stack2(k1)), nt, preferred_element_type=jnp.float32)
            a2 = lax.dot_general(_bf16(q2[:, sl]), _bf16(stack2(k2)), nt, preferred_element_type=jnp.float32)
            a_off = a1 + jnp.where(level2, a2, 0.0)
            lhs = jnp.concatenate([_bf16(a_off), _bf16(qe[:, sl])], axis=1)
            rhs = jnp.concatenate([_bf16(stack2(v)), _bf16(st)], axis=0)
            o_off.append(_dot(lhs, rhs))
            ecol = jnp.broadcast_to(e_last[:, sl], (LANES, LANES)).T
            upd = lax.dot_general(_bf16(kd[:, sl]), _bf16(v[:, sl]), (((0,), (0,)), ((), ())),
                                  preferred_element_type=jnp.float32)
            state[bb] = jnp.where(same_head, ecol * st + upd, 0.0)
        o = o + jnp.concatenate(o_off, axis=1)

        ss = _group_sum(o * o, bd)
        y = o * lax.rsqrt(ss * (1.0 / HEAD_DIM) + EPS) * gain_ref[...] * g_ref[pl.ds(r0, ch), :]
        out_ref[pl.ds(r0, ch), :] = _bf16(y)
        return 0

    lax.fori_loop(0, n_chunks, chunk, 0)


def _hgrn(rq, rlf, rk, rv, rg, tri, bd, gain, batch, seq, rows):
    t = rq.shape[0]
    steps = seq // rows
    blk = pl.BlockSpec((rows, HGRN_W), lambda b, j: (b * steps + j, 0))
    full = lambda a: pl.BlockSpec(a.shape, lambda b, j: (0,) * a.ndim)
    return pl.pallas_call(
        functools.partial(_hgrn_kernel, n_chunks=rows // HGRN_CHUNK),
        grid=(batch, steps),
        in_specs=[blk, blk, blk, blk, blk, full(tri), full(bd), full(gain)],
        out_specs=blk,
        out_shape=jax.ShapeDtypeStruct((t, HGRN_W), jnp.bfloat16),
        scratch_shapes=[pltpu.VMEM((HGRN_W // LANES, LANES, LANES), jnp.float32)]
        + [pltpu.VMEM((HGRN_CHUNK, HGRN_W), jnp.float32) for _ in range(3)],
        compiler_params=pltpu.CompilerParams(dimension_semantics=("arbitrary", "arbitrary"),
                                             vmem_limit_bytes=VMEM_LIMIT_BYTES),
        name="hgrn2",
    )(rq, rlf, rk, rv, rg, tri, bd, gain)


INFO_E1, INFO_E2, INFO_R1, INFO_R2, INFO_W1, INFO_W2 = range(6)


def _mix_and_norm(x_ref, ya_ref, yc_ref, yr_ref, w_ref, g_ref):
    xn = (x_ref[...]
          + _dot(ya_ref[...], w_ref[0:ATTN_W, :])
          + _dot(yc_ref[...], w_ref[ATTN_W:ATTN_W + CONV_C, :])
          + _dot(yr_ref[...], w_ref[ATTN_W + CONV_C:, :]))
    h = xn * lax.rsqrt(jnp.mean(xn * xn, axis=-1, keepdims=True) + EPS) * g_ref[...]
    return xn, h


def _outproj_dense_kernel(x_ref, ya_ref, yc_ref, yr_ref, w_ref, g_ref, xo_ref, h_ref):
    xn, h = _mix_and_norm(x_ref, ya_ref, yc_ref, yr_ref, w_ref, g_ref)
    xo_ref[...] = xn
    h_ref[...] = _bf16(h)


def _outproj_routed_kernel(x_ref, ya_ref, yc_ref, yr_ref, w_ref, g_ref, wr_ref, br_ref, tri_ref,
                           xo_ref, h_ref, info_ref, cnt_ref, cnt_s):
    @pl.when(pl.program_id(0) == 0)
    def _():
        cnt_s[...] = jnp.zeros_like(cnt_s)

    xn, h = _mix_and_norm(x_ref, ya_ref, yc_ref, yr_ref, w_ref, g_ref)
    xo_ref[...] = xn
    h_ref[...] = h
    logits = jnp.dot(h, wr_ref[...], preferred_element_type=jnp.float32,
                     precision=lax.Precision.HIGHEST) + br_ref[...]
    lane = lax.broadcasted_iota(jnp.int32, logits.shape, 1)
    logits = jnp.where(lane < N_EXPERTS, logits, MASK_VALUE)
    m1 = jnp.max(logits, axis=-1, keepdims=True)
    i1 = jnp.min(jnp.where(logits == m1, lane, LANES), axis=-1, keepdims=True)
    rest = jnp.where(lane == i1, MASK_VALUE, logits)
    m2 = jnp.max(rest, axis=-1, keepdims=True)
    i2 = jnp.min(jnp.where(rest == m2, lane, LANES), axis=-1, keepdims=True)
    e2 = jnp.exp(m2 - m1)
    w1 = 1.0 / (1.0 + e2)
    w2 = e2 * w1
    hit = (lane == i1) | (lane == i2)
    onehot = jnp.where(hit, 1.0, 0.0)
    incl = _dot(tri_ref[...], _bf16(onehot))
    rank = cnt_s[...] + incl - onehot
    r1 = jnp.sum(jnp.where(lane == i1, rank, 0.0), axis=-1, keepdims=True)
    r2 = jnp.sum(jnp.where(lane == i2, rank, 0.0), axis=-1, keepdims=True)
    cnt_new = cnt_s[...] + incl[incl.shape[0] - 1:, :]
    cnt_s[...] = cnt_new
    cnt_ref[...] = jnp.broadcast_to(cnt_new, cnt_ref.shape)
    rec = jnp.zeros(logits.shape, jnp.float32)
    for ln, val in ((INFO_E1, i1.astype(jnp.float32)), (INFO_E2, i2.astype(jnp.float32)),
                    (INFO_R1, r1), (INFO_R2, r2), (INFO_W1, w1), (INFO_W2, w2)):
        rec = jnp.where(lane == ln, val, rec)
    info_ref[...] = rec


def _outproj(x2, ya, yc, yr, wo, g, tm, router=None):
    t, dm = x2.shape
    row = lambda w: pl.BlockSpec((tm, w), lambda i: (i, 0))
    full = lambda a: pl.BlockSpec(a.shape, lambda i: (0,) * a.ndim)
    params = pltpu.CompilerParams(dimension_semantics=("arbitrary",), vmem_limit_bytes=VMEM_LIMIT_BYTES)
    base_specs = [row(dm), row(ATTN_W), row(CONV_C), row(HGRN_W), full(wo), full(g)]
    if router is None:
        return pl.pallas_call(
            _outproj_dense_kernel,
            grid=(t // tm,),
            in_specs=base_specs,
            out_specs=(row(dm), row(dm)),
            out_shape=(jax.ShapeDtypeStruct((t, dm), jnp.float32), jax.ShapeDtypeStruct((t, dm), jnp.bfloat16)),
            compiler_params=params,
            name="outproj_dense",
        )(x2, ya, yc, yr, wo, g)
    wr, br, tri = router
    return pl.pallas_call(
        _outproj_routed_kernel,
        grid=(t // tm,),
        in_specs=base_specs + [full(wr), full(br), full(tri)],
        out_specs=(row(dm), row(dm), row(LANES), pl.BlockSpec((SUBLANES, LANES), lambda i: (0, 0))),
        out_shape=(jax.ShapeDtypeStruct((t, dm), jnp.float32), jax.ShapeDtypeStruct((t, dm), jnp.float32),
                   jax.ShapeDtypeStruct((t, LANES), jnp.float32),
                   jax.ShapeDtypeStruct((SUBLANES, LANES), jnp.float32)),
        scratch_shapes=[pltpu.VMEM((1, LANES), jnp.float32)],
        compiler_params=params,
        name="outproj_routed",
    )(x2, ya, yc, yr, wo, g, wr, br, tri)


def _swiglu_step(h_ref, wg_ref, wu_ref, wd_ref):
    h = _bf16(h_ref[...])
    gt = _dot(h, wg_ref[...])
    up = _dot(h, wu_ref[...])
    act = _bf16(gt * (1.0 / (1.0 + jnp.exp(-gt))) * up)
    return _dot(act, wd_ref[...])


def _ffn_dense_kernel(h_ref, x_ref, wg_ref, wu_ref, wd_ref, out_ref, acc):
    f = pl.program_id(1)

    @pl.when(f == 0)
    def _():
        acc[...] = x_ref[...]

    acc[...] += _swiglu_step(h_ref, wg_ref, wu_ref, wd_ref)

    @pl.when(f == pl.num_programs(1) - 1)
    def _():
        out_ref[...] = acc[...]


def _ffn_dense(h, x2, wg, wu, wd, tm, fc):
    t, dm = x2.shape
    dff = wg.shape[1]
    return pl.pallas_call(
        _ffn_dense_kernel,
        grid=(t // tm, dff // fc),
        in_specs=[
            pl.BlockSpec((tm, dm), lambda i, f: (i, 0)),
            pl.BlockSpec((tm, dm), lambda i, f: (i, 0)),
            pl.BlockSpec((dm, fc), lambda i, f: (0, f)),
            pl.BlockSpec((dm, fc), lambda i, f: (0, f)),
            pl.BlockSpec((fc, dm), lambda i, f: (f, 0)),
        ],
        out_specs=pl.BlockSpec((tm, dm), lambda i, f: (i, 0)),
        out_shape=jax.ShapeDtypeStruct((t, dm), jnp.float32),
        scratch_shapes=[pltpu.VMEM((tm, dm), jnp.float32)],
        compiler_params=pltpu.CompilerParams(dimension_semantics=("arbitrary", "arbitrary"),
                                             vmem_limit_bytes=VMEM_LIMIT_BYTES),
        name="swiglu_dense",
    )(h, x2, wg, wu, wd)


def _ffn_grouped_kernel(te_ref, ta_ref, h_ref, wg_ref, wu_ref, wd_ref, out_ref, acc):
    i = pl.program_id(0)
    f = pl.program_id(1)

    @pl.when(ta_ref[i] == 1)
    def _():
        @pl.when(f == 0)
        def _():
            acc[...] = jnp.zeros_like(acc)

        acc[...] += _swiglu_step(h_ref, wg_ref, wu_ref, wd_ref)

        @pl.when(f == pl.num_programs(1) - 1)
        def _():
            out_ref[...] = acc[...]

    @pl.when((ta_ref[i] == 0) & (f == pl.num_programs(1) - 1))
    def _():
        out_ref[...] = jnp.zeros_like(out_ref)


def _ffn_grouped(tile_expert, tile_active, xs, wg, wu, wd, tm, fc):
    r, dm = xs.shape
    dff = wg.shape[2]
    nf = dff // fc
    fidx = lambda i, f, ta: f * ta[i] + (nf - 1) * (1 - ta[i])
    grid_spec = pltpu.PrefetchScalarGridSpec(
        num_scalar_prefetch=2,
        grid=(r // tm, nf),
        in_specs=[
            pl.BlockSpec((tm, dm), lambda i, f, te, ta: (i, 0)),
            pl.BlockSpec((None, dm, fc), lambda i, f, te, ta: (te[i], 0, fidx(i, f, ta))),
            pl.BlockSpec((None, dm, fc), lambda i, f, te, ta: (te[i], 0, fidx(i, f, ta))),
            pl.BlockSpec((None, fc, dm), lambda i, f, te, ta: (te[i], fidx(i, f, ta), 0)),
        ],
        out_specs=pl.BlockSpec((tm, dm), lambda i, f, te, ta: (i, 0)),
        scratch_shapes=[pltpu.VMEM((tm, dm), jnp.float32)],
    )
    return pl.pallas_call(
        _ffn_grouped_kernel,
        grid_spec=grid_spec,
        out_shape=jax.ShapeDtypeStruct((r, dm), jnp.float32),
        compiler_params=pltpu.CompilerParams(dimension_semantics=("arbitrary", "arbitrary"),
                                             vmem_limit_bytes=VMEM_LIMIT_BYTES),
        name="swiglu_grouped",
    )(tile_expert, tile_active, xs, wg, wu, wd)


def _row_copy(src, dst, sem):
    return pltpu.make_async_copy(src, dst, sem)


def _dispatch_kernel(p1_ref, p2_ref, h_ref, xs_in_ref, xs_ref, sem, *, tb):
    del xs_in_ref
    base = pl.program_id(0) * tb

    def issue(r, _):
        src = h_ref.at[pl.ds(r, 1), :]
        _row_copy(src, xs_ref.at[pl.ds(p1_ref[base + r], 1), :], sem).start()
        _row_copy(src, xs_ref.at[pl.ds(p2_ref[base + r], 1), :], sem).start()
        return 0

    lax.fori_loop(0, tb, issue, 0)

    def drain(r, _):
        src = h_ref.at[pl.ds(0, 1), :]
        dst = xs_ref.at[pl.ds(0, 1), :]
        _row_copy(src, dst, sem).wait()
        _row_copy(src, dst, sem).wait()
        return 0

    lax.fori_loop(0, tb, drain, 0)


def _dispatch(pos1, pos2, h, xs_init, tb):
    t, dm = h.shape
    grid_spec = pltpu.PrefetchScalarGridSpec(
        num_scalar_prefetch=2,
        grid=(t // tb,),
        in_specs=[pl.BlockSpec((tb, dm), lambda i, p1, p2: (i, 0)),
                  pl.BlockSpec(memory_space=pl.ANY)],
        out_specs=pl.BlockSpec(memory_space=pl.ANY),
        scratch_shapes=[pltpu.SemaphoreType.DMA(())],
    )
    return pl.pallas_call(
        functools.partial(_dispatch_kernel, tb=tb),
        grid_spec=grid_spec,
        out_shape=jax.ShapeDtypeStruct(xs_init.shape, xs_init.dtype),
        input_output_aliases={3: 0},
        compiler_params=pltpu.CompilerParams(dimension_semantics=("arbitrary",), has_side_effects=True),
        name="moe_dispatch",
    )(pos1, pos2, h, xs_init)


def _combine_kernel(p1_ref, p2_ref, y_ref, x_ref, info_ref, out_ref, buf, sem, *, tb):
    base = pl.program_id(0) * tb

    def issue(r, _):
        _row_copy(y_ref.at[pl.ds(p1_ref[base + r], 1), :], buf.at[0, pl.ds(r, 1), :], sem).start()
        _row_copy(y_ref.at[pl.ds(p2_ref[base + r], 1), :], buf.at[1, pl.ds(r, 1), :], sem).start()
        return 0

    lax.fori_loop(0, tb, issue, 0)

    def drain(r, _):
        src = y_ref.at[pl.ds(0, 1), :]
        dst = buf.at[0, pl.ds(0, 1), :]
        _row_copy(src, dst, sem).wait()
        _row_copy(src, dst, sem).wait()
        return 0

    lax.fori_loop(0, tb, drain, 0)
    info = info_ref[...]
    lane = lax.broadcasted_iota(jnp.int32, info.shape, 1)
    w1 = jnp.sum(jnp.where(lane == INFO_W1, info, 0.0), axis=-1, keepdims=True)
    w2 = jnp.sum(jnp.where(lane == INFO_W2, info, 0.0), axis=-1, keepdims=True)
    out_ref[...] = x_ref[...] + w1 * buf[0] + w2 * buf[1]


def _combine(pos1, pos2, y, x2, info, tb):
    t, dm = x2.shape
    grid_spec = pltpu.PrefetchScalarGridSpec(
        num_scalar_prefetch=2,
        grid=(t // tb,),
        in_specs=[pl.BlockSpec(memory_space=pl.ANY),
                  pl.BlockSpec((tb, dm), lambda i, p1, p2: (i, 0)),
                  pl.BlockSpec((tb, LANES), lambda i, p1, p2: (i, 0))],
        out_specs=pl.BlockSpec((tb, dm), lambda i, p1, p2: (i, 0)),
        scratch_shapes=[pltpu.VMEM((2, tb, dm), jnp.float32), pltpu.SemaphoreType.DMA(())],
    )
    return pl.pallas_call(
        functools.partial(_combine_kernel, tb=tb),
        grid_spec=grid_spec,
        out_shape=jax.ShapeDtypeStruct((t, dm), jnp.float32),
        compiler_params=pltpu.CompilerParams(dimension_semantics=("arbitrary",)),
        name="moe_combine",
    )(pos1, pos2, y, x2, info)


def _moe_routing_tables(info, counts, tm, n_tiles):
    cnt = counts[0, :N_EXPERTS].astype(jnp.int32)
    padded = ((cnt + tm - 1) // tm) * tm
    ends = jnp.cumsum(padded)
    offsets = ends - padded
    e1 = info[:, INFO_E1].astype(jnp.int32)
    e2 = info[:, INFO_E2].astype(jnp.int32)
    pos1 = offsets[e1] + info[:, INFO_R1].astype(jnp.int32)
    pos2 = offsets[e2] + info[:, INFO_R2].astype(jnp.int32)
    start = jnp.arange(n_tiles, dtype=jnp.int32) * tm
    tile_expert = jnp.minimum(jnp.sum((start[:, None] >= ends[None, :]).astype(jnp.int32), axis=1), N_EXPERTS - 1)
    tile_active = (start < ends[-1]).astype(jnp.int32)
    return pos1, pos2, tile_expert, tile_active


def _pad_heads(w):
    d = w.shape[0]
    w = w.reshape(d, ATTN_HEADS, HEAD_DIM)
    return jnp.pad(w, ((0, 0), (0, 0), (0, PAD_HEAD - HEAD_DIM))).reshape(d, QP_W)


def _pack_w_in(w):
    s = np.cumsum([0, ATTN_W, ATTN_W, ATTN_W, ATTN_HEADS, CONV_C, CONV_C, CONV_C, HGRN_W, HGRN_W, HGRN_W, HGRN_W])
    seg = [w[:, s[n]:s[n + 1]] for n in range(11)]
    a_q, a_k, a_v, a_f, c_x, c_b, c_c, r_q, r_f, r_i, r_g = seg
    a_f = jnp.pad(a_f, ((0, 0), (0, LANES - ATTN_HEADS)))
    return _bf16(jnp.concatenate([_pad_heads(a_q), _pad_heads(a_k), a_v, c_x, c_b, c_c, r_q, r_f, r_i, r_g, a_f],
                                 axis=1))


def _selection_constants():
    selq = np.zeros((3 * LANES, QP_W), np.float32)
    selk = np.zeros((3 * LANES, QP_W), np.float32)
    oneq = np.zeros((1, QP_W), np.float32)
    onek = np.zeros((1, QP_W), np.float32)
    for hd in range(ATTN_HEADS):
        base = hd * PAD_HEAD + AUG0
        for piece in range(3):
            selq[piece * LANES + hd, base + piece] = 1.0
            selk[piece * LANES + hd, base + 3 + piece] = -1.0
            oneq[0, base + 3 + piece] = 1.0
            onek[0, base + piece] = 1.0
    return (jnp.asarray(selq, jnp.bfloat16), jnp.asarray(selk, jnp.bfloat16),
            jnp.asarray(oneq), jnp.asarray(onek))


def _pad_gain(gain, mult):
    g = jnp.pad(gain.astype(jnp.float32) * mult, (0, PAD_HEAD - HEAD_DIM))
    return jnp.tile(g, ATTN_HEADS).reshape(1, QP_W)


def kernel(x, norm_mix, w_in, attn_f_bias, q_norm_gain, k_norm_gain, conv_w, hgrn_lb_logits, mix_out_gain, w_out,
           norm_ffn, ffn_w_gate, ffn_w_up, ffn_w_down, moe_router_w, moe_router_b, moe_w_gate, moe_w_up, moe_w_down):
    batch, seq, dm = x.shape
    depth = w_in.shape[0]
    t = batch * seq
    f32 = jnp.float32
    tm = min(512, seq)
    tq = min(512, seq)
    hg_rows = min(256, seq)
    tm_ffn = min(1024, t)
    tb_moe = min(256, t)
    dff = ffn_w_gate.shape[-1]
    fc = dff // 4

    p_lb = jax.nn.softmax(hgrn_lb_logits.astype(f32), axis=0)
    lb_all = jnp.cumsum(p_lb, axis=0) - p_lb[0]

    tri_m = _bf16(jnp.tril(jnp.ones((tm, tm), f32)))
    tri_c = _bf16(jnp.tril(jnp.ones((HGRN_CHUNK, HGRN_CHUNK), f32)))
    grp = np.arange(HGRN_W) // HEAD_DIM
    bd = jnp.asarray(grp[:, None] == grp[None, :], jnp.bfloat16)
    selq, selk, oneq, onek = _selection_constants()
    scale = 1.0 / math.sqrt(HEAD_DIM)

    x2 = x.reshape(t, dm)
    for l in range(depth):
        wp = _pack_w_in(w_in[l])
        fb = jnp.pad(attn_f_bias[l].astype(f32), (0, LANES - ATTN_HEADS)).reshape(1, LANES)
        gq = _pad_gain(q_norm_gain[l], scale * LOG2E)
        gk = _pad_gain(k_norm_gain[l], 1.0)
        mog = mix_out_gain[l].astype(f32)
        ga = jnp.broadcast_to(mog[:ATTN_W].reshape(ATTN_HEADS, HEAD_DIM, 1), (ATTN_HEADS, HEAD_DIM, tq))
        gc = mog[ATTN_W:ATTN_W + CONV_C].reshape(1, CONV_C)
        gr = mog[ATTN_W + CONV_C:].reshape(1, HGRN_W)

        qa, ka, v, yc, rq, rlf, rk, rv, rg = _inproj(
            x2, seq, norm_mix[l].astype(f32).reshape(1, dm), wp, tri_m, fb, gq, gk, selq, selk, oneq, onek,
            conv_w[l].astype(f32), gc, bd, lb_all[l].reshape(1, HGRN_W), tm)

        vt = v.reshape(batch, seq, ATTN_HEADS, HEAD_DIM).transpose(0, 2, 3, 1)
        ya = _attention(qa.reshape(batch, seq, QP_W), ka.reshape(batch, seq, QP_W), vt, ga, tq)
        yr = _hgrn(rq, rlf, rk, rv, rg, tri_c, bd, gr, batch, seq, hg_rows)

        j = l // 2
        wo = _bf16(w_out[l])
        gf = norm_ffn[l].astype(f32).reshape(1, dm)
        if l % 2 == 0:
            xo, h2 = _outproj(x2, ya.reshape(t, ATTN_W), yc, yr, wo, gf, tm)
            x2 = _ffn_dense(h2, xo, _bf16(ffn_w_gate[j]), _bf16(ffn_w_up[j]), _bf16(ffn_w_down[j]), tm_ffn, fc)
        else:
            wr = jnp.pad(moe_router_w[j].astype(f32), ((0, 0), (0, LANES - N_EXPERTS)))
            br = jnp.pad(moe_router_b[j].astype(f32), (0, LANES - N_EXPERTS)).reshape(1, LANES)
            xo, h2, info, counts = _outproj(x2, ya.reshape(t, ATTN_W), yc, yr, wo, gf, tm, router=(wr, br, tri_m))
            n_tiles = (2 * t) // tm_ffn + N_EXPERTS
            pos1, pos2, tile_expert, tile_active = _moe_routing_tables(info, counts, tm_ffn, n_tiles)
            xs = _dispatch(pos1, pos2, h2, jnp.zeros((n_tiles * tm_ffn, dm), f32), tb_moe)
            ys = _ffn_grouped(tile_expert, tile_active, xs,
                              _bf16(moe_w_gate[j]), _bf16(moe_w_up[j]), _bf16(moe_w_down[j]), tm_ffn, fc)
            x2 = _combine(pos1, pos2, ys, xo, info, tb_moe)
    return x2.reshape(batch, seq, dm)
```

```python
import functools
import math

import jax
import jax.numpy as jnp
import numpy as np
from jax import lax
from jax.experimental import pallas as pl
from jax.experimental.pallas import tpu as pltpu

HEAD_DIM = 64
ATTN_HEADS = 8
CONV_C = 256
HGRN_W = 256
ATTN_W = ATTN_HEADS * HEAD_DIM
HGRN_CHUNK = 64
HGRN_SUB = 16
N_EXPERTS = 8
EPS = 1e-6
MASK_VALUE = -1e30
MASK_LOG_DECAY = -1e4
TINY = 1e-30
LOG2E = math.log2(math.e)

LANES = 128
SUBLANES = 8
VMEM_LIMIT_BYTES = 56 * 1024 * 1024

PAD_HEAD = LANES
QP_W = ATTN_HEADS * PAD_HEAD
OFF_Q = 0
OFF_K = OFF_Q + QP_W
OFF_V = OFF_K + QP_W
OFF_CX = OFF_V + ATTN_W
OFF_CB = OFF_CX + CONV_C
OFF_CC = OFF_CB + CONV_C
OFF_RQ = OFF_CC + CONV_C
OFF_RF = OFF_RQ + HGRN_W
OFF_RI = OFF_RF + HGRN_W
OFF_RG = OFF_RI + HGRN_W
OFF_AF = OFF_RG + HGRN_W
D_PACK = OFF_AF + LANES
AUG0 = HEAD_DIM


def _bf16(x):
    return x.astype(jnp.bfloat16)


def _split3(x):
    p1 = _bf16(x)
    r1 = x - p1.astype(jnp.float32)
    p2 = _bf16(r1)
    r2 = r1 - p2.astype(jnp.float32)
    return p1, p2, _bf16(r2)


def _dot(a, b):
    return jnp.dot(a, b, preferred_element_type=jnp.float32)


def _group_sum(x, bd):
    hi = _bf16(x)
    lo = _bf16(x - hi.astype(jnp.float32))
    return _dot(hi, bd) + _dot(lo, bd)


def _silu(x):
    return x * (1.0 / (1.0 + jnp.exp(-x)))


def _sigmoid(x):
    return 1.0 / (1.0 + jnp.exp(-x))


def _inproj_kernel(x_ref, g_ref, w_ref, tri_ref, fb_ref, gq_ref, gk_ref, selq_ref, selk_ref,
                   oneq_ref, onek_ref, convw_ref, gc_ref, bd_ref, lb_ref,
                   qa_ref, ka_ref, v_ref, yc_ref, rq_ref, rlf_ref, rk_ref, rv_ref, rg_ref,
                   dcarry, ucarry, *, tiles_per_seq):
    i = pl.program_id(0)

    @pl.when(i % tiles_per_seq == 0)
    def _():
        dcarry[...] = jnp.zeros_like(dcarry)
        ucarry[...] = jnp.zeros_like(ucarry)

    x = x_ref[...]
    h = x * lax.rsqrt(jnp.mean(x * x, axis=-1, keepdims=True) + EPS) * g_ref[...]
    hb = _bf16(h)

    zf = _dot(hb, w_ref[:, OFF_AF:OFF_AF + LANES]) + fb_ref[...]
    ls = jnp.minimum(zf, 0.0) - jnp.log(1.0 + jnp.exp(-jnp.abs(zf)))
    p1, p2, p3 = _split3(ls)
    loc = _dot(tri_ref[...], jnp.concatenate([p1, p2, p3], axis=1))
    d = dcarry[...] + loc[:, 0:LANES] + loc[:, LANES:2 * LANES] + loc[:, 2 * LANES:3 * LANES]
    dcarry[...] = d[d.shape[0] - 1:, :]
    e1, e2, e3 = _split3(d * LOG2E)
    ecat = jnp.concatenate([e1, e2, e3], axis=1)
    dq_part = _dot(ecat, selq_ref[...])
    dk_part = _dot(ecat, selk_ref[...])

    zq = _dot(hb, w_ref[:, OFF_Q:OFF_Q + QP_W])
    zk = _dot(hb, w_ref[:, OFF_K:OFF_K + QP_W])
    for hd in range(ATTN_HEADS):
        sl = slice(hd * PAD_HEAD, (hd + 1) * PAD_HEAD)
        qb = zq[:, sl]
        qn = qb * lax.rsqrt(jnp.sum(qb * qb, axis=-1, keepdims=True) * (1.0 / HEAD_DIM) + EPS) * gq_ref[:, sl]
        qa_ref[:, sl] = _bf16(qn + dq_part[:, sl] + oneq_ref[:, sl])
        kb = zk[:, sl]
        kn = kb * lax.rsqrt(jnp.sum(kb * kb, axis=-1, keepdims=True) * (1.0 / HEAD_DIM) + EPS) * gk_ref[:, sl]
        ka_ref[:, sl] = _bf16(kn + dk_part[:, sl] + onek_ref[:, sl])
    v_ref[...] = _bf16(_dot(hb, w_ref[:, OFF_V:OFF_V + ATTN_W])).T

    cx = _dot(hb, w_ref[:, OFF_CX:OFF_CX + CONV_C])
    cb = _dot(hb, w_ref[:, OFF_CB:OFF_CB + CONV_C])
    cc = _dot(hb, w_ref[:, OFF_CC:OFF_CC + CONV_C])
    u = cc * cx
    uc = ucarry[...]
    row8 = lax.broadcasted_iota(jnp.int32, (SUBLANES, CONV_C), 0)
    r1 = pltpu.roll(u, 1, 0)
    r2 = pltpu.roll(u, 2, 0)
    top1 = jnp.where(row8 < 1, pltpu.roll(uc, 1, 0), r1[0:SUBLANES])
    top2 = jnp.where(row8 < 2, pltpu.roll(uc, 2, 0), r2[0:SUBLANES])
    u1 = jnp.concatenate([top1, r1[SUBLANES:]], axis=0)
    u2 = jnp.concatenate([top2, r2[SUBLANES:]], axis=0)
    ucarry[...] = u[u.shape[0] - SUBLANES:, :]
    yc = cb * (u2 * convw_ref[0:1, :] + u1 * convw_ref[1:2, :] + u * convw_ref[2:3, :])
    ssc = _group_sum(yc * yc, bd_ref[...])
    yc_ref[...] = _bf16(yc * lax.rsqrt(ssc * (1.0 / HEAD_DIM) + EPS) * gc_ref[...])

    lb = lb_ref[...]
    zr = _dot(hb, w_ref[:, OFF_RF:OFF_RF + HGRN_W])
    sg = _sigmoid(zr)
    f = lb + (1.0 - lb) * sg
    rlf_ref[...] = jnp.log(jnp.maximum(f, TINY))
    rk_ref[...] = (1.0 - lb) * _sigmoid(-zr)
    rq_ref[...] = _silu(_dot(hb, w_ref[:, OFF_RQ:OFF_RQ + HGRN_W]))
    rv_ref[...] = _dot(hb, w_ref[:, OFF_RI:OFF_RI + HGRN_W])
    rg_ref[...] = _silu(_dot(hb, w_ref[:, OFF_RG:OFF_RG + HGRN_W]))


def _inproj(x2, seq, g, wp, tri, fb, gq, gk, selq, selk, oneq, onek, convw, gc, bd, lb, tm):
    t, dm = x2.shape
    full = lambda a: pl.BlockSpec(a.shape, lambda i: (0,) * a.ndim)
    row = lambda w: pl.BlockSpec((tm, w), lambda i: (i, 0))
    consts = (g, wp, tri, fb, gq, gk, selq, selk, oneq, onek, convw, gc, bd, lb)
    tiles_per_seq = seq // tm
    out_shape = (
        jax.ShapeDtypeStruct((t, QP_W), jnp.bfloat16),
        jax.ShapeDtypeStruct((t, QP_W), jnp.bfloat16),
        jax.ShapeDtypeStruct((t // seq, ATTN_W, seq), jnp.bfloat16),
        jax.ShapeDtypeStruct((t, CONV_C), jnp.bfloat16),
    ) + tuple(jax.ShapeDtypeStruct((t, HGRN_W), jnp.float32) for _ in range(5))
    vt_spec = pl.BlockSpec((None, ATTN_W, tm), lambda i: (i // tiles_per_seq, 0, i % tiles_per_seq))
    out_specs = (row(QP_W), row(QP_W), vt_spec, row(CONV_C)) + tuple(row(HGRN_W) for _ in range(5))
    return pl.pallas_call(
        functools.partial(_inproj_kernel, tiles_per_seq=seq // tm),
        grid=(t // tm,),
        in_specs=[row(dm)] + [full(a) for a in consts],
        out_specs=out_specs,
        out_shape=out_shape,
        scratch_shapes=[pltpu.VMEM((1, LANES), jnp.float32), pltpu.VMEM((SUBLANES, CONV_C), jnp.float32)],
        compiler_params=pltpu.CompilerParams(dimension_semantics=("arbitrary",),
                                             vmem_limit_bytes=VMEM_LIMIT_BYTES),
        name="inproj",
    )(x2, *consts)


def _attn_kernel(qa_ref, ka_ref, vt_ref, gain_ref, out_ref, m_s, acc_s, s_a, s_b, *, tq):
    i = pl.program_id(2)
    ones_rows = jnp.ones((2 * SUBLANES, tq), jnp.bfloat16)
    m_s[...] = jnp.full_like(m_s, MASK_VALUE)
    acc_s[...] = jnp.zeros_like(acc_s)

    def scores(j, s_buf):
        k0 = pl.multiple_of(j * tq, tq)
        for hh in range(2):
            sl = slice(hh * PAD_HEAD, (hh + 1) * PAD_HEAD)
            s_buf[hh] = lax.dot_general(ka_ref[pl.ds(k0, tq), sl], qa_ref[:, sl],
                                        (((1,), (1,)), ((), ())), preferred_element_type=jnp.float32)

    def consume(j, s_buf, masked):
        k0 = pl.multiple_of(j * tq, tq)
        for hh in range(2):
            s = s_buf[hh]
            if masked:
                kpos = lax.broadcasted_iota(jnp.int32, (tq, tq), 0)
                qpos = lax.broadcasted_iota(jnp.int32, (tq, tq), 1)
                s = jnp.where(kpos <= qpos, s, MASK_VALUE)
            m = m_s[hh]
            m_new = jnp.maximum(m, jnp.max(s, axis=0, keepdims=True))
            p = jnp.exp2(s - m_new)
            alpha = jnp.exp2(m - m_new)
            vaug = jnp.concatenate([vt_ref[hh, :, pl.ds(k0, tq)], ones_rows], axis=0)
            acc_s[hh] = acc_s[hh] * alpha + _dot(vaug, _bf16(p))
            m_s[hh] = m_new

    scores(0, s_a)

    def body(jj, _):
        j = 2 * jj
        scores(j + 1, s_b)
        consume(j, s_a, False)
        scores(j + 2, s_a)
        consume(j + 1, s_b, False)
        return 0

    lax.fori_loop(0, i // 2, body, 0)

    @pl.when(i % 2 == 1)
    def _():
        scores(i, s_b)
        consume(i - 1, s_a, False)
        consume(i, s_b, True)

    @pl.when(i % 2 == 0)
    def _():
        consume(i, s_a, True)

    ys = []
    for hh in range(2):
        acc = acc_s[hh]
        o = acc[0:HEAD_DIM] * (1.0 / acc[HEAD_DIM:HEAD_DIM + 1])
        ms = jnp.mean(o * o, axis=0, keepdims=True)
        ys.append(o * lax.rsqrt(ms + EPS) * gain_ref[hh])
    out_ref[...] = _bf16(jnp.concatenate(ys, axis=0).T)


def _attention(qa, ka, vt, gain, tq):
    b, s, _ = qa.shape
    return pl.pallas_call(
        functools.partial(_attn_kernel, tq=tq),
        grid=(b, ATTN_HEADS // 2, s // tq),
        in_specs=[
            pl.BlockSpec((None, tq, 2 * PAD_HEAD), lambda bi, hp, i: (bi, i, hp)),
            pl.BlockSpec((None, s, 2 * PAD_HEAD), lambda bi, hp, i: (bi, 0, hp)),
            pl.BlockSpec((None, 2, HEAD_DIM, s), lambda bi, hp, i: (bi, hp, 0, 0)),
            pl.BlockSpec((2, HEAD_DIM, tq), lambda bi, hp, i: (hp, 0, 0)),
        ],
        out_specs=pl.BlockSpec((None, tq, 2 * HEAD_DIM), lambda bi, hp, i: (bi, i, hp)),
        out_shape=jax.ShapeDtypeStruct((b, s, ATTN_W), jnp.bfloat16),
        scratch_shapes=[pltpu.VMEM((2, 1, tq), jnp.float32),
                        pltpu.VMEM((2, HEAD_DIM + 2 * SUBLANES, tq), jnp.float32),
                        pltpu.VMEM((2, tq, tq), jnp.float32),
                        pltpu.VMEM((2, tq, tq), jnp.float32)],
        compiler_params=pltpu.CompilerParams(dimension_semantics=("arbitrary", "arbitrary", "arbitrary"),
                                             vmem_limit_bytes=VMEM_LIMIT_BYTES),
        name="fox_attention",
    )(qa, ka, vt, gain)


def _hgrn_kernel(q_ref, lf_ref, k_ref, v_ref, g_ref, tri_ref, bd_ref, gain_ref, out_ref,
                 state, c_s, k_s, v_s, *, n_chunks):
    @pl.when(pl.program_id(1) == 0)
    def _():
        state[...] = jnp.zeros_like(state)

    ch = HGRN_CHUNK
    sb = HGRN_SUB
    nsb = ch // sb
    row_sb = lax.broadcasted_iota(jnp.int32, (sb, HGRN_W), 0)
    r128 = lax.broadcasted_iota(jnp.int32, (LANES, LANES), 0)
    c128 = lax.broadcasted_iota(jnp.int32, (LANES, LANES), 1)
    same_head = (r128 < HEAD_DIM) == (c128 < HEAD_DIM)
    t64 = lax.broadcasted_iota(jnp.int32, (ch, LANES), 0) // sb
    s64 = (lax.broadcasted_iota(jnp.int32, (ch, LANES), 1) % HEAD_DIM) // sb
    level2 = ((t64 == 1) & (s64 == 0)) | ((t64 == 3) & (s64 == 2))
    lane_head0 = lax.broadcasted_iota(jnp.int32, (ch, LANES), 1) < HEAD_DIM
    bd = bd_ref[...]
    zeros_sb = jnp.zeros((sb, HGRN_W), jnp.float32)

    def chunk(ci, _):
        r0 = pl.multiple_of(ci * ch, ch)
        q = q_ref[pl.ds(r0, ch), :]
        k = k_ref[pl.ds(r0, ch), :]
        v = v_ref[pl.ds(r0, ch), :]
        p1, p2, p3 = _split3(lf_ref[pl.ds(r0, ch), :])
        cc = _dot(tri_ref[...], jnp.concatenate([p1, p2, p3], axis=1))
        c = cc[:, 0:HGRN_W] + cc[:, HGRN_W:2 * HGRN_W] + cc[:, 2 * HGRN_W:3 * HGRN_W]
        c_s[...] = c
        k_s[...] = k
        v_s[...] = v
        blk = lambda a, n: a[n * sb:(n + 1) * sb]

        ps = []
        for n in range(nsb):
            cn, qn = blk(c, n), blk(q, n)
            for s in range(sb):
                r = n * sb + s
                dec = jnp.exp(jnp.where(row_sb >= s, cn - c_s[r:r + 1, :], MASK_LOG_DECAY))
                ps.append(_bf16(qn * k_s[r:r + 1, :] * dec))
        a_d = _dot(jnp.concatenate(ps, axis=0), bd)
        o_parts = []
        for n in range(nsb):
            acc = jnp.zeros((sb, HGRN_W), jnp.float32)
            for s in range(sb):
                r = n * sb + s
                acc = acc + a_d[r * sb:(r + 1) * sb] * v_s[r:r + 1, :]
            o_parts.append(acc)
        o = jnp.concatenate(o_parts, axis=0)

        ref1 = c[2 * sb - 1:2 * sb]
        ref2a = c[sb - 1:sb]
        ref2b = c[3 * sb - 1:3 * sb]
        hi, lo = slice(2 * sb, 4 * sb), slice(0, 2 * sb)
        zeros_half = jnp.zeros((2 * sb, HGRN_W), jnp.float32)
        q1 = jnp.concatenate([zeros_half, q[hi] * jnp.exp(c[hi] - ref1)], axis=0)
        k1 = jnp.concatenate([k[lo] * jnp.exp(ref1 - c[lo]), zeros_half], axis=0)
        q2 = jnp.concatenate([zeros_sb, blk(q, 1) * jnp.exp(blk(c, 1) - ref2a),
                              zeros_sb, blk(q, 3) * jnp.exp(blk(c, 3) - ref2b)], axis=0)
        k2 = jnp.concatenate([blk(k, 0) * jnp.exp(ref2a - blk(c, 0)), zeros_sb,
                              blk(k, 2) * jnp.exp(ref2b - blk(c, 2)), zeros_sb], axis=0)

        c_last = c[ch - 1:ch, :]
        qe = q * jnp.exp(c)
        kd = k * jnp.exp(c_last - c)
        e_last = jnp.exp(c_last)
        nt = (((1,), (1,)), ((), ()))
        o_off = []
        for bb in range(HGRN_W // LANES):
            sl = slice(bb * LANES, (bb + 1) * LANES)
            st = state[bb]
            stack2 = lambda a: jnp.concatenate([jnp.where(lane_head0, a[:, sl], 0.0),
                                                jnp.where(lane_head0, 0.0, a[:, sl])], axis=0)
            a1 = lax.dot_general(_bf16(q1[:, sl]), _bf16(stack2(k1)), nt, preferred_element_type=jnp.float32)
            a2 = lax.dot_general(_bf16(q2[:, sl]), _bf16(stack2(k2)), nt, preferred_element_type=jnp.float32)
            a_off = a1 + jnp.where(level2, a2, 0.0)
            lhs = jnp.concatenate([_bf16(a_off), _bf16(qe[:, sl])], axis=1)
            rhs = jnp.concatenate([_bf16(stack2(v)), _bf16(st)], axis=0)
            o_off.append(_dot(lhs, rhs))
            ecol = jnp.broadcast_to(e_last[:, sl], (LANES, LANES)).T
            upd = lax.dot_general(_bf16(kd[:, sl]), _bf16(v[:, sl]), (((0,), (0,)), ((), ())),
                                  preferred_element_type=jnp.float32)
            state[bb] = jnp.where(same_head, ecol * st + upd, 0.0)
        o = o + jnp.concatenate(o_off, axis=1)

        ss = _group_sum(o * o, bd)
        y = o * lax.rsqrt(ss * (1.0 / HEAD_DIM) + EPS) * gain_ref[...] * g_ref[pl.ds(r0, ch), :]
        out_ref[pl.ds(r0, ch), :] = _bf16(y)
        return 0

    lax.fori_loop(0, n_chunks, chunk, 0)


def _hgrn(rq, rlf, rk, rv, rg, tri, bd, gain, batch, seq, rows):
    t = rq.shape[0]
    steps = seq // rows
    blk = pl.BlockSpec((rows, HGRN_W), lambda b, j: (b * steps + j, 0))
    full = lambda a: pl.BlockSpec(a.shape, lambda b, j: (0,) * a.ndim)
    return pl.pallas_call(
        functools.partial(_hgrn_kernel, n_chunks=rows // HGRN_CHUNK),
        grid=(batch, steps),
        in_specs=[blk, blk, blk, blk, blk, full(tri), full(bd), full(gain)],
        out_specs=blk,
        out_shape=jax.ShapeDtypeStruct((t, HGRN_W), jnp.bfloat16),
        scratch_shapes=[pltpu.VMEM((HGRN_W // LANES, LANES, LANES), jnp.float32)]
        + [pltpu.VMEM((HGRN_CHUNK, HGRN_W), jnp.float32) for _ in range(3)],
        compiler_params=pltpu.CompilerParams(dimension_semantics=("arbitrary", "arbitrary"),
                                             vmem_limit_bytes=VMEM_LIMIT_BYTES),
        name="hgrn2",
    )(rq, rlf, rk, rv, rg, tri, bd, gain)


INFO_E1, INFO_E2, INFO_R1, INFO_R2, INFO_W1, INFO_W2 = range(6)


def _mix_and_norm(x_ref, ya_ref, yc_ref, yr_ref, w_ref, g_ref):
    xn = (x_ref[...]
          + _dot(ya_ref[...], w_ref[0:ATTN_W, :])
          + _dot(yc_ref[...], w_ref[ATTN_W:ATTN_W + CONV_C, :])
          + _dot(yr_ref[...], w_ref[ATTN_W + CONV_C:, :]))
    h = xn * lax.rsqrt(jnp.mean(xn * xn, axis=-1, keepdims=True) + EPS) * g_ref[...]
    return xn, h


def _outproj_dense_kernel(x_ref, ya_ref, yc_ref, yr_ref, w_ref, g_ref, xo_ref, h_ref):
    xn, h = _mix_and_norm(x_ref, ya_ref, yc_ref, yr_ref, w_ref, g_ref)
    xo_ref[...] = xn
    h_ref[...] = _bf16(h)


def _outproj_routed_kernel(x_ref, ya_ref, yc_ref, yr_ref, w_ref, g_ref, wr_ref, br_ref, tri_ref,
                           xo_ref, h_ref, info_ref, cnt_ref, cnt_s):
    @pl.when(pl.program_id(0) == 0)
    def _():
        cnt_s[...] = jnp.zeros_like(cnt_s)

    xn, h = _mix_and_norm(x_ref, ya_ref, yc_ref, yr_ref, w_ref, g_ref)
    xo_ref[...] = xn
    h_ref[...] = h
    h_hi = _bf16(h)
    h_lo = _bf16(h - h_hi.astype(jnp.float32))
    hw = _dot(h_hi, wr_ref[...])
    logits = hw[:, 0:LANES] + hw[:, LANES:2 * LANES] + _dot(h_lo, wr_ref[:, 0:LANES]) + br_ref[...]
    lane = lax.broadcasted_iota(jnp.int32, logits.shape, 1)
    logits = jnp.where(lane < N_EXPERTS, logits, MASK_VALUE)
    m1 = jnp.max(logits, axis=-1, keepdims=True)
    i1 = jnp.min(jnp.where(logits == m1, lane, LANES), axis=-1, keepdims=True)
    rest = jnp.where(lane == i1, MASK_VALUE, logits)
    m2 = jnp.max(rest, axis=-1, keepdims=True)
    i2 = jnp.min(jnp.where(rest == m2, lane, LANES), axis=-1, keepdims=True)
    e2 = jnp.exp(m2 - m1)
    w1 = 1.0 / (1.0 + e2)
    w2 = e2 * w1
    hit = (lane == i1) | (lane == i2)
    onehot = jnp.where(hit, 1.0, 0.0)
    incl = _dot(tri_ref[...], _bf16(onehot))
    rank = cnt_s[...] + incl - onehot
    r1 = jnp.sum(jnp.where(lane == i1, rank, 0.0), axis=-1, keepdims=True)
    r2 = jnp.sum(jnp.where(lane == i2, rank, 0.0), axis=-1, keepdims=True)
    cnt_new = cnt_s[...] + incl[incl.shape[0] - 1:, :]
    cnt_s[...] = cnt_new
    cnt_ref[...] = jnp.broadcast_to(cnt_new, cnt_ref.shape)
    rec = jnp.zeros(logits.shape, jnp.float32)
    for ln, val in ((INFO_E1, i1.astype(jnp.float32)), (INFO_E2, i2.astype(jnp.float32)),
                    (INFO_R1, r1), (INFO_R2, r2), (INFO_W1, w1), (INFO_W2, w2)):
        rec = jnp.where(lane == ln, val, rec)
    info_ref[...] = rec


def _outproj(x2, ya, yc, yr, wo, g, tm, router=None):
    t, dm = x2.shape
    row = lambda w: pl.BlockSpec((tm, w), lambda i: (i, 0))
    full = lambda a: pl.BlockSpec(a.shape, lambda i: (0,) * a.ndim)
    params = pltpu.CompilerParams(dimension_semantics=("arbitrary",), vmem_limit_bytes=VMEM_LIMIT_BYTES)
    base_specs = [row(dm), row(ATTN_W), row(CONV_C), row(HGRN_W), full(wo), full(g)]
    if router is None:
        return pl.pallas_call(
            _outproj_dense_kernel,
            grid=(t // tm,),
            in_specs=base_specs,
            out_specs=(row(dm), row(dm)),
            out_shape=(jax.ShapeDtypeStruct((t, dm), jnp.float32), jax.ShapeDtypeStruct((t, dm), jnp.bfloat16)),
            compiler_params=params,
            name="outproj_dense",
        )(x2, ya, yc, yr, wo, g)
    wr, br, tri = router
    return pl.pallas_call(
        _outproj_routed_kernel,
        grid=(t // tm,),
        in_specs=base_specs + [full(wr), full(br), full(tri)],
        out_specs=(row(dm), row(dm), row(LANES), pl.BlockSpec((SUBLANES, LANES), lambda i: (0, 0))),
        out_shape=(jax.ShapeDtypeStruct((t, dm), jnp.float32), jax.ShapeDtypeStruct((t, dm), jnp.float32),
                   jax.ShapeDtypeStruct((t, LANES), jnp.float32),
                   jax.ShapeDtypeStruct((SUBLANES, LANES), jnp.float32)),
        scratch_shapes=[pltpu.VMEM((1, LANES), jnp.float32)],
        compiler_params=params,
        name="outproj_routed",
    )(x2, ya, yc, yr, wo, g, wr, br, tri)


def _swiglu_step(h_ref, wg_ref, wu_ref, wd_ref):
    h = _bf16(h_ref[...])
    gt = _dot(h, _bf16(wg_ref[...]))
    up = _dot(h, _bf16(wu_ref[...]))
    act = _bf16(gt * (1.0 / (1.0 + jnp.exp(-gt))) * up)
    return _dot(act, _bf16(wd_ref[...]))


def _ffn_dense_kernel(h_ref, x_ref, wg_ref, wu_ref, wd_ref, out_ref, acc):
    f = pl.program_id(1)

    @pl.when(f == 0)
    def _():
        acc[...] = x_ref[...]

    acc[...] += _swiglu_step(h_ref, wg_ref, wu_ref, wd_ref)

    @pl.when(f == pl.num_programs(1) - 1)
    def _():
        out_ref[...] = acc[...]


def _ffn_dense(h, x2, wg, wu, wd, tm, fc):
    t, dm = x2.shape
    dff = wg.shape[1]
    return pl.pallas_call(
        _ffn_dense_kernel,
        grid=(t // tm, dff // fc),
        in_specs=[
            pl.BlockSpec((tm, dm), lambda i, f: (i, 0)),
            pl.BlockSpec((tm, dm), lambda i, f: (i, 0)),
            pl.BlockSpec((dm, fc), lambda i, f: (0, f)),
            pl.BlockSpec((dm, fc), lambda i, f: (0, f)),
            pl.BlockSpec((fc, dm), lambda i, f: (f, 0)),
        ],
        out_specs=pl.BlockSpec((tm, dm), lambda i, f: (i, 0)),
        out_shape=jax.ShapeDtypeStruct((t, dm), jnp.float32),
        scratch_shapes=[pltpu.VMEM((tm, dm), jnp.float32)],
        compiler_params=pltpu.CompilerParams(dimension_semantics=("arbitrary", "arbitrary"),
                                             vmem_limit_bytes=VMEM_LIMIT_BYTES),
        name="swiglu_dense",
    )(h, x2, wg, wu, wd)


def _ffn_grouped_kernel(te_ref, ta_ref, h_ref, wg_ref, wu_ref, wd_ref, out_ref, acc):
    i = pl.program_id(0)
    f = pl.program_id(1)

    @pl.when(ta_ref[i] == 1)
    def _():
        @pl.when(f == 0)
        def _():
            acc[...] = jnp.zeros_like(acc)

        acc[...] += _swiglu_step(h_ref, wg_ref, wu_ref, wd_ref)

        @pl.when(f == pl.num_programs(1) - 1)
        def _():
            out_ref[...] = acc[...]

    @pl.when((ta_ref[i] == 0) & (f == pl.num_programs(1) - 1))
    def _():
        out_ref[...] = jnp.zeros_like(out_ref)


def _ffn_grouped(tile_expert, tile_active, xs, wg, wu, wd, tm, fc):
    r, dm = xs.shape
    dff = wg.shape[2]
    nf = dff // fc
    fidx = lambda i, f, ta: f * ta[i] + (nf - 1) * (1 - ta[i])
    grid_spec = pltpu.PrefetchScalarGridSpec(
        num_scalar_prefetch=2,
        grid=(r // tm, nf),
        in_specs=[
            pl.BlockSpec((tm, dm), lambda i, f, te, ta: (i, 0)),
            pl.BlockSpec((None, dm, fc), lambda i, f, te, ta: (te[i], 0, fidx(i, f, ta))),
            pl.BlockSpec((None, dm, fc), lambda i, f, te, ta: (te[i], 0, fidx(i, f, ta))),
            pl.BlockSpec((None, fc, dm), lambda i, f, te, ta: (te[i], fidx(i, f, ta), 0)),
        ],
        out_specs=pl.BlockSpec((tm, dm), lambda i, f, te, ta: (i, 0)),
        scratch_shapes=[pltpu.VMEM((tm, dm), jnp.float32)],
    )
    return pl.pallas_call(
        _ffn_grouped_kernel,
        grid_spec=grid_spec,
        out_shape=jax.ShapeDtypeStruct((r, dm), jnp.float32),
        compiler_params=pltpu.CompilerParams(dimension_semantics=("arbitrary", "arbitrary"),
                                             vmem_limit_bytes=VMEM_LIMIT_BYTES),
        name="swiglu_grouped",
    )(tile_expert, tile_active, xs, wg, wu, wd)


def _row_copy(src, dst, sem):
    return pltpu.make_async_copy(src, dst, sem)


MOE_ISSUE_UNROLL = 8


def _wait_rows(ref, n_rows, sem):
    blk = ref.at[pl.ds(0, n_rows), :]
    pltpu.make_async_copy(blk, blk, sem).wait()


def _dispatch_kernel(p1_ref, p2_ref, h_ref, xs_in_ref, xs_ref, sems, *, tb, n_chunks):
    del xs_in_ref

    def issue_chunk(c, sem):
        def issue(rr, _):
            for u in range(MOE_ISSUE_UNROLL):
                tok = c * tb + rr * MOE_ISSUE_UNROLL + u
                src = h_ref.at[pl.ds(tok, 1), :]
                _row_copy(src, xs_ref.at[pl.ds(p1_ref[tok], 1), :], sem).start()
                _row_copy(src, xs_ref.at[pl.ds(p2_ref[tok], 1), :], sem).start()
            return 0

        lax.fori_loop(0, tb // MOE_ISSUE_UNROLL, issue, 0)

    issue_chunk(0, sems.at[0])

    def pair(cc, _):
        c = 2 * cc + 1
        issue_chunk(c, sems.at[1])
        _wait_rows(xs_ref, 2 * tb, sems.at[0])

        @pl.when(c + 1 < n_chunks)
        def _():
            issue_chunk(c + 1, sems.at[0])

        _wait_rows(xs_ref, 2 * tb, sems.at[1])
        return 0

    lax.fori_loop(0, n_chunks // 2, pair, 0)
    if n_chunks % 2 == 1:
        _wait_rows(xs_ref, 2 * tb, sems.at[0])


def _dispatch(pos1, pos2, h, xs_init, tb):
    t, dm = h.shape
    n_chunks = t // tb
    grid_spec = pltpu.PrefetchScalarGridSpec(
        num_scalar_prefetch=2,
        grid=(1,),
        in_specs=[pl.BlockSpec(memory_space=pl.ANY), pl.BlockSpec(memory_space=pl.ANY)],
        out_specs=pl.BlockSpec(memory_space=pl.ANY),
        scratch_shapes=[pltpu.SemaphoreType.DMA((2,))],
    )
    return pl.pallas_call(
        functools.partial(_dispatch_kernel, tb=tb, n_chunks=n_chunks),
        grid_spec=grid_spec,
        out_shape=jax.ShapeDtypeStruct(xs_init.shape, xs_init.dtype),
        input_output_aliases={3: 0},
        compiler_params=pltpu.CompilerParams(dimension_semantics=("arbitrary",), has_side_effects=True),
        name="moe_dispatch",
    )(pos1, pos2, h, xs_init)


def _combine_kernel(p1_ref, p2_ref, y_ref, x_ref, info_ref, out_ref, buf_a, buf_b, sems, *, tb):
    i = pl.program_id(0)
    n = pl.num_programs(0)

    def issue_step(step, buf, sem):
        def issue(rr, _):
            for u in range(MOE_ISSUE_UNROLL):
                r = rr * MOE_ISSUE_UNROLL + u
                tok = step * tb + r
                _row_copy(y_ref.at[pl.ds(p1_ref[tok], 1), :], buf.at[0, pl.ds(r, 1), :], sem).start()
                _row_copy(y_ref.at[pl.ds(p2_ref[tok], 1), :], buf.at[1, pl.ds(r, 1), :], sem).start()
            return 0

        lax.fori_loop(0, tb // MOE_ISSUE_UNROLL, issue, 0)

    def finish(buf, sem):
        _wait_rows(y_ref, 2 * tb, sem)
        info = info_ref[...]
        lane = lax.broadcasted_iota(jnp.int32, info.shape, 1)
        w1 = jnp.sum(jnp.where(lane == INFO_W1, info, 0.0), axis=-1, keepdims=True)
        w2 = jnp.sum(jnp.where(lane == INFO_W2, info, 0.0), axis=-1, keepdims=True)
        out_ref[...] = x_ref[...] + w1 * buf[0] + w2 * buf[1]

    @pl.when(i == 0)
    def _():
        issue_step(0, buf_a, sems.at[0])

    @pl.when(i % 2 == 0)
    def _():
        @pl.when(i + 1 < n)
        def _():
            issue_step(i + 1, buf_b, sems.at[1])

        finish(buf_a, sems.at[0])

    @pl.when(i % 2 == 1)
    def _():
        @pl.when(i + 1 < n)
        def _():
            issue_step(i + 1, buf_a, sems.at[0])

        finish(buf_b, sems.at[1])


def _combine(pos1, pos2, y, x2, info, tb):
    t, dm = x2.shape
    grid_spec = pltpu.PrefetchScalarGridSpec(
        num_scalar_prefetch=2,
        grid=(t // tb,),
        in_specs=[pl.BlockSpec(memory_space=pl.ANY),
                  pl.BlockSpec((tb, dm), lambda i, p1, p2: (i, 0)),
                  pl.BlockSpec((tb, LANES), lambda i, p1, p2: (i, 0))],
        out_specs=pl.BlockSpec((tb, dm), lambda i, p1, p2: (i, 0)),
        scratch_shapes=[pltpu.VMEM((2, tb, dm), jnp.float32), pltpu.VMEM((2, tb, dm), jnp.float32),
                        pltpu.SemaphoreType.DMA((2,))],
    )
    return pl.pallas_call(
        functools.partial(_combine_kernel, tb=tb),
        grid_spec=grid_spec,
        out_shape=jax.ShapeDtypeStruct((t, dm), jnp.float32),
        compiler_params=pltpu.CompilerParams(dimension_semantics=("arbitrary",)),
        name="moe_combine",
    )(pos1, pos2, y, x2, info)


def _moe_routing_tables(info, counts, tm, n_tiles):
    cnt = counts[0, :N_EXPERTS].astype(jnp.int32)
    padded = ((cnt + tm - 1) // tm) * tm
    ends = jnp.cumsum(padded)
    offsets = ends - padded
    e1 = info[:, INFO_E1].astype(jnp.int32)
    e2 = info[:, INFO_E2].astype(jnp.int32)
    pos1 = offsets[e1] + info[:, INFO_R1].astype(jnp.int32)
    pos2 = offsets[e2] + info[:, INFO_R2].astype(jnp.int32)
    start = jnp.arange(n_tiles, dtype=jnp.int32) * tm
    tile_expert = jnp.minimum(jnp.sum((start[:, None] >= ends[None, :]).astype(jnp.int32), axis=1), N_EXPERTS - 1)
    tile_active = (start < ends[-1]).astype(jnp.int32)
    return pos1, pos2, tile_expert, tile_active


def _pad_heads(w):
    d = w.shape[0]
    w = w.reshape(d, ATTN_HEADS, HEAD_DIM)
    return jnp.pad(w, ((0, 0), (0, 0), (0, PAD_HEAD - HEAD_DIM))).reshape(d, QP_W)


def _pack_w_in(w):
    s = np.cumsum([0, ATTN_W, ATTN_W, ATTN_W, ATTN_HEADS, CONV_C, CONV_C, CONV_C, HGRN_W, HGRN_W, HGRN_W, HGRN_W])
    seg = [w[:, s[n]:s[n + 1]] for n in range(11)]
    a_q, a_k, a_v, a_f, c_x, c_b, c_c, r_q, r_f, r_i, r_g = seg
    a_f = jnp.pad(a_f, ((0, 0), (0, LANES - ATTN_HEADS)))
    return _bf16(jnp.concatenate([_pad_heads(a_q), _pad_heads(a_k), a_v, c_x, c_b, c_c, r_q, r_f, r_i, r_g, a_f],
                                 axis=1))


def _selection_constants():
    selq = np.zeros((3 * LANES, QP_W), np.float32)
    selk = np.zeros((3 * LANES, QP_W), np.float32)
    oneq = np.zeros((1, QP_W), np.float32)
    onek = np.zeros((1, QP_W), np.float32)
    for hd in range(ATTN_HEADS):
        base = hd * PAD_HEAD + AUG0
        for piece in range(3):
            selq[piece * LANES + hd, base + piece] = 1.0
            selk[piece * LANES + hd, base + 3 + piece] = -1.0
            oneq[0, base + 3 + piece] = 1.0
            onek[0, base + piece] = 1.0
    return (jnp.asarray(selq, jnp.bfloat16), jnp.asarray(selk, jnp.bfloat16),
            jnp.asarray(oneq), jnp.asarray(onek))


def _pad_gain(gain, mult):
    g = jnp.pad(gain.astype(jnp.float32) * mult, (0, PAD_HEAD - HEAD_DIM))
    return jnp.tile(g, ATTN_HEADS).reshape(1, QP_W)


def kernel(x, norm_mix, w_in, attn_f_bias, q_norm_gain, k_norm_gain, conv_w, hgrn_lb_logits, mix_out_gain, w_out,
           norm_ffn, ffn_w_gate, ffn_w_up, ffn_w_down, moe_router_w, moe_router_b, moe_w_gate, moe_w_up, moe_w_down):
    batch, seq, dm = x.shape
    depth = w_in.shape[0]
    t = batch * seq
    f32 = jnp.float32
    tm = min(512, seq)
    tq = min(512, seq)
    hg_rows = min(256, seq)
    tm_ffn = min(1024, t)
    tb_moe = min(256, t)
    dff = ffn_w_gate.shape[-1]
    fc = dff // 4

    p_lb = jax.nn.softmax(hgrn_lb_logits.astype(f32), axis=0)
    lb_all = jnp.cumsum(p_lb, axis=0) - p_lb[0]

    tri_m = _bf16(jnp.tril(jnp.ones((tm, tm), f32)))
    tri_c = _bf16(jnp.tril(jnp.ones((HGRN_CHUNK, HGRN_CHUNK), f32)))
    grp = np.arange(HGRN_W) // HEAD_DIM
    bd = jnp.asarray(grp[:, None] == grp[None, :], jnp.bfloat16)
    selq, selk, oneq, onek = _selection_constants()
    scale = 1.0 / math.sqrt(HEAD_DIM)

    x2 = x.reshape(t, dm)
    for l in range(depth):
        wp = _pack_w_in(w_in[l])
        fb = jnp.pad(attn_f_bias[l].astype(f32), (0, LANES - ATTN_HEADS)).reshape(1, LANES)
        gq = _pad_gain(q_norm_gain[l], scale * LOG2E)
        gk = _pad_gain(k_norm_gain[l], 1.0)
        mog = mix_out_gain[l].astype(f32)
        ga = jnp.broadcast_to(mog[:ATTN_W].reshape(ATTN_HEADS, HEAD_DIM, 1), (ATTN_HEADS, HEAD_DIM, tq))
        gc = mog[ATTN_W:ATTN_W + CONV_C].reshape(1, CONV_C)
        gr = mog[ATTN_W + CONV_C:].reshape(1, HGRN_W)

        qa, ka, v, yc, rq, rlf, rk, rv, rg = _inproj(
            x2, seq, norm_mix[l].astype(f32).reshape(1, dm), wp, tri_m, fb, gq, gk, selq, selk, oneq, onek,
            conv_w[l].astype(f32), gc, bd, lb_all[l].reshape(1, HGRN_W), tm)

        vt = v.reshape(batch, ATTN_HEADS, HEAD_DIM, seq)
        ya = _attention(qa.reshape(batch, seq, QP_W), ka.reshape(batch, seq, QP_W), vt, ga, tq)
        yr = _hgrn(rq, rlf, rk, rv, rg, tri_c, bd, gr, batch, seq, hg_rows)

        j = l // 2
        wo = _bf16(w_out[l])
        gf = norm_ffn[l].astype(f32).reshape(1, dm)
        if l % 2 == 0:
            xo, h2 = _outproj(x2, ya.reshape(t, ATTN_W), yc, yr, wo, gf, tm)
            x2 = _ffn_dense(h2, xo, ffn_w_gate[j], ffn_w_up[j], ffn_w_down[j], tm_ffn, fc)
        else:
            wr32 = jnp.pad(moe_router_w[j].astype(f32), ((0, 0), (0, LANES - N_EXPERTS)))
            wr_hi = _bf16(wr32)
            wr = jnp.concatenate([wr_hi, _bf16(wr32 - wr_hi.astype(f32))], axis=1)
            br = jnp.pad(moe_router_b[j].astype(f32), (0, LANES - N_EXPERTS)).reshape(1, LANES)
            xo, h2, info, counts = _outproj(x2, ya.reshape(t, ATTN_W), yc, yr, wo, gf, tm, router=(wr, br, tri_m))
            n_tiles = (2 * t) // tm_ffn + N_EXPERTS
            pos1, pos2, tile_expert, tile_active = _moe_routing_tables(info, counts, tm_ffn, n_tiles)
            xs = _dispatch(pos1, pos2, h2, jnp.zeros((n_tiles * tm_ffn, dm), f32), tb_moe)
            ys = _ffn_grouped(tile_expert, tile_active, xs,
                              moe_w_gate[j], moe_w_up[j], moe_w_down[j], tm_ffn, fc)
            x2 = _combine(pos1, pos2, ys, xo, info, tb_moe)
    return x2.reshape(batch, seq, dm)
```

```python
import functools
import math

import jax
import jax.numpy as jnp
import numpy as np
from jax import lax
from jax.experimental import pallas as pl
from jax.experimental.pallas import tpu as pltpu

HEAD_DIM = 64
ATTN_HEADS = 8
CONV_C = 256
HGRN_W = 256
ATTN_W = ATTN_HEADS * HEAD_DIM
HGRN_CHUNK = 64
HGRN_SUB = 16
N_EXPERTS = 8
EPS = 1e-6
MASK_VALUE = -1e30
MASK_LOG_DECAY = -1e4
TINY = 1e-30
LOG2E = math.log2(math.e)

LANES = 128
SUBLANES = 8
VMEM_LIMIT_BYTES = 56 * 1024 * 1024

PAD_HEAD = LANES
QP_W = ATTN_HEADS * PAD_HEAD
OFF_Q = 0
OFF_K = OFF_Q + QP_W
OFF_V = OFF_K + QP_W
OFF_CX = OFF_V + ATTN_W
OFF_CB = OFF_CX + CONV_C
OFF_CC = OFF_CB + CONV_C
OFF_RQ = OFF_CC + CONV_C
OFF_RF = OFF_RQ + HGRN_W
OFF_RI = OFF_RF + HGRN_W
OFF_RG = OFF_RI + HGRN_W
OFF_AF = OFF_RG + HGRN_W
D_PACK = OFF_AF + LANES
AUG0 = HEAD_DIM


def _bf16(x):
    return x.astype(jnp.bfloat16)


def _split3(x):
    p1 = _bf16(x)
    r1 = x - p1.astype(jnp.float32)
    p2 = _bf16(r1)
    r2 = r1 - p2.astype(jnp.float32)
    return p1, p2, _bf16(r2)


def _dot(a, b):
    return jnp.dot(a, b, preferred_element_type=jnp.float32)


def _group_sum(x, bd):
    hi = _bf16(x)
    lo = _bf16(x - hi.astype(jnp.float32))
    return _dot(hi, bd) + _dot(lo, bd)


def _silu(x):
    return x * (1.0 / (1.0 + jnp.exp(-x)))


def _sigmoid(x):
    return 1.0 / (1.0 + jnp.exp(-x))


def _inproj_kernel(x_ref, g_ref, w_ref, tri_ref, fb_ref, gq_ref, gk_ref, selq_ref, selk_ref,
                   oneq_ref, onek_ref, convw_ref, gc_ref, bd_ref, lb_ref,
                   qa_ref, ka_ref, v_ref, yc_ref, rq_ref, rlf_ref, rk_ref, rv_ref, rg_ref,
                   dcarry, ucarry, *, tiles_per_seq):
    i = pl.program_id(0)

    @pl.when(i % tiles_per_seq == 0)
    def _():
        dcarry[...] = jnp.zeros_like(dcarry)
        ucarry[...] = jnp.zeros_like(ucarry)

    x = x_ref[...]
    h = x * lax.rsqrt(jnp.mean(x * x, axis=-1, keepdims=True) + EPS) * g_ref[...]
    hb = _bf16(h)

    zf = _dot(hb, w_ref[:, OFF_AF:OFF_AF + LANES]) + fb_ref[...]
    ls = jnp.minimum(zf, 0.0) - jnp.log(1.0 + jnp.exp(-jnp.abs(zf)))
    p1, p2, p3 = _split3(ls)
    loc = _dot(tri_ref[...], jnp.concatenate([p1, p2, p3], axis=1))
    d = dcarry[...] + loc[:, 0:LANES] + loc[:, LANES:2 * LANES] + loc[:, 2 * LANES:3 * LANES]
    dcarry[...] = d[d.shape[0] - 1:, :]
    e1, e2, e3 = _split3(d * LOG2E)
    ecat = jnp.concatenate([e1, e2, e3], axis=1)
    dq_part = _dot(ecat, selq_ref[...])
    dk_part = _dot(ecat, selk_ref[...])

    zq = _dot(hb, w_ref[:, OFF_Q:OFF_Q + QP_W])
    zk = _dot(hb, w_ref[:, OFF_K:OFF_K + QP_W])
    for hd in range(ATTN_HEADS):
        sl = slice(hd * PAD_HEAD, (hd + 1) * PAD_HEAD)
        qb = zq[:, sl]
        qn = qb * lax.rsqrt(jnp.sum(qb * qb, axis=-1, keepdims=True) * (1.0 / HEAD_DIM) + EPS) * gq_ref[:, sl]
        qa_ref[:, sl] = _bf16(qn + dq_part[:, sl] + oneq_ref[:, sl])
        kb = zk[:, sl]
        kn = kb * lax.rsqrt(jnp.sum(kb * kb, axis=-1, keepdims=True) * (1.0 / HEAD_DIM) + EPS) * gk_ref[:, sl]
        ka_ref[:, sl] = _bf16(kn + dk_part[:, sl] + onek_ref[:, sl])
    v_ref[...] = _bf16(_dot(hb, w_ref[:, OFF_V:OFF_V + ATTN_W])).T

    cx = _dot(hb, w_ref[:, OFF_CX:OFF_CX + CONV_C])
    cb = _dot(hb, w_ref[:, OFF_CB:OFF_CB + CONV_C])
    cc = _dot(hb, w_ref[:, OFF_CC:OFF_CC + CONV_C])
    u = cc * cx
    uc = ucarry[...]
    row8 = lax.broadcasted_iota(jnp.int32, (SUBLANES, CONV_C), 0)
    r1 = pltpu.roll(u, 1, 0)
    r2 = pltpu.roll(u, 2, 0)
    top1 = jnp.where(row8 < 1, pltpu.roll(uc, 1, 0), r1[0:SUBLANES])
    top2 = jnp.where(row8 < 2, pltpu.roll(uc, 2, 0), r2[0:SUBLANES])
    u1 = jnp.concatenate([top1, r1[SUBLANES:]], axis=0)
    u2 = jnp.concatenate([top2, r2[SUBLANES:]], axis=0)
    ucarry[...] = u[u.shape[0] - SUBLANES:, :]
    yc = cb * (u2 * convw_ref[0:1, :] + u1 * convw_ref[1:2, :] + u * convw_ref[2:3, :])
    ssc = _group_sum(yc * yc, bd_ref[...])
    yc_ref[...] = _bf16(yc * lax.rsqrt(ssc * (1.0 / HEAD_DIM) + EPS) * gc_ref[...])

    lb = lb_ref[...]
    zr = _dot(hb, w_ref[:, OFF_RF:OFF_RF + HGRN_W])
    sg = _sigmoid(zr)
    f = lb + (1.0 - lb) * sg
    rlf_ref[...] = jnp.log(jnp.maximum(f, TINY))
    rk_ref[...] = (1.0 - lb) * _sigmoid(-zr)
    rq_ref[...] = _silu(_dot(hb, w_ref[:, OFF_RQ:OFF_RQ + HGRN_W]))
    rv_ref[...] = _dot(hb, w_ref[:, OFF_RI:OFF_RI + HGRN_W])
    rg_ref[...] = _silu(_dot(hb, w_ref[:, OFF_RG:OFF_RG + HGRN_W]))


def _inproj(x2, seq, g, wp, tri, fb, gq, gk, selq, selk, oneq, onek, convw, gc, bd, lb, tm):
    t, dm = x2.shape
    full = lambda a: pl.BlockSpec(a.shape, lambda i: (0,) * a.ndim)
    row = lambda w: pl.BlockSpec((tm, w), lambda i: (i, 0))
    consts = (g, wp, tri, fb, gq, gk, selq, selk, oneq, onek, convw, gc, bd, lb)
    tiles_per_seq = seq // tm
    out_shape = (
        jax.ShapeDtypeStruct((t, QP_W), jnp.bfloat16),
        jax.ShapeDtypeStruct((t, QP_W), jnp.bfloat16),
        jax.ShapeDtypeStruct((t // seq, ATTN_W, seq), jnp.bfloat16),
        jax.ShapeDtypeStruct((t, CONV_C), jnp.bfloat16),
    ) + tuple(jax.ShapeDtypeStruct((t, HGRN_W), jnp.float32) for _ in range(5))
    vt_spec = pl.BlockSpec((None, ATTN_W, tm), lambda i: (i // tiles_per_seq, 0, i % tiles_per_seq))
    out_specs = (row(QP_W), row(QP_W), vt_spec, row(CONV_C)) + tuple(row(HGRN_W) for _ in range(5))
    return pl.pallas_call(
        functools.partial(_inproj_kernel, tiles_per_seq=seq // tm),
        grid=(t // tm,),
        in_specs=[row(dm)] + [full(a) for a in consts],
        out_specs=out_specs,
        out_shape=out_shape,
        scratch_shapes=[pltpu.VMEM((1, LANES), jnp.float32), pltpu.VMEM((SUBLANES, CONV_C), jnp.float32)],
        compiler_params=pltpu.CompilerParams(dimension_semantics=("arbitrary",),
                                             vmem_limit_bytes=VMEM_LIMIT_BYTES),
        name="inproj",
    )(x2, *consts)


def _attn_kernel(qa_ref, ka_ref, vt_ref, gain_ref, out_ref, m_s, acc_s, s_a, s_b, *, tq):
    i = pl.program_id(2)
    ones_rows = jnp.ones((2 * SUBLANES, tq), jnp.bfloat16)
    m_s[...] = jnp.full_like(m_s, MASK_VALUE)
    acc_s[...] = jnp.zeros_like(acc_s)

    def scores(j, s_buf):
        k0 = pl.multiple_of(j * tq, tq)
        for hh in range(2):
            sl = slice(hh * PAD_HEAD, (hh + 1) * PAD_HEAD)
            s_buf[hh] = lax.dot_general(ka_ref[pl.ds(k0, tq), sl], qa_ref[:, sl],
                                        (((1,), (1,)), ((), ())), preferred_element_type=jnp.float32)

    def consume(j, s_buf, masked):
        k0 = pl.multiple_of(j * tq, tq)
        for hh in range(2):
            s = s_buf[hh]
            if masked:
                kpos = lax.broadcasted_iota(jnp.int32, (tq, tq), 0)
                qpos = lax.broadcasted_iota(jnp.int32, (tq, tq), 1)
                s = jnp.where(kpos <= qpos, s, MASK_VALUE)
            m = m_s[hh]
            m_new = jnp.maximum(m, jnp.max(s, axis=0, keepdims=True))
            p = jnp.exp2(s - m_new)
            alpha = jnp.exp2(m - m_new)
            vaug = jnp.concatenate([vt_ref[hh, :, pl.ds(k0, tq)], ones_rows], axis=0)
            acc_s[hh] = acc_s[hh] * alpha + _dot(vaug, _bf16(p))
            m_s[hh] = m_new

    scores(0, s_a)

    def body(jj, _):
        j = 2 * jj
        scores(j + 1, s_b)
        consume(j, s_a, False)
        scores(j + 2, s_a)
        consume(j + 1, s_b, False)
        return 0

    lax.fori_loop(0, i // 2, body, 0)

    @pl.when(i % 2 == 1)
    def _():
        scores(i, s_b)
        consume(i - 1, s_a, False)
        consume(i, s_b, True)

    @pl.when(i % 2 == 0)
    def _():
        consume(i, s_a, True)

    ys = []
    for hh in range(2):
        acc = acc_s[hh]
        o = acc[0:HEAD_DIM] * (1.0 / acc[HEAD_DIM:HEAD_DIM + 1])
        ms = jnp.mean(o * o, axis=0, keepdims=True)
        ys.append(o * lax.rsqrt(ms + EPS) * gain_ref[hh])
    out_ref[...] = _bf16(jnp.concatenate(ys, axis=0).T)


def _attention(qa, ka, vt, gain, tq):
    b, s, _ = qa.shape
    return pl.pallas_call(
        functools.partial(_attn_kernel, tq=tq),
        grid=(b, ATTN_HEADS // 2, s // tq),
        in_specs=[
            pl.BlockSpec((None, tq, 2 * PAD_HEAD), lambda bi, hp, i: (bi, i, hp)),
            pl.BlockSpec((None, s, 2 * PAD_HEAD), lambda bi, hp, i: (bi, 0, hp)),
            pl.BlockSpec((None, 2, HEAD_DIM, s), lambda bi, hp, i: (bi, hp, 0, 0)),
            pl.BlockSpec((2, HEAD_DIM, tq), lambda bi, hp, i: (hp, 0, 0)),
        ],
        out_specs=pl.BlockSpec((None, tq, 2 * HEAD_DIM), lambda bi, hp, i: (bi, i, hp)),
        out_shape=jax.ShapeDtypeStruct((b, s, ATTN_W), jnp.bfloat16),
        scratch_shapes=[pltpu.VMEM((2, 1, tq), jnp.float32),
                        pltpu.VMEM((2, HEAD_DIM + 2 * SUBLANES, tq), jnp.float32),
                        pltpu.VMEM((2, tq, tq), jnp.float32),
                        pltpu.VMEM((2, tq, tq), jnp.float32)],
        compiler_params=pltpu.CompilerParams(dimension_semantics=("arbitrary", "arbitrary", "arbitrary"),
                                             vmem_limit_bytes=VMEM_LIMIT_BYTES),
        name="fox_attention",
    )(qa, ka, vt, gain)


def _hgrn_kernel(q_ref, lf_ref, k_ref, v_ref, g_ref, tri_ref, bd_ref, gain_ref, out_ref,
                 state, c_s, k_s, v_s, *, n_chunks):
    @pl.when(pl.program_id(1) == 0)
    def _():
        state[...] = jnp.zeros_like(state)

    ch = HGRN_CHUNK
    sb = HGRN_SUB
    nsb = ch // sb
    row_sb = lax.broadcasted_iota(jnp.int32, (sb, HGRN_W), 0)
    r128 = lax.broadcasted_iota(jnp.int32, (LANES, LANES), 0)
    c128 = lax.broadcasted_iota(jnp.int32, (LANES, LANES), 1)
    same_head = (r128 < HEAD_DIM) == (c128 < HEAD_DIM)
    t64 = lax.broadcasted_iota(jnp.int32, (ch, LANES), 0) // sb
    s64 = (lax.broadcasted_iota(jnp.int32, (ch, LANES), 1) % HEAD_DIM) // sb
    level2 = ((t64 == 1) & (s64 == 0)) | ((t64 == 3) & (s64 == 2))
    lane_head0 = lax.broadcasted_iota(jnp.int32, (ch, LANES), 1) < HEAD_DIM
    bd = bd_ref[...]
    zeros_sb = jnp.zeros((sb, HGRN_W), jnp.float32)

    def chunk(ci, _):
        r0 = pl.multiple_of(ci * ch, ch)
        q = q_ref[pl.ds(r0, ch), :]
        k = k_ref[pl.ds(r0, ch), :]
        v = v_ref[pl.ds(r0, ch), :]
        p1, p2, p3 = _split3(lf_ref[pl.ds(r0, ch), :])
        cc = _dot(tri_ref[...], jnp.concatenate([p1, p2, p3], axis=1))
        c = cc[:, 0:HGRN_W] + cc[:, HGRN_W:2 * HGRN_W] + cc[:, 2 * HGRN_W:3 * HGRN_W]
        c_s[...] = c
        k_s[...] = k
        v_s[...] = v
        blk = lambda a, n: a[n * sb:(n + 1) * sb]

        ps = []
        for n in range(nsb):
            cn, qn = blk(c, n), blk(q, n)
            for s in range(sb):
                r = n * sb + s
                dec = jnp.exp(jnp.where(row_sb >= s, cn - c_s[r:r + 1, :], MASK_LOG_DECAY))
                ps.append(_bf16(qn * k_s[r:r + 1, :] * dec))
        a_d = _dot(jnp.concatenate(ps, axis=0), bd)
        o_parts = []
        for n in range(nsb):
            acc = jnp.zeros((sb, HGRN_W), jnp.float32)
            for s in range(sb):
                r = n * sb + s
                acc = acc + a_d[r * sb:(r + 1) * sb] * v_s[r:r + 1, :]
            o_parts.append(acc)
        o = jnp.concatenate(o_parts, axis=0)

        ref1 = c[2 * sb - 1:2 * sb]
        ref2a = c[sb - 1:sb]
        ref2b = c[3 * sb - 1:3 * sb]
        hi, lo = slice(2 * sb, 4 * sb), slice(0, 2 * sb)
        zeros_half = jnp.zeros((2 * sb, HGRN_W), jnp.float32)
        q1 = jnp.concatenate([zeros_half, q[hi] * jnp.exp(c[hi] - ref1)], axis=0)
        k1 = jnp.concatenate([k[lo] * jnp.exp(ref1 - c[lo]), zeros_half], axis=0)
        q2 = jnp.concatenate([zeros_sb, blk(q, 1) * jnp.exp(blk(c, 1) - ref2a),
                              zeros_sb, blk(q, 3) * jnp.exp(blk(c, 3) - ref2b)], axis=0)
        k2 = jnp.concatenate([blk(k, 0) * jnp.exp(ref2a - blk(c, 0)), zeros_sb,
                              blk(k, 2) * jnp.exp(ref2b - blk(c, 2)), zeros_sb], axis=0)

        c_last = c[ch - 1:ch, :]
        qe = q * jnp.exp(c)
        kd = k * jnp.exp(c_last - c)
        e_last = jnp.exp(c_last)
        nt = (((1,), (1,)), ((), ()))
        o_off = []
        for bb in range(HGRN_W // LANES):
            sl = slice(bb * LANES, (bb + 1) * LANES)
            st = state[bb]
            stack2 = lambda a: jnp.concatenate([jnp.where(lane_head0, a[:, sl], 0.0),
                                                jnp.where(lane_head0, 0.0, a[:, sl])], axis=0)
            a1 = lax.dot_general(_bf16(q1[:, sl]), _bf16(stack2(k1)), nt, preferred_element_type=jnp.float32)
            a2 = lax.dot_general(_bf16(q2[:, sl]), _bf16(stack2(k2)), nt, preferred_element_type=jnp.float32)
            a_off = a1 + jnp.where(level2, a2, 0.0)
            lhs = jnp.concatenate([_bf16(a_off), _bf16(qe[:, sl])], axis=1)
            rhs = jnp.concatenate([_bf16(stack2(v)), _bf16(st)], axis=0)
            o_off.append(_dot(lhs, rhs))
            ecol = jnp.broadcast_to(e_last[:, sl], (LANES, LANES)).T
            upd = lax.dot_general(_bf16(kd[:, sl]), _bf16(v[:, sl]), (((0,), (0,)), ((), ())),
                                  preferred_element_type=jnp.float32)
            state[bb] = jnp.where(same_head, ecol * st + upd, 0.0)
        o = o + jnp.concatenate(o_off, axis=1)

        ss = _group_sum(o * o, bd)
        y = o * lax.rsqrt(ss * (1.0 / HEAD_DIM) + EPS) * gain_ref[...] * g_ref[pl.ds(r0, ch), :]
        out_ref[pl.ds(r0, ch), :] = _bf16(y)
        return 0

    lax.fori_loop(0, n_chunks, chunk, 0)


def _hgrn(rq, rlf, rk, rv, rg, tri, bd, gain, batch, seq, rows):
    t = rq.shape[0]
    steps = seq // rows
    blk = pl.BlockSpec((rows, HGRN_W), lambda b, j: (b * steps + j, 0))
    full = lambda a: pl.BlockSpec(a.shape, lambda b, j: (0,) * a.ndim)
    return pl.pallas_call(
        functools.partial(_hgrn_kernel, n_chunks=rows // HGRN_CHUNK),
        grid=(batch, steps),
        in_specs=[blk, blk, blk, blk, blk, full(tri), full(bd), full(gain)],
        out_specs=blk,
        out_shape=jax.ShapeDtypeStruct((t, HGRN_W), jnp.bfloat16),
        scratch_shapes=[pltpu.VMEM((HGRN_W // LANES, LANES, LANES), jnp.float32)]
        + [pltpu.VMEM((HGRN_CHUNK, HGRN_W), jnp.float32) for _ in range(3)],
        compiler_params=pltpu.CompilerParams(dimension_semantics=("arbitrary", "arbitrary"),
                                             vmem_limit_bytes=VMEM_LIMIT_BYTES),
        name="hgrn2",
    )(rq, rlf, rk, rv, rg, tri, bd, gain)


INFO_E1, INFO_E2, INFO_R1, INFO_R2, INFO_W1, INFO_W2 = range(6)


def _mix_and_norm(x_ref, ya_ref, yc_ref, yr_ref, w_ref, g_ref):
    xn = (x_ref[...]
          + _dot(ya_ref[...], w_ref[0:ATTN_W, :])
          + _dot(yc_ref[...], w_ref[ATTN_W:ATTN_W + CONV_C, :])
          + _dot(yr_ref[...], w_ref[ATTN_W + CONV_C:, :]))
    h = xn * lax.rsqrt(jnp.mean(xn * xn, axis=-1, keepdims=True) + EPS) * g_ref[...]
    return xn, h


def _outproj_dense_kernel(x_ref, ya_ref, yc_ref, yr_ref, w_ref, g_ref, xo_ref, h_ref):
    xn, h = _mix_and_norm(x_ref, ya_ref, yc_ref, yr_ref, w_ref, g_ref)
    xo_ref[...] = xn
    h_ref[...] = _bf16(h)


def _outproj_routed_kernel(x_ref, ya_ref, yc_ref, yr_ref, w_ref, g_ref, wr_ref, br_ref, tri_ref,
                           xo_ref, h_ref, info_ref, cnt_ref, cnt_s):
    @pl.when(pl.program_id(0) == 0)
    def _():
        cnt_s[...] = jnp.zeros_like(cnt_s)

    xn, h = _mix_and_norm(x_ref, ya_ref, yc_ref, yr_ref, w_ref, g_ref)
    xo_ref[...] = xn
    h_ref[...] = h
    h_hi = _bf16(h)
    h_lo = _bf16(h - h_hi.astype(jnp.float32))
    hw = _dot(h_hi, wr_ref[...])
    logits = hw[:, 0:LANES] + hw[:, LANES:2 * LANES] + _dot(h_lo, wr_ref[:, 0:LANES]) + br_ref[...]
    lane = lax.broadcasted_iota(jnp.int32, logits.shape, 1)
    logits = jnp.where(lane < N_EXPERTS, logits, MASK_VALUE)
    m1 = jnp.max(logits, axis=-1, keepdims=True)
    i1 = jnp.min(jnp.where(logits == m1, lane, LANES), axis=-1, keepdims=True)
    rest = jnp.where(lane == i1, MASK_VALUE, logits)
    m2 = jnp.max(rest, axis=-1, keepdims=True)
    i2 = jnp.min(jnp.where(rest == m2, lane, LANES), axis=-1, keepdims=True)
    e2 = jnp.exp(m2 - m1)
    w1 = 1.0 / (1.0 + e2)
    w2 = e2 * w1
    hit = (lane == i1) | (lane == i2)
    onehot = jnp.where(hit, 1.0, 0.0)
    incl = _dot(tri_ref[...], _bf16(onehot))
    rank = cnt_s[...] + incl - onehot
    r1 = jnp.sum(jnp.where(lane == i1, rank, 0.0), axis=-1, keepdims=True)
    r2 = jnp.sum(jnp.where(lane == i2, rank, 0.0), axis=-1, keepdims=True)
    cnt_new = cnt_s[...] + incl[incl.shape[0] - 1:, :]
    cnt_s[...] = cnt_new
    cnt_ref[...] = jnp.broadcast_to(cnt_new, cnt_ref.shape)
    rec = jnp.zeros(logits.shape, jnp.float32)
    for ln, val in ((INFO_E1, i1.astype(jnp.float32)), (INFO_E2, i2.astype(jnp.float32)),
                    (INFO_R1, r1), (INFO_R2, r2), (INFO_W1, w1), (INFO_W2, w2)):
        rec = jnp.where(lane == ln, val, rec)
    info_ref[...] = rec


def _outproj(x2, ya, yc, yr, wo, g, tm, router=None):
    t, dm = x2.shape
    row = lambda w: pl.BlockSpec((tm, w), lambda i: (i, 0))
    full = lambda a: pl.BlockSpec(a.shape, lambda i: (0,) * a.ndim)
    params = pltpu.CompilerParams(dimension_semantics=("arbitrary",), vmem_limit_bytes=VMEM_LIMIT_BYTES)
    base_specs = [row(dm), row(ATTN_W), row(CONV_C), row(HGRN_W), full(wo), full(g)]
    if router is None:
        return pl.pallas_call(
            _outproj_dense_kernel,
            grid=(t // tm,),
            in_specs=base_specs,
            out_specs=(row(dm), row(dm)),
            out_shape=(jax.ShapeDtypeStruct((t, dm), jnp.float32), jax.ShapeDtypeStruct((t, dm), jnp.bfloat16)),
            compiler_params=params,
            name="outproj_dense",
        )(x2, ya, yc, yr, wo, g)
    wr, br, tri = router
    return pl.pallas_call(
        _outproj_routed_kernel,
        grid=(t // tm,),
        in_specs=base_specs + [full(wr), full(br), full(tri)],
        out_specs=(row(dm), row(dm), row(LANES), pl.BlockSpec((SUBLANES, LANES), lambda i: (0, 0))),
        out_shape=(jax.ShapeDtypeStruct((t, dm), jnp.float32), jax.ShapeDtypeStruct((t, dm), jnp.float32),
                   jax.ShapeDtypeStruct((t, LANES), jnp.float32),
                   jax.ShapeDtypeStruct((SUBLANES, LANES), jnp.float32)),
        scratch_shapes=[pltpu.VMEM((1, LANES), jnp.float32)],
        compiler_params=params,
        name="outproj_routed",
    )(x2, ya, yc, yr, wo, g, wr, br, tri)


def _swiglu_step(h_ref, wg_ref, wu_ref, wd_ref):
    h = _bf16(h_ref[...])
    gt = _dot(h, _bf16(wg_ref[...]))
    up = _dot(h, _bf16(wu_ref[...]))
    act = _bf16(gt * (1.0 / (1.0 + jnp.exp(-gt))) * up)
    return _dot(act, _bf16(wd_ref[...]))


def _ffn_dense_kernel(h_ref, x_ref, wg_ref, wu_ref, wd_ref, out_ref, acc):
    f = pl.program_id(1)

    @pl.when(f == 0)
    def _():
        acc[...] = x_ref[...]

    acc[...] += _swiglu_step(h_ref, wg_ref, wu_ref, wd_ref)

    @pl.when(f == pl.num_programs(1) - 1)
    def _():
        out_ref[...] = acc[...]


def _ffn_dense(h, x2, wg, wu, wd, tm, fc):
    t, dm = x2.shape
    dff = wg.shape[1]
    return pl.pallas_call(
        _ffn_dense_kernel,
        grid=(t // tm, dff // fc),
        in_specs=[
            pl.BlockSpec((tm, dm), lambda i, f: (i, 0)),
            pl.BlockSpec((tm, dm), lambda i, f: (i, 0)),
            pl.BlockSpec((dm, fc), lambda i, f: (0, f)),
            pl.BlockSpec((dm, fc), lambda i, f: (0, f)),
            pl.BlockSpec((fc, dm), lambda i, f: (f, 0)),
        ],
        out_specs=pl.BlockSpec((tm, dm), lambda i, f: (i, 0)),
        out_shape=jax.ShapeDtypeStruct((t, dm), jnp.float32),
        scratch_shapes=[pltpu.VMEM((tm, dm), jnp.float32)],
        compiler_params=pltpu.CompilerParams(dimension_semantics=("arbitrary", "arbitrary"),
                                             vmem_limit_bytes=VMEM_LIMIT_BYTES),
        name="swiglu_dense",
    )(h, x2, wg, wu, wd)


def _ffn_grouped_kernel(te_ref, ta_ref, h_ref, wg_ref, wu_ref, wd_ref, out_ref, acc):
    i = pl.program_id(0)
    f = pl.program_id(1)

    @pl.when(ta_ref[i] == 1)
    def _():
        @pl.when(f == 0)
        def _():
            acc[...] = jnp.zeros_like(acc)

        acc[...] += _swiglu_step(h_ref, wg_ref, wu_ref, wd_ref)

        @pl.when(f == pl.num_programs(1) - 1)
        def _():
            out_ref[...] = acc[...]

    @pl.when((ta_ref[i] == 0) & (f == pl.num_programs(1) - 1))
    def _():
        out_ref[...] = jnp.zeros_like(out_ref)


def _ffn_grouped(tile_expert, tile_active, xs, wg, wu, wd, tm, fc):
    r, dm = xs.shape
    dff = wg.shape[2]
    nf = dff // fc
    fidx = lambda i, f, ta: f * ta[i] + (nf - 1) * (1 - ta[i])
    grid_spec = pltpu.PrefetchScalarGridSpec(
        num_scalar_prefetch=2,
        grid=(r // tm, nf),
        in_specs=[
            pl.BlockSpec((tm, dm), lambda i, f, te, ta: (i, 0)),
            pl.BlockSpec((None, dm, fc), lambda i, f, te, ta: (te[i], 0, fidx(i, f, ta))),
            pl.BlockSpec((None, dm, fc), lambda i, f, te, ta: (te[i], 0, fidx(i, f, ta))),
            pl.BlockSpec((None, fc, dm), lambda i, f, te, ta: (te[i], fidx(i, f, ta), 0)),
        ],
        out_specs=pl.BlockSpec((tm, dm), lambda i, f, te, ta: (i, 0)),
        scratch_shapes=[pltpu.VMEM((tm, dm), jnp.float32)],
    )
    return pl.pallas_call(
        _ffn_grouped_kernel,
        grid_spec=grid_spec,
        out_shape=jax.ShapeDtypeStruct((r, dm), jnp.float32),
        compiler_params=pltpu.CompilerParams(dimension_semantics=("arbitrary", "arbitrary"),
                                             vmem_limit_bytes=VMEM_LIMIT_BYTES),
        name="swiglu_grouped",
    )(tile_expert, tile_active, xs, wg, wu, wd)


def _row_copy(src, dst, sem):
    return pltpu.make_async_copy(src, dst, sem)


MOE_ISSUE_UNROLL = 8


def _wait_rows(ref, n_rows, sem):
    blk = ref.at[pl.ds(0, n_rows), :]
    pltpu.make_async_copy(blk, blk, sem).wait()


MOE_STAGE_BUFS = 3


def _dispatch_kernel(p1_ref, p2_ref, h_ref, xs_in_ref, xs_ref, stage, load_sems, scat_sems, *, tb, n_chunks):
    del xs_in_ref

    def load(c, slot):
        return pltpu.make_async_copy(h_ref.at[pl.ds(c * tb, tb), :], stage.at[slot], load_sems.at[slot])

    load(0, 0).start()

    def chunk(c, _):
        slot = lax.rem(c, MOE_STAGE_BUFS)
        load(c, slot).wait()

        @pl.when(c >= 2)
        def _():
            _wait_rows(xs_ref, 2 * tb, scat_sems.at[lax.rem(c - 2, MOE_STAGE_BUFS)])

        @pl.when(c + 1 < n_chunks)
        def _():
            load(c + 1, lax.rem(c + 1, MOE_STAGE_BUFS)).start()

        sem = scat_sems.at[slot]

        def issue(rr, _):
            for u in range(MOE_ISSUE_UNROLL):
                r = rr * MOE_ISSUE_UNROLL + u
                tok = c * tb + r
                src = stage.at[slot, pl.ds(r, 1), :]
                _row_copy(src, xs_ref.at[pl.ds(p1_ref[tok], 1), :], sem).start()
                _row_copy(src, xs_ref.at[pl.ds(p2_ref[tok], 1), :], sem).start()
            return 0

        lax.fori_loop(0, tb // MOE_ISSUE_UNROLL, issue, 0)
        return 0

    lax.fori_loop(0, n_chunks, chunk, 0)
    for c in range(max(n_chunks - 2, 0), n_chunks):
        _wait_rows(xs_ref, 2 * tb, scat_sems.at[c % MOE_STAGE_BUFS])


def _dispatch(pos1, pos2, h, xs_init, tb):
    t, dm = h.shape
    n_chunks = t // tb
    grid_spec = pltpu.PrefetchScalarGridSpec(
        num_scalar_prefetch=2,
        grid=(1,),
        in_specs=[pl.BlockSpec(memory_space=pl.ANY), pl.BlockSpec(memory_space=pl.ANY)],
        out_specs=pl.BlockSpec(memory_space=pl.ANY),
        scratch_shapes=[pltpu.VMEM((MOE_STAGE_BUFS, tb, dm), h.dtype),
                        pltpu.SemaphoreType.DMA((MOE_STAGE_BUFS,)),
                        pltpu.SemaphoreType.DMA((MOE_STAGE_BUFS,))],
    )
    return pl.pallas_call(
        functools.partial(_dispatch_kernel, tb=tb, n_chunks=n_chunks),
        grid_spec=grid_spec,
        out_shape=jax.ShapeDtypeStruct(xs_init.shape, xs_init.dtype),
        input_output_aliases={3: 0},
        compiler_params=pltpu.CompilerParams(dimension_semantics=("arbitrary",), has_side_effects=True),
        name="moe_dispatch",
    )(pos1, pos2, h, xs_init)


def _combine_kernel(p1_ref, p2_ref, y_ref, x_ref, info_ref, out_ref, buf_a, buf_b, sems, *, tb):
    i = pl.program_id(0)
    n = pl.num_programs(0)

    def issue_step(step, buf, sem):
        def issue(rr, _):
            for u in range(MOE_ISSUE_UNROLL):
                r = rr * MOE_ISSUE_UNROLL + u
                tok = step * tb + r
                _row_copy(y_ref.at[pl.ds(p1_ref[tok], 1), :], buf.at[0, pl.ds(r, 1), :], sem).start()
                _row_copy(y_ref.at[pl.ds(p2_ref[tok], 1), :], buf.at[1, pl.ds(r, 1), :], sem).start()
            return 0

        lax.fori_loop(0, tb // MOE_ISSUE_UNROLL, issue, 0)

    def finish(buf, sem):
        _wait_rows(y_ref, 2 * tb, sem)
        info = info_ref[...]
        lane = lax.broadcasted_iota(jnp.int32, info.shape, 1)
        w1 = jnp.sum(jnp.where(lane == INFO_W1, info, 0.0), axis=-1, keepdims=True)
        w2 = jnp.sum(jnp.where(lane == INFO_W2, info, 0.0), axis=-1, keepdims=True)
        out_ref[...] = x_ref[...] + w1 * buf[0] + w2 * buf[1]

    @pl.when(i == 0)
    def _():
        issue_step(0, buf_a, sems.at[0])

    @pl.when(i % 2 == 0)
    def _():
        @pl.when(i + 1 < n)
        def _():
            issue_step(i + 1, buf_b, sems.at[1])

        finish(buf_a, sems.at[0])

    @pl.when(i % 2 == 1)
    def _():
        @pl.when(i + 1 < n)
        def _():
            issue_step(i + 1, buf_a, sems.at[0])

        finish(buf_b, sems.at[1])


def _combine(pos1, pos2, y, x2, info, tb):
    t, dm = x2.shape
    grid_spec = pltpu.PrefetchScalarGridSpec(
        num_scalar_prefetch=2,
        grid=(t // tb,),
        in_specs=[pl.BlockSpec(memory_space=pl.ANY),
                  pl.BlockSpec((tb, dm), lambda i, p1, p2: (i, 0)),
                  pl.BlockSpec((tb, LANES), lambda i, p1, p2: (i, 0))],
        out_specs=pl.BlockSpec((tb, dm), lambda i, p1, p2: (i, 0)),
        scratch_shapes=[pltpu.VMEM((2, tb, dm), jnp.float32), pltpu.VMEM((2, tb, dm), jnp.float32),
                        pltpu.SemaphoreType.DMA((2,))],
    )
    return pl.pallas_call(
        functools.partial(_combine_kernel, tb=tb),
        grid_spec=grid_spec,
        out_shape=jax.ShapeDtypeStruct((t, dm), jnp.float32),
        compiler_params=pltpu.CompilerParams(dimension_semantics=("arbitrary",)),
        name="moe_combine",
    )(pos1, pos2, y, x2, info)


def _moe_routing_tables(info, counts, tm, n_tiles):
    cnt = counts[0, :N_EXPERTS].astype(jnp.int32)
    padded = ((cnt + tm - 1) // tm) * tm
    ends = jnp.cumsum(padded)
    offsets = ends - padded
    e1 = info[:, INFO_E1].astype(jnp.int32)
    e2 = info[:, INFO_E2].astype(jnp.int32)
    pos1 = offsets[e1] + info[:, INFO_R1].astype(jnp.int32)
    pos2 = offsets[e2] + info[:, INFO_R2].astype(jnp.int32)
    start = jnp.arange(n_tiles, dtype=jnp.int32) * tm
    tile_expert = jnp.minimum(jnp.sum((start[:, None] >= ends[None, :]).astype(jnp.int32), axis=1), N_EXPERTS - 1)
    tile_active = (start < ends[-1]).astype(jnp.int32)
    return pos1, pos2, tile_expert, tile_active


def _pad_heads(w):
    d = w.shape[0]
    w = w.reshape(d, ATTN_HEADS, HEAD_DIM)
    return jnp.pad(w, ((0, 0), (0, 0), (0, PAD_HEAD - HEAD_DIM))).reshape(d, QP_W)


def _pack_w_in(w):
    s = np.cumsum([0, ATTN_W, ATTN_W, ATTN_W, ATTN_HEADS, CONV_C, CONV_C, CONV_C, HGRN_W, HGRN_W, HGRN_W, HGRN_W])
    seg = [w[:, s[n]:s[n + 1]] for n in range(11)]
    a_q, a_k, a_v, a_f, c_x, c_b, c_c, r_q, r_f, r_i, r_g = seg
    a_f = jnp.pad(a_f, ((0, 0), (0, LANES - ATTN_HEADS)))
    return _bf16(jnp.concatenate([_pad_heads(a_q), _pad_heads(a_k), a_v, c_x, c_b, c_c, r_q, r_f, r_i, r_g, a_f],
                                 axis=1))


def _selection_constants():
    selq = np.zeros((3 * LANES, QP_W), np.float32)
    selk = np.zeros((3 * LANES, QP_W), np.float32)
    oneq = np.zeros((1, QP_W), np.float32)
    onek = np.zeros((1, QP_W), np.float32)
    for hd in range(ATTN_HEADS):
        base = hd * PAD_HEAD + AUG0
        for piece in range(3):
            selq[piece * LANES + hd, base + piece] = 1.0
            selk[piece * LANES + hd, base + 3 + piece] = -1.0
            oneq[0, base + 3 + piece] = 1.0
            onek[0, base + piece] = 1.0
    return (jnp.asarray(selq, jnp.bfloat16), jnp.asarray(selk, jnp.bfloat16),
            jnp.asarray(oneq), jnp.asarray(onek))


def _pad_gain(gain, mult):
    g = jnp.pad(gain.astype(jnp.float32) * mult, (0, PAD_HEAD - HEAD_DIM))
    return jnp.tile(g, ATTN_HEADS).reshape(1, QP_W)


def kernel(x, norm_mix, w_in, attn_f_bias, q_norm_gain, k_norm_gain, conv_w, hgrn_lb_logits, mix_out_gain, w_out,
           norm_ffn, ffn_w_gate, ffn_w_up, ffn_w_down, moe_router_w, moe_router_b, moe_w_gate, moe_w_up, moe_w_down):
    batch, seq, dm = x.shape
    depth = w_in.shape[0]
    t = batch * seq
    f32 = jnp.float32
    tm = min(512, seq)
    tq = min(512, seq)
    hg_rows = min(256, seq)
    tm_ffn = min(1024, t)
    tb_moe = min(256, t)
    dff = ffn_w_gate.shape[-1]
    fc = dff // 4

    p_lb = jax.nn.softmax(hgrn_lb_logits.astype(f32), axis=0)
    lb_all = jnp.cumsum(p_lb, axis=0) - p_lb[0]

    tri_m = _bf16(jnp.tril(jnp.ones((tm, tm), f32)))
    tri_c = _bf16(jnp.tril(jnp.ones((HGRN_CHUNK, HGRN_CHUNK), f32)))
    grp = np.arange(HGRN_W) // HEAD_DIM
    bd = jnp.asarray(grp[:, None] == grp[None, :], jnp.bfloat16)
    selq, selk, oneq, onek = _selection_constants()
    scale = 1.0 / math.sqrt(HEAD_DIM)

    x2 = x.reshape(t, dm)
    for l in range(depth):
        wp = _pack_w_in(w_in[l])
        fb = jnp.pad(attn_f_bias[l].astype(f32), (0, LANES - ATTN_HEADS)).reshape(1, LANES)
        gq = _pad_gain(q_norm_gain[l], scale * LOG2E)
        gk = _pad_gain(k_norm_gain[l], 1.0)
        mog = mix_out_gain[l].astype(f32)
        ga = jnp.broadcast_to(mog[:ATTN_W].reshape(ATTN_HEADS, HEAD_DIM, 1), (ATTN_HEADS, HEAD_DIM, tq))
        gc = mog[ATTN_W:ATTN_W + CONV_C].reshape(1, CONV_C)
        gr = mog[ATTN_W + CONV_C:].reshape(1, HGRN_W)

        qa, ka, v, yc, rq, rlf, rk, rv, rg = _inproj(
            x2, seq, norm_mix[l].astype(f32).reshape(1, dm), wp, tri_m, fb, gq, gk, selq, selk, oneq, onek,
            conv_w[l].astype(f32), gc, bd, lb_all[l].reshape(1, HGRN_W), tm)

        vt = v.reshape(batch, ATTN_HEADS, HEAD_DIM, seq)
        ya = _attention(qa.reshape(batch, seq, QP_W), ka.reshape(batch, seq, QP_W), vt, ga, tq)
        yr = _hgrn(rq, rlf, rk, rv, rg, tri_c, bd, gr, batch, seq, hg_rows)

        j = l // 2
        wo = _bf16(w_out[l])
        gf = norm_ffn[l].astype(f32).reshape(1, dm)
        if l % 2 == 0:
            xo, h2 = _outproj(x2, ya.reshape(t, ATTN_W), yc, yr, wo, gf, tm)
            x2 = _ffn_dense(h2, xo, ffn_w_gate[j], ffn_w_up[j], ffn_w_down[j], tm_ffn, fc)
        else:
            wr32 = jnp.pad(moe_router_w[j].astype(f32), ((0, 0), (0, LANES - N_EXPERTS)))
            wr_hi = _bf16(wr32)
            wr = jnp.concatenate([wr_hi, _bf16(wr32 - wr_hi.astype(f32))], axis=1)
            br = jnp.pad(moe_router_b[j].astype(f32), (0, LANES - N_EXPERTS)).reshape(1, LANES)
            xo, h2, info, counts = _outproj(x2, ya.reshape(t, ATTN_W), yc, yr, wo, gf, tm, router=(wr, br, tri_m))
            n_tiles = (2 * t) // tm_ffn + N_EXPERTS
            pos1, pos2, tile_expert, tile_active = _moe_routing_tables(info, counts, tm_ffn, n_tiles)
            xs = _dispatch(pos1, pos2, h2, jnp.zeros((n_tiles * tm_ffn, dm), f32), tb_moe)
            ys = _ffn_grouped(tile_expert, tile_active, xs,
                              moe_w_gate[j], moe_w_up[j], moe_w_down[j], tm_ffn, fc)
            x2 = _combine(pos1, pos2, ys, xo, info, tb_moe)
    return x2.reshape(batch, seq, dm)
```

```python
import functools
import math

import jax
import jax.numpy as jnp
import numpy as np
from jax import lax
from jax.experimental import pallas as pl
from jax.experimental.pallas import tpu as pltpu

HEAD_DIM = 64
ATTN_HEADS = 8
CONV_C = 256
HGRN_W = 256
ATTN_W = ATTN_HEADS * HEAD_DIM
HGRN_CHUNK = 64
HGRN_SUB = 16
N_EXPERTS = 8
EPS = 1e-6
MASK_VALUE = -1e30
MASK_LOG_DECAY = -1e4
TINY = 1e-30
LOG2E = math.log2(math.e)

LANES = 128
SUBLANES = 8
VMEM_LIMIT_BYTES = 56 * 1024 * 1024

PAD_HEAD = LANES
QP_W = ATTN_HEADS * PAD_HEAD
OFF_Q = 0
OFF_K = OFF_Q + QP_W
OFF_V = OFF_K + QP_W
OFF_CX = OFF_V + ATTN_W
OFF_CB = OFF_CX + CONV_C
OFF_CC = OFF_CB + CONV_C
OFF_RQ = OFF_CC + CONV_C
OFF_RF = OFF_RQ + HGRN_W
OFF_RI = OFF_RF + HGRN_W
OFF_RG = OFF_RI + HGRN_W
OFF_AF = OFF_RG + HGRN_W
D_PACK = OFF_AF + LANES
AUG0 = HEAD_DIM


def _bf16(x):
    return x.astype(jnp.bfloat16)


def _split3(x):
    p1 = _bf16(x)
    r1 = x - p1.astype(jnp.float32)
    p2 = _bf16(r1)
    r2 = r1 - p2.astype(jnp.float32)
    return p1, p2, _bf16(r2)


def _dot(a, b):
    return jnp.dot(a, b, preferred_element_type=jnp.float32)


def _group_sum(x, bd):
    hi = _bf16(x)
    lo = _bf16(x - hi.astype(jnp.float32))
    return _dot(hi, bd) + _dot(lo, bd)


def _silu(x):
    return x * (1.0 / (1.0 + jnp.exp(-x)))


def _sigmoid(x):
    return 1.0 / (1.0 + jnp.exp(-x))


def _inproj_kernel(x_ref, g_ref, w_ref, tri_ref, fb_ref, gq_ref, gk_ref, selq_ref, selk_ref,
                   oneq_ref, onek_ref, convw_ref, gc_ref, bd_ref, lb_ref,
                   qa_ref, ka_ref, v_ref, yc_ref, rq_ref, rlf_ref, rk_ref, rv_ref, rg_ref,
                   dcarry, ucarry, *, tiles_per_seq):
    i = pl.program_id(0)

    @pl.when(i % tiles_per_seq == 0)
    def _():
        dcarry[...] = jnp.zeros_like(dcarry)
        ucarry[...] = jnp.zeros_like(ucarry)

    x = x_ref[...]
    h = x * lax.rsqrt(jnp.mean(x * x, axis=-1, keepdims=True) + EPS) * g_ref[...]
    hb = _bf16(h)

    zf = _dot(hb, w_ref[:, OFF_AF:OFF_AF + LANES]) + fb_ref[...]
    ls = jnp.minimum(zf, 0.0) - jnp.log(1.0 + jnp.exp(-jnp.abs(zf)))
    p1, p2, p3 = _split3(ls)
    loc = _dot(tri_ref[...], jnp.concatenate([p1, p2, p3], axis=1))
    d = dcarry[...] + loc[:, 0:LANES] + loc[:, LANES:2 * LANES] + loc[:, 2 * LANES:3 * LANES]
    dcarry[...] = d[d.shape[0] - 1:, :]
    e1, e2, e3 = _split3(d * LOG2E)
    ecat = jnp.concatenate([e1, e2, e3], axis=1)
    dq_part = _dot(ecat, selq_ref[...])
    dk_part = _dot(ecat, selk_ref[...])

    zq = _dot(hb, w_ref[:, OFF_Q:OFF_Q + QP_W])
    zk = _dot(hb, w_ref[:, OFF_K:OFF_K + QP_W])
    for hd in range(ATTN_HEADS):
        sl = slice(hd * PAD_HEAD, (hd + 1) * PAD_HEAD)
        qb = zq[:, sl]
        qn = qb * lax.rsqrt(jnp.sum(qb * qb, axis=-1, keepdims=True) * (1.0 / HEAD_DIM) + EPS) * gq_ref[:, sl]
        qa_ref[:, sl] = _bf16(qn + dq_part[:, sl] + oneq_ref[:, sl])
        kb = zk[:, sl]
        kn = kb * lax.rsqrt(jnp.sum(kb * kb, axis=-1, keepdims=True) * (1.0 / HEAD_DIM) + EPS) * gk_ref[:, sl]
        ka_ref[:, sl] = _bf16(kn + dk_part[:, sl] + onek_ref[:, sl])
    v_ref[...] = _bf16(_dot(hb, w_ref[:, OFF_V:OFF_V + ATTN_W])).T

    cx = _dot(hb, w_ref[:, OFF_CX:OFF_CX + CONV_C])
    cb = _dot(hb, w_ref[:, OFF_CB:OFF_CB + CONV_C])
    cc = _dot(hb, w_ref[:, OFF_CC:OFF_CC + CONV_C])
    u = cc * cx
    uc = ucarry[...]
    row8 = lax.broadcasted_iota(jnp.int32, (SUBLANES, CONV_C), 0)
    r1 = pltpu.roll(u, 1, 0)
    r2 = pltpu.roll(u, 2, 0)
    top1 = jnp.where(row8 < 1, pltpu.roll(uc, 1, 0), r1[0:SUBLANES])
    top2 = jnp.where(row8 < 2, pltpu.roll(uc, 2, 0), r2[0:SUBLANES])
    u1 = jnp.concatenate([top1, r1[SUBLANES:]], axis=0)
    u2 = jnp.concatenate([top2, r2[SUBLANES:]], axis=0)
    ucarry[...] = u[u.shape[0] - SUBLANES:, :]
    yc = cb * (u2 * convw_ref[0:1, :] + u1 * convw_ref[1:2, :] + u * convw_ref[2:3, :])
    ssc = _group_sum(yc * yc, bd_ref[...])
    yc_ref[...] = _bf16(yc * lax.rsqrt(ssc * (1.0 / HEAD_DIM) + EPS) * gc_ref[...])

    lb = lb_ref[...]
    zr = _dot(hb, w_ref[:, OFF_RF:OFF_RF + HGRN_W])
    sg = _sigmoid(zr)
    f = lb + (1.0 - lb) * sg
    rlf_ref[...] = jnp.log(jnp.maximum(f, TINY))
    rk_ref[...] = (1.0 - lb) * _sigmoid(-zr)
    rq_ref[...] = _silu(_dot(hb, w_ref[:, OFF_RQ:OFF_RQ + HGRN_W]))
    rv_ref[...] = _dot(hb, w_ref[:, OFF_RI:OFF_RI + HGRN_W])
    rg_ref[...] = _silu(_dot(hb, w_ref[:, OFF_RG:OFF_RG + HGRN_W]))


def _inproj(x2, seq, g, wp, tri, fb, gq, gk, selq, selk, oneq, onek, convw, gc, bd, lb, tm):
    t, dm = x2.shape
    full = lambda a: pl.BlockSpec(a.shape, lambda i: (0,) * a.ndim)
    row = lambda w: pl.BlockSpec((tm, w), lambda i: (i, 0))
    consts = (g, wp, tri, fb, gq, gk, selq, selk, oneq, onek, convw, gc, bd, lb)
    tiles_per_seq = seq // tm
    out_shape = (
        jax.ShapeDtypeStruct((t, QP_W), jnp.bfloat16),
        jax.ShapeDtypeStruct((t, QP_W), jnp.bfloat16),
        jax.ShapeDtypeStruct((t // seq, ATTN_W, seq), jnp.bfloat16),
        jax.ShapeDtypeStruct((t, CONV_C), jnp.bfloat16),
    ) + tuple(jax.ShapeDtypeStruct((t, HGRN_W), jnp.float32) for _ in range(5))
    vt_spec = pl.BlockSpec((None, ATTN_W, tm), lambda i: (i // tiles_per_seq, 0, i % tiles_per_seq))
    out_specs = (row(QP_W), row(QP_W), vt_spec, row(CONV_C)) + tuple(row(HGRN_W) for _ in range(5))
    return pl.pallas_call(
        functools.partial(_inproj_kernel, tiles_per_seq=seq // tm),
        grid=(t // tm,),
        in_specs=[row(dm)] + [full(a) for a in consts],
        out_specs=out_specs,
        out_shape=out_shape,
        scratch_shapes=[pltpu.VMEM((1, LANES), jnp.float32), pltpu.VMEM((SUBLANES, CONV_C), jnp.float32)],
        compiler_params=pltpu.CompilerParams(dimension_semantics=("arbitrary",),
                                             vmem_limit_bytes=VMEM_LIMIT_BYTES),
        name="inproj",
    )(x2, *consts)


def _attn_kernel(qa_ref, ka_ref, vt_ref, gain_ref, out_ref, m_s, acc_s, s_a, s_b, *, tq):
    i = pl.program_id(2)
    ones_rows = jnp.ones((2 * SUBLANES, tq), jnp.bfloat16)
    m_s[...] = jnp.full_like(m_s, MASK_VALUE)
    acc_s[...] = jnp.zeros_like(acc_s)

    def scores(j, s_buf):
        k0 = pl.multiple_of(j * tq, tq)
        for hh in range(2):
            sl = slice(hh * PAD_HEAD, (hh + 1) * PAD_HEAD)
            s_buf[hh] = lax.dot_general(ka_ref[pl.ds(k0, tq), sl], qa_ref[:, sl],
                                        (((1,), (1,)), ((), ())), preferred_element_type=jnp.float32)

    def consume(j, s_buf, masked):
        k0 = pl.multiple_of(j * tq, tq)
        for hh in range(2):
            s = s_buf[hh]
            if masked:
                kpos = lax.broadcasted_iota(jnp.int32, (tq, tq), 0)
                qpos = lax.broadcasted_iota(jnp.int32, (tq, tq), 1)
                s = jnp.where(kpos <= qpos, s, MASK_VALUE)
            m = m_s[hh]
            m_new = jnp.maximum(m, jnp.max(s, axis=0, keepdims=True))
            p = jnp.exp2(s - m_new)
            alpha = jnp.exp2(m - m_new)
            vaug = jnp.concatenate([vt_ref[hh, :, pl.ds(k0, tq)], ones_rows], axis=0)
            acc_s[hh] = acc_s[hh] * alpha + _dot(vaug, _bf16(p))
            m_s[hh] = m_new

    scores(0, s_a)

    def pair(j):
        scores(j + 1, s_b)
        consume(j, s_a, False)
        scores(j + 2, s_a)
        consume(j + 1, s_b, False)

    def body(jj, _):
        pair(4 * jj)
        pair(4 * jj + 2)
        return 0

    lax.fori_loop(0, i // 4, body, 0)
    done = 4 * (i // 4)

    @pl.when(i - done >= 2)
    def _():
        pair(done)

    @pl.when(i % 2 == 1)
    def _():
        scores(i, s_b)
        consume(i - 1, s_a, False)
        consume(i, s_b, True)

    @pl.when(i % 2 == 0)
    def _():
        consume(i, s_a, True)

    ys = []
    for hh in range(2):
        acc = acc_s[hh]
        o = acc[0:HEAD_DIM] * (1.0 / acc[HEAD_DIM:HEAD_DIM + 1])
        ms = jnp.mean(o * o, axis=0, keepdims=True)
        ys.append(o * lax.rsqrt(ms + EPS) * gain_ref[hh])
    out_ref[...] = _bf16(jnp.concatenate(ys, axis=0).T)


def _attention(qa, ka, vt, gain, tq):
    b, s, _ = qa.shape
    return pl.pallas_call(
        functools.partial(_attn_kernel, tq=tq),
        grid=(b, ATTN_HEADS // 2, s // tq),
        in_specs=[
            pl.BlockSpec((None, tq, 2 * PAD_HEAD), lambda bi, hp, i: (bi, i, hp)),
            pl.BlockSpec((None, s, 2 * PAD_HEAD), lambda bi, hp, i: (bi, 0, hp)),
            pl.BlockSpec((None, 2, HEAD_DIM, s), lambda bi, hp, i: (bi, hp, 0, 0)),
            pl.BlockSpec((2, HEAD_DIM, tq), lambda bi, hp, i: (hp, 0, 0)),
        ],
        out_specs=pl.BlockSpec((None, tq, 2 * HEAD_DIM), lambda bi, hp, i: (bi, i, hp)),
        out_shape=jax.ShapeDtypeStruct((b, s, ATTN_W), jnp.bfloat16),
        scratch_shapes=[pltpu.VMEM((2, 1, tq), jnp.float32),
                        pltpu.VMEM((2, HEAD_DIM + 2 * SUBLANES, tq), jnp.float32),
                        pltpu.VMEM((2, tq, tq), jnp.float32),
                        pltpu.VMEM((2, tq, tq), jnp.float32)],
        compiler_params=pltpu.CompilerParams(dimension_semantics=("arbitrary", "arbitrary", "arbitrary"),
                                             vmem_limit_bytes=VMEM_LIMIT_BYTES),
        name="fox_attention",
    )(qa, ka, vt, gain)


def _hgrn_kernel(q_ref, lf_ref, k_ref, v_ref, g_ref, tri_ref, bd_ref, gain_ref, out_ref,
                 state, c_all, k_all, v_all, *, n_chunks):
    @pl.when(pl.program_id(1) == 0)
    def _():
        state[...] = jnp.zeros_like(state)

    ch = HGRN_CHUNK
    sb = HGRN_SUB
    nsb = ch // sb
    row_sb = lax.broadcasted_iota(jnp.int32, (sb, HGRN_W), 0)
    r128 = lax.broadcasted_iota(jnp.int32, (LANES, LANES), 0)
    c128 = lax.broadcasted_iota(jnp.int32, (LANES, LANES), 1)
    same_head = (r128 < HEAD_DIM) == (c128 < HEAD_DIM)
    t64 = lax.broadcasted_iota(jnp.int32, (ch, LANES), 0) // sb
    s64 = (lax.broadcasted_iota(jnp.int32, (ch, LANES), 1) % HEAD_DIM) // sb
    level2 = ((t64 == 1) & (s64 == 0)) | ((t64 == 3) & (s64 == 2))
    lane_head0 = lax.broadcasted_iota(jnp.int32, (ch, LANES), 1) < HEAD_DIM
    bd = bd_ref[...]
    zeros_sb = jnp.zeros((sb, HGRN_W), jnp.float32)

    def chunk(ci):
        r0 = ci * ch
        c_s, k_s, v_s = c_all.at[ci % 2], k_all.at[ci % 2], v_all.at[ci % 2]
        q = q_ref[pl.ds(r0, ch), :]
        k = k_ref[pl.ds(r0, ch), :]
        v = v_ref[pl.ds(r0, ch), :]
        p1, p2, p3 = _split3(lf_ref[pl.ds(r0, ch), :])
        cc = _dot(tri_ref[...], jnp.concatenate([p1, p2, p3], axis=1))
        c = cc[:, 0:HGRN_W] + cc[:, HGRN_W:2 * HGRN_W] + cc[:, 2 * HGRN_W:3 * HGRN_W]
        c_s[...] = c
        k_s[...] = k
        v_s[...] = v
        blk = lambda a, n: a[n * sb:(n + 1) * sb]

        ps = []
        for n in range(nsb):
            cn, qn = blk(c, n), blk(q, n)
            for s in range(sb):
                r = n * sb + s
                dec = jnp.exp(jnp.where(row_sb >= s, cn - c_s[r:r + 1, :], MASK_LOG_DECAY))
                ps.append(_bf16(qn * k_s[r:r + 1, :] * dec))
        a_d = _dot(jnp.concatenate(ps, axis=0), bd)
        o_parts = []
        for n in range(nsb):
            acc = jnp.zeros((sb, HGRN_W), jnp.float32)
            for s in range(sb):
                r = n * sb + s
                acc = acc + a_d[r * sb:(r + 1) * sb] * v_s[r:r + 1, :]
            o_parts.append(acc)
        o = jnp.concatenate(o_parts, axis=0)

        ref1 = c[2 * sb - 1:2 * sb]
        ref2a = c[sb - 1:sb]
        ref2b = c[3 * sb - 1:3 * sb]
        hi, lo = slice(2 * sb, 4 * sb), slice(0, 2 * sb)
        zeros_half = jnp.zeros((2 * sb, HGRN_W), jnp.float32)
        q1 = jnp.concatenate([zeros_half, q[hi] * jnp.exp(c[hi] - ref1)], axis=0)
        k1 = jnp.concatenate([k[lo] * jnp.exp(ref1 - c[lo]), zeros_half], axis=0)
        q2 = jnp.concatenate([zeros_sb, blk(q, 1) * jnp.exp(blk(c, 1) - ref2a),
                              zeros_sb, blk(q, 3) * jnp.exp(blk(c, 3) - ref2b)], axis=0)
        k2 = jnp.concatenate([blk(k, 0) * jnp.exp(ref2a - blk(c, 0)), zeros_sb,
                              blk(k, 2) * jnp.exp(ref2b - blk(c, 2)), zeros_sb], axis=0)

        c_last = c[ch - 1:ch, :]
        qe = q * jnp.exp(c)
        kd = k * jnp.exp(c_last - c)
        e_last = jnp.exp(c_last)
        nt = (((1,), (1,)), ((), ()))
        o_off = []
        for bb in range(HGRN_W // LANES):
            sl = slice(bb * LANES, (bb + 1) * LANES)
            st = state[bb]
            stack2 = lambda a: jnp.concatenate([jnp.where(lane_head0, a[:, sl], 0.0),
                                                jnp.where(lane_head0, 0.0, a[:, sl])], axis=0)
            a1 = lax.dot_general(_bf16(q1[:, sl]), _bf16(stack2(k1)), nt, preferred_element_type=jnp.float32)
            a2 = lax.dot_general(_bf16(q2[:, sl]), _bf16(stack2(k2)), nt, preferred_element_type=jnp.float32)
            a_off = a1 + jnp.where(level2, a2, 0.0)
            lhs = jnp.concatenate([_bf16(a_off), _bf16(qe[:, sl])], axis=1)
            rhs = jnp.concatenate([_bf16(stack2(v)), _bf16(st)], axis=0)
            o_off.append(_dot(lhs, rhs))
            ecol = jnp.broadcast_to(e_last[:, sl], (LANES, LANES)).T
            upd = lax.dot_general(_bf16(kd[:, sl]), _bf16(v[:, sl]), (((0,), (0,)), ((), ())),
                                  preferred_element_type=jnp.float32)
            state[bb] = jnp.where(same_head, ecol * st + upd, 0.0)
        o = o + jnp.concatenate(o_off, axis=1)

        ss = _group_sum(o * o, bd)
        y = o * lax.rsqrt(ss * (1.0 / HEAD_DIM) + EPS) * gain_ref[...] * g_ref[pl.ds(r0, ch), :]
        out_ref[pl.ds(r0, ch), :] = _bf16(y)

    for ci in range(n_chunks):
        chunk(ci)


def _hgrn(rq, rlf, rk, rv, rg, tri, bd, gain, batch, seq, rows):
    t = rq.shape[0]
    steps = seq // rows
    blk = pl.BlockSpec((rows, HGRN_W), lambda b, j: (b * steps + j, 0))
    full = lambda a: pl.BlockSpec(a.shape, lambda b, j: (0,) * a.ndim)
    return pl.pallas_call(
        functools.partial(_hgrn_kernel, n_chunks=rows // HGRN_CHUNK),
        grid=(batch, steps),
        in_specs=[blk, blk, blk, blk, blk, full(tri), full(bd), full(gain)],
        out_specs=blk,
        out_shape=jax.ShapeDtypeStruct((t, HGRN_W), jnp.bfloat16),
        scratch_shapes=[pltpu.VMEM((HGRN_W // LANES, LANES, LANES), jnp.float32)]
        + [pltpu.VMEM((2, HGRN_CHUNK, HGRN_W), jnp.float32) for _ in range(3)],
        compiler_params=pltpu.CompilerParams(dimension_semantics=("arbitrary", "arbitrary"),
                                             vmem_limit_bytes=VMEM_LIMIT_BYTES),
        name="hgrn2",
    )(rq, rlf, rk, rv, rg, tri, bd, gain)


INFO_E1, INFO_E2, INFO_R1, INFO_R2, INFO_W1, INFO_W2 = range(6)


def _mix_and_norm(x_ref, ya_ref, yc_ref, yr_ref, w_ref, g_ref):
    xn = (x_ref[...]
          + _dot(ya_ref[...], w_ref[0:ATTN_W, :])
          + _dot(yc_ref[...], w_ref[ATTN_W:ATTN_W + CONV_C, :])
          + _dot(yr_ref[...], w_ref[ATTN_W + CONV_C:, :]))
    h = xn * lax.rsqrt(jnp.mean(xn * xn, axis=-1, keepdims=True) + EPS) * g_ref[...]
    return xn, h


def _outproj_dense_kernel(x_ref, ya_ref, yc_ref, yr_ref, w_ref, g_ref, xo_ref, h_ref):
    xn, h = _mix_and_norm(x_ref, ya_ref, yc_ref, yr_ref, w_ref, g_ref)
    xo_ref[...] = xn
    h_ref[...] = _bf16(h)


def _outproj_routed_kernel(x_ref, ya_ref, yc_ref, yr_ref, w_ref, g_ref, wr_ref, br_ref, tri_ref,
                           xo_ref, h_ref, info_ref, cnt_ref, cnt_s):
    @pl.when(pl.program_id(0) == 0)
    def _():
        cnt_s[...] = jnp.zeros_like(cnt_s)

    xn, h = _mix_and_norm(x_ref, ya_ref, yc_ref, yr_ref, w_ref, g_ref)
    xo_ref[...] = xn
    h_ref[...] = h
    h_hi = _bf16(h)
    h_lo = _bf16(h - h_hi.astype(jnp.float32))
    hw = _dot(h_hi, wr_ref[...])
    logits = hw[:, 0:LANES] + hw[:, LANES:2 * LANES] + _dot(h_lo, wr_ref[:, 0:LANES]) + br_ref[...]
    lane = lax.broadcasted_iota(jnp.int32, logits.shape, 1)
    logits = jnp.where(lane < N_EXPERTS, logits, MASK_VALUE)
    m1 = jnp.max(logits, axis=-1, keepdims=True)
    i1 = jnp.min(jnp.where(logits == m1, lane, LANES), axis=-1, keepdims=True)
    rest = jnp.where(lane == i1, MASK_VALUE, logits)
    m2 = jnp.max(rest, axis=-1, keepdims=True)
    i2 = jnp.min(jnp.where(rest == m2, lane, LANES), axis=-1, keepdims=True)
    e2 = jnp.exp(m2 - m1)
    w1 = 1.0 / (1.0 + e2)
    w2 = e2 * w1
    hit = (lane == i1) | (lane == i2)
    onehot = jnp.where(hit, 1.0, 0.0)
    incl = _dot(tri_ref[...], _bf16(onehot))
    rank = cnt_s[...] + incl - onehot
    r1 = jnp.sum(jnp.where(lane == i1, rank, 0.0), axis=-1, keepdims=True)
    r2 = jnp.sum(jnp.where(lane == i2, rank, 0.0), axis=-1, keepdims=True)
    cnt_new = cnt_s[...] + incl[incl.shape[0] - 1:, :]
    cnt_s[...] = cnt_new
    cnt_ref[...] = jnp.broadcast_to(cnt_new, cnt_ref.shape)
    rec = jnp.zeros(logits.shape, jnp.float32)
    for ln, val in ((INFO_E1, i1.astype(jnp.float32)), (INFO_E2, i2.astype(jnp.float32)),
                    (INFO_R1, r1), (INFO_R2, r2), (INFO_W1, w1), (INFO_W2, w2)):
        rec = jnp.where(lane == ln, val, rec)
    info_ref[...] = rec


def _outproj(x2, ya, yc, yr, wo, g, tm, router=None):
    t, dm = x2.shape
    row = lambda w: pl.BlockSpec((tm, w), lambda i: (i, 0))
    full = lambda a: pl.BlockSpec(a.shape, lambda i: (0,) * a.ndim)
    params = pltpu.CompilerParams(dimension_semantics=("arbitrary",), vmem_limit_bytes=VMEM_LIMIT_BYTES)
    base_specs = [row(dm), row(ATTN_W), row(CONV_C), row(HGRN_W), full(wo), full(g)]
    if router is None:
        return pl.pallas_call(
            _outproj_dense_kernel,
            grid=(t // tm,),
            in_specs=base_specs,
            out_specs=(row(dm), row(dm)),
            out_shape=(jax.ShapeDtypeStruct((t, dm), jnp.float32), jax.ShapeDtypeStruct((t, dm), jnp.bfloat16)),
            compiler_params=params,
            name="outproj_dense",
        )(x2, ya, yc, yr, wo, g)
    wr, br, tri = router
    return pl.pallas_call(
        _outproj_routed_kernel,
        grid=(t // tm,),
        in_specs=base_specs + [full(wr), full(br), full(tri)],
        out_specs=(row(dm), row(dm), row(LANES), pl.BlockSpec((SUBLANES, LANES), lambda i: (0, 0))),
        out_shape=(jax.ShapeDtypeStruct((t, dm), jnp.float32), jax.ShapeDtypeStruct((t, dm), jnp.float32),
                   jax.ShapeDtypeStruct((t, LANES), jnp.float32),
                   jax.ShapeDtypeStruct((SUBLANES, LANES), jnp.float32)),
        scratch_shapes=[pltpu.VMEM((1, LANES), jnp.float32)],
        compiler_params=params,
        name="outproj_routed",
    )(x2, ya, yc, yr, wo, g, wr, br, tri)


def _swiglu_step(h_ref, wg_ref, wu_ref, wd_ref):
    h = _bf16(h_ref[...])
    gt = _dot(h, _bf16(wg_ref[...]))
    up = _dot(h, _bf16(wu_ref[...]))
    act = _bf16(gt * (1.0 / (1.0 + jnp.exp(-gt))) * up)
    return _dot(act, _bf16(wd_ref[...]))


def _ffn_dense_kernel(h_ref, x_ref, wg_ref, wu_ref, wd_ref, out_ref, acc):
    f = pl.program_id(1)

    @pl.when(f == 0)
    def _():
        acc[...] = x_ref[...]

    acc[...] += _swiglu_step(h_ref, wg_ref, wu_ref, wd_ref)

    @pl.when(f == pl.num_programs(1) - 1)
    def _():
        out_ref[...] = acc[...]


def _ffn_dense(h, x2, wg, wu, wd, tm, fc):
    t, dm = x2.shape
    dff = wg.shape[1]
    return pl.pallas_call(
        _ffn_dense_kernel,
        grid=(t // tm, dff // fc),
        in_specs=[
            pl.BlockSpec((tm, dm), lambda i, f: (i, 0)),
            pl.BlockSpec((tm, dm), lambda i, f: (i, 0)),
            pl.BlockSpec((dm, fc), lambda i, f: (0, f)),
            pl.BlockSpec((dm, fc), lambda i, f: (0, f)),
            pl.BlockSpec((fc, dm), lambda i, f: (f, 0)),
        ],
        out_specs=pl.BlockSpec((tm, dm), lambda i, f: (i, 0)),
        out_shape=jax.ShapeDtypeStruct((t, dm), jnp.float32),
        scratch_shapes=[pltpu.VMEM((tm, dm), jnp.float32)],
        compiler_params=pltpu.CompilerParams(dimension_semantics=("arbitrary", "arbitrary"),
                                             vmem_limit_bytes=VMEM_LIMIT_BYTES),
        name="swiglu_dense",
    )(h, x2, wg, wu, wd)


def _ffn_grouped_kernel(te_ref, ta_ref, h_ref, wg_ref, wu_ref, wd_ref, out_ref, acc):
    i = pl.program_id(0)
    f = pl.program_id(1)

    @pl.when(ta_ref[i] == 1)
    def _():
        @pl.when(f == 0)
        def _():
            acc[...] = jnp.zeros_like(acc)

        acc[...] += _swiglu_step(h_ref, wg_ref, wu_ref, wd_ref)

        @pl.when(f == pl.num_programs(1) - 1)
        def _():
            out_ref[...] = acc[...]

    @pl.when((ta_ref[i] == 0) & (f == pl.num_programs(1) - 1))
    def _():
        out_ref[...] = jnp.zeros_like(out_ref)


def _ffn_grouped(tile_expert, tile_active, xs, wg, wu, wd, tm, fc):
    r, dm = xs.shape
    dff = wg.shape[2]
    nf = dff // fc
    fidx = lambda i, f, ta: f * ta[i] + (nf - 1) * (1 - ta[i])
    grid_spec = pltpu.PrefetchScalarGridSpec(
        num_scalar_prefetch=2,
        grid=(r // tm, nf),
        in_specs=[
            pl.BlockSpec((tm, dm), lambda i, f, te, ta: (i, 0)),
            pl.BlockSpec((None, dm, fc), lambda i, f, te, ta: (te[i], 0, fidx(i, f, ta))),
            pl.BlockSpec((None, dm, fc), lambda i, f, te, ta: (te[i], 0, fidx(i, f, ta))),
            pl.BlockSpec((None, fc, dm), lambda i, f, te, ta: (te[i], fidx(i, f, ta), 0)),
        ],
        out_specs=pl.BlockSpec((tm, dm), lambda i, f, te, ta: (i, 0)),
        scratch_shapes=[pltpu.VMEM((tm, dm), jnp.float32)],
    )
    return pl.pallas_call(
        _ffn_grouped_kernel,
        grid_spec=grid_spec,
        out_shape=jax.ShapeDtypeStruct((r, dm), jnp.float32),
        compiler_params=pltpu.CompilerParams(dimension_semantics=("arbitrary", "arbitrary"),
                                             vmem_limit_bytes=VMEM_LIMIT_BYTES),
        name="swiglu_grouped",
    )(tile_expert, tile_active, xs, wg, wu, wd)


def _row_copy(src, dst, sem):
    return pltpu.make_async_copy(src, dst, sem)


MOE_ISSUE_UNROLL = 8


def _wait_rows(ref, n_rows, sem):
    blk = ref.at[pl.ds(0, n_rows), :]
    pltpu.make_async_copy(blk, blk, sem).wait()


MOE_STAGE_BUFS = 3


def _dispatch_kernel(p1_ref, p2_ref, h_ref, xs_in_ref, xs_ref, stage, load_sems, scat_sems, *, tb, n_chunks):
    del xs_in_ref

    def load(c, slot):
        return pltpu.make_async_copy(h_ref.at[pl.ds(c * tb, tb), :], stage.at[slot], load_sems.at[slot])

    load(0, 0).start()

    def chunk(c, _):
        slot = lax.rem(c, MOE_STAGE_BUFS)
        load(c, slot).wait()

        @pl.when(c >= 2)
        def _():
            _wait_rows(xs_ref, 2 * tb, scat_sems.at[lax.rem(c - 2, MOE_STAGE_BUFS)])

        @pl.when(c + 1 < n_chunks)
        def _():
            load(c + 1, lax.rem(c + 1, MOE_STAGE_BUFS)).start()

        sem = scat_sems.at[slot]

        def issue(rr, _):
            for u in range(MOE_ISSUE_UNROLL):
                r = rr * MOE_ISSUE_UNROLL + u
                tok = c * tb + r
                src = stage.at[slot, pl.ds(r, 1), :]
                _row_copy(src, xs_ref.at[pl.ds(p1_ref[tok], 1), :], sem).start()
                _row_copy(src, xs_ref.at[pl.ds(p2_ref[tok], 1), :], sem).start()
            return 0

        lax.fori_loop(0, tb // MOE_ISSUE_UNROLL, issue, 0)
        return 0

    lax.fori_loop(0, n_chunks, chunk, 0)
    for c in range(max(n_chunks - 2, 0), n_chunks):
        _wait_rows(xs_ref, 2 * tb, scat_sems.at[c % MOE_STAGE_BUFS])


def _dispatch(pos1, pos2, h, xs_init, tb):
    t, dm = h.shape
    n_chunks = t // tb
    grid_spec = pltpu.PrefetchScalarGridSpec(
        num_scalar_prefetch=2,
        grid=(1,),
        in_specs=[pl.BlockSpec(memory_space=pl.ANY), pl.BlockSpec(memory_space=pl.ANY)],
        out_specs=pl.BlockSpec(memory_space=pl.ANY),
        scratch_shapes=[pltpu.VMEM((MOE_STAGE_BUFS, tb, dm), h.dtype),
                        pltpu.SemaphoreType.DMA((MOE_STAGE_BUFS,)),
                        pltpu.SemaphoreType.DMA((MOE_STAGE_BUFS,))],
    )
    return pl.pallas_call(
        functools.partial(_dispatch_kernel, tb=tb, n_chunks=n_chunks),
        grid_spec=grid_spec,
        out_shape=jax.ShapeDtypeStruct(xs_init.shape, xs_init.dtype),
        input_output_aliases={3: 0},
        compiler_params=pltpu.CompilerParams(dimension_semantics=("arbitrary",), has_side_effects=True),
        name="moe_dispatch",
    )(pos1, pos2, h, xs_init)


def _combine_kernel(p1_ref, p2_ref, y_ref, x_ref, info_ref, out_ref, buf_a, buf_b, sems, *, tb):
    i = pl.program_id(0)
    n = pl.num_programs(0)

    def issue_step(step, buf, sem):
        def issue(rr, _):
            for u in range(MOE_ISSUE_UNROLL):
                r = rr * MOE_ISSUE_UNROLL + u
                tok = step * tb + r
                _row_copy(y_ref.at[pl.ds(p1_ref[tok], 1), :], buf.at[0, pl.ds(r, 1), :], sem).start()
                _row_copy(y_ref.at[pl.ds(p2_ref[tok], 1), :], buf.at[1, pl.ds(r, 1), :], sem).start()
            return 0

        lax.fori_loop(0, tb // MOE_ISSUE_UNROLL, issue, 0)

    def finish(buf, sem):
        _wait_rows(y_ref, 2 * tb, sem)
        info = info_ref[...]
        lane = lax.broadcasted_iota(jnp.int32, info.shape, 1)
        w1 = jnp.sum(jnp.where(lane == INFO_W1, info, 0.0), axis=-1, keepdims=True)
        w2 = jnp.sum(jnp.where(lane == INFO_W2, info, 0.0), axis=-1, keepdims=True)
        out_ref[...] = x_ref[...] + w1 * buf[0] + w2 * buf[1]

    @pl.when(i == 0)
    def _():
        issue_step(0, buf_a, sems.at[0])

    @pl.when(i % 2 == 0)
    def _():
        @pl.when(i + 1 < n)
        def _():
            issue_step(i + 1, buf_b, sems.at[1])

        finish(buf_a, sems.at[0])

    @pl.when(i % 2 == 1)
    def _():
        @pl.when(i + 1 < n)
        def _():
            issue_step(i + 1, buf_a, sems.at[0])

        finish(buf_b, sems.at[1])


def _combine(pos1, pos2, y, x2, info, tb):
    t, dm = x2.shape
    grid_spec = pltpu.PrefetchScalarGridSpec(
        num_scalar_prefetch=2,
        grid=(t // tb,),
        in_specs=[pl.BlockSpec(memory_space=pl.ANY),
                  pl.BlockSpec((tb, dm), lambda i, p1, p2: (i, 0)),
                  pl.BlockSpec((tb, LANES), lambda i, p1, p2: (i, 0))],
        out_specs=pl.BlockSpec((tb, dm), lambda i, p1, p2: (i, 0)),
        scratch_shapes=[pltpu.VMEM((2, tb, dm), jnp.float32), pltpu.VMEM((2, tb, dm), jnp.float32),
                        pltpu.SemaphoreType.DMA((2,))],
    )
    return pl.pallas_call(
        functools.partial(_combine_kernel, tb=tb),
        grid_spec=grid_spec,
        out_shape=jax.ShapeDtypeStruct((t, dm), jnp.float32),
        compiler_params=pltpu.CompilerParams(dimension_semantics=("arbitrary",)),
        name="moe_combine",
    )(pos1, pos2, y, x2, info)


def _moe_routing_tables(info, counts, tm, n_tiles):
    cnt = counts[0, :N_EXPERTS].astype(jnp.int32)
    padded = ((cnt + tm - 1) // tm) * tm
    ends = jnp.cumsum(padded)
    offsets = ends - padded
    e1 = info[:, INFO_E1].astype(jnp.int32)
    e2 = info[:, INFO_E2].astype(jnp.int32)
    pos1 = offsets[e1] + info[:, INFO_R1].astype(jnp.int32)
    pos2 = offsets[e2] + info[:, INFO_R2].astype(jnp.int32)
    start = jnp.arange(n_tiles, dtype=jnp.int32) * tm
    tile_expert = jnp.minimum(jnp.sum((start[:, None] >= ends[None, :]).astype(jnp.int32), axis=1), N_EXPERTS - 1)
    tile_active = (start < ends[-1]).astype(jnp.int32)
    return pos1, pos2, tile_expert, tile_active


def _pad_heads(w):
    d = w.shape[0]
    w = w.reshape(d, ATTN_HEADS, HEAD_DIM)
    return jnp.pad(w, ((0, 0), (0, 0), (0, PAD_HEAD - HEAD_DIM))).reshape(d, QP_W)


def _pack_w_in(w):
    s = np.cumsum([0, ATTN_W, ATTN_W, ATTN_W, ATTN_HEADS, CONV_C, CONV_C, CONV_C, HGRN_W, HGRN_W, HGRN_W, HGRN_W])
    seg = [w[:, s[n]:s[n + 1]] for n in range(11)]
    a_q, a_k, a_v, a_f, c_x, c_b, c_c, r_q, r_f, r_i, r_g = seg
    a_f = jnp.pad(a_f, ((0, 0), (0, LANES - ATTN_HEADS)))
    return _bf16(jnp.concatenate([_pad_heads(a_q), _pad_heads(a_k), a_v, c_x, c_b, c_c, r_q, r_f, r_i, r_g, a_f],
                                 axis=1))


def _selection_constants():
    selq = np.zeros((3 * LANES, QP_W), np.float32)
    selk = np.zeros((3 * LANES, QP_W), np.float32)
    oneq = np.zeros((1, QP_W), np.float32)
    onek = np.zeros((1, QP_W), np.float32)
    for hd in range(ATTN_HEADS):
        base = hd * PAD_HEAD + AUG0
        for piece in range(3):
            selq[piece * LANES + hd, base + piece] = 1.0
            selk[piece * LANES + hd, base + 3 + piece] = -1.0
            oneq[0, base + 3 + piece] = 1.0
            onek[0, base + piece] = 1.0
    return (jnp.asarray(selq, jnp.bfloat16), jnp.asarray(selk, jnp.bfloat16),
            jnp.asarray(oneq), jnp.asarray(onek))


def _pad_gain(gain, mult):
    g = jnp.pad(gain.astype(jnp.float32) * mult, (0, PAD_HEAD - HEAD_DIM))
    return jnp.tile(g, ATTN_HEADS).reshape(1, QP_W)


def kernel(x, norm_mix, w_in, attn_f_bias, q_norm_gain, k_norm_gain, conv_w, hgrn_lb_logits, mix_out_gain, w_out,
           norm_ffn, ffn_w_gate, ffn_w_up, ffn_w_down, moe_router_w, moe_router_b, moe_w_gate, moe_w_up, moe_w_down):
    batch, seq, dm = x.shape
    depth = w_in.shape[0]
    t = batch * seq
    f32 = jnp.float32
    tm = min(512, seq)
    tq = min(512, seq)
    hg_rows = min(512, seq)
    tm_ffn = min(1024, t)
    tb_moe = min(256, t)
    dff = ffn_w_gate.shape[-1]
    fc = dff // 4

    p_lb = jax.nn.softmax(hgrn_lb_logits.astype(f32), axis=0)
    lb_all = jnp.cumsum(p_lb, axis=0) - p_lb[0]

    tri_m = _bf16(jnp.tril(jnp.ones((tm, tm), f32)))
    tri_c = _bf16(jnp.tril(jnp.ones((HGRN_CHUNK, HGRN_CHUNK), f32)))
    grp = np.arange(HGRN_W) // HEAD_DIM
    bd = jnp.asarray(grp[:, None] == grp[None, :], jnp.bfloat16)
    selq, selk, oneq, onek = _selection_constants()
    scale = 1.0 / math.sqrt(HEAD_DIM)

    x2 = x.reshape(t, dm)
    for l in range(depth):
        wp = _pack_w_in(w_in[l])
        fb = jnp.pad(attn_f_bias[l].astype(f32), (0, LANES - ATTN_HEADS)).reshape(1, LANES)
        gq = _pad_gain(q_norm_gain[l], scale * LOG2E)
        gk = _pad_gain(k_norm_gain[l], 1.0)
        mog = mix_out_gain[l].astype(f32)
        ga = jnp.broadcast_to(mog[:ATTN_W].reshape(ATTN_HEADS, HEAD_DIM, 1), (ATTN_HEADS, HEAD_DIM, tq))
        gc = mog[ATTN_W:ATTN_W + CONV_C].reshape(1, CONV_C)
        gr = mog[ATTN_W + CONV_C:].reshape(1, HGRN_W)

        qa, ka, v, yc, rq, rlf, rk, rv, rg = _inproj(
            x2, seq, norm_mix[l].astype(f32).reshape(1, dm), wp, tri_m, fb, gq, gk, selq, selk, oneq, onek,
            conv_w[l].astype(f32), gc, bd, lb_all[l].reshape(1, HGRN_W), tm)

        vt = v.reshape(batch, ATTN_HEADS, HEAD_DIM, seq)
        ya = _attention(qa.reshape(batch, seq, QP_W), ka.reshape(batch, seq, QP_W), vt, ga, tq)
        yr = _hgrn(rq, rlf, rk, rv, rg, tri_c, bd, gr, batch, seq, hg_rows)

        j = l // 2
        wo = _bf16(w_out[l])
        gf = norm_ffn[l].astype(f32).reshape(1, dm)
        if l % 2 == 0:
            xo, h2 = _outproj(x2, ya.reshape(t, ATTN_W), yc, yr, wo, gf, tm)
            x2 = _ffn_dense(h2, xo, ffn_w_gate[j], ffn_w_up[j], ffn_w_down[j], tm_ffn, fc)
        else:
            wr32 = jnp.pad(moe_router_w[j].astype(f32), ((0, 0), (0, LANES - N_EXPERTS)))
            wr_hi = _bf16(wr32)
            wr = jnp.concatenate([wr_hi, _bf16(wr32 - wr_hi.astype(f32))], axis=1)
            br = jnp.pad(moe_router_b[j].astype(f32), (0, LANES - N_EXPERTS)).reshape(1, LANES)
            xo, h2, info, counts = _outproj(x2, ya.reshape(t, ATTN_W), yc, yr, wo, gf, tm, router=(wr, br, tri_m))
            n_tiles = (2 * t) // tm_ffn + N_EXPERTS
            pos1, pos2, tile_expert, tile_active = _moe_routing_tables(info, counts, tm_ffn, n_tiles)
            xs = _dispatch(pos1, pos2, h2, jnp.zeros((n_tiles * tm_ffn, dm), f32), tb_moe)
            ys = _ffn_grouped(tile_expert, tile_active, xs,
                              moe_w_gate[j], moe_w_up[j], moe_w_down[j], tm_ffn, fc)
            x2 = _combine(pos1, pos2, ys, xo, info, tb_moe)
    return x2.reshape(batch, seq, dm)
```

```python
import functools
import math

import jax
import jax.numpy as jnp
import numpy as np
from jax import lax
from jax.experimental import pallas as pl
from jax.experimental.pallas import tpu as pltpu

HEAD_DIM = 64
ATTN_HEADS = 8
CONV_C = 256
HGRN_W = 256
ATTN_W = ATTN_HEADS * HEAD_DIM
HGRN_CHUNK = 64
HGRN_SUB = 16
N_EXPERTS = 8
EPS = 1e-6
MASK_VALUE = -1e30
MASK_LOG_DECAY = -1e4
TINY = 1e-30
LOG2E = math.log2(math.e)

LANES = 128
SUBLANES = 8
VMEM_LIMIT_BYTES = 56 * 1024 * 1024

PAD_HEAD = LANES
QP_W = ATTN_HEADS * PAD_HEAD
OFF_Q = 0
OFF_K = OFF_Q + QP_W
OFF_V = OFF_K + QP_W
OFF_CX = OFF_V + ATTN_W
OFF_CB = OFF_CX + CONV_C
OFF_CC = OFF_CB + CONV_C
OFF_RQ = OFF_CC + CONV_C
OFF_RF = OFF_RQ + HGRN_W
OFF_RI = OFF_RF + HGRN_W
OFF_RG = OFF_RI + HGRN_W
OFF_AF = OFF_RG + HGRN_W
D_PACK = OFF_AF + LANES
AUG0 = HEAD_DIM


def _bf16(x):
    return x.astype(jnp.bfloat16)


def _split3(x):
    p1 = _bf16(x)
    r1 = x - p1.astype(jnp.float32)
    p2 = _bf16(r1)
    r2 = r1 - p2.astype(jnp.float32)
    return p1, p2, _bf16(r2)


def _dot(a, b):
    return jnp.dot(a, b, preferred_element_type=jnp.float32)


def _group_sum(x, bd):
    hi = _bf16(x)
    lo = _bf16(x - hi.astype(jnp.float32))
    return _dot(hi, bd) + _dot(lo, bd)


def _silu(x):
    return x * (1.0 / (1.0 + jnp.exp(-x)))


def _sigmoid(x):
    return 1.0 / (1.0 + jnp.exp(-x))


def _inproj_kernel(x_ref, g_ref, w_ref, tri_ref, fb_ref, gq_ref, gk_ref, selq_ref, selk_ref,
                   oneq_ref, onek_ref, convw_ref, gc_ref, bd_ref, lb_ref,
                   qa_ref, ka_ref, v_ref, yc_ref, rq_ref, rlf_ref, rk_ref, rv_ref, rg_ref,
                   dcarry, ucarry, *, tiles_per_seq):
    i = pl.program_id(0)

    @pl.when(i % tiles_per_seq == 0)
    def _():
        dcarry[...] = jnp.zeros_like(dcarry)
        ucarry[...] = jnp.zeros_like(ucarry)

    x = x_ref[...]
    h = x * lax.rsqrt(jnp.mean(x * x, axis=-1, keepdims=True) + EPS) * g_ref[...]
    hb = _bf16(h)

    zf = _dot(hb, w_ref[:, OFF_AF:OFF_AF + LANES]) + fb_ref[...]
    ls = jnp.minimum(zf, 0.0) - jnp.log(1.0 + jnp.exp(-jnp.abs(zf)))
    p1, p2, p3 = _split3(ls)
    loc = _dot(tri_ref[...], jnp.concatenate([p1, p2, p3], axis=1))
    d = dcarry[...] + loc[:, 0:LANES] + loc[:, LANES:2 * LANES] + loc[:, 2 * LANES:3 * LANES]
    dcarry[...] = d[d.shape[0] - 1:, :]
    e1, e2, e3 = _split3(d * LOG2E)
    ecat = jnp.concatenate([e1, e2, e3], axis=1)
    dq_part = _dot(ecat, selq_ref[...])
    dk_part = _dot(ecat, selk_ref[...])

    zq = _dot(hb, w_ref[:, OFF_Q:OFF_Q + QP_W])
    zk = _dot(hb, w_ref[:, OFF_K:OFF_K + QP_W])
    for hd in range(ATTN_HEADS):
        sl = slice(hd * PAD_HEAD, (hd + 1) * PAD_HEAD)
        qb = zq[:, sl]
        qn = qb * lax.rsqrt(jnp.sum(qb * qb, axis=-1, keepdims=True) * (1.0 / HEAD_DIM) + EPS) * gq_ref[:, sl]
        qa_ref[:, sl] = _bf16(qn + dq_part[:, sl] + oneq_ref[:, sl])
        kb = zk[:, sl]
        kn = kb * lax.rsqrt(jnp.sum(kb * kb, axis=-1, keepdims=True) * (1.0 / HEAD_DIM) + EPS) * gk_ref[:, sl]
        ka_ref[:, sl] = _bf16(kn + dk_part[:, sl] + onek_ref[:, sl])
    v_ref[...] = _bf16(_dot(hb, w_ref[:, OFF_V:OFF_V + ATTN_W])).T

    cx = _dot(hb, w_ref[:, OFF_CX:OFF_CX + CONV_C])
    cb = _dot(hb, w_ref[:, OFF_CB:OFF_CB + CONV_C])
    cc = _dot(hb, w_ref[:, OFF_CC:OFF_CC + CONV_C])
    u = cc * cx
    uc = ucarry[...]
    row8 = lax.broadcasted_iota(jnp.int32, (SUBLANES, CONV_C), 0)
    r1 = pltpu.roll(u, 1, 0)
    r2 = pltpu.roll(u, 2, 0)
    top1 = jnp.where(row8 < 1, pltpu.roll(uc, 1, 0), r1[0:SUBLANES])
    top2 = jnp.where(row8 < 2, pltpu.roll(uc, 2, 0), r2[0:SUBLANES])
    u1 = jnp.concatenate([top1, r1[SUBLANES:]], axis=0)
    u2 = jnp.concatenate([top2, r2[SUBLANES:]], axis=0)
    ucarry[...] = u[u.shape[0] - SUBLANES:, :]
    yc = cb * (u2 * convw_ref[0:1, :] + u1 * convw_ref[1:2, :] + u * convw_ref[2:3, :])
    ssc = _group_sum(yc * yc, bd_ref[...])
    yc_ref[...] = _bf16(yc * lax.rsqrt(ssc * (1.0 / HEAD_DIM) + EPS) * gc_ref[...])

    lb = lb_ref[...]
    zr = _dot(hb, w_ref[:, OFF_RF:OFF_RF + HGRN_W])
    sg = _sigmoid(zr)
    f = lb + (1.0 - lb) * sg
    rlf_ref[...] = jnp.log(jnp.maximum(f, TINY))
    rk_ref[...] = (1.0 - lb) * _sigmoid(-zr)
    rq_ref[...] = _silu(_dot(hb, w_ref[:, OFF_RQ:OFF_RQ + HGRN_W]))
    rv_ref[...] = _dot(hb, w_ref[:, OFF_RI:OFF_RI + HGRN_W])
    rg_ref[...] = _silu(_dot(hb, w_ref[:, OFF_RG:OFF_RG + HGRN_W]))


def _inproj(x2, seq, g, wp, tri, fb, gq, gk, selq, selk, oneq, onek, convw, gc, bd, lb, tm):
    t, dm = x2.shape
    full = lambda a: pl.BlockSpec(a.shape, lambda i: (0,) * a.ndim)
    row = lambda w: pl.BlockSpec((tm, w), lambda i: (i, 0))
    consts = (g, wp, tri, fb, gq, gk, selq, selk, oneq, onek, convw, gc, bd, lb)
    tiles_per_seq = seq // tm
    out_shape = (
        jax.ShapeDtypeStruct((t, QP_W), jnp.bfloat16),
        jax.ShapeDtypeStruct((t, QP_W), jnp.bfloat16),
        jax.ShapeDtypeStruct((t // seq, ATTN_W, seq), jnp.bfloat16),
        jax.ShapeDtypeStruct((t, CONV_C), jnp.bfloat16),
    ) + tuple(jax.ShapeDtypeStruct((t, HGRN_W), jnp.float32) for _ in range(5))
    vt_spec = pl.BlockSpec((None, ATTN_W, tm), lambda i: (i // tiles_per_seq, 0, i % tiles_per_seq))
    out_specs = (row(QP_W), row(QP_W), vt_spec, row(CONV_C)) + tuple(row(HGRN_W) for _ in range(5))
    return pl.pallas_call(
        functools.partial(_inproj_kernel, tiles_per_seq=seq // tm),
        grid=(t // tm,),
        in_specs=[row(dm)] + [full(a) for a in consts],
        out_specs=out_specs,
        out_shape=out_shape,
        scratch_shapes=[pltpu.VMEM((1, LANES), jnp.float32), pltpu.VMEM((SUBLANES, CONV_C), jnp.float32)],
        compiler_params=pltpu.CompilerParams(dimension_semantics=("arbitrary",),
                                             vmem_limit_bytes=VMEM_LIMIT_BYTES),
        name="inproj",
    )(x2, *consts)


def _attn_kernel(qa_ref, ka_ref, vt_ref, gain_ref, out_ref, m_s, acc_s, s_a, s_b, *, tq):
    i = pl.program_id(2)
    ones_rows = jnp.ones((2 * SUBLANES, tq), jnp.bfloat16)
    m_s[...] = jnp.full_like(m_s, MASK_VALUE)
    acc_s[...] = jnp.zeros_like(acc_s)

    def scores(j, s_buf):
        k0 = pl.multiple_of(j * tq, tq)
        for hh in range(2):
            sl = slice(hh * PAD_HEAD, (hh + 1) * PAD_HEAD)
            s_buf[hh] = lax.dot_general(ka_ref[pl.ds(k0, tq), sl], qa_ref[:, sl],
                                        (((1,), (1,)), ((), ())), preferred_element_type=jnp.float32)

    def consume(j, s_buf, masked):
        k0 = pl.multiple_of(j * tq, tq)
        for hh in range(2):
            s = s_buf[hh]
            if masked:
                kpos = lax.broadcasted_iota(jnp.int32, (tq, tq), 0)
                qpos = lax.broadcasted_iota(jnp.int32, (tq, tq), 1)
                s = jnp.where(kpos <= qpos, s, MASK_VALUE)
            m = m_s[hh]
            m_new = jnp.maximum(m, jnp.max(s, axis=0, keepdims=True))
            p = jnp.exp2(_bf16(s - m_new))
            alpha = jnp.exp2(m - m_new)
            vaug = jnp.concatenate([vt_ref[hh, :, pl.ds(k0, tq)], ones_rows], axis=0)
            acc_s[hh] = acc_s[hh] * alpha + _dot(vaug, p)
            m_s[hh] = m_new

    scores(0, s_a)

    def pair(j):
        scores(j + 1, s_b)
        consume(j, s_a, False)
        scores(j + 2, s_a)
        consume(j + 1, s_b, False)

    def body(jj, _):
        pair(4 * jj)
        pair(4 * jj + 2)
        return 0

    lax.fori_loop(0, i // 4, body, 0)
    done = 4 * (i // 4)

    @pl.when(i - done >= 2)
    def _():
        pair(done)

    @pl.when(i % 2 == 1)
    def _():
        scores(i, s_b)
        consume(i - 1, s_a, False)
        consume(i, s_b, True)

    @pl.when(i % 2 == 0)
    def _():
        consume(i, s_a, True)

    ys = []
    for hh in range(2):
        acc = acc_s[hh]
        o = acc[0:HEAD_DIM] * (1.0 / acc[HEAD_DIM:HEAD_DIM + 1])
        ms = jnp.mean(o * o, axis=0, keepdims=True)
        ys.append(o * lax.rsqrt(ms + EPS) * gain_ref[hh])
    out_ref[...] = _bf16(jnp.concatenate(ys, axis=0).T)


def _attention(qa, ka, vt, gain, tq):
    b, s, _ = qa.shape
    return pl.pallas_call(
        functools.partial(_attn_kernel, tq=tq),
        grid=(b, ATTN_HEADS // 2, s // tq),
        in_specs=[
            pl.BlockSpec((None, tq, 2 * PAD_HEAD), lambda bi, hp, i: (bi, i, hp)),
            pl.BlockSpec((None, s, 2 * PAD_HEAD), lambda bi, hp, i: (bi, 0, hp)),
            pl.BlockSpec((None, 2, HEAD_DIM, s), lambda bi, hp, i: (bi, hp, 0, 0)),
            pl.BlockSpec((2, HEAD_DIM, tq), lambda bi, hp, i: (hp, 0, 0)),
        ],
        out_specs=pl.BlockSpec((None, tq, 2 * HEAD_DIM), lambda bi, hp, i: (bi, i, hp)),
        out_shape=jax.ShapeDtypeStruct((b, s, ATTN_W), jnp.bfloat16),
        scratch_shapes=[pltpu.VMEM((2, 1, tq), jnp.float32),
                        pltpu.VMEM((2, HEAD_DIM + 2 * SUBLANES, tq), jnp.float32),
                        pltpu.VMEM((2, tq, tq), jnp.float32),
                        pltpu.VMEM((2, tq, tq), jnp.float32)],
        compiler_params=pltpu.CompilerParams(dimension_semantics=("arbitrary", "arbitrary", "arbitrary"),
                                             vmem_limit_bytes=VMEM_LIMIT_BYTES),
        name="fox_attention",
    )(qa, ka, vt, gain)


def _hgrn_kernel(q_ref, lf_ref, k_ref, v_ref, g_ref, tri_ref, bd_ref, gain_ref, out_ref,
                 state, c_all, k_all, v_all, *, n_chunks):
    @pl.when(pl.program_id(1) == 0)
    def _():
        state[...] = jnp.zeros_like(state)

    ch = HGRN_CHUNK
    sb = HGRN_SUB
    nsb = ch // sb
    row_sb = lax.broadcasted_iota(jnp.int32, (sb, HGRN_W), 0)
    r128 = lax.broadcasted_iota(jnp.int32, (LANES, LANES), 0)
    c128 = lax.broadcasted_iota(jnp.int32, (LANES, LANES), 1)
    same_head = (r128 < HEAD_DIM) == (c128 < HEAD_DIM)
    t64 = lax.broadcasted_iota(jnp.int32, (ch, LANES), 0) // sb
    s64 = (lax.broadcasted_iota(jnp.int32, (ch, LANES), 1) % HEAD_DIM) // sb
    level2 = ((t64 == 1) & (s64 == 0)) | ((t64 == 3) & (s64 == 2))
    lane_head0 = lax.broadcasted_iota(jnp.int32, (ch, LANES), 1) < HEAD_DIM
    bd = bd_ref[...]
    zeros_sb = jnp.zeros((sb, HGRN_W), jnp.float32)

    def chunk(ci):
        r0 = ci * ch
        c_s, k_s, v_s = c_all.at[ci % 2], k_all.at[ci % 2], v_all.at[ci % 2]
        q = q_ref[pl.ds(r0, ch), :]
        k = k_ref[pl.ds(r0, ch), :]
        v = v_ref[pl.ds(r0, ch), :]
        p1, p2, p3 = _split3(lf_ref[pl.ds(r0, ch), :])
        cc = _dot(tri_ref[...], jnp.concatenate([p1, p2, p3], axis=1))
        c = cc[:, 0:HGRN_W] + cc[:, HGRN_W:2 * HGRN_W] + cc[:, 2 * HGRN_W:3 * HGRN_W]
        c_s[...] = c
        k_s[...] = k
        v_s[...] = v
        blk = lambda a, n: a[n * sb:(n + 1) * sb]

        ps = []
        for n in range(nsb):
            cn, qn = blk(c, n), blk(q, n)
            for s in range(sb):
                r = n * sb + s
                dec = jnp.exp(jnp.where(row_sb >= s, cn - c_s[r:r + 1, :], MASK_LOG_DECAY))
                ps.append(_bf16(qn * k_s[r:r + 1, :] * dec))
        a_d = _dot(jnp.concatenate(ps, axis=0), bd)
        o_parts = []
        for n in range(nsb):
            acc = jnp.zeros((sb, HGRN_W), jnp.float32)
            for s in range(sb):
                r = n * sb + s
                acc = acc + a_d[r * sb:(r + 1) * sb] * v_s[r:r + 1, :]
            o_parts.append(acc)
        o = jnp.concatenate(o_parts, axis=0)

        ref1 = c[2 * sb - 1:2 * sb]
        ref2a = c[sb - 1:sb]
        ref2b = c[3 * sb - 1:3 * sb]
        hi, lo = slice(2 * sb, 4 * sb), slice(0, 2 * sb)
        zeros_half = jnp.zeros((2 * sb, HGRN_W), jnp.float32)
        q1 = jnp.concatenate([zeros_half, q[hi] * jnp.exp(c[hi] - ref1)], axis=0)
        k1 = jnp.concatenate([k[lo] * jnp.exp(ref1 - c[lo]), zeros_half], axis=0)
        q2 = jnp.concatenate([zeros_sb, blk(q, 1) * jnp.exp(blk(c, 1) - ref2a),
                              zeros_sb, blk(q, 3) * jnp.exp(blk(c, 3) - ref2b)], axis=0)
        k2 = jnp.concatenate([blk(k, 0) * jnp.exp(ref2a - blk(c, 0)), zeros_sb,
                              blk(k, 2) * jnp.exp(ref2b - blk(c, 2)), zeros_sb], axis=0)

        c_last = c[ch - 1:ch, :]
        qe = q * jnp.exp(c)
        kd = k * jnp.exp(c_last - c)
        e_last = jnp.exp(c_last)
        nt = (((1,), (1,)), ((), ()))
        o_off = []
        for bb in range(HGRN_W // LANES):
            sl = slice(bb * LANES, (bb + 1) * LANES)
            st = state[bb]
            stack2 = lambda a: jnp.concatenate([jnp.where(lane_head0, a[:, sl], 0.0),
                                                jnp.where(lane_head0, 0.0, a[:, sl])], axis=0)
            a1 = lax.dot_general(_bf16(q1[:, sl]), _bf16(stack2(k1)), nt, preferred_element_type=jnp.float32)
            a2 = lax.dot_general(_bf16(q2[:, sl]), _bf16(stack2(k2)), nt, preferred_element_type=jnp.float32)
            a_off = a1 + jnp.where(level2, a2, 0.0)
            lhs = jnp.concatenate([_bf16(a_off), _bf16(qe[:, sl])], axis=1)
            rhs = jnp.concatenate([_bf16(stack2(v)), _bf16(st)], axis=0)
            o_off.append(_dot(lhs, rhs))
            ecol = jnp.broadcast_to(e_last[:, sl], (LANES, LANES)).T
            upd = lax.dot_general(_bf16(kd[:, sl]), _bf16(v[:, sl]), (((0,), (0,)), ((), ())),
                                  preferred_element_type=jnp.float32)
            state[bb] = jnp.where(same_head, ecol * st + upd, 0.0)
        o = o + jnp.concatenate(o_off, axis=1)

        ss = _group_sum(o * o, bd)
        y = o * lax.rsqrt(ss * (1.0 / HEAD_DIM) + EPS) * gain_ref[...] * g_ref[pl.ds(r0, ch), :]
        out_ref[pl.ds(r0, ch), :] = _bf16(y)

    for ci in range(n_chunks):
        chunk(ci)


def _hgrn(rq, rlf, rk, rv, rg, tri, bd, gain, batch, seq, rows):
    t = rq.shape[0]
    steps = seq // rows
    blk = pl.BlockSpec((rows, HGRN_W), lambda b, j: (b * steps + j, 0))
    full = lambda a: pl.BlockSpec(a.shape, lambda b, j: (0,) * a.ndim)
    return pl.pallas_call(
        functools.partial(_hgrn_kernel, n_chunks=rows // HGRN_CHUNK),
        grid=(batch, steps),
        in_specs=[blk, blk, blk, blk, blk, full(tri), full(bd), full(gain)],
        out_specs=blk,
        out_shape=jax.ShapeDtypeStruct((t, HGRN_W), jnp.bfloat16),
        scratch_shapes=[pltpu.VMEM((HGRN_W // LANES, LANES, LANES), jnp.float32)]
        + [pltpu.VMEM((2, HGRN_CHUNK, HGRN_W), jnp.float32) for _ in range(3)],
        compiler_params=pltpu.CompilerParams(dimension_semantics=("arbitrary", "arbitrary"),
                                             vmem_limit_bytes=VMEM_LIMIT_BYTES),
        name="hgrn2",
    )(rq, rlf, rk, rv, rg, tri, bd, gain)


INFO_E1, INFO_E2, INFO_R1, INFO_R2, INFO_W1, INFO_W2 = range(6)


def _mix_and_norm(x_ref, ya_ref, yc_ref, yr_ref, w_ref, g_ref):
    xn = (x_ref[...]
          + _dot(ya_ref[...], w_ref[0:ATTN_W, :])
          + _dot(yc_ref[...], w_ref[ATTN_W:ATTN_W + CONV_C, :])
          + _dot(yr_ref[...], w_ref[ATTN_W + CONV_C:, :]))
    h = xn * lax.rsqrt(jnp.mean(xn * xn, axis=-1, keepdims=True) + EPS) * g_ref[...]
    return xn, h


def _outproj_dense_kernel(x_ref, ya_ref, yc_ref, yr_ref, w_ref, g_ref, xo_ref, h_ref):
    xn, h = _mix_and_norm(x_ref, ya_ref, yc_ref, yr_ref, w_ref, g_ref)
    xo_ref[...] = xn
    h_ref[...] = _bf16(h)


def _outproj_routed_kernel(x_ref, ya_ref, yc_ref, yr_ref, w_ref, g_ref, wr_ref, br_ref, tri_ref,
                           xo_ref, h_ref, info_ref, cnt_ref, cnt_s):
    @pl.when(pl.program_id(0) == 0)
    def _():
        cnt_s[...] = jnp.zeros_like(cnt_s)

    xn, h = _mix_and_norm(x_ref, ya_ref, yc_ref, yr_ref, w_ref, g_ref)
    xo_ref[...] = xn
    h_ref[...] = h
    h_hi = _bf16(h)
    h_lo = _bf16(h - h_hi.astype(jnp.float32))
    hw = _dot(h_hi, wr_ref[...])
    logits = hw[:, 0:LANES] + hw[:, LANES:2 * LANES] + _dot(h_lo, wr_ref[:, 0:LANES]) + br_ref[...]
    lane = lax.broadcasted_iota(jnp.int32, logits.shape, 1)
    logits = jnp.where(lane < N_EXPERTS, logits, MASK_VALUE)
    m1 = jnp.max(logits, axis=-1, keepdims=True)
    i1 = jnp.min(jnp.where(logits == m1, lane, LANES), axis=-1, keepdims=True)
    rest = jnp.where(lane == i1, MASK_VALUE, logits)
    m2 = jnp.max(rest, axis=-1, keepdims=True)
    i2 = jnp.min(jnp.where(rest == m2, lane, LANES), axis=-1, keepdims=True)
    e2 = jnp.exp(m2 - m1)
    w1 = 1.0 / (1.0 + e2)
    w2 = e2 * w1
    hit = (lane == i1) | (lane == i2)
    onehot = jnp.where(hit, 1.0, 0.0)
    incl = _dot(tri_ref[...], _bf16(onehot))
    rank = cnt_s[...] + incl - onehot
    r1 = jnp.sum(jnp.where(lane == i1, rank, 0.0), axis=-1, keepdims=True)
    r2 = jnp.sum(jnp.where(lane == i2, rank, 0.0), axis=-1, keepdims=True)
    cnt_new = cnt_s[...] + incl[incl.shape[0] - 1:, :]
    cnt_s[...] = cnt_new
    cnt_ref[...] = jnp.broadcast_to(cnt_new, cnt_ref.shape)
    rec = jnp.zeros(logits.shape, jnp.float32)
    for ln, val in ((INFO_E1, i1.astype(jnp.float32)), (INFO_E2, i2.astype(jnp.float32)),
                    (INFO_R1, r1), (INFO_R2, r2), (INFO_W1, w1), (INFO_W2, w2)):
        rec = jnp.where(lane == ln, val, rec)
    info_ref[...] = rec


def _outproj(x2, ya, yc, yr, wo, g, tm, router=None):
    t, dm = x2.shape
    row = lambda w: pl.BlockSpec((tm, w), lambda i: (i, 0))
    full = lambda a: pl.BlockSpec(a.shape, lambda i: (0,) * a.ndim)
    params = pltpu.CompilerParams(dimension_semantics=("arbitrary",), vmem_limit_bytes=VMEM_LIMIT_BYTES)
    base_specs = [row(dm), row(ATTN_W), row(CONV_C), row(HGRN_W), full(wo), full(g)]
    if router is None:
        return pl.pallas_call(
            _outproj_dense_kernel,
            grid=(t // tm,),
            in_specs=base_specs,
            out_specs=(row(dm), row(dm)),
            out_shape=(jax.ShapeDtypeStruct((t, dm), jnp.float32), jax.ShapeDtypeStruct((t, dm), jnp.bfloat16)),
            compiler_params=params,
            name="outproj_dense",
        )(x2, ya, yc, yr, wo, g)
    wr, br, tri = router
    return pl.pallas_call(
        _outproj_routed_kernel,
        grid=(t // tm,),
        in_specs=base_specs + [full(wr), full(br), full(tri)],
        out_specs=(row(dm), row(dm), row(LANES), pl.BlockSpec((SUBLANES, LANES), lambda i: (0, 0))),
        out_shape=(jax.ShapeDtypeStruct((t, dm), jnp.float32), jax.ShapeDtypeStruct((t, dm), jnp.float32),
                   jax.ShapeDtypeStruct((t, LANES), jnp.float32),
                   jax.ShapeDtypeStruct((SUBLANES, LANES), jnp.float32)),
        scratch_shapes=[pltpu.VMEM((1, LANES), jnp.float32)],
        compiler_params=params,
        name="outproj_routed",
    )(x2, ya, yc, yr, wo, g, wr, br, tri)


def _swiglu_step(h_ref, wg_ref, wu_ref, wd_ref):
    h = _bf16(h_ref[...])
    gt = _dot(h, _bf16(wg_ref[...]))
    up = _dot(h, _bf16(wu_ref[...]))
    act = _bf16(gt * (1.0 / (1.0 + jnp.exp(-gt))) * up)
    return _dot(act, _bf16(wd_ref[...]))


def _ffn_dense_kernel(h_ref, x_ref, wg_ref, wu_ref, wd_ref, out_ref, acc):
    f = pl.program_id(1)

    @pl.when(f == 0)
    def _():
        acc[...] = x_ref[...]

    acc[...] += _swiglu_step(h_ref, wg_ref, wu_ref, wd_ref)

    @pl.when(f == pl.num_programs(1) - 1)
    def _():
        out_ref[...] = acc[...]


def _ffn_dense(h, x2, wg, wu, wd, tm, fc):
    t, dm = x2.shape
    dff = wg.shape[1]
    return pl.pallas_call(
        _ffn_dense_kernel,
        grid=(t // tm, dff // fc),
        in_specs=[
            pl.BlockSpec((tm, dm), lambda i, f: (i, 0)),
            pl.BlockSpec((tm, dm), lambda i, f: (i, 0)),
            pl.BlockSpec((dm, fc), lambda i, f: (0, f)),
            pl.BlockSpec((dm, fc), lambda i, f: (0, f)),
            pl.BlockSpec((fc, dm), lambda i, f: (f, 0)),
        ],
        out_specs=pl.BlockSpec((tm, dm), lambda i, f: (i, 0)),
        out_shape=jax.ShapeDtypeStruct((t, dm), jnp.float32),
        scratch_shapes=[pltpu.VMEM((tm, dm), jnp.float32)],
        compiler_params=pltpu.CompilerParams(dimension_semantics=("arbitrary", "arbitrary"),
                                             vmem_limit_bytes=VMEM_LIMIT_BYTES),
        name="swiglu_dense",
    )(h, x2, wg, wu, wd)


def _ffn_grouped_kernel(te_ref, ta_ref, h_ref, wg_ref, wu_ref, wd_ref, out_ref, acc):
    i = pl.program_id(0)
    f = pl.program_id(1)

    @pl.when(ta_ref[i] == 1)
    def _():
        @pl.when(f == 0)
        def _():
            acc[...] = jnp.zeros_like(acc)

        acc[...] += _swiglu_step(h_ref, wg_ref, wu_ref, wd_ref)

        @pl.when(f == pl.num_programs(1) - 1)
        def _():
            out_ref[...] = acc[...]

    @pl.when((ta_ref[i] == 0) & (f == pl.num_programs(1) - 1))
    def _():
        out_ref[...] = jnp.zeros_like(out_ref)


def _ffn_grouped(tile_expert, tile_active, xs, wg, wu, wd, tm, fc):
    dm = xs.shape[1]
    n_tiles = tile_expert.shape[0]
    r = n_tiles * tm
    dff = wg.shape[2]
    nf = dff // fc
    fidx = lambda i, f, ta: f * ta[i] + (nf - 1) * (1 - ta[i])
    grid_spec = pltpu.PrefetchScalarGridSpec(
        num_scalar_prefetch=2,
        grid=(n_tiles, nf),
        in_specs=[
            pl.BlockSpec((tm, dm), lambda i, f, te, ta: (i * ta[i], 0)),
            pl.BlockSpec((None, dm, fc), lambda i, f, te, ta: (te[i], 0, fidx(i, f, ta))),
            pl.BlockSpec((None, dm, fc), lambda i, f, te, ta: (te[i], 0, fidx(i, f, ta))),
            pl.BlockSpec((None, fc, dm), lambda i, f, te, ta: (te[i], fidx(i, f, ta), 0)),
        ],
        out_specs=pl.BlockSpec((tm, dm), lambda i, f, te, ta: (i, 0)),
        scratch_shapes=[pltpu.VMEM((tm, dm), jnp.float32)],
    )
    return pl.pallas_call(
        _ffn_grouped_kernel,
        grid_spec=grid_spec,
        out_shape=jax.ShapeDtypeStruct((r, dm), jnp.float32),
        compiler_params=pltpu.CompilerParams(dimension_semantics=("arbitrary", "arbitrary"),
                                             vmem_limit_bytes=VMEM_LIMIT_BYTES),
        name="swiglu_grouped",
    )(tile_expert, tile_active, xs, wg, wu, wd)


def _row_copy(src, dst, sem):
    return pltpu.make_async_copy(src, dst, sem)


MOE_ISSUE_UNROLL = 8


def _wait_rows(ref, n_rows, sem):
    blk = ref.at[pl.ds(0, n_rows), :]
    pltpu.make_async_copy(blk, blk, sem).wait()


MOE_STAGE_BUFS = 3


def _dispatch_kernel(p1_ref, p2_ref, pad_ref, h_ref, xs_ref, stage, zrow, load_sems, scat_sems, zero_sem,
                     *, tb, n_chunks):
    def load(c, slot):
        return pltpu.make_async_copy(h_ref.at[pl.ds(c * tb, tb), :], stage.at[slot], load_sems.at[slot])

    load(0, 0).start()

    zrow[...] = jnp.zeros_like(zrow)
    for e in range(N_EXPERTS):
        def zero(r, _, e=e):
            _row_copy(zrow.at[pl.ds(0, 1), :], xs_ref.at[pl.ds(pad_ref[e] + r, 1), :], zero_sem).start()
            return 0

        lax.fori_loop(0, pad_ref[N_EXPERTS + e], zero, 0)
    for e in range(N_EXPERTS):
        def zero_done(r, _):
            _wait_rows(xs_ref, 1, zero_sem)
            return 0

        lax.fori_loop(0, pad_ref[N_EXPERTS + e], zero_done, 0)

    def chunk(c, _):
        slot = lax.rem(c, MOE_STAGE_BUFS)
        load(c, slot).wait()

        @pl.when(c >= 2)
        def _():
            _wait_rows(xs_ref, 2 * tb, scat_sems.at[lax.rem(c - 2, MOE_STAGE_BUFS)])

        @pl.when(c + 1 < n_chunks)
        def _():
            load(c + 1, lax.rem(c + 1, MOE_STAGE_BUFS)).start()

        sem = scat_sems.at[slot]

        def issue(rr, _):
            for u in range(MOE_ISSUE_UNROLL):
                r = rr * MOE_ISSUE_UNROLL + u
                tok = c * tb + r
                src = stage.at[slot, pl.ds(r, 1), :]
                _row_copy(src, xs_ref.at[pl.ds(p1_ref[tok], 1), :], sem).start()
                _row_copy(src, xs_ref.at[pl.ds(p2_ref[tok], 1), :], sem).start()
            return 0

        lax.fori_loop(0, tb // MOE_ISSUE_UNROLL, issue, 0)
        return 0

    lax.fori_loop(0, n_chunks, chunk, 0)
    for c in range(max(n_chunks - 2, 0), n_chunks):
        _wait_rows(xs_ref, 2 * tb, scat_sems.at[c % MOE_STAGE_BUFS])


def _dispatch(pos1, pos2, pad_start, h, n_tiles, tb, tm):
    t, dm = h.shape
    n_chunks = t // tb
    grid_spec = pltpu.PrefetchScalarGridSpec(
        num_scalar_prefetch=3,
        grid=(1,),
        in_specs=[pl.BlockSpec(memory_space=pl.ANY)],
        out_specs=pl.BlockSpec(memory_space=pl.ANY),
        scratch_shapes=[pltpu.VMEM((MOE_STAGE_BUFS, tb, dm), h.dtype),
                        pltpu.VMEM((SUBLANES, dm), h.dtype),
                        pltpu.SemaphoreType.DMA((MOE_STAGE_BUFS,)),
                        pltpu.SemaphoreType.DMA((MOE_STAGE_BUFS,)),
                        pltpu.SemaphoreType.DMA(())],
    )
    return pl.pallas_call(
        functools.partial(_dispatch_kernel, tb=tb, n_chunks=n_chunks),
        grid_spec=grid_spec,
        out_shape=jax.ShapeDtypeStruct((n_tiles * tm, dm), h.dtype),
        compiler_params=pltpu.CompilerParams(dimension_semantics=("arbitrary",), has_side_effects=True),
        name="moe_dispatch",
    )(pos1, pos2, pad_start, h)


def _combine_kernel(p1_ref, p2_ref, y_ref, x_ref, info_ref, out_ref, buf_a, buf_b, sems, *, tb):
    i = pl.program_id(0)
    n = pl.num_programs(0)

    def issue_step(step, buf, sem):
        def issue(rr, _):
            for u in range(MOE_ISSUE_UNROLL):
                r = rr * MOE_ISSUE_UNROLL + u
                tok = step * tb + r
                _row_copy(y_ref.at[pl.ds(p1_ref[tok], 1), :], buf.at[0, pl.ds(r, 1), :], sem).start()
                _row_copy(y_ref.at[pl.ds(p2_ref[tok], 1), :], buf.at[1, pl.ds(r, 1), :], sem).start()
            return 0

        lax.fori_loop(0, tb // MOE_ISSUE_UNROLL, issue, 0)

    def finish(buf, sem):
        _wait_rows(y_ref, 2 * tb, sem)
        info = info_ref[...]
        lane = lax.broadcasted_iota(jnp.int32, info.shape, 1)
        w1 = jnp.sum(jnp.where(lane == INFO_W1, info, 0.0), axis=-1, keepdims=True)
        w2 = jnp.sum(jnp.where(lane == INFO_W2, info, 0.0), axis=-1, keepdims=True)
        out_ref[...] = x_ref[...] + w1 * buf[0] + w2 * buf[1]

    @pl.when(i == 0)
    def _():
        issue_step(0, buf_a, sems.at[0])

    @pl.when(i % 2 == 0)
    def _():
        @pl.when(i + 1 < n)
        def _():
            issue_step(i + 1, buf_b, sems.at[1])

        finish(buf_a, sems.at[0])

    @pl.when(i % 2 == 1)
    def _():
        @pl.when(i + 1 < n)
        def _():
            issue_step(i + 1, buf_a, sems.at[0])

        finish(buf_b, sems.at[1])


def _combine(pos1, pos2, y, x2, info, tb):
    t, dm = x2.shape
    grid_spec = pltpu.PrefetchScalarGridSpec(
        num_scalar_prefetch=2,
        grid=(t // tb,),
        in_specs=[pl.BlockSpec(memory_space=pl.ANY),
                  pl.BlockSpec((tb, dm), lambda i, p1, p2: (i, 0)),
                  pl.BlockSpec((tb, LANES), lambda i, p1, p2: (i, 0))],
        out_specs=pl.BlockSpec((tb, dm), lambda i, p1, p2: (i, 0)),
        scratch_shapes=[pltpu.VMEM((2, tb, dm), jnp.float32), pltpu.VMEM((2, tb, dm), jnp.float32),
                        pltpu.SemaphoreType.DMA((2,))],
    )
    return pl.pallas_call(
        functools.partial(_combine_kernel, tb=tb),
        grid_spec=grid_spec,
        out_shape=jax.ShapeDtypeStruct((t, dm), jnp.float32),
        compiler_params=pltpu.CompilerParams(dimension_semantics=("arbitrary",)),
        name="moe_combine",
    )(pos1, pos2, y, x2, info)


def _moe_routing_tables(info, counts, tm, n_tiles):
    cnt = counts[0, :N_EXPERTS].astype(jnp.int32)
    padded = ((cnt + tm - 1) // tm) * tm
    ends = jnp.cumsum(padded)
    offsets = ends - padded
    e1 = info[:, INFO_E1].astype(jnp.int32)
    e2 = info[:, INFO_E2].astype(jnp.int32)
    pos1 = offsets[e1] + info[:, INFO_R1].astype(jnp.int32)
    pos2 = offsets[e2] + info[:, INFO_R2].astype(jnp.int32)
    start = jnp.arange(n_tiles, dtype=jnp.int32) * tm
    tile_expert = jnp.minimum(jnp.sum((start[:, None] >= ends[None, :]).astype(jnp.int32), axis=1), N_EXPERTS - 1)
    tile_active = (start < ends[-1]).astype(jnp.int32)
    pad_rows = jnp.concatenate([offsets + cnt, padded - cnt])
    return pos1, pos2, pad_rows, tile_expert, tile_active


def _pad_heads(w):
    d = w.shape[0]
    w = w.reshape(d, ATTN_HEADS, HEAD_DIM)
    return jnp.pad(w, ((0, 0), (0, 0), (0, PAD_HEAD - HEAD_DIM))).reshape(d, QP_W)


def _pack_w_in(w):
    s = np.cumsum([0, ATTN_W, ATTN_W, ATTN_W, ATTN_HEADS, CONV_C, CONV_C, CONV_C, HGRN_W, HGRN_W, HGRN_W, HGRN_W])
    seg = [w[:, s[n]:s[n + 1]] for n in range(11)]
    a_q, a_k, a_v, a_f, c_x, c_b, c_c, r_q, r_f, r_i, r_g = seg
    a_f = jnp.pad(a_f, ((0, 0), (0, LANES - ATTN_HEADS)))
    return _bf16(jnp.concatenate([_pad_heads(a_q), _pad_heads(a_k), a_v, c_x, c_b, c_c, r_q, r_f, r_i, r_g, a_f],
                                 axis=1))


def _selection_constants():
    selq = np.zeros((3 * LANES, QP_W), np.float32)
    selk = np.zeros((3 * LANES, QP_W), np.float32)
    oneq = np.zeros((1, QP_W), np.float32)
    onek = np.zeros((1, QP_W), np.float32)
    for hd in range(ATTN_HEADS):
        base = hd * PAD_HEAD + AUG0
        for piece in range(3):
            selq[piece * LANES + hd, base + piece] = 1.0
            selk[piece * LANES + hd, base + 3 + piece] = -1.0
            oneq[0, base + 3 + piece] = 1.0
            onek[0, base + piece] = 1.0
    return (jnp.asarray(selq, jnp.bfloat16), jnp.asarray(selk, jnp.bfloat16),
            jnp.asarray(oneq), jnp.asarray(onek))


def _pad_gain(gain, mult):
    g = jnp.pad(gain.astype(jnp.float32) * mult, (0, PAD_HEAD - HEAD_DIM))
    return jnp.tile(g, ATTN_HEADS).reshape(1, QP_W)


def kernel(x, norm_mix, w_in, attn_f_bias, q_norm_gain, k_norm_gain, conv_w, hgrn_lb_logits, mix_out_gain, w_out,
           norm_ffn, ffn_w_gate, ffn_w_up, ffn_w_down, moe_router_w, moe_router_b, moe_w_gate, moe_w_up, moe_w_down):
    batch, seq, dm = x.shape
    depth = w_in.shape[0]
    t = batch * seq
    f32 = jnp.float32
    tm = min(512, seq)
    tq = min(512, seq)
    hg_rows = min(512, seq)
    tm_ffn = min(1024, t)
    tb_moe = min(256, t)
    dff = ffn_w_gate.shape[-1]
    fc = dff // 4

    p_lb = jax.nn.softmax(hgrn_lb_logits.astype(f32), axis=0)
    lb_all = jnp.cumsum(p_lb, axis=0) - p_lb[0]

    tri_m = _bf16(jnp.tril(jnp.ones((tm, tm), f32)))
    tri_c = _bf16(jnp.tril(jnp.ones((HGRN_CHUNK, HGRN_CHUNK), f32)))
    grp = np.arange(HGRN_W) // HEAD_DIM
    bd = jnp.asarray(grp[:, None] == grp[None, :], jnp.bfloat16)
    selq, selk, oneq, onek = _selection_constants()
    scale = 1.0 / math.sqrt(HEAD_DIM)

    x2 = x.reshape(t, dm)
    for l in range(depth):
        wp = _pack_w_in(w_in[l])
        fb = jnp.pad(attn_f_bias[l].astype(f32), (0, LANES - ATTN_HEADS)).reshape(1, LANES)
        gq = _pad_gain(q_norm_gain[l], scale * LOG2E)
        gk = _pad_gain(k_norm_gain[l], 1.0)
        mog = mix_out_gain[l].astype(f32)
        ga = jnp.broadcast_to(mog[:ATTN_W].reshape(ATTN_HEADS, HEAD_DIM, 1), (ATTN_HEADS, HEAD_DIM, tq))
        gc = mog[ATTN_W:ATTN_W + CONV_C].reshape(1, CONV_C)
        gr = mog[ATTN_W + CONV_C:].reshape(1, HGRN_W)

        qa, ka, v, yc, rq, rlf, rk, rv, rg = _inproj(
            x2, seq, norm_mix[l].astype(f32).reshape(1, dm), wp, tri_m, fb, gq, gk, selq, selk, oneq, onek,
            conv_w[l].astype(f32), gc, bd, lb_all[l].reshape(1, HGRN_W), tm)

        vt = v.reshape(batch, ATTN_HEADS, HEAD_DIM, seq)
        ya = _attention(qa.reshape(batch, seq, QP_W), ka.reshape(batch, seq, QP_W), vt, ga, tq)
        yr = _hgrn(rq, rlf, rk, rv, rg, tri_c, bd, gr, batch, seq, hg_rows)

        j = l // 2
        wo = _bf16(w_out[l])
        gf = norm_ffn[l].astype(f32).reshape(1, dm)
        if l % 2 == 0:
            xo, h2 = _outproj(x2, ya.reshape(t, ATTN_W), yc, yr, wo, gf, tm)
            x2 = _ffn_dense(h2, xo, ffn_w_gate[j], ffn_w_up[j], ffn_w_down[j], tm_ffn, fc)
        else:
            wr32 = jnp.pad(moe_router_w[j].astype(f32), ((0, 0), (0, LANES - N_EXPERTS)))
            wr_hi = _bf16(wr32)
            wr = jnp.concatenate([wr_hi, _bf16(wr32 - wr_hi.astype(f32))], axis=1)
            br = jnp.pad(moe_router_b[j].astype(f32), (0, LANES - N_EXPERTS)).reshape(1, LANES)
            xo, h2, info, counts = _outproj(x2, ya.reshape(t, ATTN_W), yc, yr, wo, gf, tm, router=(wr, br, tri_m))
            n_tiles = (2 * t) // tm_ffn + N_EXPERTS
            pos1, pos2, pad_start, tile_expert, tile_active = _moe_routing_tables(info, counts, tm_ffn, n_tiles)
            xs = _dispatch(pos1, pos2, pad_start, h2, n_tiles, tb_moe, tm_ffn)
            ys = _ffn_grouped(tile_expert, tile_active, xs,
                              moe_w_gate[j], moe_w_up[j], moe_w_down[j], tm_ffn, fc)
            x2 = _combine(pos1, pos2, ys, xo, info, tb_moe)
    return x2.reshape(batch, seq, dm)
```

```python
import functools
import math

import jax
import jax.numpy as jnp
import numpy as np
from jax import lax
from jax.experimental import pallas as pl
from jax.experimental.pallas import tpu as pltpu

HEAD_DIM = 64
ATTN_HEADS = 8
CONV_C = 256
HGRN_W = 256
ATTN_W = ATTN_HEADS * HEAD_DIM
HGRN_CHUNK = 64
HGRN_SUB = 16
N_EXPERTS = 8
EPS = 1e-6
MASK_VALUE = -1e30
MASK_LOG_DECAY = -1e4
TINY = 1e-30
LOG2E = math.log2(math.e)

LANES = 128
SUBLANES = 8
VMEM_LIMIT_BYTES = 56 * 1024 * 1024

PAD_HEAD = LANES
QP_W = ATTN_HEADS * PAD_HEAD
OFF_Q = 0
OFF_K = OFF_Q + QP_W
OFF_V = OFF_K + QP_W
OFF_CX = OFF_V + ATTN_W
OFF_CB = OFF_CX + CONV_C
OFF_CC = OFF_CB + CONV_C
OFF_RQ = OFF_CC + CONV_C
OFF_RF = OFF_RQ + HGRN_W
OFF_RI = OFF_RF + HGRN_W
OFF_RG = OFF_RI + HGRN_W
OFF_AF = OFF_RG + HGRN_W
D_PACK = OFF_AF + LANES
AUG0 = HEAD_DIM


def _bf16(x):
    return x.astype(jnp.bfloat16)


def _split3(x):
    p1 = _bf16(x)
    r1 = x - p1.astype(jnp.float32)
    p2 = _bf16(r1)
    r2 = r1 - p2.astype(jnp.float32)
    return p1, p2, _bf16(r2)


def _dot(a, b):
    return jnp.dot(a, b, preferred_element_type=jnp.float32)


def _group_sum(x, bd):
    hi = _bf16(x)
    lo = _bf16(x - hi.astype(jnp.float32))
    return _dot(hi, bd) + _dot(lo, bd)


def _silu(x):
    return x * (1.0 / (1.0 + jnp.exp(-x)))


def _sigmoid(x):
    return 1.0 / (1.0 + jnp.exp(-x))


def _inproj_kernel(x_ref, g_ref, w_ref, tri_ref, fb_ref, gq_ref, gk_ref, selq_ref, selk_ref,
                   oneq_ref, onek_ref, convw_ref, gc_ref, bd_ref, lb_ref,
                   qa_ref, ka_ref, v_ref, yc_ref, rq_ref, rlf_ref, rk_ref, rv_ref, rg_ref,
                   dcarry, ucarry, *, tiles_per_seq):
    i = pl.program_id(0)

    @pl.when(i % tiles_per_seq == 0)
    def _():
        dcarry[...] = jnp.zeros_like(dcarry)
        ucarry[...] = jnp.zeros_like(ucarry)

    x = x_ref[...]
    h = x * lax.rsqrt(jnp.mean(x * x, axis=-1, keepdims=True) + EPS) * g_ref[...]
    hb = _bf16(h)

    zf = _dot(hb, w_ref[:, OFF_AF:OFF_AF + LANES]) + fb_ref[...]
    ls = jnp.minimum(zf, 0.0) - jnp.log(1.0 + jnp.exp(-jnp.abs(zf)))
    p1, p2, p3 = _split3(ls)
    loc = _dot(tri_ref[...], jnp.concatenate([p1, p2, p3], axis=1))
    d = dcarry[...] + loc[:, 0:LANES] + loc[:, LANES:2 * LANES] + loc[:, 2 * LANES:3 * LANES]
    dcarry[...] = d[d.shape[0] - 1:, :]
    e1, e2, e3 = (e.astype(jnp.float32) for e in _split3(d * LOG2E))
    lane = lax.broadcasted_iota(jnp.int32, e1.shape, 1)
    epack = jnp.where(lane < ATTN_HEADS, e1,
                      jnp.where(lane < 2 * ATTN_HEADS, pltpu.roll(e2, ATTN_HEADS, 1),
                                jnp.where(lane < 3 * ATTN_HEADS, pltpu.roll(e3, 2 * ATTN_HEADS, 1), 0.0)))
    epack = _bf16(epack)
    dq_part = _dot(epack, selq_ref[...])
    dk_part = _dot(epack, selk_ref[...])

    zq = _dot(hb, w_ref[:, OFF_Q:OFF_Q + QP_W])
    zk = _dot(hb, w_ref[:, OFF_K:OFF_K + QP_W])
    for hd in range(ATTN_HEADS):
        sl = slice(hd * PAD_HEAD, (hd + 1) * PAD_HEAD)
        qb = zq[:, sl]
        qn = qb * lax.rsqrt(jnp.sum(qb * qb, axis=-1, keepdims=True) * (1.0 / HEAD_DIM) + EPS) * gq_ref[:, sl]
        qa_ref[:, sl] = _bf16(qn + dq_part[:, sl] + oneq_ref[:, sl])
        kb = zk[:, sl]
        kn = kb * lax.rsqrt(jnp.sum(kb * kb, axis=-1, keepdims=True) * (1.0 / HEAD_DIM) + EPS) * gk_ref[:, sl]
        ka_ref[:, sl] = _bf16(kn + dk_part[:, sl] + onek_ref[:, sl])
    v_ref[...] = _bf16(_dot(hb, w_ref[:, OFF_V:OFF_V + ATTN_W])).T

    cx = _dot(hb, w_ref[:, OFF_CX:OFF_CX + CONV_C])
    cb = _dot(hb, w_ref[:, OFF_CB:OFF_CB + CONV_C])
    cc = _dot(hb, w_ref[:, OFF_CC:OFF_CC + CONV_C])
    u = cc * cx
    uc = ucarry[...]
    row8 = lax.broadcasted_iota(jnp.int32, (SUBLANES, CONV_C), 0)
    r1 = pltpu.roll(u, 1, 0)
    r2 = pltpu.roll(u, 2, 0)
    top1 = jnp.where(row8 < 1, pltpu.roll(uc, 1, 0), r1[0:SUBLANES])
    top2 = jnp.where(row8 < 2, pltpu.roll(uc, 2, 0), r2[0:SUBLANES])
    u1 = jnp.concatenate([top1, r1[SUBLANES:]], axis=0)
    u2 = jnp.concatenate([top2, r2[SUBLANES:]], axis=0)
    ucarry[...] = u[u.shape[0] - SUBLANES:, :]
    yc = cb * (u2 * convw_ref[0:1, :] + u1 * convw_ref[1:2, :] + u * convw_ref[2:3, :])
    ssc = _group_sum(yc * yc, bd_ref[...])
    yc_ref[...] = _bf16(yc * lax.rsqrt(ssc * (1.0 / HEAD_DIM) + EPS) * gc_ref[...])

    lb = lb_ref[...]
    zr = _dot(hb, w_ref[:, OFF_RF:OFF_RF + HGRN_W])
    sg = _sigmoid(zr)
    f = lb + (1.0 - lb) * sg
    rlf_ref[...] = jnp.log(jnp.maximum(f, TINY))
    rk_ref[...] = (1.0 - lb) * _sigmoid(-zr)
    rq_ref[...] = _silu(_dot(hb, w_ref[:, OFF_RQ:OFF_RQ + HGRN_W]))
    rv_ref[...] = _dot(hb, w_ref[:, OFF_RI:OFF_RI + HGRN_W])
    rg_ref[...] = _silu(_dot(hb, w_ref[:, OFF_RG:OFF_RG + HGRN_W]))


def _inproj(x2, seq, g, wp, tri, fb, gq, gk, selq, selk, oneq, onek, convw, gc, bd, lb, tm):
    t, dm = x2.shape
    full = lambda a: pl.BlockSpec(a.shape, lambda i: (0,) * a.ndim)
    row = lambda w: pl.BlockSpec((tm, w), lambda i: (i, 0))
    consts = (g, wp, tri, fb, gq, gk, selq, selk, oneq, onek, convw, gc, bd, lb)
    tiles_per_seq = seq // tm
    out_shape = (
        jax.ShapeDtypeStruct((t, QP_W), jnp.bfloat16),
        jax.ShapeDtypeStruct((t, QP_W), jnp.bfloat16),
        jax.ShapeDtypeStruct((t // seq, ATTN_W, seq), jnp.bfloat16),
        jax.ShapeDtypeStruct((t, CONV_C), jnp.bfloat16),
    ) + tuple(jax.ShapeDtypeStruct((t, HGRN_W), jnp.float32) for _ in range(5))
    vt_spec = pl.BlockSpec((None, ATTN_W, tm), lambda i: (i // tiles_per_seq, 0, i % tiles_per_seq))
    out_specs = (row(QP_W), row(QP_W), vt_spec, row(CONV_C)) + tuple(row(HGRN_W) for _ in range(5))
    return pl.pallas_call(
        functools.partial(_inproj_kernel, tiles_per_seq=seq // tm),
        grid=(t // tm,),
        in_specs=[row(dm)] + [full(a) for a in consts],
        out_specs=out_specs,
        out_shape=out_shape,
        scratch_shapes=[pltpu.VMEM((1, LANES), jnp.float32), pltpu.VMEM((SUBLANES, CONV_C), jnp.float32)],
        compiler_params=pltpu.CompilerParams(dimension_semantics=("arbitrary",),
                                             vmem_limit_bytes=VMEM_LIMIT_BYTES),
        name="inproj",
    )(x2, *consts)


def _attn_kernel(qa_ref, ka_ref, vt_ref, gain_ref, out_ref, m_s, acc_s, s_a, s_b, qt_s, *, tq):
    i = pl.program_id(2)
    ones_rows = jnp.ones((2 * SUBLANES, tq), jnp.bfloat16)
    m_s[...] = jnp.full_like(m_s, MASK_VALUE)
    acc_s[...] = jnp.zeros_like(acc_s)

    for hh in range(2):
        qt_s[hh] = qa_ref[:, hh * PAD_HEAD:(hh + 1) * PAD_HEAD].astype(jnp.float32).T.astype(jnp.bfloat16)

    def scores(j, s_buf):
        k0 = pl.multiple_of(j * tq, tq)
        for hh in range(2):
            sl = slice(hh * PAD_HEAD, (hh + 1) * PAD_HEAD)
            s_buf[hh] = _dot(ka_ref[pl.ds(k0, tq), sl], qt_s[hh])

    def consume(j, s_buf, masked):
        k0 = pl.multiple_of(j * tq, tq)
        for hh in range(2):
            s = s_buf[hh]
            if masked:
                kpos = lax.broadcasted_iota(jnp.int32, (tq, tq), 0)
                qpos = lax.broadcasted_iota(jnp.int32, (tq, tq), 1)
                s = jnp.where(kpos <= qpos, s, MASK_VALUE)
            m = m_s[hh]
            m_new = jnp.maximum(m, jnp.max(s, axis=0, keepdims=True))
            p = jnp.exp2(s - m_new)
            alpha = jnp.exp2(m - m_new)
            vaug = jnp.concatenate([vt_ref[hh, :, pl.ds(k0, tq)], ones_rows], axis=0)
            acc_s[hh] = acc_s[hh] * alpha + _dot(vaug, _bf16(p))
            m_s[hh] = m_new

    scores(0, s_a)

    def pair(j):
        scores(j + 1, s_b)
        consume(j, s_a, False)
        scores(j + 2, s_a)
        consume(j + 1, s_b, False)

    def body(jj, _):
        pair(4 * jj)
        pair(4 * jj + 2)
        return 0

    lax.fori_loop(0, i // 4, body, 0)
    done = 4 * (i // 4)

    @pl.when(i - done >= 2)
    def _():
        pair(done)

    @pl.when(i % 2 == 1)
    def _():
        scores(i, s_b)
        consume(i - 1, s_a, False)
        consume(i, s_b, True)

    @pl.when(i % 2 == 0)
    def _():
        consume(i, s_a, True)

    ys = []
    for hh in range(2):
        acc = acc_s[hh]
        o = acc[0:HEAD_DIM] * (1.0 / acc[HEAD_DIM:HEAD_DIM + 1])
        ms = jnp.mean(o * o, axis=0, keepdims=True)
        ys.append(o * lax.rsqrt(ms + EPS) * gain_ref[hh])
    out_ref[...] = _bf16(jnp.concatenate(ys, axis=0).T)


def _attention(qa, ka, vt, gain, tq):
    b, s, _ = qa.shape
    return pl.pallas_call(
        functools.partial(_attn_kernel, tq=tq),
        grid=(b, ATTN_HEADS // 2, s // tq),
        in_specs=[
            pl.BlockSpec((None, tq, 2 * PAD_HEAD), lambda bi, hp, i: (bi, i, hp)),
            pl.BlockSpec((None, s, 2 * PAD_HEAD), lambda bi, hp, i: (bi, 0, hp)),
            pl.BlockSpec((None, 2, HEAD_DIM, s), lambda bi, hp, i: (bi, hp, 0, 0)),
            pl.BlockSpec((2, HEAD_DIM, tq), lambda bi, hp, i: (hp, 0, 0)),
        ],
        out_specs=pl.BlockSpec((None, tq, 2 * HEAD_DIM), lambda bi, hp, i: (bi, i, hp)),
        out_shape=jax.ShapeDtypeStruct((b, s, ATTN_W), jnp.bfloat16),
        scratch_shapes=[pltpu.VMEM((2, 1, tq), jnp.float32),
                        pltpu.VMEM((2, HEAD_DIM + 2 * SUBLANES, tq), jnp.float32),
                        pltpu.VMEM((2, tq, tq), jnp.float32),
                        pltpu.VMEM((2, tq, tq), jnp.float32),
                        pltpu.VMEM((2, PAD_HEAD, tq), jnp.bfloat16)],
        compiler_params=pltpu.CompilerParams(dimension_semantics=("arbitrary", "arbitrary", "arbitrary"),
                                             vmem_limit_bytes=VMEM_LIMIT_BYTES),
        name="fox_attention",
    )(qa, ka, vt, gain)


def _hgrn_kernel(q_ref, lf_ref, k_ref, v_ref, g_ref, tri_ref, bd_ref, gain_ref, out_ref,
                 state, c_all, k_all, v_all, *, n_chunks):
    @pl.when(pl.program_id(1) == 0)
    def _():
        state[...] = jnp.zeros_like(state)

    ch = HGRN_CHUNK
    sb = HGRN_SUB
    nsb = ch // sb
    row_sb = lax.broadcasted_iota(jnp.int32, (sb, HGRN_W), 0)
    r128 = lax.broadcasted_iota(jnp.int32, (LANES, LANES), 0)
    c128 = lax.broadcasted_iota(jnp.int32, (LANES, LANES), 1)
    same_head = (r128 < HEAD_DIM) == (c128 < HEAD_DIM)
    t64 = lax.broadcasted_iota(jnp.int32, (ch, LANES), 0) // sb
    s64 = (lax.broadcasted_iota(jnp.int32, (ch, LANES), 1) % HEAD_DIM) // sb
    level2 = ((t64 == 1) & (s64 == 0)) | ((t64 == 3) & (s64 == 2))
    lane_head0 = lax.broadcasted_iota(jnp.int32, (ch, LANES), 1) < HEAD_DIM
    bd = bd_ref[...]
    zeros_sb = jnp.zeros((sb, HGRN_W), jnp.float32)

    def chunk(ci):
        r0 = ci * ch
        c_s, k_s, v_s = c_all.at[ci % 2], k_all.at[ci % 2], v_all.at[ci % 2]
        q = q_ref[pl.ds(r0, ch), :]
        k = k_ref[pl.ds(r0, ch), :]
        v = v_ref[pl.ds(r0, ch), :]
        p1, p2, p3 = _split3(lf_ref[pl.ds(r0, ch), :])
        cc = _dot(tri_ref[...], jnp.concatenate([p1, p2, p3], axis=1))
        c = cc[:, 0:HGRN_W] + cc[:, HGRN_W:2 * HGRN_W] + cc[:, 2 * HGRN_W:3 * HGRN_W]
        c_s[...] = c
        k_s[...] = k
        v_s[...] = v
        blk = lambda a, n: a[n * sb:(n + 1) * sb]

        ps = []
        for n in range(nsb):
            cn, qn = blk(c, n), blk(q, n)
            for s in range(sb):
                r = n * sb + s
                dec = jnp.exp(jnp.where(row_sb >= s, cn - c_s[r:r + 1, :], MASK_LOG_DECAY))
                ps.append(_bf16(qn * k_s[r:r + 1, :] * dec))
        a_d = _dot(jnp.concatenate(ps, axis=0), bd)
        o_parts = []
        for n in range(nsb):
            acc = jnp.zeros((sb, HGRN_W), jnp.float32)
            for s in range(sb):
                r = n * sb + s
                acc = acc + a_d[r * sb:(r + 1) * sb] * v_s[r:r + 1, :]
            o_parts.append(acc)
        o = jnp.concatenate(o_parts, axis=0)

        ref1 = c[2 * sb - 1:2 * sb]
        ref2a = c[sb - 1:sb]
        ref2b = c[3 * sb - 1:3 * sb]
        hi, lo = slice(2 * sb, 4 * sb), slice(0, 2 * sb)
        zeros_half = jnp.zeros((2 * sb, HGRN_W), jnp.float32)
        q1 = jnp.concatenate([zeros_half, q[hi] * jnp.exp(c[hi] - ref1)], axis=0)
        k1 = jnp.concatenate([k[lo] * jnp.exp(ref1 - c[lo]), zeros_half], axis=0)
        q2 = jnp.concatenate([zeros_sb, blk(q, 1) * jnp.exp(blk(c, 1) - ref2a),
                              zeros_sb, blk(q, 3) * jnp.exp(blk(c, 3) - ref2b)], axis=0)
        k2 = jnp.concatenate([blk(k, 0) * jnp.exp(ref2a - blk(c, 0)), zeros_sb,
                              blk(k, 2) * jnp.exp(ref2b - blk(c, 2)), zeros_sb], axis=0)

        c_last = c[ch - 1:ch, :]
        qe = q * jnp.exp(c)
        kd = k * jnp.exp(c_last - c)
        e_last = jnp.exp(c_last)
        nt = (((1,), (1,)), ((), ()))
        o_off = []
        for bb in range(HGRN_W // LANES):
            sl = slice(bb * LANES, (bb + 1) * LANES)
            st = state[bb]
            stack2 = lambda a: jnp.concatenate([jnp.where(lane_head0, a[:, sl], 0.0),
                                                jnp.where(lane_head0, 0.0, a[:, sl])], axis=0)
            a1 = lax.dot_general(_bf16(q1[:, sl]), _bf16(stack2(k1)), nt, preferred_element_type=jnp.float32)
            a2 = lax.dot_general(_bf16(q2[:, sl]), _bf16(stack2(k2)), nt, preferred_element_type=jnp.float32)
            a_off = a1 + jnp.where(level2, a2, 0.0)
            lhs = jnp.concatenate([_bf16(a_off), _bf16(qe[:, sl])], axis=1)
            rhs = jnp.concatenate([_bf16(stack2(v)), _bf16(st)], axis=0)
            o_off.append(_dot(lhs, rhs))
            ecol = jnp.broadcast_to(e_last[:, sl], (LANES, LANES)).T
            upd = lax.dot_general(_bf16(kd[:, sl]), _bf16(v[:, sl]), (((0,), (0,)), ((), ())),
                                  preferred_element_type=jnp.float32)
            state[bb] = jnp.where(same_head, ecol * st + upd, 0.0)
        o = o + jnp.concatenate(o_off, axis=1)

        ss = _group_sum(o * o, bd)
        y = o * lax.rsqrt(ss * (1.0 / HEAD_DIM) + EPS) * gain_ref[...] * g_ref[pl.ds(r0, ch), :]
        out_ref[pl.ds(r0, ch), :] = _bf16(y)

    for ci in range(n_chunks):
        chunk(ci)


def _hgrn(rq, rlf, rk, rv, rg, tri, bd, gain, batch, seq, rows):
    t = rq.shape[0]
    steps = seq // rows
    blk = pl.BlockSpec((rows, HGRN_W), lambda b, j: (b * steps + j, 0))
    full = lambda a: pl.BlockSpec(a.shape, lambda b, j: (0,) * a.ndim)
    return pl.pallas_call(
        functools.partial(_hgrn_kernel, n_chunks=rows // HGRN_CHUNK),
        grid=(batch, steps),
        in_specs=[blk, blk, blk, blk, blk, full(tri), full(bd), full(gain)],
        out_specs=blk,
        out_shape=jax.ShapeDtypeStruct((t, HGRN_W), jnp.bfloat16),
        scratch_shapes=[pltpu.VMEM((HGRN_W // LANES, LANES, LANES), jnp.float32)]
        + [pltpu.VMEM((2, HGRN_CHUNK, HGRN_W), jnp.float32) for _ in range(3)],
        compiler_params=pltpu.CompilerParams(dimension_semantics=("arbitrary", "arbitrary"),
                                             vmem_limit_bytes=VMEM_LIMIT_BYTES),
        name="hgrn2",
    )(rq, rlf, rk, rv, rg, tri, bd, gain)


INFO_E1, INFO_E2, INFO_R1, INFO_R2, INFO_W1, INFO_W2 = range(6)


def _mix_and_norm(x_ref, ya_ref, yc_ref, yr_ref, w_ref, g_ref):
    xn = (x_ref[...]
          + _dot(ya_ref[...], w_ref[0:ATTN_W, :])
          + _dot(yc_ref[...], w_ref[ATTN_W:ATTN_W + CONV_C, :])
          + _dot(yr_ref[...], w_ref[ATTN_W + CONV_C:, :]))
    h = xn * lax.rsqrt(jnp.mean(xn * xn, axis=-1, keepdims=True) + EPS) * g_ref[...]
    return xn, h


def _outproj_dense_kernel(x_ref, ya_ref, yc_ref, yr_ref, w_ref, g_ref, xo_ref, h_ref):
    xn, h = _mix_and_norm(x_ref, ya_ref, yc_ref, yr_ref, w_ref, g_ref)
    xo_ref[...] = xn
    h_ref[...] = _bf16(h)


def _outproj_routed_kernel(x_ref, ya_ref, yc_ref, yr_ref, w_ref, g_ref, wr_ref, br_ref, tri_ref,
                           xo_ref, h_ref, info_ref, cnt_ref, cnt_s):
    @pl.when(pl.program_id(0) == 0)
    def _():
        cnt_s[...] = jnp.zeros_like(cnt_s)

    xn, h = _mix_and_norm(x_ref, ya_ref, yc_ref, yr_ref, w_ref, g_ref)
    xo_ref[...] = xn
    h_ref[...] = h
    h_hi = _bf16(h)
    h_lo = _bf16(h - h_hi.astype(jnp.float32))
    hw = _dot(h_hi, wr_ref[...])
    logits = hw[:, 0:LANES] + hw[:, LANES:2 * LANES] + _dot(h_lo, wr_ref[:, 0:LANES]) + br_ref[...]
    lane = lax.broadcasted_iota(jnp.int32, logits.shape, 1)
    logits = jnp.where(lane < N_EXPERTS, logits, MASK_VALUE)
    m1 = jnp.max(logits, axis=-1, keepdims=True)
    i1 = jnp.min(jnp.where(logits == m1, lane, LANES), axis=-1, keepdims=True)
    rest = jnp.where(lane == i1, MASK_VALUE, logits)
    m2 = jnp.max(rest, axis=-1, keepdims=True)
    i2 = jnp.min(jnp.where(rest == m2, lane, LANES), axis=-1, keepdims=True)
    e2 = jnp.exp(m2 - m1)
    w1 = 1.0 / (1.0 + e2)
    w2 = e2 * w1
    hit = (lane == i1) | (lane == i2)
    onehot = jnp.where(hit, 1.0, 0.0)
    incl = _dot(tri_ref[...], _bf16(onehot))
    rank = cnt_s[...] + incl - onehot
    r1 = jnp.sum(jnp.where(lane == i1, rank, 0.0), axis=-1, keepdims=True)
    r2 = jnp.sum(jnp.where(lane == i2, rank, 0.0), axis=-1, keepdims=True)
    cnt_new = cnt_s[...] + incl[incl.shape[0] - 1:, :]
    cnt_s[...] = cnt_new
    cnt_ref[...] = jnp.broadcast_to(cnt_new, cnt_ref.shape)
    rec = jnp.zeros(logits.shape, jnp.float32)
    for ln, val in ((INFO_E1, i1.astype(jnp.float32)), (INFO_E2, i2.astype(jnp.float32)),
                    (INFO_R1, r1), (INFO_R2, r2), (INFO_W1, w1), (INFO_W2, w2)):
        rec = jnp.where(lane == ln, val, rec)
    info_ref[...] = rec


def _outproj(x2, ya, yc, yr, wo, g, tm, router=None):
    t, dm = x2.shape
    row = lambda w: pl.BlockSpec((tm, w), lambda i: (i, 0))
    full = lambda a: pl.BlockSpec(a.shape, lambda i: (0,) * a.ndim)
    params = pltpu.CompilerParams(dimension_semantics=("arbitrary",), vmem_limit_bytes=VMEM_LIMIT_BYTES)
    base_specs = [row(dm), row(ATTN_W), row(CONV_C), row(HGRN_W), full(wo), full(g)]
    if router is None:
        return pl.pallas_call(
            _outproj_dense_kernel,
            grid=(t // tm,),
            in_specs=base_specs,
            out_specs=(row(dm), row(dm)),
            out_shape=(jax.ShapeDtypeStruct((t, dm), jnp.float32), jax.ShapeDtypeStruct((t, dm), jnp.bfloat16)),
            compiler_params=params,
            name="outproj_dense",
        )(x2, ya, yc, yr, wo, g)
    wr, br, tri = router
    return pl.pallas_call(
        _outproj_routed_kernel,
        grid=(t // tm,),
        in_specs=base_specs + [full(wr), full(br), full(tri)],
        out_specs=(row(dm), row(dm), row(LANES), pl.BlockSpec((SUBLANES, LANES), lambda i: (0, 0))),
        out_shape=(jax.ShapeDtypeStruct((t, dm), jnp.float32), jax.ShapeDtypeStruct((t, dm), jnp.float32),
                   jax.ShapeDtypeStruct((t, LANES), jnp.float32),
                   jax.ShapeDtypeStruct((SUBLANES, LANES), jnp.float32)),
        scratch_shapes=[pltpu.VMEM((1, LANES), jnp.float32)],
        compiler_params=params,
        name="outproj_routed",
    )(x2, ya, yc, yr, wo, g, wr, br, tri)


def _swiglu_step(h_ref, wg_ref, wu_ref, wd_ref):
    h = _bf16(h_ref[...])
    gt = _dot(h, _bf16(wg_ref[...]))
    up = _dot(h, _bf16(wu_ref[...]))
    act = _bf16(gt * (1.0 / (1.0 + jnp.exp(-gt))) * up)
    return _dot(act, _bf16(wd_ref[...]))


def _ffn_dense_kernel(h_ref, x_ref, wg_ref, wu_ref, wd_ref, out_ref, acc):
    f = pl.program_id(1)

    @pl.when(f == 0)
    def _():
        acc[...] = x_ref[...]

    acc[...] += _swiglu_step(h_ref, wg_ref, wu_ref, wd_ref)

    @pl.when(f == pl.num_programs(1) - 1)
    def _():
        out_ref[...] = acc[...]


def _ffn_dense(h, x2, wg, wu, wd, tm, fc):
    t, dm = x2.shape
    dff = wg.shape[1]
    return pl.pallas_call(
        _ffn_dense_kernel,
        grid=(t // tm, dff // fc),
        in_specs=[
            pl.BlockSpec((tm, dm), lambda i, f: (i, 0)),
            pl.BlockSpec((tm, dm), lambda i, f: (i, 0)),
            pl.BlockSpec((dm, fc), lambda i, f: (0, f)),
            pl.BlockSpec((dm, fc), lambda i, f: (0, f)),
            pl.BlockSpec((fc, dm), lambda i, f: (f, 0)),
        ],
        out_specs=pl.BlockSpec((tm, dm), lambda i, f: (i, 0)),
        out_shape=jax.ShapeDtypeStruct((t, dm), jnp.float32),
        scratch_shapes=[pltpu.VMEM((tm, dm), jnp.float32)],
        compiler_params=pltpu.CompilerParams(dimension_semantics=("arbitrary", "arbitrary"),
                                             vmem_limit_bytes=VMEM_LIMIT_BYTES),
        name="swiglu_dense",
    )(h, x2, wg, wu, wd)


def _ffn_grouped_kernel(te_ref, ta_ref, h_ref, wg_ref, wu_ref, wd_ref, out_ref, acc):
    i = pl.program_id(0)
    f = pl.program_id(1)

    @pl.when(ta_ref[i] == 1)
    def _():
        @pl.when(f == 0)
        def _():
            acc[...] = jnp.zeros_like(acc)

        acc[...] += _swiglu_step(h_ref, wg_ref, wu_ref, wd_ref)

        @pl.when(f == pl.num_programs(1) - 1)
        def _():
            out_ref[...] = acc[...]

    @pl.when((ta_ref[i] == 0) & (f == pl.num_programs(1) - 1))
    def _():
        out_ref[...] = jnp.zeros_like(out_ref)


def _ffn_grouped(tile_expert, tile_active, xs, wg, wu, wd, tm, fc):
    dm = xs.shape[1]
    n_tiles = tile_expert.shape[0]
    r = n_tiles * tm
    dff = wg.shape[2]
    nf = dff // fc
    fidx = lambda i, f, ta: f * ta[i] + (nf - 1) * (1 - ta[i])
    grid_spec = pltpu.PrefetchScalarGridSpec(
        num_scalar_prefetch=2,
        grid=(n_tiles, nf),
        in_specs=[
            pl.BlockSpec((tm, dm), lambda i, f, te, ta: (i * ta[i], 0)),
            pl.BlockSpec((None, dm, fc), lambda i, f, te, ta: (te[i], 0, fidx(i, f, ta))),
            pl.BlockSpec((None, dm, fc), lambda i, f, te, ta: (te[i], 0, fidx(i, f, ta))),
            pl.BlockSpec((None, fc, dm), lambda i, f, te, ta: (te[i], fidx(i, f, ta), 0)),
        ],
        out_specs=pl.BlockSpec((tm, dm), lambda i, f, te, ta: (i, 0)),
        scratch_shapes=[pltpu.VMEM((tm, dm), jnp.float32)],
    )
    return pl.pallas_call(
        _ffn_grouped_kernel,
        grid_spec=grid_spec,
        out_shape=jax.ShapeDtypeStruct((r, dm), jnp.float32),
        compiler_params=pltpu.CompilerParams(dimension_semantics=("arbitrary", "arbitrary"),
                                             vmem_limit_bytes=VMEM_LIMIT_BYTES),
        name="swiglu_grouped",
    )(tile_expert, tile_active, xs, wg, wu, wd)


def _row_copy(src, dst, sem):
    return pltpu.make_async_copy(src, dst, sem)


MOE_ISSUE_UNROLL = 8


def _wait_rows(ref, n_rows, sem):
    blk = ref.at[pl.ds(0, n_rows), :]
    pltpu.make_async_copy(blk, blk, sem).wait()


MOE_STAGE_BUFS = 3


def _dispatch_kernel(p1_ref, p2_ref, pad_ref, h_ref, xs_ref, stage, zrow, load_sems, scat_sems, zero_sem,
                     *, tb, n_chunks):
    def load(c, slot):
        return pltpu.make_async_copy(h_ref.at[pl.ds(c * tb, tb), :], stage.at[slot], load_sems.at[slot])

    load(0, 0).start()

    zrow[...] = jnp.zeros_like(zrow)
    for e in range(N_EXPERTS):
        def zero(r, _, e=e):
            _row_copy(zrow.at[pl.ds(0, 1), :], xs_ref.at[pl.ds(pad_ref[e] + r, 1), :], zero_sem).start()
            return 0

        lax.fori_loop(0, pad_ref[N_EXPERTS + e], zero, 0)
    for e in range(N_EXPERTS):
        def zero_done(r, _):
            _wait_rows(xs_ref, 1, zero_sem)
            return 0

        lax.fori_loop(0, pad_ref[N_EXPERTS + e], zero_done, 0)

    def chunk(c, _):
        slot = lax.rem(c, MOE_STAGE_BUFS)
        load(c, slot).wait()

        @pl.when(c >= 2)
        def _():
            _wait_rows(xs_ref, 2 * tb, scat_sems.at[lax.rem(c - 2, MOE_STAGE_BUFS)])

        @pl.when(c + 1 < n_chunks)
        def _():
            load(c + 1, lax.rem(c + 1, MOE_STAGE_BUFS)).start()

        sem = scat_sems.at[slot]

        def issue(rr, _):
            for u in range(MOE_ISSUE_UNROLL):
                r = rr * MOE_ISSUE_UNROLL + u
                tok = c * tb + r
                src = stage.at[slot, pl.ds(r, 1), :]
                _row_copy(src, xs_ref.at[pl.ds(p1_ref[tok], 1), :], sem).start()
                _row_copy(src, xs_ref.at[pl.ds(p2_ref[tok], 1), :], sem).start()
            return 0

        lax.fori_loop(0, tb // MOE_ISSUE_UNROLL, issue, 0)
        return 0

    lax.fori_loop(0, n_chunks, chunk, 0)
    for c in range(max(n_chunks - 2, 0), n_chunks):
        _wait_rows(xs_ref, 2 * tb, scat_sems.at[c % MOE_STAGE_BUFS])


def _dispatch(pos1, pos2, pad_start, h, n_tiles, tb, tm):
    t, dm = h.shape
    n_chunks = t // tb
    grid_spec = pltpu.PrefetchScalarGridSpec(
        num_scalar_prefetch=3,
        grid=(1,),
        in_specs=[pl.BlockSpec(memory_space=pl.ANY)],
        out_specs=pl.BlockSpec(memory_space=pl.ANY),
        scratch_shapes=[pltpu.VMEM((MOE_STAGE_BUFS, tb, dm), h.dtype),
                        pltpu.VMEM((SUBLANES, dm), h.dtype),
                        pltpu.SemaphoreType.DMA((MOE_STAGE_BUFS,)),
                        pltpu.SemaphoreType.DMA((MOE_STAGE_BUFS,)),
                        pltpu.SemaphoreType.DMA(())],
    )
    return pl.pallas_call(
        functools.partial(_dispatch_kernel, tb=tb, n_chunks=n_chunks),
        grid_spec=grid_spec,
        out_shape=jax.ShapeDtypeStruct((n_tiles * tm, dm), h.dtype),
        compiler_params=pltpu.CompilerParams(dimension_semantics=("arbitrary",), has_side_effects=True),
        name="moe_dispatch",
    )(pos1, pos2, pad_start, h)


def _combine_kernel(p1_ref, p2_ref, y_ref, x_ref, info_ref, out_ref, buf_a, buf_b, sems, *, tb):
    i = pl.program_id(0)
    n = pl.num_programs(0)

    def issue_step(step, buf, sem):
        def issue(rr, _):
            for u in range(MOE_ISSUE_UNROLL):
                r = rr * MOE_ISSUE_UNROLL + u
                tok = step * tb + r
                _row_copy(y_ref.at[pl.ds(p1_ref[tok], 1), :], buf.at[0, pl.ds(r, 1), :], sem).start()
                _row_copy(y_ref.at[pl.ds(p2_ref[tok], 1), :], buf.at[1, pl.ds(r, 1), :], sem).start()
            return 0

        lax.fori_loop(0, tb // MOE_ISSUE_UNROLL, issue, 0)

    def finish(buf, sem):
        _wait_rows(y_ref, 2 * tb, sem)
        info = info_ref[...]
        lane = lax.broadcasted_iota(jnp.int32, info.shape, 1)
        w1 = jnp.sum(jnp.where(lane == INFO_W1, info, 0.0), axis=-1, keepdims=True)
        w2 = jnp.sum(jnp.where(lane == INFO_W2, info, 0.0), axis=-1, keepdims=True)
        out_ref[...] = x_ref[...] + w1 * buf[0] + w2 * buf[1]

    @pl.when(i == 0)
    def _():
        issue_step(0, buf_a, sems.at[0])

    @pl.when(i % 2 == 0)
    def _():
        @pl.when(i + 1 < n)
        def _():
            issue_step(i + 1, buf_b, sems.at[1])

        finish(buf_a, sems.at[0])

    @pl.when(i % 2 == 1)
    def _():
        @pl.when(i + 1 < n)
        def _():
            issue_step(i + 1, buf_a, sems.at[0])

        finish(buf_b, sems.at[1])


def _combine(pos1, pos2, y, x2, info, tb):
    t, dm = x2.shape
    grid_spec = pltpu.PrefetchScalarGridSpec(
        num_scalar_prefetch=2,
        grid=(t // tb,),
        in_specs=[pl.BlockSpec(memory_space=pl.ANY),
                  pl.BlockSpec((tb, dm), lambda i, p1, p2: (i, 0)),
                  pl.BlockSpec((tb, LANES), lambda i, p1, p2: (i, 0))],
        out_specs=pl.BlockSpec((tb, dm), lambda i, p1, p2: (i, 0)),
        scratch_shapes=[pltpu.VMEM((2, tb, dm), jnp.float32), pltpu.VMEM((2, tb, dm), jnp.float32),
                        pltpu.SemaphoreType.DMA((2,))],
    )
    return pl.pallas_call(
        functools.partial(_combine_kernel, tb=tb),
        grid_spec=grid_spec,
        out_shape=jax.ShapeDtypeStruct((t, dm), jnp.float32),
        compiler_params=pltpu.CompilerParams(dimension_semantics=("arbitrary",)),
        name="moe_combine",
    )(pos1, pos2, y, x2, info)


def _moe_routing_tables(info, counts, tm, n_tiles):
    cnt = counts[0, :N_EXPERTS].astype(jnp.int32)
    padded = ((cnt + tm - 1) // tm) * tm
    ends = jnp.cumsum(padded)
    offsets = ends - padded
    e1 = info[:, INFO_E1].astype(jnp.int32)
    e2 = info[:, INFO_E2].astype(jnp.int32)
    pos1 = offsets[e1] + info[:, INFO_R1].astype(jnp.int32)
    pos2 = offsets[e2] + info[:, INFO_R2].astype(jnp.int32)
    start = jnp.arange(n_tiles, dtype=jnp.int32) * tm
    tile_expert = jnp.minimum(jnp.sum((start[:, None] >= ends[None, :]).astype(jnp.int32), axis=1), N_EXPERTS - 1)
    tile_active = (start < ends[-1]).astype(jnp.int32)
    pad_rows = jnp.concatenate([offsets + cnt, padded - cnt])
    return pos1, pos2, pad_rows, tile_expert, tile_active


def _pad_heads(w):
    d = w.shape[0]
    w = w.reshape(d, ATTN_HEADS, HEAD_DIM)
    return jnp.pad(w, ((0, 0), (0, 0), (0, PAD_HEAD - HEAD_DIM))).reshape(d, QP_W)


def _pack_w_in(w):
    s = np.cumsum([0, ATTN_W, ATTN_W, ATTN_W, ATTN_HEADS, CONV_C, CONV_C, CONV_C, HGRN_W, HGRN_W, HGRN_W, HGRN_W])
    seg = [w[:, s[n]:s[n + 1]] for n in range(11)]
    a_q, a_k, a_v, a_f, c_x, c_b, c_c, r_q, r_f, r_i, r_g = seg
    a_f = jnp.pad(a_f, ((0, 0), (0, LANES - ATTN_HEADS)))
    return _bf16(jnp.concatenate([_pad_heads(a_q), _pad_heads(a_k), a_v, c_x, c_b, c_c, r_q, r_f, r_i, r_g, a_f],
                                 axis=1))


def _selection_constants():
    selq = np.zeros((LANES, QP_W), np.float32)
    selk = np.zeros((LANES, QP_W), np.float32)
    oneq = np.zeros((1, QP_W), np.float32)
    onek = np.zeros((1, QP_W), np.float32)
    for hd in range(ATTN_HEADS):
        base = hd * PAD_HEAD + AUG0
        for piece in range(3):
            selq[piece * ATTN_HEADS + hd, base + piece] = 1.0
            selk[piece * ATTN_HEADS + hd, base + 3 + piece] = -1.0
            oneq[0, base + 3 + piece] = 1.0
            onek[0, base + piece] = 1.0
    return (jnp.asarray(selq, jnp.bfloat16), jnp.asarray(selk, jnp.bfloat16),
            jnp.asarray(oneq), jnp.asarray(onek))


def _pad_gain(gain, mult):
    g = jnp.pad(gain.astype(jnp.float32) * mult, (0, PAD_HEAD - HEAD_DIM))
    return jnp.tile(g, ATTN_HEADS).reshape(1, QP_W)


def kernel(x, norm_mix, w_in, attn_f_bias, q_norm_gain, k_norm_gain, conv_w, hgrn_lb_logits, mix_out_gain, w_out,
           norm_ffn, ffn_w_gate, ffn_w_up, ffn_w_down, moe_router_w, moe_router_b, moe_w_gate, moe_w_up, moe_w_down):
    batch, seq, dm = x.shape
    depth = w_in.shape[0]
    t = batch * seq
    f32 = jnp.float32
    tm = min(512, seq)
    tq = min(512, seq)
    hg_rows = min(512, seq)
    tm_ffn = min(1024, t)
    tb_moe = min(256, t)
    dff = ffn_w_gate.shape[-1]
    fc = dff // 4

    p_lb = jax.nn.softmax(hgrn_lb_logits.astype(f32), axis=0)
    lb_all = jnp.cumsum(p_lb, axis=0) - p_lb[0]

    tri_m = _bf16(jnp.tril(jnp.ones((tm, tm), f32)))
    tri_c = _bf16(jnp.tril(jnp.ones((HGRN_CHUNK, HGRN_CHUNK), f32)))
    grp = np.arange(HGRN_W) // HEAD_DIM
    bd = jnp.asarray(grp[:, None] == grp[None, :], jnp.bfloat16)
    selq, selk, oneq, onek = _selection_constants()
    scale = 1.0 / math.sqrt(HEAD_DIM)

    x2 = x.reshape(t, dm)
    for l in range(depth):
        wp = _pack_w_in(w_in[l])
        fb = jnp.pad(attn_f_bias[l].astype(f32), (0, LANES - ATTN_HEADS)).reshape(1, LANES)
        gq = _pad_gain(q_norm_gain[l], scale * LOG2E)
        gk = _pad_gain(k_norm_gain[l], 1.0)
        mog = mix_out_gain[l].astype(f32)
        ga = jnp.broadcast_to(mog[:ATTN_W].reshape(ATTN_HEADS, HEAD_DIM, 1), (ATTN_HEADS, HEAD_DIM, tq))
        gc = mog[ATTN_W:ATTN_W + CONV_C].reshape(1, CONV_C)
        gr = mog[ATTN_W + CONV_C:].reshape(1, HGRN_W)

        qa, ka, v, yc, rq, rlf, rk, rv, rg = _inproj(
            x2, seq, norm_mix[l].astype(f32).reshape(1, dm), wp, tri_m, fb, gq, gk, selq, selk, oneq, onek,
            conv_w[l].astype(f32), gc, bd, lb_all[l].reshape(1, HGRN_W), tm)

        vt = v.reshape(batch, ATTN_HEADS, HEAD_DIM, seq)
        ya = _attention(qa.reshape(batch, seq, QP_W), ka.reshape(batch, seq, QP_W), vt, ga, tq)
        yr = _hgrn(rq, rlf, rk, rv, rg, tri_c, bd, gr, batch, seq, hg_rows)

        j = l // 2
        wo = _bf16(w_out[l])
        gf = norm_ffn[l].astype(f32).reshape(1, dm)
        if l % 2 == 0:
            xo, h2 = _outproj(x2, ya.reshape(t, ATTN_W), yc, yr, wo, gf, tm)
            x2 = _ffn_dense(h2, xo, ffn_w_gate[j], ffn_w_up[j], ffn_w_down[j], tm_ffn, fc)
        else:
            wr32 = jnp.pad(moe_router_w[j].astype(f32), ((0, 0), (0, LANES - N_EXPERTS)))
            wr_hi = _bf16(wr32)
            wr = jnp.concatenate([wr_hi, _bf16(wr32 - wr_hi.astype(f32))], axis=1)
            br = jnp.pad(moe_router_b[j].astype(f32), (0, LANES - N_EXPERTS)).reshape(1, LANES)
            xo, h2, info, counts = _outproj(x2, ya.reshape(t, ATTN_W), yc, yr, wo, gf, tm, router=(wr, br, tri_m))
            n_tiles = (2 * t) // tm_ffn + N_EXPERTS
            pos1, pos2, pad_start, tile_expert, tile_active = _moe_routing_tables(info, counts, tm_ffn, n_tiles)
            xs = _dispatch(pos1, pos2, pad_start, h2, n_tiles, tb_moe, tm_ffn)
            ys = _ffn_grouped(tile_expert, tile_active, xs,
                              moe_w_gate[j], moe_w_up[j], moe_w_down[j], tm_ffn, fc)
            x2 = _combine(pos1, pos2, ys, xo, info, tb_moe)
    return x2.reshape(batch, seq, dm)
```

```python
import functools
import math

import jax
import jax.numpy as jnp
import numpy as np
from jax import lax
from jax.experimental import pallas as pl
from jax.experimental.pallas import tpu as pltpu

HEAD_DIM = 64
ATTN_HEADS = 8
CONV_C = 256
HGRN_W = 256
ATTN_W = ATTN_HEADS * HEAD_DIM
HGRN_CHUNK = 64
HGRN_SUB = 16
N_EXPERTS = 8
EPS = 1e-6
MASK_VALUE = -1e30
MASK_LOG_DECAY = -1e4
TINY = 1e-30
LOG2E = math.log2(math.e)

LANES = 128
SUBLANES = 8
VMEM_LIMIT_BYTES = 56 * 1024 * 1024

PAD_HEAD = LANES
QP_W = ATTN_HEADS * PAD_HEAD
OFF_Q = 0
OFF_K = OFF_Q + QP_W
OFF_V = OFF_K + QP_W
OFF_CX = OFF_V + ATTN_W
OFF_CB = OFF_CX + CONV_C
OFF_CC = OFF_CB + CONV_C
OFF_RQ = OFF_CC + CONV_C
OFF_RF = OFF_RQ + HGRN_W
OFF_RI = OFF_RF + HGRN_W
OFF_RG = OFF_RI + HGRN_W
OFF_AF = OFF_RG + HGRN_W
D_PACK = OFF_AF + LANES
AUG0 = HEAD_DIM


def _bf16(x):
    return x.astype(jnp.bfloat16)


def _split3(x):
    p1 = _bf16(x)
    r1 = x - p1.astype(jnp.float32)
    p2 = _bf16(r1)
    r2 = r1 - p2.astype(jnp.float32)
    return p1, p2, _bf16(r2)


def _dot(a, b):
    return jnp.dot(a, b, preferred_element_type=jnp.float32)


def _group_sum(x, bd):
    hi = _bf16(x)
    lo = _bf16(x - hi.astype(jnp.float32))
    return _dot(hi, bd) + _dot(lo, bd)


def _silu(x):
    return x * (1.0 / (1.0 + jnp.exp(-x)))


def _sigmoid(x):
    return 1.0 / (1.0 + jnp.exp(-x))


def _inproj_kernel(x_ref, g_ref, w_ref, tri_ref, fb_ref, gq_ref, gk_ref, selq_ref, selk_ref,
                   oneq_ref, onek_ref, convw_ref, gc_ref, bd_ref, lb_ref,
                   qa_ref, ka_ref, v_ref, yc_ref, rq_ref, rlf_ref, rk_ref, rv_ref, rg_ref,
                   dcarry, ucarry, *, tiles_per_seq):
    i = pl.program_id(0)

    @pl.when(i % tiles_per_seq == 0)
    def _():
        dcarry[...] = jnp.zeros_like(dcarry)
        ucarry[...] = jnp.zeros_like(ucarry)

    x = x_ref[...]
    h = x * lax.rsqrt(jnp.mean(x * x, axis=-1, keepdims=True) + EPS) * g_ref[...]
    hb = _bf16(h)

    zf = _dot(hb, w_ref[:, OFF_AF:OFF_AF + LANES]) + fb_ref[...]
    ls = jnp.minimum(zf, 0.0) - jnp.log(1.0 + jnp.exp(-jnp.abs(zf)))
    p1, p2, p3 = _split3(ls)
    loc = _dot(tri_ref[...], jnp.concatenate([p1, p2, p3], axis=1))
    d = dcarry[...] + loc[:, 0:LANES] + loc[:, LANES:2 * LANES] + loc[:, 2 * LANES:3 * LANES]
    dcarry[...] = d[d.shape[0] - 1:, :]
    e1, e2, e3 = (e.astype(jnp.float32) for e in _split3(d * LOG2E))
    lane = lax.broadcasted_iota(jnp.int32, e1.shape, 1)
    epack = jnp.where(lane < ATTN_HEADS, e1,
                      jnp.where(lane < 2 * ATTN_HEADS, pltpu.roll(e2, ATTN_HEADS, 1),
                                jnp.where(lane < 3 * ATTN_HEADS, pltpu.roll(e3, 2 * ATTN_HEADS, 1), 0.0)))
    epack = _bf16(epack)
    dq_part = _dot(epack, selq_ref[...])
    dk_part = _dot(epack, selk_ref[...])

    zq = _dot(hb, w_ref[:, OFF_Q:OFF_Q + QP_W])
    zk = _dot(hb, w_ref[:, OFF_K:OFF_K + QP_W])
    for hd in range(ATTN_HEADS):
        sl = slice(hd * PAD_HEAD, (hd + 1) * PAD_HEAD)
        qb = zq[:, sl]
        qn = qb * lax.rsqrt(jnp.sum(qb * qb, axis=-1, keepdims=True) * (1.0 / HEAD_DIM) + EPS) * gq_ref[:, sl]
        qa_ref[:, sl] = _bf16(qn + dq_part[:, sl] + oneq_ref[:, sl])
        kb = zk[:, sl]
        kn = kb * lax.rsqrt(jnp.sum(kb * kb, axis=-1, keepdims=True) * (1.0 / HEAD_DIM) + EPS) * gk_ref[:, sl]
        ka_ref[:, sl] = _bf16(kn + dk_part[:, sl] + onek_ref[:, sl])
    v_ref[...] = _bf16(_dot(hb, w_ref[:, OFF_V:OFF_V + ATTN_W])).T

    cx = _dot(hb, w_ref[:, OFF_CX:OFF_CX + CONV_C])
    cb = _dot(hb, w_ref[:, OFF_CB:OFF_CB + CONV_C])
    cc = _dot(hb, w_ref[:, OFF_CC:OFF_CC + CONV_C])
    u = cc * cx
    uc = ucarry[...]
    row8 = lax.broadcasted_iota(jnp.int32, (SUBLANES, CONV_C), 0)
    r1 = pltpu.roll(u, 1, 0)
    r2 = pltpu.roll(u, 2, 0)
    top1 = jnp.where(row8 < 1, pltpu.roll(uc, 1, 0), r1[0:SUBLANES])
    top2 = jnp.where(row8 < 2, pltpu.roll(uc, 2, 0), r2[0:SUBLANES])
    u1 = jnp.concatenate([top1, r1[SUBLANES:]], axis=0)
    u2 = jnp.concatenate([top2, r2[SUBLANES:]], axis=0)
    ucarry[...] = u[u.shape[0] - SUBLANES:, :]
    yc = cb * (u2 * convw_ref[0:1, :] + u1 * convw_ref[1:2, :] + u * convw_ref[2:3, :])
    ssc = _group_sum(yc * yc, bd_ref[...])
    yc_ref[...] = _bf16(yc * lax.rsqrt(ssc * (1.0 / HEAD_DIM) + EPS) * gc_ref[...])

    lb = lb_ref[...]
    zr = _dot(hb, w_ref[:, OFF_RF:OFF_RF + HGRN_W])
    sg = _sigmoid(zr)
    f = lb + (1.0 - lb) * sg
    rlf_ref[...] = jnp.log(jnp.maximum(f, TINY))
    rk_ref[...] = (1.0 - lb) * _sigmoid(-zr)
    rq_ref[...] = _silu(_dot(hb, w_ref[:, OFF_RQ:OFF_RQ + HGRN_W]))
    rv_ref[...] = _dot(hb, w_ref[:, OFF_RI:OFF_RI + HGRN_W])
    rg_ref[...] = _silu(_dot(hb, w_ref[:, OFF_RG:OFF_RG + HGRN_W]))


def _inproj(x2, seq, g, wp, tri, fb, gq, gk, selq, selk, oneq, onek, convw, gc, bd, lb, tm):
    t, dm = x2.shape
    full = lambda a: pl.BlockSpec(a.shape, lambda i: (0,) * a.ndim)
    row = lambda w: pl.BlockSpec((tm, w), lambda i: (i, 0))
    consts = (g, wp, tri, fb, gq, gk, selq, selk, oneq, onek, convw, gc, bd, lb)
    tiles_per_seq = seq // tm
    out_shape = (
        jax.ShapeDtypeStruct((t, QP_W), jnp.bfloat16),
        jax.ShapeDtypeStruct((t, QP_W), jnp.bfloat16),
        jax.ShapeDtypeStruct((t // seq, ATTN_W, seq), jnp.bfloat16),
        jax.ShapeDtypeStruct((t, CONV_C), jnp.bfloat16),
    ) + tuple(jax.ShapeDtypeStruct((t, HGRN_W), jnp.float32) for _ in range(5))
    vt_spec = pl.BlockSpec((None, ATTN_W, tm), lambda i: (i // tiles_per_seq, 0, i % tiles_per_seq))
    out_specs = (row(QP_W), row(QP_W), vt_spec, row(CONV_C)) + tuple(row(HGRN_W) for _ in range(5))
    return pl.pallas_call(
        functools.partial(_inproj_kernel, tiles_per_seq=seq // tm),
        grid=(t // tm,),
        in_specs=[row(dm)] + [full(a) for a in consts],
        out_specs=out_specs,
        out_shape=out_shape,
        scratch_shapes=[pltpu.VMEM((1, LANES), jnp.float32), pltpu.VMEM((SUBLANES, CONV_C), jnp.float32)],
        compiler_params=pltpu.CompilerParams(dimension_semantics=("arbitrary",),
                                             vmem_limit_bytes=VMEM_LIMIT_BYTES),
        name="inproj",
    )(x2, *consts)


def _attn_kernel(qa_ref, ka_ref, vt_ref, gain_ref, out_ref, m_s, acc_s, s_a, s_b, qt_s, *, tq):
    i = pl.program_id(2)
    ones_rows = jnp.ones((2 * SUBLANES, tq), jnp.bfloat16)
    m_s[...] = jnp.full_like(m_s, MASK_VALUE)
    acc_s[...] = jnp.zeros_like(acc_s)

    for hh in range(2):
        qt_s[hh] = qa_ref[:, hh * PAD_HEAD:(hh + 1) * PAD_HEAD].astype(jnp.float32).T.astype(jnp.bfloat16)

    def scores(j, s_buf):
        k0 = pl.multiple_of(j * tq, tq)
        for hh in range(2):
            sl = slice(hh * PAD_HEAD, (hh + 1) * PAD_HEAD)
            s_buf[hh] = _dot(ka_ref[pl.ds(k0, tq), sl], qt_s[hh])

    def consume(j, s_buf, masked):
        k0 = pl.multiple_of(j * tq, tq)
        for hh in range(2):
            s = s_buf[hh]
            if masked:
                kpos = lax.broadcasted_iota(jnp.int32, (tq, tq), 0)
                qpos = lax.broadcasted_iota(jnp.int32, (tq, tq), 1)
                s = jnp.where(kpos <= qpos, s, MASK_VALUE)
            m = m_s[hh]
            m_new = jnp.maximum(m, jnp.max(s, axis=0, keepdims=True))
            p = jnp.exp2(s - m_new)
            alpha = jnp.exp2(m - m_new)
            vaug = jnp.concatenate([vt_ref[hh, :, pl.ds(k0, tq)], ones_rows], axis=0)
            acc_s[hh] = acc_s[hh] * alpha + _dot(vaug, _bf16(p))
            m_s[hh] = m_new

    scores(0, s_a)

    def pair(j):
        scores(j + 1, s_b)
        consume(j, s_a, False)
        scores(j + 2, s_a)
        consume(j + 1, s_b, False)

    def body(jj, _):
        pair(4 * jj)
        pair(4 * jj + 2)
        return 0

    lax.fori_loop(0, i // 4, body, 0)
    done = 4 * (i // 4)

    @pl.when(i - done >= 2)
    def _():
        pair(done)

    @pl.when(i % 2 == 1)
    def _():
        scores(i, s_b)
        consume(i - 1, s_a, False)
        consume(i, s_b, True)

    @pl.when(i % 2 == 0)
    def _():
        consume(i, s_a, True)

    ys = []
    for hh in range(2):
        acc = acc_s[hh]
        o = acc[0:HEAD_DIM] * (1.0 / acc[HEAD_DIM:HEAD_DIM + 1])
        ms = jnp.mean(o * o, axis=0, keepdims=True)
        ys.append(o * lax.rsqrt(ms + EPS) * gain_ref[hh])
    out_ref[...] = _bf16(jnp.concatenate(ys, axis=0).T)


def _attention(qa, ka, vt, gain, tq):
    b, s, _ = qa.shape
    return pl.pallas_call(
        functools.partial(_attn_kernel, tq=tq),
        grid=(b, ATTN_HEADS // 2, s // tq),
        in_specs=[
            pl.BlockSpec((None, tq, 2 * PAD_HEAD), lambda bi, hp, i: (bi, i, hp)),
            pl.BlockSpec((None, s, 2 * PAD_HEAD), lambda bi, hp, i: (bi, 0, hp)),
            pl.BlockSpec((None, 2, HEAD_DIM, s), lambda bi, hp, i: (bi, hp, 0, 0)),
            pl.BlockSpec((2, HEAD_DIM, tq), lambda bi, hp, i: (hp, 0, 0)),
        ],
        out_specs=pl.BlockSpec((None, tq, 2 * HEAD_DIM), lambda bi, hp, i: (bi, i, hp)),
        out_shape=jax.ShapeDtypeStruct((b, s, ATTN_W), jnp.bfloat16),
        scratch_shapes=[pltpu.VMEM((2, 1, tq), jnp.float32),
                        pltpu.VMEM((2, HEAD_DIM + 2 * SUBLANES, tq), jnp.float32),
                        pltpu.VMEM((2, tq, tq), jnp.float32),
                        pltpu.VMEM((2, tq, tq), jnp.float32),
                        pltpu.VMEM((2, PAD_HEAD, tq), jnp.bfloat16)],
        compiler_params=pltpu.CompilerParams(dimension_semantics=("arbitrary", "arbitrary", "arbitrary"),
                                             vmem_limit_bytes=VMEM_LIMIT_BYTES),
        name="fox_attention",
    )(qa, ka, vt, gain)


def _hgrn_kernel(q_ref, lf_ref, k_ref, v_ref, g_ref, tri_ref, bd_ref, gain_ref, out_ref,
                 state, c_all, k_all, v_all, *, n_chunks):
    @pl.when(pl.program_id(1) == 0)
    def _():
        state[...] = jnp.zeros_like(state)

    ch = HGRN_CHUNK
    sb = HGRN_SUB
    nsb = ch // sb
    row_sb = lax.broadcasted_iota(jnp.int32, (sb, HGRN_W), 0)
    r128 = lax.broadcasted_iota(jnp.int32, (LANES, LANES), 0)
    c128 = lax.broadcasted_iota(jnp.int32, (LANES, LANES), 1)
    same_head = (r128 < HEAD_DIM) == (c128 < HEAD_DIM)
    t64 = lax.broadcasted_iota(jnp.int32, (ch, LANES), 0) // sb
    s64 = (lax.broadcasted_iota(jnp.int32, (ch, LANES), 1) % HEAD_DIM) // sb
    level2 = ((t64 == 1) & (s64 == 0)) | ((t64 == 3) & (s64 == 2))
    lane_head0 = lax.broadcasted_iota(jnp.int32, (ch, LANES), 1) < HEAD_DIM
    bd = bd_ref[...]
    zeros_sb = jnp.zeros((sb, HGRN_W), jnp.float32)

    def chunk(ci):
        r0 = ci * ch
        c_s, k_s, v_s = c_all.at[ci % 2], k_all.at[ci % 2], v_all.at[ci % 2]
        q = q_ref[pl.ds(r0, ch), :]
        k = k_ref[pl.ds(r0, ch), :]
        v = v_ref[pl.ds(r0, ch), :]
        p1, p2, p3 = _split3(lf_ref[pl.ds(r0, ch), :])
        cc = _dot(tri_ref[...], jnp.concatenate([p1, p2, p3], axis=1))
        c = cc[:, 0:HGRN_W] + cc[:, HGRN_W:2 * HGRN_W] + cc[:, 2 * HGRN_W:3 * HGRN_W]
        c_s[...] = c
        k_s[...] = k
        v_s[...] = v
        blk = lambda a, n: a[n * sb:(n + 1) * sb]

        ps = []
        for n in range(nsb):
            cn, qn = blk(c, n), blk(q, n)
            for s in range(sb):
                r = n * sb + s
                dec = jnp.exp(jnp.where(row_sb >= s, cn - c_s[r:r + 1, :], MASK_LOG_DECAY))
                ps.append(_bf16(qn * k_s[r:r + 1, :] * dec))
        a_d = _dot(jnp.concatenate(ps, axis=0), bd)
        o_parts = []
        for n in range(nsb):
            acc = jnp.zeros((sb, HGRN_W), jnp.float32)
            for s in range(sb):
                r = n * sb + s
                acc = acc + a_d[r * sb:(r + 1) * sb] * v_s[r:r + 1, :]
            o_parts.append(acc)
        o = jnp.concatenate(o_parts, axis=0)

        ref1 = c[2 * sb - 1:2 * sb]
        ref2a = c[sb - 1:sb]
        ref2b = c[3 * sb - 1:3 * sb]
        hi, lo = slice(2 * sb, 4 * sb), slice(0, 2 * sb)
        zeros_half = jnp.zeros((2 * sb, HGRN_W), jnp.float32)
        q1 = jnp.concatenate([zeros_half, q[hi] * jnp.exp(c[hi] - ref1)], axis=0)
        k1 = jnp.concatenate([k[lo] * jnp.exp(ref1 - c[lo]), zeros_half], axis=0)
        q2 = jnp.concatenate([zeros_sb, blk(q, 1) * jnp.exp(blk(c, 1) - ref2a),
                              zeros_sb, blk(q, 3) * jnp.exp(blk(c, 3) - ref2b)], axis=0)
        k2 = jnp.concatenate([blk(k, 0) * jnp.exp(ref2a - blk(c, 0)), zeros_sb,
                              blk(k, 2) * jnp.exp(ref2b - blk(c, 2)), zeros_sb], axis=0)

        c_last = c[ch - 1:ch, :]
        qe = q * jnp.exp(c)
        kd = k * jnp.exp(c_last - c)
        e_last = jnp.exp(c_last)
        nt = (((1,), (1,)), ((), ()))
        o_off = []
        for bb in range(HGRN_W // LANES):
            sl = slice(bb * LANES, (bb + 1) * LANES)
            st = state[bb]
            stack2 = lambda a: jnp.concatenate([jnp.where(lane_head0, a[:, sl], 0.0),
                                                jnp.where(lane_head0, 0.0, a[:, sl])], axis=0)
            a1 = lax.dot_general(_bf16(q1[:, sl]), _bf16(stack2(k1)), nt, preferred_element_type=jnp.float32)
            a2 = lax.dot_general(_bf16(q2[:, sl]), _bf16(stack2(k2)), nt, preferred_element_type=jnp.float32)
            a_off = a1 + jnp.where(level2, a2, 0.0)
            lhs = jnp.concatenate([_bf16(a_off), _bf16(qe[:, sl])], axis=1)
            rhs = jnp.concatenate([_bf16(stack2(v)), _bf16(st)], axis=0)
            o_off.append(_dot(lhs, rhs))
            ecol = jnp.broadcast_to(e_last[:, sl], (LANES, LANES)).T
            upd = lax.dot_general(_bf16(kd[:, sl]), _bf16(v[:, sl]), (((0,), (0,)), ((), ())),
                                  preferred_element_type=jnp.float32)
            state[bb] = jnp.where(same_head, ecol * st + upd, 0.0)
        o = o + jnp.concatenate(o_off, axis=1)

        ss = _group_sum(o * o, bd)
        y = o * lax.rsqrt(ss * (1.0 / HEAD_DIM) + EPS) * gain_ref[...] * g_ref[pl.ds(r0, ch), :]
        out_ref[pl.ds(r0, ch), :] = _bf16(y)

    for ci in range(n_chunks):
        chunk(ci)


def _hgrn(rq, rlf, rk, rv, rg, tri, bd, gain, batch, seq, rows):
    t = rq.shape[0]
    steps = seq // rows
    blk = pl.BlockSpec((rows, HGRN_W), lambda b, j: (b * steps + j, 0))
    full = lambda a: pl.BlockSpec(a.shape, lambda b, j: (0,) * a.ndim)
    return pl.pallas_call(
        functools.partial(_hgrn_kernel, n_chunks=rows // HGRN_CHUNK),
        grid=(batch, steps),
        in_specs=[blk, blk, blk, blk, blk, full(tri), full(bd), full(gain)],
        out_specs=blk,
        out_shape=jax.ShapeDtypeStruct((t, HGRN_W), jnp.bfloat16),
        scratch_shapes=[pltpu.VMEM((HGRN_W // LANES, LANES, LANES), jnp.float32)]
        + [pltpu.VMEM((2, HGRN_CHUNK, HGRN_W), jnp.float32) for _ in range(3)],
        compiler_params=pltpu.CompilerParams(dimension_semantics=("arbitrary", "arbitrary"),
                                             vmem_limit_bytes=VMEM_LIMIT_BYTES),
        name="hgrn2",
    )(rq, rlf, rk, rv, rg, tri, bd, gain)


INFO_E1, INFO_E2, INFO_R1, INFO_R2, INFO_W1, INFO_W2 = range(6)


def _mix_and_norm(x_ref, ya_ref, yc_ref, yr_ref, w_ref, g_ref):
    xn = (x_ref[...]
          + _dot(ya_ref[...], w_ref[0:ATTN_W, :])
          + _dot(yc_ref[...], w_ref[ATTN_W:ATTN_W + CONV_C, :])
          + _dot(yr_ref[...], w_ref[ATTN_W + CONV_C:, :]))
    h = xn * lax.rsqrt(jnp.mean(xn * xn, axis=-1, keepdims=True) + EPS) * g_ref[...]
    return xn, h


def _outproj_dense_kernel(x_ref, ya_ref, yc_ref, yr_ref, w_ref, g_ref, xo_ref, h_ref):
    xn, h = _mix_and_norm(x_ref, ya_ref, yc_ref, yr_ref, w_ref, g_ref)
    xo_ref[...] = xn
    h_ref[...] = _bf16(h)


def _outproj_routed_kernel(x_ref, ya_ref, yc_ref, yr_ref, w_ref, g_ref, wr_ref, br_ref, tri_ref,
                           xo_ref, h_ref, info_ref, info_t_ref, cnt_ref, cnt_s):
    @pl.when(pl.program_id(0) == 0)
    def _():
        cnt_s[...] = jnp.zeros_like(cnt_s)

    xn, h = _mix_and_norm(x_ref, ya_ref, yc_ref, yr_ref, w_ref, g_ref)
    xo_ref[...] = xn
    h_ref[...] = h
    h_hi = _bf16(h)
    h_lo = _bf16(h - h_hi.astype(jnp.float32))
    hw = _dot(h_hi, wr_ref[...])
    logits = hw[:, 0:LANES] + hw[:, LANES:2 * LANES] + _dot(h_lo, wr_ref[:, 0:LANES]) + br_ref[...]
    lane = lax.broadcasted_iota(jnp.int32, logits.shape, 1)
    logits = jnp.where(lane < N_EXPERTS, logits, MASK_VALUE)
    m1 = jnp.max(logits, axis=-1, keepdims=True)
    i1 = jnp.min(jnp.where(logits == m1, lane, LANES), axis=-1, keepdims=True)
    rest = jnp.where(lane == i1, MASK_VALUE, logits)
    m2 = jnp.max(rest, axis=-1, keepdims=True)
    i2 = jnp.min(jnp.where(rest == m2, lane, LANES), axis=-1, keepdims=True)
    e2 = jnp.exp(m2 - m1)
    w1 = 1.0 / (1.0 + e2)
    w2 = e2 * w1
    hit = (lane == i1) | (lane == i2)
    onehot = jnp.where(hit, 1.0, 0.0)
    incl = _dot(tri_ref[...], _bf16(onehot))
    rank = cnt_s[...] + incl - onehot
    r1 = jnp.sum(jnp.where(lane == i1, rank, 0.0), axis=-1, keepdims=True)
    r2 = jnp.sum(jnp.where(lane == i2, rank, 0.0), axis=-1, keepdims=True)
    cnt_new = cnt_s[...] + incl[incl.shape[0] - 1:, :]
    cnt_s[...] = cnt_new
    cnt_ref[...] = jnp.broadcast_to(cnt_new, cnt_ref.shape)
    rec = jnp.zeros(logits.shape, jnp.float32)
    for ln, val in ((INFO_E1, i1.astype(jnp.float32)), (INFO_E2, i2.astype(jnp.float32)),
                    (INFO_R1, r1), (INFO_R2, r2), (INFO_W1, w1), (INFO_W2, w2)):
        rec = jnp.where(lane == ln, val, rec)
    info_ref[...] = rec
    info_t_ref[...] = rec.T[0:SUBLANES, :]


def _outproj(x2, ya, yc, yr, wo, g, tm, router=None):
    t, dm = x2.shape
    row = lambda w: pl.BlockSpec((tm, w), lambda i: (i, 0))
    full = lambda a: pl.BlockSpec(a.shape, lambda i: (0,) * a.ndim)
    params = pltpu.CompilerParams(dimension_semantics=("arbitrary",), vmem_limit_bytes=VMEM_LIMIT_BYTES)
    base_specs = [row(dm), row(ATTN_W), row(CONV_C), row(HGRN_W), full(wo), full(g)]
    if router is None:
        return pl.pallas_call(
            _outproj_dense_kernel,
            grid=(t // tm,),
            in_specs=base_specs,
            out_specs=(row(dm), row(dm)),
            out_shape=(jax.ShapeDtypeStruct((t, dm), jnp.float32), jax.ShapeDtypeStruct((t, dm), jnp.bfloat16)),
            compiler_params=params,
            name="outproj_dense",
        )(x2, ya, yc, yr, wo, g)
    wr, br, tri = router
    return pl.pallas_call(
        _outproj_routed_kernel,
        grid=(t // tm,),
        in_specs=base_specs + [full(wr), full(br), full(tri)],
        out_specs=(row(dm), row(dm), row(LANES), pl.BlockSpec((SUBLANES, tm), lambda i: (0, i)),
                   pl.BlockSpec((SUBLANES, LANES), lambda i: (0, 0))),
        out_shape=(jax.ShapeDtypeStruct((t, dm), jnp.float32), jax.ShapeDtypeStruct((t, dm), jnp.float32),
                   jax.ShapeDtypeStruct((t, LANES), jnp.float32),
                   jax.ShapeDtypeStruct((SUBLANES, t), jnp.float32),
                   jax.ShapeDtypeStruct((SUBLANES, LANES), jnp.float32)),
        scratch_shapes=[pltpu.VMEM((1, LANES), jnp.float32)],
        compiler_params=params,
        name="outproj_routed",
    )(x2, ya, yc, yr, wo, g, wr, br, tri)


def _swiglu_step(h_ref, wg_ref, wu_ref, wd_ref):
    h = _bf16(h_ref[...])
    gt = _dot(h, _bf16(wg_ref[...]))
    up = _dot(h, _bf16(wu_ref[...]))
    act = _bf16(gt * (1.0 / (1.0 + jnp.exp(-gt))) * up)
    return _dot(act, _bf16(wd_ref[...]))


def _ffn_dense_kernel(h_ref, x_ref, wg_ref, wu_ref, wd_ref, out_ref, acc):
    f = pl.program_id(1)

    @pl.when(f == 0)
    def _():
        acc[...] = x_ref[...]

    acc[...] += _swiglu_step(h_ref, wg_ref, wu_ref, wd_ref)

    @pl.when(f == pl.num_programs(1) - 1)
    def _():
        out_ref[...] = acc[...]


def _ffn_dense(h, x2, wg, wu, wd, tm, fc):
    t, dm = x2.shape
    dff = wg.shape[1]
    return pl.pallas_call(
        _ffn_dense_kernel,
        grid=(t // tm, dff // fc),
        in_specs=[
            pl.BlockSpec((tm, dm), lambda i, f: (i, 0)),
            pl.BlockSpec((tm, dm), lambda i, f: (i, 0)),
            pl.BlockSpec((dm, fc), lambda i, f: (0, f)),
            pl.BlockSpec((dm, fc), lambda i, f: (0, f)),
            pl.BlockSpec((fc, dm), lambda i, f: (f, 0)),
        ],
        out_specs=pl.BlockSpec((tm, dm), lambda i, f: (i, 0)),
        out_shape=jax.ShapeDtypeStruct((t, dm), jnp.float32),
        scratch_shapes=[pltpu.VMEM((tm, dm), jnp.float32)],
        compiler_params=pltpu.CompilerParams(dimension_semantics=("arbitrary", "arbitrary"),
                                             vmem_limit_bytes=VMEM_LIMIT_BYTES),
        name="swiglu_dense",
    )(h, x2, wg, wu, wd)


def _ffn_grouped_kernel(te_ref, ta_ref, h_ref, wg_ref, wu_ref, wd_ref, out_ref, acc):
    i = pl.program_id(0)
    f = pl.program_id(1)

    @pl.when(ta_ref[i] == 1)
    def _():
        @pl.when(f == 0)
        def _():
            acc[...] = jnp.zeros_like(acc)

        acc[...] += _swiglu_step(h_ref, wg_ref, wu_ref, wd_ref)

        @pl.when(f == pl.num_programs(1) - 1)
        def _():
            out_ref[...] = acc[...]

    @pl.when((ta_ref[i] == 0) & (f == pl.num_programs(1) - 1))
    def _():
        out_ref[...] = jnp.zeros_like(out_ref)


def _ffn_grouped(tile_expert, tile_active, xs, wg, wu, wd, tm, fc):
    dm = xs.shape[1]
    n_tiles = tile_expert.shape[0]
    r = n_tiles * tm
    dff = wg.shape[2]
    nf = dff // fc
    fidx = lambda i, f, ta: f * ta[i] + (nf - 1) * (1 - ta[i])
    grid_spec = pltpu.PrefetchScalarGridSpec(
        num_scalar_prefetch=2,
        grid=(n_tiles, nf),
        in_specs=[
            pl.BlockSpec((tm, dm), lambda i, f, te, ta: (i * ta[i], 0)),
            pl.BlockSpec((None, dm, fc), lambda i, f, te, ta: (te[i], 0, fidx(i, f, ta))),
            pl.BlockSpec((None, dm, fc), lambda i, f, te, ta: (te[i], 0, fidx(i, f, ta))),
            pl.BlockSpec((None, fc, dm), lambda i, f, te, ta: (te[i], fidx(i, f, ta), 0)),
        ],
        out_specs=pl.BlockSpec((tm, dm), lambda i, f, te, ta: (i, 0)),
        scratch_shapes=[pltpu.VMEM((tm, dm), jnp.float32)],
    )
    return pl.pallas_call(
        _ffn_grouped_kernel,
        grid_spec=grid_spec,
        out_shape=jax.ShapeDtypeStruct((r, dm), jnp.float32),
        compiler_params=pltpu.CompilerParams(dimension_semantics=("arbitrary", "arbitrary"),
                                             vmem_limit_bytes=VMEM_LIMIT_BYTES),
        name="swiglu_grouped",
    )(tile_expert, tile_active, xs, wg, wu, wd)


def _row_copy(src, dst, sem):
    return pltpu.make_async_copy(src, dst, sem)


MOE_ISSUE_UNROLL = 8


def _wait_rows(ref, n_rows, sem):
    blk = ref.at[pl.ds(0, n_rows), :]
    pltpu.make_async_copy(blk, blk, sem).wait()


MOE_STAGE_BUFS = 3


def _dispatch_kernel(p1_ref, p2_ref, pad_ref, h_ref, xs_ref, stage, zrow, load_sems, scat_sems, zero_sem,
                     *, tb, n_chunks):
    def load(c, slot):
        return pltpu.make_async_copy(h_ref.at[pl.ds(c * tb, tb), :], stage.at[slot], load_sems.at[slot])

    load(0, 0).start()

    zrow[...] = jnp.zeros_like(zrow)
    for e in range(N_EXPERTS):
        def zero(r, _, e=e):
            _row_copy(zrow.at[pl.ds(0, 1), :], xs_ref.at[pl.ds(pad_ref[e] + r, 1), :], zero_sem).start()
            return 0

        lax.fori_loop(0, pad_ref[N_EXPERTS + e], zero, 0)
    for e in range(N_EXPERTS):
        def zero_done(r, _):
            _wait_rows(xs_ref, 1, zero_sem)
            return 0

        lax.fori_loop(0, pad_ref[N_EXPERTS + e], zero_done, 0)

    def chunk(c, _):
        slot = lax.rem(c, MOE_STAGE_BUFS)
        load(c, slot).wait()

        @pl.when(c >= 2)
        def _():
            _wait_rows(xs_ref, 2 * tb, scat_sems.at[lax.rem(c - 2, MOE_STAGE_BUFS)])

        @pl.when(c + 1 < n_chunks)
        def _():
            load(c + 1, lax.rem(c + 1, MOE_STAGE_BUFS)).start()

        sem = scat_sems.at[slot]

        def issue(rr, _):
            for u in range(MOE_ISSUE_UNROLL):
                r = rr * MOE_ISSUE_UNROLL + u
                tok = c * tb + r
                src = stage.at[slot, pl.ds(r, 1), :]
                _row_copy(src, xs_ref.at[pl.ds(p1_ref[tok], 1), :], sem).start(priority=0)
                _row_copy(src, xs_ref.at[pl.ds(p2_ref[tok], 1), :], sem).start(priority=1)
            return 0

        lax.fori_loop(0, tb // MOE_ISSUE_UNROLL, issue, 0)
        return 0

    lax.fori_loop(0, n_chunks, chunk, 0)
    for c in range(max(n_chunks - 2, 0), n_chunks):
        _wait_rows(xs_ref, 2 * tb, scat_sems.at[c % MOE_STAGE_BUFS])


def _dispatch(pos1, pos2, pad_start, h, n_tiles, tb, tm):
    t, dm = h.shape
    n_chunks = t // tb
    grid_spec = pltpu.PrefetchScalarGridSpec(
        num_scalar_prefetch=3,
        grid=(1,),
        in_specs=[pl.BlockSpec(memory_space=pl.ANY)],
        out_specs=pl.BlockSpec(memory_space=pl.ANY),
        scratch_shapes=[pltpu.VMEM((MOE_STAGE_BUFS, tb, dm), h.dtype),
                        pltpu.VMEM((SUBLANES, dm), h.dtype),
                        pltpu.SemaphoreType.DMA((MOE_STAGE_BUFS,)),
                        pltpu.SemaphoreType.DMA((MOE_STAGE_BUFS,)),
                        pltpu.SemaphoreType.DMA(())],
    )
    return pl.pallas_call(
        functools.partial(_dispatch_kernel, tb=tb, n_chunks=n_chunks),
        grid_spec=grid_spec,
        out_shape=jax.ShapeDtypeStruct((n_tiles * tm, dm), h.dtype),
        compiler_params=pltpu.CompilerParams(dimension_semantics=("arbitrary",), has_side_effects=True),
        name="moe_dispatch",
    )(pos1, pos2, pad_start, h)


def _combine_kernel(p1_ref, p2_ref, y_ref, x_ref, info_ref, out_ref, buf_a, buf_b, sems, *, tb):
    i = pl.program_id(0)
    n = pl.num_programs(0)

    def issue_step(step, buf, sem):
        def issue(rr, _):
            for u in range(MOE_ISSUE_UNROLL):
                r = rr * MOE_ISSUE_UNROLL + u
                tok = step * tb + r
                _row_copy(y_ref.at[pl.ds(p1_ref[tok], 1), :], buf.at[0, pl.ds(r, 1), :], sem).start(priority=0)
                _row_copy(y_ref.at[pl.ds(p2_ref[tok], 1), :], buf.at[1, pl.ds(r, 1), :], sem).start(priority=1)
            return 0

        lax.fori_loop(0, tb // MOE_ISSUE_UNROLL, issue, 0)

    def finish(buf, sem):
        _wait_rows(y_ref, 2 * tb, sem)
        info = info_ref[...]
        lane = lax.broadcasted_iota(jnp.int32, info.shape, 1)
        w1 = jnp.sum(jnp.where(lane == INFO_W1, info, 0.0), axis=-1, keepdims=True)
        w2 = jnp.sum(jnp.where(lane == INFO_W2, info, 0.0), axis=-1, keepdims=True)
        out_ref[...] = x_ref[...] + w1 * buf[0] + w2 * buf[1]

    @pl.when(i == 0)
    def _():
        issue_step(0, buf_a, sems.at[0])

    @pl.when(i % 2 == 0)
    def _():
        @pl.when(i + 1 < n)
        def _():
            issue_step(i + 1, buf_b, sems.at[1])

        finish(buf_a, sems.at[0])

    @pl.when(i % 2 == 1)
    def _():
        @pl.when(i + 1 < n)
        def _():
            issue_step(i + 1, buf_a, sems.at[0])

        finish(buf_b, sems.at[1])


def _combine(pos1, pos2, y, x2, info, tb):
    t, dm = x2.shape
    grid_spec = pltpu.PrefetchScalarGridSpec(
        num_scalar_prefetch=2,
        grid=(t // tb,),
        in_specs=[pl.BlockSpec(memory_space=pl.ANY),
                  pl.BlockSpec((tb, dm), lambda i, p1, p2: (i, 0)),
                  pl.BlockSpec((tb, LANES), lambda i, p1, p2: (i, 0))],
        out_specs=pl.BlockSpec((tb, dm), lambda i, p1, p2: (i, 0)),
        scratch_shapes=[pltpu.VMEM((2, tb, dm), jnp.float32), pltpu.VMEM((2, tb, dm), jnp.float32),
                        pltpu.SemaphoreType.DMA((2,))],
    )
    return pl.pallas_call(
        functools.partial(_combine_kernel, tb=tb),
        grid_spec=grid_spec,
        out_shape=jax.ShapeDtypeStruct((t, dm), jnp.float32),
        compiler_params=pltpu.CompilerParams(dimension_semantics=("arbitrary",)),
        name="moe_combine",
    )(pos1, pos2, y, x2, info)


def _moe_routing_tables(info_t, counts, tm, n_tiles):
    cnt = counts[0, :N_EXPERTS].astype(jnp.int32)
    padded = ((cnt + tm - 1) // tm) * tm
    ends = jnp.cumsum(padded)
    offsets = ends - padded
    e1 = info_t[INFO_E1].astype(jnp.int32)
    e2 = info_t[INFO_E2].astype(jnp.int32)
    pos1 = offsets[e1] + info_t[INFO_R1].astype(jnp.int32)
    pos2 = offsets[e2] + info_t[INFO_R2].astype(jnp.int32)
    start = jnp.arange(n_tiles, dtype=jnp.int32) * tm
    tile_expert = jnp.minimum(jnp.sum((start[:, None] >= ends[None, :]).astype(jnp.int32), axis=1), N_EXPERTS - 1)
    tile_active = (start < ends[-1]).astype(jnp.int32)
    pad_rows = jnp.concatenate([offsets + cnt, padded - cnt])
    return pos1, pos2, pad_rows, tile_expert, tile_active


def _pad_heads(w):
    d = w.shape[0]
    w = w.reshape(d, ATTN_HEADS, HEAD_DIM)
    return jnp.pad(w, ((0, 0), (0, 0), (0, PAD_HEAD - HEAD_DIM))).reshape(d, QP_W)


def _pack_w_in(w):
    s = np.cumsum([0, ATTN_W, ATTN_W, ATTN_W, ATTN_HEADS, CONV_C, CONV_C, CONV_C, HGRN_W, HGRN_W, HGRN_W, HGRN_W])
    seg = [w[:, s[n]:s[n + 1]] for n in range(11)]
    a_q, a_k, a_v, a_f, c_x, c_b, c_c, r_q, r_f, r_i, r_g = seg
    a_f = jnp.pad(a_f, ((0, 0), (0, LANES - ATTN_HEADS)))
    return _bf16(jnp.concatenate([_pad_heads(a_q), _pad_heads(a_k), a_v, c_x, c_b, c_c, r_q, r_f, r_i, r_g, a_f],
                                 axis=1))


def _selection_constants():
    selq = np.zeros((LANES, QP_W), np.float32)
    selk = np.zeros((LANES, QP_W), np.float32)
    oneq = np.zeros((1, QP_W), np.float32)
    onek = np.zeros((1, QP_W), np.float32)
    for hd in range(ATTN_HEADS):
        base = hd * PAD_HEAD + AUG0
        for piece in range(3):
            selq[piece * ATTN_HEADS + hd, base + piece] = 1.0
            selk[piece * ATTN_HEADS + hd, base + 3 + piece] = -1.0
            oneq[0, base + 3 + piece] = 1.0
            onek[0, base + piece] = 1.0
    return (jnp.asarray(selq, jnp.bfloat16), jnp.asarray(selk, jnp.bfloat16),
            jnp.asarray(oneq), jnp.asarray(onek))


def _pad_gain(gain, mult):
    g = jnp.pad(gain.astype(jnp.float32) * mult, (0, PAD_HEAD - HEAD_DIM))
    return jnp.tile(g, ATTN_HEADS).reshape(1, QP_W)


def kernel(x, norm_mix, w_in, attn_f_bias, q_norm_gain, k_norm_gain, conv_w, hgrn_lb_logits, mix_out_gain, w_out,
           norm_ffn, ffn_w_gate, ffn_w_up, ffn_w_down, moe_router_w, moe_router_b, moe_w_gate, moe_w_up, moe_w_down):
    batch, seq, dm = x.shape
    depth = w_in.shape[0]
    t = batch * seq
    f32 = jnp.float32
    tm = min(512, seq)
    tq = min(512, seq)
    hg_rows = min(512, seq)
    tm_ffn = min(1024, t)
    tb_moe = min(256, t)
    dff = ffn_w_gate.shape[-1]
    fc = dff // 4

    p_lb = jax.nn.softmax(hgrn_lb_logits.astype(f32), axis=0)
    lb_all = jnp.cumsum(p_lb, axis=0) - p_lb[0]

    tri_m = _bf16(jnp.tril(jnp.ones((tm, tm), f32)))
    tri_c = _bf16(jnp.tril(jnp.ones((HGRN_CHUNK, HGRN_CHUNK), f32)))
    grp = np.arange(HGRN_W) // HEAD_DIM
    bd = jnp.asarray(grp[:, None] == grp[None, :], jnp.bfloat16)
    selq, selk, oneq, onek = _selection_constants()
    scale = 1.0 / math.sqrt(HEAD_DIM)

    x2 = x.reshape(t, dm)
    for l in range(depth):
        wp = _pack_w_in(w_in[l])
        fb = jnp.pad(attn_f_bias[l].astype(f32), (0, LANES - ATTN_HEADS)).reshape(1, LANES)
        gq = _pad_gain(q_norm_gain[l], scale * LOG2E)
        gk = _pad_gain(k_norm_gain[l], 1.0)
        mog = mix_out_gain[l].astype(f32)
        ga = jnp.broadcast_to(mog[:ATTN_W].reshape(ATTN_HEADS, HEAD_DIM, 1), (ATTN_HEADS, HEAD_DIM, tq))
        gc = mog[ATTN_W:ATTN_W + CONV_C].reshape(1, CONV_C)
        gr = mog[ATTN_W + CONV_C:].reshape(1, HGRN_W)

        qa, ka, v, yc, rq, rlf, rk, rv, rg = _inproj(
            x2, seq, norm_mix[l].astype(f32).reshape(1, dm), wp, tri_m, fb, gq, gk, selq, selk, oneq, onek,
            conv_w[l].astype(f32), gc, bd, lb_all[l].reshape(1, HGRN_W), tm)

        vt = v.reshape(batch, ATTN_HEADS, HEAD_DIM, seq)
        ya = _attention(qa.reshape(batch, seq, QP_W), ka.reshape(batch, seq, QP_W), vt, ga, tq)
        yr = _hgrn(rq, rlf, rk, rv, rg, tri_c, bd, gr, batch, seq, hg_rows)

        j = l // 2
        wo = _bf16(w_out[l])
        gf = norm_ffn[l].astype(f32).reshape(1, dm)
        if l % 2 == 0:
            xo, h2 = _outproj(x2, ya.reshape(t, ATTN_W), yc, yr, wo, gf, tm)
            x2 = _ffn_dense(h2, xo, ffn_w_gate[j], ffn_w_up[j], ffn_w_down[j], tm_ffn, fc)
        else:
            wr32 = jnp.pad(moe_router_w[j].astype(f32), ((0, 0), (0, LANES - N_EXPERTS)))
            wr_hi = _bf16(wr32)
            wr = jnp.concatenate([wr_hi, _bf16(wr32 - wr_hi.astype(f32))], axis=1)
            br = jnp.pad(moe_router_b[j].astype(f32), (0, LANES - N_EXPERTS)).reshape(1, LANES)
            xo, h2, info, info_t, counts = _outproj(x2, ya.reshape(t, ATTN_W), yc, yr, wo, gf, tm,
                                                    router=(wr, br, tri_m))
            n_tiles = (2 * t) // tm_ffn + N_EXPERTS
            pos1, pos2, pad_rows, tile_expert, tile_active = _moe_routing_tables(info_t, counts, tm_ffn, n_tiles)
            xs = _dispatch(pos1, pos2, pad_rows, h2, n_tiles, tb_moe, tm_ffn)
            ys = _ffn_grouped(tile_expert, tile_active, xs,
                              moe_w_gate[j], moe_w_up[j], moe_w_down[j], tm_ffn, fc)
            x2 = _combine(pos1, pos2, ys, xo, info, tb_moe)
    return x2.reshape(batch, seq, dm)
```

```python
import functools
import math

import jax
import jax.numpy as jnp
import numpy as np
from jax import lax
from jax.experimental import pallas as pl
from jax.experimental.pallas import tpu as pltpu

HEAD_DIM = 64
ATTN_HEADS = 8
CONV_C = 256
HGRN_W = 256
ATTN_W = ATTN_HEADS * HEAD_DIM
HGRN_CHUNK = 64
HGRN_SUB = 16
N_EXPERTS = 8
EPS = 1e-6
MASK_VALUE = -1e30
MASK_LOG_DECAY = -1e4
TINY = 1e-30
LOG2E = math.log2(math.e)

LANES = 128
SUBLANES = 8
VMEM_LIMIT_BYTES = 56 * 1024 * 1024

PAIR_W = 2 * HEAD_DIM
N_PAIRS = ATTN_HEADS // 2
OFF_Q = 0
OFF_K = OFF_Q + ATTN_W
OFF_V = OFF_K + ATTN_W
OFF_CX = OFF_V + ATTN_W
OFF_CB = OFF_CX + CONV_C
OFF_CC = OFF_CB + CONV_C
OFF_RQ = OFF_CC + CONV_C
OFF_RF = OFF_RQ + HGRN_W
OFF_RI = OFF_RF + HGRN_W
OFF_RG = OFF_RI + HGRN_W
OFF_AF = OFF_RG + HGRN_W
D_PACK = OFF_AF + LANES


def _bf16(x):
    return x.astype(jnp.bfloat16)


def _split3(x):
    p1 = _bf16(x)
    r1 = x - p1.astype(jnp.float32)
    p2 = _bf16(r1)
    r2 = r1 - p2.astype(jnp.float32)
    return p1, p2, _bf16(r2)


def _dot(a, b):
    return jnp.dot(a, b, preferred_element_type=jnp.float32)


def _group_sum(x, bd):
    hi = _bf16(x)
    lo = _bf16(x - hi.astype(jnp.float32))
    return _dot(hi, bd) + _dot(lo, bd)


def _silu(x):
    return x * (1.0 / (1.0 + jnp.exp(-x)))


def _sigmoid(x):
    return 1.0 / (1.0 + jnp.exp(-x))


def _inproj_kernel(x_ref, g_ref, w_ref, tri_ref, fb_ref, gq_ref, gk_ref, selq_ref, selk_ref,
                   oneq_ref, onek_ref, convw_ref, gc_ref, bd_ref, lb_ref,
                   q_ref, qaug_ref, k_ref, kaug_ref, v_ref, yc_ref, rq_ref, rlf_ref, rk_ref, rv_ref, rg_ref,
                   dcarry, ucarry, *, tiles_per_seq):
    i = pl.program_id(0)

    @pl.when(i % tiles_per_seq == 0)
    def _():
        dcarry[...] = jnp.zeros_like(dcarry)
        ucarry[...] = jnp.zeros_like(ucarry)

    x = x_ref[...]
    h = x * lax.rsqrt(jnp.mean(x * x, axis=-1, keepdims=True) + EPS) * g_ref[...]
    hb = _bf16(h)

    zf = _dot(hb, w_ref[:, OFF_AF:OFF_AF + LANES]) + fb_ref[...]
    ls = jnp.minimum(zf, 0.0) - jnp.log(1.0 + jnp.exp(-jnp.abs(zf)))
    p1, p2, p3 = _split3(ls)
    loc = _dot(tri_ref[...], jnp.concatenate([p1, p2, p3], axis=1))
    d = dcarry[...] + loc[:, 0:LANES] + loc[:, LANES:2 * LANES] + loc[:, 2 * LANES:3 * LANES]
    dcarry[...] = d[d.shape[0] - 1:, :]
    e1, e2, e3 = (e.astype(jnp.float32) for e in _split3(d * LOG2E))
    lane = lax.broadcasted_iota(jnp.int32, e1.shape, 1)
    epack = jnp.where(lane < ATTN_HEADS, e1,
                      jnp.where(lane < 2 * ATTN_HEADS, pltpu.roll(e2, ATTN_HEADS, 1),
                                jnp.where(lane < 3 * ATTN_HEADS, pltpu.roll(e3, 2 * ATTN_HEADS, 1), 0.0)))
    epack = _bf16(epack)
    qaug_ref[...] = _bf16(_dot(epack, selq_ref[...]) + oneq_ref[...])
    kaug_ref[...] = _bf16(_dot(epack, selk_ref[...]) + onek_ref[...])

    zq = _dot(hb, w_ref[:, OFF_Q:OFF_Q + ATTN_W])
    zk = _dot(hb, w_ref[:, OFF_K:OFF_K + ATTN_W])
    first = lax.broadcasted_iota(jnp.int32, (x.shape[0], PAIR_W), 1) < HEAD_DIM

    def head_norm(z):
        sq = z * z
        s0 = jnp.sum(jnp.where(first, sq, 0.0), axis=-1, keepdims=True)
        s1 = jnp.sum(jnp.where(first, 0.0, sq), axis=-1, keepdims=True)
        return z * lax.rsqrt(jnp.where(first, s0, s1) * (1.0 / HEAD_DIM) + EPS)

    for pb in range(N_PAIRS):
        sl = slice(pb * PAIR_W, (pb + 1) * PAIR_W)
        q_ref[:, sl] = _bf16(head_norm(zq[:, sl]) * gq_ref[:, sl])
        k_ref[:, sl] = _bf16(head_norm(zk[:, sl]) * gk_ref[:, sl])
    v_ref[...] = _bf16(_dot(hb, w_ref[:, OFF_V:OFF_V + ATTN_W])).T

    cx = _dot(hb, w_ref[:, OFF_CX:OFF_CX + CONV_C])
    cb = _dot(hb, w_ref[:, OFF_CB:OFF_CB + CONV_C])
    cc = _dot(hb, w_ref[:, OFF_CC:OFF_CC + CONV_C])
    u = cc * cx
    uc = ucarry[...]
    row8 = lax.broadcasted_iota(jnp.int32, (SUBLANES, CONV_C), 0)
    r1 = pltpu.roll(u, 1, 0)
    r2 = pltpu.roll(u, 2, 0)
    top1 = jnp.where(row8 < 1, pltpu.roll(uc, 1, 0), r1[0:SUBLANES])
    top2 = jnp.where(row8 < 2, pltpu.roll(uc, 2, 0), r2[0:SUBLANES])
    u1 = jnp.concatenate([top1, r1[SUBLANES:]], axis=0)
    u2 = jnp.concatenate([top2, r2[SUBLANES:]], axis=0)
    ucarry[...] = u[u.shape[0] - SUBLANES:, :]
    yc = cb * (u2 * convw_ref[0:1, :] + u1 * convw_ref[1:2, :] + u * convw_ref[2:3, :])
    ssc = _group_sum(yc * yc, bd_ref[...])
    yc_ref[...] = _bf16(yc * lax.rsqrt(ssc * (1.0 / HEAD_DIM) + EPS) * gc_ref[...])

    lb = lb_ref[...]
    zr = _dot(hb, w_ref[:, OFF_RF:OFF_RF + HGRN_W])
    sg = _sigmoid(zr)
    f = lb + (1.0 - lb) * sg
    rlf_ref[...] = jnp.log(jnp.maximum(f, TINY))
    rk_ref[...] = (1.0 - lb) * _sigmoid(-zr)
    rq_ref[...] = _silu(_dot(hb, w_ref[:, OFF_RQ:OFF_RQ + HGRN_W]))
    rv_ref[...] = _dot(hb, w_ref[:, OFF_RI:OFF_RI + HGRN_W])
    rg_ref[...] = _silu(_dot(hb, w_ref[:, OFF_RG:OFF_RG + HGRN_W]))


def _inproj(x2, seq, g, wp, tri, fb, gq, gk, selq, selk, oneq, onek, convw, gc, bd, lb, tm):
    t, dm = x2.shape
    full = lambda a: pl.BlockSpec(a.shape, lambda i: (0,) * a.ndim)
    row = lambda w: pl.BlockSpec((tm, w), lambda i: (i, 0))
    consts = (g, wp, tri, fb, gq, gk, selq, selk, oneq, onek, convw, gc, bd, lb)
    tiles_per_seq = seq // tm
    out_shape = tuple(jax.ShapeDtypeStruct((t, ATTN_W), jnp.bfloat16) for _ in range(4)) + (
        jax.ShapeDtypeStruct((t // seq, ATTN_W, seq), jnp.bfloat16),
        jax.ShapeDtypeStruct((t, CONV_C), jnp.bfloat16),
    ) + tuple(jax.ShapeDtypeStruct((t, HGRN_W), jnp.float32) for _ in range(5))
    vt_spec = pl.BlockSpec((None, ATTN_W, tm), lambda i: (i // tiles_per_seq, 0, i % tiles_per_seq))
    out_specs = (tuple(row(ATTN_W) for _ in range(4)) + (vt_spec, row(CONV_C))
                 + tuple(row(HGRN_W) for _ in range(5)))
    return pl.pallas_call(
        functools.partial(_inproj_kernel, tiles_per_seq=seq // tm),
        grid=(t // tm,),
        in_specs=[row(dm)] + [full(a) for a in consts],
        out_specs=out_specs,
        out_shape=out_shape,
        scratch_shapes=[pltpu.VMEM((1, LANES), jnp.float32), pltpu.VMEM((SUBLANES, CONV_C), jnp.float32)],
        compiler_params=pltpu.CompilerParams(dimension_semantics=("arbitrary",),
                                             vmem_limit_bytes=VMEM_LIMIT_BYTES),
        name="inproj",
    )(x2, *consts)


def _attn_kernel(q_ref, qaug_ref, k_ref, kaug_ref, vt_ref, gain_ref, out_ref, m_s, acc_s, s_a, s_b, qt_s, *, tq):
    i = pl.program_id(2)
    ones_rows = jnp.ones((2 * SUBLANES, tq), jnp.bfloat16)
    m_s[...] = jnp.full_like(m_s, MASK_VALUE)
    acc_s[...] = jnp.zeros_like(acc_s)

    lane = lax.broadcasted_iota(jnp.int32, (tq, PAIR_W), 1)
    qf = q_ref[...].astype(jnp.float32)
    af = qaug_ref[...].astype(jnp.float32)
    for hh in range(2):
        mine = (lane < HEAD_DIM) if hh == 0 else (lane >= HEAD_DIM)
        cat = jnp.concatenate([jnp.where(mine, qf, 0.0), jnp.where(mine, af, 0.0)], axis=1)
        qt_s[hh] = _bf16(cat.T)

    def scores(j, s_buf):
        k0 = pl.multiple_of(j * tq, tq)
        krows = jnp.concatenate([k_ref[pl.ds(k0, tq), :], kaug_ref[pl.ds(k0, tq), :]], axis=1)
        for hh in range(2):
            s_buf[hh] = _dot(krows, qt_s[hh])

    def consume(j, s_buf, masked):
        k0 = pl.multiple_of(j * tq, tq)
        for hh in range(2):
            s = s_buf[hh]
            if masked:
                kpos = lax.broadcasted_iota(jnp.int32, (tq, tq), 0)
                qpos = lax.broadcasted_iota(jnp.int32, (tq, tq), 1)
                s = jnp.where(kpos <= qpos, s, MASK_VALUE)
            m = m_s[hh]
            m_new = jnp.maximum(m, jnp.max(s, axis=0, keepdims=True))
            p = jnp.exp2(s - m_new)
            alpha = jnp.exp2(m - m_new)
            vaug = jnp.concatenate([vt_ref[hh, :, pl.ds(k0, tq)], ones_rows], axis=0)
            acc_s[hh] = acc_s[hh] * alpha + _dot(vaug, _bf16(p))
            m_s[hh] = m_new

    scores(0, s_a)

    def pair(j):
        scores(j + 1, s_b)
        consume(j, s_a, False)
        scores(j + 2, s_a)
        consume(j + 1, s_b, False)

    def body(jj, _):
        pair(4 * jj)
        pair(4 * jj + 2)
        return 0

    lax.fori_loop(0, i // 4, body, 0)
    done = 4 * (i // 4)

    @pl.when(i - done >= 2)
    def _():
        pair(done)

    @pl.when(i % 2 == 1)
    def _():
        scores(i, s_b)
        consume(i - 1, s_a, False)
        consume(i, s_b, True)

    @pl.when(i % 2 == 0)
    def _():
        consume(i, s_a, True)

    ys = []
    for hh in range(2):
        acc = acc_s[hh]
        o = acc[0:HEAD_DIM] * (1.0 / acc[HEAD_DIM:HEAD_DIM + 1])
        ms = jnp.mean(o * o, axis=0, keepdims=True)
        ys.append(o * lax.rsqrt(ms + EPS) * gain_ref[hh])
    out_ref[...] = _bf16(jnp.concatenate(ys, axis=0).T)


def _attention(q, qaug, k, kaug, vt, gain, tq):
    b, s, _ = q.shape
    q_spec = pl.BlockSpec((None, tq, PAIR_W), lambda bi, hp, i: (bi, i, hp))
    k_spec = pl.BlockSpec((None, s, PAIR_W), lambda bi, hp, i: (bi, 0, hp))
    return pl.pallas_call(
        functools.partial(_attn_kernel, tq=tq),
        grid=(b, N_PAIRS, s // tq),
        in_specs=[
            q_spec, q_spec, k_spec, k_spec,
            pl.BlockSpec((None, 2, HEAD_DIM, s), lambda bi, hp, i: (bi, hp, 0, 0)),
            pl.BlockSpec((2, HEAD_DIM, tq), lambda bi, hp, i: (hp, 0, 0)),
        ],
        out_specs=pl.BlockSpec((None, tq, 2 * HEAD_DIM), lambda bi, hp, i: (bi, i, hp)),
        out_shape=jax.ShapeDtypeStruct((b, s, ATTN_W), jnp.bfloat16),
        scratch_shapes=[pltpu.VMEM((2, 1, tq), jnp.float32),
                        pltpu.VMEM((2, HEAD_DIM + 2 * SUBLANES, tq), jnp.float32),
                        pltpu.VMEM((2, tq, tq), jnp.float32),
                        pltpu.VMEM((2, tq, tq), jnp.float32),
                        pltpu.VMEM((2, 2 * PAIR_W, tq), jnp.bfloat16)],
        compiler_params=pltpu.CompilerParams(dimension_semantics=("arbitrary", "arbitrary", "arbitrary"),
                                             vmem_limit_bytes=VMEM_LIMIT_BYTES),
        name="fox_attention",
    )(q, qaug, k, kaug, vt, gain)


def _hgrn_kernel(q_ref, lf_ref, k_ref, v_ref, g_ref, tri_ref, bd_ref, gain_ref, out_ref,
                 state, c_all, k_all, v_all, *, n_chunks):
    @pl.when(pl.program_id(1) == 0)
    def _():
        state[...] = jnp.zeros_like(state)

    ch = HGRN_CHUNK
    sb = HGRN_SUB
    nsb = ch // sb
    row_sb = lax.broadcasted_iota(jnp.int32, (sb, HGRN_W), 0)
    r128 = lax.broadcasted_iota(jnp.int32, (LANES, LANES), 0)
    c128 = lax.broadcasted_iota(jnp.int32, (LANES, LANES), 1)
    same_head = (r128 < HEAD_DIM) == (c128 < HEAD_DIM)
    t64 = lax.broadcasted_iota(jnp.int32, (ch, LANES), 0) // sb
    s64 = (lax.broadcasted_iota(jnp.int32, (ch, LANES), 1) % HEAD_DIM) // sb
    level2 = ((t64 == 1) & (s64 == 0)) | ((t64 == 3) & (s64 == 2))
    lane_head0 = lax.broadcasted_iota(jnp.int32, (ch, LANES), 1) < HEAD_DIM
    bd = bd_ref[...]
    zeros_sb = jnp.zeros((sb, HGRN_W), jnp.float32)

    def chunk(ci):
        r0 = ci * ch
        c_s, k_s, v_s = c_all.at[ci % 2], k_all.at[ci % 2], v_all.at[ci % 2]
        q = q_ref[pl.ds(r0, ch), :]
        k = k_ref[pl.ds(r0, ch), :]
        v = v_ref[pl.ds(r0, ch), :]
        p1, p2, p3 = _split3(lf_ref[pl.ds(r0, ch), :])
        cc = _dot(tri_ref[...], jnp.concatenate([p1, p2, p3], axis=1))
        c = cc[:, 0:HGRN_W] + cc[:, HGRN_W:2 * HGRN_W] + cc[:, 2 * HGRN_W:3 * HGRN_W]
        c_s[...] = c
        k_s[...] = k
        v_s[...] = v
        blk = lambda a, n: a[n * sb:(n + 1) * sb]

        ps = []
        for n in range(nsb):
            cn, qn = blk(c, n), blk(q, n)
            for s in range(sb):
                r = n * sb + s
                dec = jnp.exp(jnp.where(row_sb >= s, cn - c_s[r:r + 1, :], MASK_LOG_DECAY))
                ps.append(_bf16(qn * k_s[r:r + 1, :] * dec))
        a_d = _dot(jnp.concatenate(ps, axis=0), bd)
        o_parts = []
        for n in range(nsb):
            acc = jnp.zeros((sb, HGRN_W), jnp.float32)
            for s in range(sb):
                r = n * sb + s
                acc = acc + a_d[r * sb:(r + 1) * sb] * v_s[r:r + 1, :]
            o_parts.append(acc)
        o = jnp.concatenate(o_parts, axis=0)

        ref1 = c[2 * sb - 1:2 * sb]
        ref2a = c[sb - 1:sb]
        ref2b = c[3 * sb - 1:3 * sb]
        hi, lo = slice(2 * sb, 4 * sb), slice(0, 2 * sb)
        zeros_half = jnp.zeros((2 * sb, HGRN_W), jnp.float32)
        q1 = jnp.concatenate([zeros_half, q[hi] * jnp.exp(c[hi] - ref1)], axis=0)
        k1 = jnp.concatenate([k[lo] * jnp.exp(ref1 - c[lo]), zeros_half], axis=0)
        q2 = jnp.concatenate([zeros_sb, blk(q, 1) * jnp.exp(blk(c, 1) - ref2a),
                              zeros_sb, blk(q, 3) * jnp.exp(blk(c, 3) - ref2b)], axis=0)
        k2 = jnp.concatenate([blk(k, 0) * jnp.exp(ref2a - blk(c, 0)), zeros_sb,
                              blk(k, 2) * jnp.exp(ref2b - blk(c, 2)), zeros_sb], axis=0)

        c_last = c[ch - 1:ch, :]
        qe = q * jnp.exp(c)
        kd = k * jnp.exp(c_last - c)
        e_last = jnp.exp(c_last)
        nt = (((1,), (1,)), ((), ()))
        o_off = []
        for bb in range(HGRN_W // LANES):
            sl = slice(bb * LANES, (bb + 1) * LANES)
            st = state[bb]
            stack2 = lambda a: jnp.concatenate([jnp.where(lane_head0, a[:, sl], 0.0),
                                                jnp.where(lane_head0, 0.0, a[:, sl])], axis=0)
            a1 = lax.dot_general(_bf16(q1[:, sl]), _bf16(stack2(k1)), nt, preferred_element_type=jnp.float32)
            a2 = lax.dot_general(_bf16(q2[:, sl]), _bf16(stack2(k2)), nt, preferred_element_type=jnp.float32)
            a_off = a1 + jnp.where(level2, a2, 0.0)
            lhs = jnp.concatenate([_bf16(a_off), _bf16(qe[:, sl])], axis=1)
            rhs = jnp.concatenate([_bf16(stack2(v)), _bf16(st)], axis=0)
            o_off.append(_dot(lhs, rhs))
            ecol = jnp.broadcast_to(e_last[:, sl], (LANES, LANES)).T
            upd = lax.dot_general(_bf16(kd[:, sl]), _bf16(v[:, sl]), (((0,), (0,)), ((), ())),
                                  preferred_element_type=jnp.float32)
            state[bb] = jnp.where(same_head, ecol * st + upd, 0.0)
        o = o + jnp.concatenate(o_off, axis=1)

        ss = _group_sum(o * o, bd)
        y = o * lax.rsqrt(ss * (1.0 / HEAD_DIM) + EPS) * gain_ref[...] * g_ref[pl.ds(r0, ch), :]
        out_ref[pl.ds(r0, ch), :] = _bf16(y)

    for ci in range(n_chunks):
        chunk(ci)


def _hgrn(rq, rlf, rk, rv, rg, tri, bd, gain, batch, seq, rows):
    t = rq.shape[0]
    steps = seq // rows
    blk = pl.BlockSpec((rows, HGRN_W), lambda b, j: (b * steps + j, 0))
    full = lambda a: pl.BlockSpec(a.shape, lambda b, j: (0,) * a.ndim)
    return pl.pallas_call(
        functools.partial(_hgrn_kernel, n_chunks=rows // HGRN_CHUNK),
        grid=(batch, steps),
        in_specs=[blk, blk, blk, blk, blk, full(tri), full(bd), full(gain)],
        out_specs=blk,
        out_shape=jax.ShapeDtypeStruct((t, HGRN_W), jnp.bfloat16),
        scratch_shapes=[pltpu.VMEM((HGRN_W // LANES, LANES, LANES), jnp.float32)]
        + [pltpu.VMEM((2, HGRN_CHUNK, HGRN_W), jnp.float32) for _ in range(3)],
        compiler_params=pltpu.CompilerParams(dimension_semantics=("arbitrary", "arbitrary"),
                                             vmem_limit_bytes=VMEM_LIMIT_BYTES),
        name="hgrn2",
    )(rq, rlf, rk, rv, rg, tri, bd, gain)


INFO_E1, INFO_E2, INFO_R1, INFO_R2, INFO_W1, INFO_W2 = range(6)


def _mix_and_norm(x_ref, ya_ref, yc_ref, yr_ref, w_ref, g_ref):
    xn = (x_ref[...]
          + _dot(ya_ref[...], w_ref[0:ATTN_W, :])
          + _dot(yc_ref[...], w_ref[ATTN_W:ATTN_W + CONV_C, :])
          + _dot(yr_ref[...], w_ref[ATTN_W + CONV_C:, :]))
    h = xn * lax.rsqrt(jnp.mean(xn * xn, axis=-1, keepdims=True) + EPS) * g_ref[...]
    return xn, h


def _outproj_dense_kernel(x_ref, ya_ref, yc_ref, yr_ref, w_ref, g_ref, xo_ref, h_ref):
    xn, h = _mix_and_norm(x_ref, ya_ref, yc_ref, yr_ref, w_ref, g_ref)
    xo_ref[...] = xn
    h_ref[...] = _bf16(h)


def _outproj_routed_kernel(x_ref, ya_ref, yc_ref, yr_ref, w_ref, g_ref, wr_ref, br_ref, tri_ref,
                           xo_ref, h_ref, info_ref, info_t_ref, cnt_ref, cnt_s):
    @pl.when(pl.program_id(0) == 0)
    def _():
        cnt_s[...] = jnp.zeros_like(cnt_s)

    xn, h = _mix_and_norm(x_ref, ya_ref, yc_ref, yr_ref, w_ref, g_ref)
    xo_ref[...] = xn
    h_ref[...] = h
    h_hi = _bf16(h)
    h_lo = _bf16(h - h_hi.astype(jnp.float32))
    hw = _dot(h_hi, wr_ref[...])
    logits = hw[:, 0:LANES] + hw[:, LANES:2 * LANES] + _dot(h_lo, wr_ref[:, 0:LANES]) + br_ref[...]
    lane = lax.broadcasted_iota(jnp.int32, logits.shape, 1)
    logits = jnp.where(lane < N_EXPERTS, logits, MASK_VALUE)
    m1 = jnp.max(logits, axis=-1, keepdims=True)
    i1 = jnp.min(jnp.where(logits == m1, lane, LANES), axis=-1, keepdims=True)
    rest = jnp.where(lane == i1, MASK_VALUE, logits)
    m2 = jnp.max(rest, axis=-1, keepdims=True)
    i2 = jnp.min(jnp.where(rest == m2, lane, LANES), axis=-1, keepdims=True)
    e2 = jnp.exp(m2 - m1)
    w1 = 1.0 / (1.0 + e2)
    w2 = e2 * w1
    hit = (lane == i1) | (lane == i2)
    onehot = jnp.where(hit, 1.0, 0.0)
    incl = _dot(tri_ref[...], _bf16(onehot))
    rank = cnt_s[...] + incl - onehot
    r1 = jnp.sum(jnp.where(lane == i1, rank, 0.0), axis=-1, keepdims=True)
    r2 = jnp.sum(jnp.where(lane == i2, rank, 0.0), axis=-1, keepdims=True)
    cnt_new = cnt_s[...] + incl[incl.shape[0] - 1:, :]
    cnt_s[...] = cnt_new
    cnt_ref[...] = jnp.broadcast_to(cnt_new, cnt_ref.shape)
    rec = jnp.zeros(logits.shape, jnp.float32)
    for ln, val in ((INFO_E1, i1.astype(jnp.float32)), (INFO_E2, i2.astype(jnp.float32)),
                    (INFO_R1, r1), (INFO_R2, r2), (INFO_W1, w1), (INFO_W2, w2)):
        rec = jnp.where(lane == ln, val, rec)
    info_ref[...] = rec
    info_t_ref[...] = rec.T[0:SUBLANES, :]


def _outproj(x2, ya, yc, yr, wo, g, tm, router=None):
    t, dm = x2.shape
    row = lambda w: pl.BlockSpec((tm, w), lambda i: (i, 0))
    full = lambda a: pl.BlockSpec(a.shape, lambda i: (0,) * a.ndim)
    params = pltpu.CompilerParams(dimension_semantics=("arbitrary",), vmem_limit_bytes=VMEM_LIMIT_BYTES)
    base_specs = [row(dm), row(ATTN_W), row(CONV_C), row(HGRN_W), full(wo), full(g)]
    if router is None:
        return pl.pallas_call(
            _outproj_dense_kernel,
            grid=(t // tm,),
            in_specs=base_specs,
            out_specs=(row(dm), row(dm)),
            out_shape=(jax.ShapeDtypeStruct((t, dm), jnp.float32), jax.ShapeDtypeStruct((t, dm), jnp.bfloat16)),
            compiler_params=params,
            name="outproj_dense",
        )(x2, ya, yc, yr, wo, g)
    wr, br, tri = router
    return pl.pallas_call(
        _outproj_routed_kernel,
        grid=(t // tm,),
        in_specs=base_specs + [full(wr), full(br), full(tri)],
        out_specs=(row(dm), row(dm), row(LANES), pl.BlockSpec((SUBLANES, tm), lambda i: (0, i)),
                   pl.BlockSpec((SUBLANES, LANES), lambda i: (0, 0))),
        out_shape=(jax.ShapeDtypeStruct((t, dm), jnp.float32), jax.ShapeDtypeStruct((t, dm), jnp.float32),
                   jax.ShapeDtypeStruct((t, LANES), jnp.float32),
                   jax.ShapeDtypeStruct((SUBLANES, t), jnp.float32),
                   jax.ShapeDtypeStruct((SUBLANES, LANES), jnp.float32)),
        scratch_shapes=[pltpu.VMEM((1, LANES), jnp.float32)],
        compiler_params=params,
        name="outproj_routed",
    )(x2, ya, yc, yr, wo, g, wr, br, tri)


def _swiglu_step(h_ref, wg_ref, wu_ref, wd_ref):
    h = _bf16(h_ref[...])
    gt = _dot(h, _bf16(wg_ref[...]))
    up = _dot(h, _bf16(wu_ref[...]))
    act = _bf16(gt * (1.0 / (1.0 + jnp.exp(-gt))) * up)
    return _dot(act, _bf16(wd_ref[...]))


def _ffn_dense_kernel(h_ref, x_ref, wg_ref, wu_ref, wd_ref, out_ref, acc):
    f = pl.program_id(1)

    @pl.when(f == 0)
    def _():
        acc[...] = x_ref[...]

    acc[...] += _swiglu_step(h_ref, wg_ref, wu_ref, wd_ref)

    @pl.when(f == pl.num_programs(1) - 1)
    def _():
        out_ref[...] = acc[...]


def _ffn_dense(h, x2, wg, wu, wd, tm, fc):
    t, dm = x2.shape
    dff = wg.shape[1]
    return pl.pallas_call(
        _ffn_dense_kernel,
        grid=(t // tm, dff // fc),
        in_specs=[
            pl.BlockSpec((tm, dm), lambda i, f: (i, 0)),
            pl.BlockSpec((tm, dm), lambda i, f: (i, 0)),
            pl.BlockSpec((dm, fc), lambda i, f: (0, f)),
            pl.BlockSpec((dm, fc), lambda i, f: (0, f)),
            pl.BlockSpec((fc, dm), lambda i, f: (f, 0)),
        ],
        out_specs=pl.BlockSpec((tm, dm), lambda i, f: (i, 0)),
        out_shape=jax.ShapeDtypeStruct((t, dm), jnp.float32),
        scratch_shapes=[pltpu.VMEM((tm, dm), jnp.float32)],
        compiler_params=pltpu.CompilerParams(dimension_semantics=("arbitrary", "arbitrary"),
                                             vmem_limit_bytes=VMEM_LIMIT_BYTES),
        name="swiglu_dense",
    )(h, x2, wg, wu, wd)


def _ffn_grouped_kernel(te_ref, ta_ref, h_ref, wg_ref, wu_ref, wd_ref, out_ref, acc):
    i = pl.program_id(0)
    f = pl.program_id(1)

    @pl.when(ta_ref[i] == 1)
    def _():
        @pl.when(f == 0)
        def _():
            acc[...] = jnp.zeros_like(acc)

        acc[...] += _swiglu_step(h_ref, wg_ref, wu_ref, wd_ref)

        @pl.when(f == pl.num_programs(1) - 1)
        def _():
            out_ref[...] = acc[...]

    @pl.when((ta_ref[i] == 0) & (f == pl.num_programs(1) - 1))
    def _():
        out_ref[...] = jnp.zeros_like(out_ref)


def _ffn_grouped(tile_expert, tile_active, xs, wg, wu, wd, tm, fc):
    dm = xs.shape[1]
    n_tiles = tile_expert.shape[0]
    r = n_tiles * tm
    dff = wg.shape[2]
    nf = dff // fc
    fidx = lambda i, f, ta: f * ta[i] + (nf - 1) * (1 - ta[i])
    grid_spec = pltpu.PrefetchScalarGridSpec(
        num_scalar_prefetch=2,
        grid=(n_tiles, nf),
        in_specs=[
            pl.BlockSpec((tm, dm), lambda i, f, te, ta: (i * ta[i], 0)),
            pl.BlockSpec((None, dm, fc), lambda i, f, te, ta: (te[i], 0, fidx(i, f, ta))),
            pl.BlockSpec((None, dm, fc), lambda i, f, te, ta: (te[i], 0, fidx(i, f, ta))),
            pl.BlockSpec((None, fc, dm), lambda i, f, te, ta: (te[i], fidx(i, f, ta), 0)),
        ],
        out_specs=pl.BlockSpec((tm, dm), lambda i, f, te, ta: (i, 0)),
        scratch_shapes=[pltpu.VMEM((tm, dm), jnp.float32)],
    )
    return pl.pallas_call(
        _ffn_grouped_kernel,
        grid_spec=grid_spec,
        out_shape=jax.ShapeDtypeStruct((r, dm), jnp.float32),
        compiler_params=pltpu.CompilerParams(dimension_semantics=("arbitrary", "arbitrary"),
                                             vmem_limit_bytes=VMEM_LIMIT_BYTES),
        name="swiglu_grouped",
    )(tile_expert, tile_active, xs, wg, wu, wd)


def _row_copy(src, dst, sem):
    return pltpu.make_async_copy(src, dst, sem)


MOE_ISSUE_UNROLL = 8


def _wait_rows(ref, n_rows, sem):
    blk = ref.at[pl.ds(0, n_rows), :]
    pltpu.make_async_copy(blk, blk, sem).wait()


MOE_STAGE_BUFS = 3


def _dispatch_kernel(p1_ref, p2_ref, pad_ref, h_ref, xs_ref, stage, zrow, load_sems, scat_sems, zero_sem,
                     *, tb, n_chunks):
    def load(c, slot):
        return pltpu.make_async_copy(h_ref.at[pl.ds(c * tb, tb), :], stage.at[slot], load_sems.at[slot])

    load(0, 0).start()

    zrow[...] = jnp.zeros_like(zrow)
    for e in range(N_EXPERTS):
        def zero(r, _, e=e):
            _row_copy(zrow.at[pl.ds(0, 1), :], xs_ref.at[pl.ds(pad_ref[e] + r, 1), :], zero_sem).start()
            return 0

        lax.fori_loop(0, pad_ref[N_EXPERTS + e], zero, 0)
    for e in range(N_EXPERTS):
        def zero_done(r, _):
            _wait_rows(xs_ref, 1, zero_sem)
            return 0

        lax.fori_loop(0, pad_ref[N_EXPERTS + e], zero_done, 0)

    def chunk(c, _):
        slot = lax.rem(c, MOE_STAGE_BUFS)
        load(c, slot).wait()

        @pl.when(c >= 2)
        def _():
            _wait_rows(xs_ref, 2 * tb, scat_sems.at[lax.rem(c - 2, MOE_STAGE_BUFS)])

        @pl.when(c + 1 < n_chunks)
        def _():
            load(c + 1, lax.rem(c + 1, MOE_STAGE_BUFS)).start()

        sem = scat_sems.at[slot]

        def issue(rr, _):
            for u in range(MOE_ISSUE_UNROLL):
                r = rr * MOE_ISSUE_UNROLL + u
                tok = c * tb + r
                src = stage.at[slot, pl.ds(r, 1), :]
                _row_copy(src, xs_ref.at[pl.ds(p1_ref[tok], 1), :], sem).start(priority=0)
                _row_copy(src, xs_ref.at[pl.ds(p2_ref[tok], 1), :], sem).start(priority=1)
            return 0

        lax.fori_loop(0, tb // MOE_ISSUE_UNROLL, issue, 0)
        return 0

    lax.fori_loop(0, n_chunks, chunk, 0)
    for c in range(max(n_chunks - 2, 0), n_chunks):
        _wait_rows(xs_ref, 2 * tb, scat_sems.at[c % MOE_STAGE_BUFS])


def _dispatch(pos1, pos2, pad_start, h, n_tiles, tb, tm):
    t, dm = h.shape
    n_chunks = t // tb
    grid_spec = pltpu.PrefetchScalarGridSpec(
        num_scalar_prefetch=3,
        grid=(1,),
        in_specs=[pl.BlockSpec(memory_space=pl.ANY)],
        out_specs=pl.BlockSpec(memory_space=pl.ANY),
        scratch_shapes=[pltpu.VMEM((MOE_STAGE_BUFS, tb, dm), h.dtype),
                        pltpu.VMEM((SUBLANES, dm), h.dtype),
                        pltpu.SemaphoreType.DMA((MOE_STAGE_BUFS,)),
                        pltpu.SemaphoreType.DMA((MOE_STAGE_BUFS,)),
                        pltpu.SemaphoreType.DMA(())],
    )
    return pl.pallas_call(
        functools.partial(_dispatch_kernel, tb=tb, n_chunks=n_chunks),
        grid_spec=grid_spec,
        out_shape=jax.ShapeDtypeStruct((n_tiles * tm, dm), h.dtype),
        compiler_params=pltpu.CompilerParams(dimension_semantics=("arbitrary",), has_side_effects=True),
        name="moe_dispatch",
    )(pos1, pos2, pad_start, h)


def _combine_kernel(p1_ref, p2_ref, y_ref, x_ref, info_ref, out_ref, buf_a, buf_b, sems, *, tb):
    i = pl.program_id(0)
    n = pl.num_programs(0)

    def issue_step(step, buf, sem):
        def issue(rr, _):
            for u in range(MOE_ISSUE_UNROLL):
                r = rr * MOE_ISSUE_UNROLL + u
                tok = step * tb + r
                _row_copy(y_ref.at[pl.ds(p1_ref[tok], 1), :], buf.at[0, pl.ds(r, 1), :], sem).start(priority=0)
                _row_copy(y_ref.at[pl.ds(p2_ref[tok], 1), :], buf.at[1, pl.ds(r, 1), :], sem).start(priority=1)
            return 0

        lax.fori_loop(0, tb // MOE_ISSUE_UNROLL, issue, 0)

    def finish(buf, sem):
        _wait_rows(y_ref, 2 * tb, sem)
        info = info_ref[...]
        lane = lax.broadcasted_iota(jnp.int32, info.shape, 1)
        w1 = jnp.sum(jnp.where(lane == INFO_W1, info, 0.0), axis=-1, keepdims=True)
        w2 = jnp.sum(jnp.where(lane == INFO_W2, info, 0.0), axis=-1, keepdims=True)
        out_ref[...] = x_ref[...] + w1 * buf[0] + w2 * buf[1]

    @pl.when(i == 0)
    def _():
        issue_step(0, buf_a, sems.at[0])

    @pl.when(i % 2 == 0)
    def _():
        @pl.when(i + 1 < n)
        def _():
            issue_step(i + 1, buf_b, sems.at[1])

        finish(buf_a, sems.at[0])

    @pl.when(i % 2 == 1)
    def _():
        @pl.when(i + 1 < n)
        def _():
            issue_step(i + 1, buf_a, sems.at[0])

        finish(buf_b, sems.at[1])


def _combine(pos1, pos2, y, x2, info, tb):
    t, dm = x2.shape
    grid_spec = pltpu.PrefetchScalarGridSpec(
        num_scalar_prefetch=2,
        grid=(t // tb,),
        in_specs=[pl.BlockSpec(memory_space=pl.ANY),
                  pl.BlockSpec((tb, dm), lambda i, p1, p2: (i, 0)),
                  pl.BlockSpec((tb, LANES), lambda i, p1, p2: (i, 0))],
        out_specs=pl.BlockSpec((tb, dm), lambda i, p1, p2: (i, 0)),
        scratch_shapes=[pltpu.VMEM((2, tb, dm), jnp.float32), pltpu.VMEM((2, tb, dm), jnp.float32),
                        pltpu.SemaphoreType.DMA((2,))],
    )
    return pl.pallas_call(
        functools.partial(_combine_kernel, tb=tb),
        grid_spec=grid_spec,
        out_shape=jax.ShapeDtypeStruct((t, dm), jnp.float32),
        compiler_params=pltpu.CompilerParams(dimension_semantics=("arbitrary",)),
        name="moe_combine",
    )(pos1, pos2, y, x2, info)


def _moe_routing_tables(info_t, counts, tm, n_tiles):
    cnt = counts[0, :N_EXPERTS].astype(jnp.int32)
    padded = ((cnt + tm - 1) // tm) * tm
    ends = jnp.cumsum(padded)
    offsets = ends - padded
    e1 = info_t[INFO_E1].astype(jnp.int32)
    e2 = info_t[INFO_E2].astype(jnp.int32)
    pos1 = offsets[e1] + info_t[INFO_R1].astype(jnp.int32)
    pos2 = offsets[e2] + info_t[INFO_R2].astype(jnp.int32)
    start = jnp.arange(n_tiles, dtype=jnp.int32) * tm
    tile_expert = jnp.minimum(jnp.sum((start[:, None] >= ends[None, :]).astype(jnp.int32), axis=1), N_EXPERTS - 1)
    tile_active = (start < ends[-1]).astype(jnp.int32)
    pad_rows = jnp.concatenate([offsets + cnt, padded - cnt])
    return pos1, pos2, pad_rows, tile_expert, tile_active


def _pack_w_in(w):
    s = np.cumsum([0, ATTN_W, ATTN_W, ATTN_W, ATTN_HEADS, CONV_C, CONV_C, CONV_C, HGRN_W, HGRN_W, HGRN_W, HGRN_W])
    seg = [w[:, s[n]:s[n + 1]] for n in range(11)]
    a_q, a_k, a_v, a_f, c_x, c_b, c_c, r_q, r_f, r_i, r_g = seg
    a_f = jnp.pad(a_f, ((0, 0), (0, LANES - ATTN_HEADS)))
    return _bf16(jnp.concatenate([a_q, a_k, a_v, c_x, c_b, c_c, r_q, r_f, r_i, r_g, a_f], axis=1))


def _selection_constants():
    selq = np.zeros((LANES, ATTN_W), np.float32)
    selk = np.zeros((LANES, ATTN_W), np.float32)
    oneq = np.zeros((1, ATTN_W), np.float32)
    onek = np.zeros((1, ATTN_W), np.float32)
    for hd in range(ATTN_HEADS):
        base = hd * HEAD_DIM
        for piece in range(3):
            selq[piece * ATTN_HEADS + hd, base + piece] = 1.0
            selk[piece * ATTN_HEADS + hd, base + 3 + piece] = -1.0
            oneq[0, base + 3 + piece] = 1.0
            onek[0, base + piece] = 1.0
    return (jnp.asarray(selq, jnp.bfloat16), jnp.asarray(selk, jnp.bfloat16),
            jnp.asarray(oneq), jnp.asarray(onek))


def _tile_gain(gain, mult):
    return jnp.tile(gain.astype(jnp.float32) * mult, ATTN_HEADS).reshape(1, ATTN_W)


def kernel(x, norm_mix, w_in, attn_f_bias, q_norm_gain, k_norm_gain, conv_w, hgrn_lb_logits, mix_out_gain, w_out,
           norm_ffn, ffn_w_gate, ffn_w_up, ffn_w_down, moe_router_w, moe_router_b, moe_w_gate, moe_w_up, moe_w_down):
    batch, seq, dm = x.shape
    depth = w_in.shape[0]
    t = batch * seq
    f32 = jnp.float32
    tm = min(512, seq)
    tq = min(512, seq)
    hg_rows = min(512, seq)
    tm_ffn = min(1024, t)
    tb_moe = min(256, t)
    dff = ffn_w_gate.shape[-1]
    fc = dff // 4

    p_lb = jax.nn.softmax(hgrn_lb_logits.astype(f32), axis=0)
    lb_all = jnp.cumsum(p_lb, axis=0) - p_lb[0]

    tri_m = _bf16(jnp.tril(jnp.ones((tm, tm), f32)))
    tri_c = _bf16(jnp.tril(jnp.ones((HGRN_CHUNK, HGRN_CHUNK), f32)))
    grp = np.arange(HGRN_W) // HEAD_DIM
    bd = jnp.asarray(grp[:, None] == grp[None, :], jnp.bfloat16)
    selq, selk, oneq, onek = _selection_constants()
    scale = 1.0 / math.sqrt(HEAD_DIM)

    x2 = x.reshape(t, dm)
    for l in range(depth):
        wp = _pack_w_in(w_in[l])
        fb = jnp.pad(attn_f_bias[l].astype(f32), (0, LANES - ATTN_HEADS)).reshape(1, LANES)
        gq = _tile_gain(q_norm_gain[l], scale * LOG2E)
        gk = _tile_gain(k_norm_gain[l], 1.0)
        mog = mix_out_gain[l].astype(f32)
        ga = jnp.broadcast_to(mog[:ATTN_W].reshape(ATTN_HEADS, HEAD_DIM, 1), (ATTN_HEADS, HEAD_DIM, tq))
        gc = mog[ATTN_W:ATTN_W + CONV_C].reshape(1, CONV_C)
        gr = mog[ATTN_W + CONV_C:].reshape(1, HGRN_W)

        q, qaug, k, kaug, v, yc, rq, rlf, rk, rv, rg = _inproj(
            x2, seq, norm_mix[l].astype(f32).reshape(1, dm), wp, tri_m, fb, gq, gk, selq, selk, oneq, onek,
            conv_w[l].astype(f32), gc, bd, lb_all[l].reshape(1, HGRN_W), tm)

        vt = v.reshape(batch, ATTN_HEADS, HEAD_DIM, seq)
        seq3 = lambda a: a.reshape(batch, seq, ATTN_W)
        ya = _attention(seq3(q), seq3(qaug), seq3(k), seq3(kaug), vt, ga, tq)
        yr = _hgrn(rq, rlf, rk, rv, rg, tri_c, bd, gr, batch, seq, hg_rows)

        j = l // 2
        wo = _bf16(w_out[l])
        gf = norm_ffn[l].astype(f32).reshape(1, dm)
        if l % 2 == 0:
            xo, h2 = _outproj(x2, ya.reshape(t, ATTN_W), yc, yr, wo, gf, tm)
            x2 = _ffn_dense(h2, xo, ffn_w_gate[j], ffn_w_up[j], ffn_w_down[j], tm_ffn, fc)
        else:
            wr32 = jnp.pad(moe_router_w[j].astype(f32), ((0, 0), (0, LANES - N_EXPERTS)))
            wr_hi = _bf16(wr32)
            wr = jnp.concatenate([wr_hi, _bf16(wr32 - wr_hi.astype(f32))], axis=1)
            br = jnp.pad(moe_router_b[j].astype(f32), (0, LANES - N_EXPERTS)).reshape(1, LANES)
            xo, h2, info, info_t, counts = _outproj(x2, ya.reshape(t, ATTN_W), yc, yr, wo, gf, tm,
                                                    router=(wr, br, tri_m))
            n_tiles = (2 * t) // tm_ffn + N_EXPERTS
            pos1, pos2, pad_rows, tile_expert, tile_active = _moe_routing_tables(info_t, counts, tm_ffn, n_tiles)
            xs = _dispatch(pos1, pos2, pad_rows, h2, n_tiles, tb_moe, tm_ffn)
            ys = _ffn_grouped(tile_expert, tile_active, xs,
                              moe_w_gate[j], moe_w_up[j], moe_w_down[j], tm_ffn, fc)
            x2 = _combine(pos1, pos2, ys, xo, info, tb_moe)
    return x2.reshape(batch, seq, dm)
```

```python
import functools
import math

import jax
import jax.numpy as jnp
import numpy as np
from jax import lax
from jax.experimental import pallas as pl
from jax.experimental.pallas import tpu as pltpu

HEAD_DIM = 64
ATTN_HEADS = 8
CONV_C = 256
HGRN_W = 256
ATTN_W = ATTN_HEADS * HEAD_DIM
HGRN_CHUNK = 64
HGRN_SUB = 16
N_EXPERTS = 8
EPS = 1e-6
MASK_VALUE = -1e30
MASK_LOG_DECAY = -1e4
TINY = 1e-30
LOG2E = math.log2(math.e)

LANES = 128
SUBLANES = 8
VMEM_LIMIT_BYTES = 56 * 1024 * 1024

PAIR_W = 2 * HEAD_DIM
N_PAIRS = ATTN_HEADS // 2
OFF_Q = 0
OFF_K = OFF_Q + ATTN_W
OFF_V = OFF_K + ATTN_W
OFF_CX = OFF_V + ATTN_W
OFF_CB = OFF_CX + CONV_C
OFF_CC = OFF_CB + CONV_C
OFF_RQ = OFF_CC + CONV_C
OFF_RF = OFF_RQ + HGRN_W
OFF_RI = OFF_RF + HGRN_W
OFF_RG = OFF_RI + HGRN_W
OFF_AF = OFF_RG + HGRN_W
D_PACK = OFF_AF + LANES


def _bf16(x):
    return x.astype(jnp.bfloat16)


def _split3(x):
    p1 = _bf16(x)
    r1 = x - p1.astype(jnp.float32)
    p2 = _bf16(r1)
    r2 = r1 - p2.astype(jnp.float32)
    return p1, p2, _bf16(r2)


def _dot(a, b):
    return jnp.dot(a, b, preferred_element_type=jnp.float32)


def _group_sum(x, bd):
    hi = _bf16(x)
    lo = _bf16(x - hi.astype(jnp.float32))
    return _dot(hi, bd) + _dot(lo, bd)


def _silu(x):
    return x * (1.0 / (1.0 + jnp.exp(-x)))


def _sigmoid(x):
    return 1.0 / (1.0 + jnp.exp(-x))


def _inproj_kernel(x_ref, g_ref, w_ref, tri_ref, fb_ref, gq_ref, gk_ref, selq_ref, selk_ref,
                   oneq_ref, onek_ref, convw_ref, gc_ref, bd_ref, lb_ref,
                   q_ref, qaug_ref, k_ref, kaug_ref, v_ref, yc_ref, rq_ref, rlf_ref, rk_ref, rv_ref, rg_ref,
                   dcarry, ucarry, *, tiles_per_seq):
    i = pl.program_id(0)

    @pl.when(i % tiles_per_seq == 0)
    def _():
        dcarry[...] = jnp.zeros_like(dcarry)
        ucarry[...] = jnp.zeros_like(ucarry)

    x = x_ref[...]
    h = x * lax.rsqrt(jnp.mean(x * x, axis=-1, keepdims=True) + EPS) * g_ref[...]
    hb = _bf16(h)

    zf = _dot(hb, w_ref[:, OFF_AF:OFF_AF + LANES]) + fb_ref[...]
    ls = jnp.minimum(zf, 0.0) - jnp.log(1.0 + jnp.exp(-jnp.abs(zf)))
    p1, p2, p3 = _split3(ls)
    loc = _dot(tri_ref[...], jnp.concatenate([p1, p2, p3], axis=1))
    d = dcarry[...] + loc[:, 0:LANES] + loc[:, LANES:2 * LANES] + loc[:, 2 * LANES:3 * LANES]
    dcarry[...] = d[d.shape[0] - 1:, :]
    e1, e2, e3 = (e.astype(jnp.float32) for e in _split3(d * LOG2E))
    lane = lax.broadcasted_iota(jnp.int32, e1.shape, 1)
    epack = jnp.where(lane < ATTN_HEADS, e1,
                      jnp.where(lane < 2 * ATTN_HEADS, pltpu.roll(e2, ATTN_HEADS, 1),
                                jnp.where(lane < 3 * ATTN_HEADS, pltpu.roll(e3, 2 * ATTN_HEADS, 1), 0.0)))
    epack = _bf16(epack)
    qaug_ref[...] = _bf16(_dot(epack, selq_ref[...]) + oneq_ref[...])
    kaug_ref[...] = _bf16(_dot(epack, selk_ref[...]) + onek_ref[...])

    zq = _dot(hb, w_ref[:, OFF_Q:OFF_Q + ATTN_W])
    zk = _dot(hb, w_ref[:, OFF_K:OFF_K + ATTN_W])
    first = lax.broadcasted_iota(jnp.int32, (x.shape[0], PAIR_W), 1) < HEAD_DIM

    def head_norm(z):
        sq = z * z
        s0 = jnp.sum(jnp.where(first, sq, 0.0), axis=-1, keepdims=True)
        s1 = jnp.sum(jnp.where(first, 0.0, sq), axis=-1, keepdims=True)
        return z * lax.rsqrt(jnp.where(first, s0, s1) * (1.0 / HEAD_DIM) + EPS)

    for pb in range(N_PAIRS):
        sl = slice(pb * PAIR_W, (pb + 1) * PAIR_W)
        q_ref[:, sl] = _bf16(head_norm(zq[:, sl]) * gq_ref[:, sl])
        k_ref[:, sl] = _bf16(head_norm(zk[:, sl]) * gk_ref[:, sl])
    v_ref[...] = _bf16(_dot(hb, w_ref[:, OFF_V:OFF_V + ATTN_W])).T

    cx = _dot(hb, w_ref[:, OFF_CX:OFF_CX + CONV_C])
    cb = _dot(hb, w_ref[:, OFF_CB:OFF_CB + CONV_C])
    cc = _dot(hb, w_ref[:, OFF_CC:OFF_CC + CONV_C])
    u = cc * cx
    uc = ucarry[...]
    row8 = lax.broadcasted_iota(jnp.int32, (SUBLANES, CONV_C), 0)
    r1 = pltpu.roll(u, 1, 0)
    r2 = pltpu.roll(u, 2, 0)
    top1 = jnp.where(row8 < 1, pltpu.roll(uc, 1, 0), r1[0:SUBLANES])
    top2 = jnp.where(row8 < 2, pltpu.roll(uc, 2, 0), r2[0:SUBLANES])
    u1 = jnp.concatenate([top1, r1[SUBLANES:]], axis=0)
    u2 = jnp.concatenate([top2, r2[SUBLANES:]], axis=0)
    ucarry[...] = u[u.shape[0] - SUBLANES:, :]
    yc = cb * (u2 * convw_ref[0:1, :] + u1 * convw_ref[1:2, :] + u * convw_ref[2:3, :])
    ssc = _group_sum(yc * yc, bd_ref[...])
    yc_ref[...] = _bf16(yc * lax.rsqrt(ssc * (1.0 / HEAD_DIM) + EPS) * gc_ref[...])

    lb = lb_ref[...]
    zr = _dot(hb, w_ref[:, OFF_RF:OFF_RF + HGRN_W])
    sg = _sigmoid(zr)
    f = lb + (1.0 - lb) * sg
    rlf_ref[...] = jnp.log(jnp.maximum(f, TINY))
    rk_ref[...] = (1.0 - lb) * _sigmoid(-zr)
    rq_ref[...] = _silu(_dot(hb, w_ref[:, OFF_RQ:OFF_RQ + HGRN_W]))
    rv_ref[...] = _dot(hb, w_ref[:, OFF_RI:OFF_RI + HGRN_W])
    rg_ref[...] = _silu(_dot(hb, w_ref[:, OFF_RG:OFF_RG + HGRN_W]))


def _inproj(x2, seq, g, wp, tri, fb, gq, gk, selq, selk, oneq, onek, convw, gc, bd, lb, tm):
    t, dm = x2.shape
    full = lambda a: pl.BlockSpec(a.shape, lambda i: (0,) * a.ndim)
    row = lambda w: pl.BlockSpec((tm, w), lambda i: (i, 0))
    consts = (g, wp, tri, fb, gq, gk, selq, selk, oneq, onek, convw, gc, bd, lb)
    tiles_per_seq = seq // tm
    out_shape = tuple(jax.ShapeDtypeStruct((t, ATTN_W), jnp.bfloat16) for _ in range(4)) + (
        jax.ShapeDtypeStruct((t // seq, ATTN_W, seq), jnp.bfloat16),
        jax.ShapeDtypeStruct((t, CONV_C), jnp.bfloat16),
    ) + tuple(jax.ShapeDtypeStruct((t, HGRN_W), jnp.float32) for _ in range(5))
    vt_spec = pl.BlockSpec((None, ATTN_W, tm), lambda i: (i // tiles_per_seq, 0, i % tiles_per_seq))
    out_specs = (tuple(row(ATTN_W) for _ in range(4)) + (vt_spec, row(CONV_C))
                 + tuple(row(HGRN_W) for _ in range(5)))
    return pl.pallas_call(
        functools.partial(_inproj_kernel, tiles_per_seq=seq // tm),
        grid=(t // tm,),
        in_specs=[row(dm)] + [full(a) for a in consts],
        out_specs=out_specs,
        out_shape=out_shape,
        scratch_shapes=[pltpu.VMEM((1, LANES), jnp.float32), pltpu.VMEM((SUBLANES, CONV_C), jnp.float32)],
        compiler_params=pltpu.CompilerParams(dimension_semantics=("arbitrary",),
                                             vmem_limit_bytes=VMEM_LIMIT_BYTES),
        name="inproj",
    )(x2, *consts)


def _attn_kernel(q_ref, qaug_ref, k_ref, kaug_ref, vt_ref, gain_ref, out_ref, m_s, acc_s, sa_ref, sb_ref,
                 mxa_ref, mxb_ref, qt_s, *, tq):
    s_a, s_b = (sa_ref, mxa_ref), (sb_ref, mxb_ref)
    i = pl.program_id(2)
    ones_rows = jnp.ones((2 * SUBLANES, tq), jnp.bfloat16)
    m_s[...] = jnp.full_like(m_s, MASK_VALUE)
    acc_s[...] = jnp.zeros_like(acc_s)

    lane = lax.broadcasted_iota(jnp.int32, (tq, PAIR_W), 1)
    qf = q_ref[...].astype(jnp.float32)
    af = qaug_ref[...].astype(jnp.float32)
    for hh in range(2):
        mine = (lane < HEAD_DIM) if hh == 0 else (lane >= HEAD_DIM)
        cat = jnp.concatenate([jnp.where(mine, qf, 0.0), jnp.where(mine, af, 0.0)], axis=1)
        qt_s[hh] = _bf16(cat.T)

    def scores(j, s_buf):
        k0 = pl.multiple_of(j * tq, tq)
        krows = jnp.concatenate([k_ref[pl.ds(k0, tq), :], kaug_ref[pl.ds(k0, tq), :]], axis=1)
        for hh in range(2):
            s = _dot(krows, qt_s[hh])
            s_buf[0][hh] = s
            s_buf[1][hh] = jnp.max(s, axis=0, keepdims=True)

    def consume(j, s_buf, masked):
        k0 = pl.multiple_of(j * tq, tq)
        for hh in range(2):
            s = s_buf[0][hh]
            if masked:
                kpos = lax.broadcasted_iota(jnp.int32, (tq, tq), 0)
                qpos = lax.broadcasted_iota(jnp.int32, (tq, tq), 1)
                s = jnp.where(kpos <= qpos, s, MASK_VALUE)
                blk_max = jnp.max(s, axis=0, keepdims=True)
            else:
                blk_max = s_buf[1][hh]
            m = m_s[hh]
            m_new = jnp.maximum(m, blk_max)
            p = jnp.exp2(s - m_new)
            alpha = jnp.exp2(m - m_new)
            vaug = jnp.concatenate([vt_ref[hh, :, pl.ds(k0, tq)], ones_rows], axis=0)
            acc_s[hh] = acc_s[hh] * alpha + _dot(vaug, _bf16(p))
            m_s[hh] = m_new

    scores(0, s_a)

    def pair(j):
        scores(j + 1, s_b)
        consume(j, s_a, False)
        scores(j + 2, s_a)
        consume(j + 1, s_b, False)

    def body(jj, _):
        pair(4 * jj)
        pair(4 * jj + 2)
        return 0

    lax.fori_loop(0, i // 4, body, 0)
    done = 4 * (i // 4)

    @pl.when(i - done >= 2)
    def _():
        pair(done)

    @pl.when(i % 2 == 1)
    def _():
        scores(i, s_b)
        consume(i - 1, s_a, False)
        consume(i, s_b, True)

    @pl.when(i % 2 == 0)
    def _():
        consume(i, s_a, True)

    ys = []
    for hh in range(2):
        acc = acc_s[hh]
        o = acc[0:HEAD_DIM] * (1.0 / acc[HEAD_DIM:HEAD_DIM + 1])
        ms = jnp.mean(o * o, axis=0, keepdims=True)
        ys.append(o * lax.rsqrt(ms + EPS) * gain_ref[hh])
    out_ref[...] = _bf16(jnp.concatenate(ys, axis=0).T)


def _attention(q, qaug, k, kaug, vt, gain, tq):
    b, s, _ = q.shape
    q_spec = pl.BlockSpec((None, tq, PAIR_W), lambda bi, hp, i: (bi, i, hp))
    k_spec = pl.BlockSpec((None, s, PAIR_W), lambda bi, hp, i: (bi, 0, hp))
    return pl.pallas_call(
        functools.partial(_attn_kernel, tq=tq),
        grid=(b, N_PAIRS, s // tq),
        in_specs=[
            q_spec, q_spec, k_spec, k_spec,
            pl.BlockSpec((None, 2, HEAD_DIM, s), lambda bi, hp, i: (bi, hp, 0, 0)),
            pl.BlockSpec((2, HEAD_DIM, tq), lambda bi, hp, i: (hp, 0, 0)),
        ],
        out_specs=pl.BlockSpec((None, tq, 2 * HEAD_DIM), lambda bi, hp, i: (bi, i, hp)),
        out_shape=jax.ShapeDtypeStruct((b, s, ATTN_W), jnp.bfloat16),
        scratch_shapes=[pltpu.VMEM((2, 1, tq), jnp.float32),
                        pltpu.VMEM((2, HEAD_DIM + 2 * SUBLANES, tq), jnp.float32),
                        pltpu.VMEM((2, tq, tq), jnp.float32),
                        pltpu.VMEM((2, tq, tq), jnp.float32),
                        pltpu.VMEM((2, 1, tq), jnp.float32),
                        pltpu.VMEM((2, 1, tq), jnp.float32),
                        pltpu.VMEM((2, 2 * PAIR_W, tq), jnp.bfloat16)],
        compiler_params=pltpu.CompilerParams(dimension_semantics=("arbitrary", "arbitrary", "arbitrary"),
                                             vmem_limit_bytes=VMEM_LIMIT_BYTES),
        name="fox_attention",
    )(q, qaug, k, kaug, vt, gain)


def _hgrn_kernel(q_ref, lf_ref, k_ref, v_ref, g_ref, tri_ref, bd_ref, gain_ref, out_ref,
                 state, c_all, k_all, v_all, *, n_chunks):
    @pl.when(pl.program_id(1) == 0)
    def _():
        state[...] = jnp.zeros_like(state)

    ch = HGRN_CHUNK
    sb = HGRN_SUB
    nsb = ch // sb
    row_sb = lax.broadcasted_iota(jnp.int32, (sb, HGRN_W), 0)
    r128 = lax.broadcasted_iota(jnp.int32, (LANES, LANES), 0)
    c128 = lax.broadcasted_iota(jnp.int32, (LANES, LANES), 1)
    same_head = (r128 < HEAD_DIM) == (c128 < HEAD_DIM)
    t64 = lax.broadcasted_iota(jnp.int32, (ch, LANES), 0) // sb
    s64 = (lax.broadcasted_iota(jnp.int32, (ch, LANES), 1) % HEAD_DIM) // sb
    level2 = ((t64 == 1) & (s64 == 0)) | ((t64 == 3) & (s64 == 2))
    lane_head0 = lax.broadcasted_iota(jnp.int32, (ch, LANES), 1) < HEAD_DIM
    bd = bd_ref[...]
    zeros_sb = jnp.zeros((sb, HGRN_W), jnp.float32)

    def chunk(ci):
        r0 = ci * ch
        c_s, k_s, v_s = c_all.at[ci % 2], k_all.at[ci % 2], v_all.at[ci % 2]
        q = q_ref[pl.ds(r0, ch), :]
        k = k_ref[pl.ds(r0, ch), :]
        v = v_ref[pl.ds(r0, ch), :]
        p1, p2, p3 = _split3(lf_ref[pl.ds(r0, ch), :])
        cc = _dot(tri_ref[...], jnp.concatenate([p1, p2, p3], axis=1))
        c = cc[:, 0:HGRN_W] + cc[:, HGRN_W:2 * HGRN_W] + cc[:, 2 * HGRN_W:3 * HGRN_W]
        c_s[...] = c
        k_s[...] = k
        v_s[...] = v
        blk = lambda a, n: a[n * sb:(n + 1) * sb]

        ps = []
        for n in range(nsb):
            cn, qn = blk(c, n), blk(q, n)
            for s in range(sb):
                r = n * sb + s
                dec = jnp.exp(jnp.where(row_sb >= s, cn - c_s[r:r + 1, :], MASK_LOG_DECAY))
                ps.append(_bf16(qn * k_s[r:r + 1, :] * dec))
        a_d = _dot(jnp.concatenate(ps, axis=0), bd)
        o_parts = []
        for n in range(nsb):
            acc = jnp.zeros((sb, HGRN_W), jnp.float32)
            for s in range(sb):
                r = n * sb + s
                acc = acc + a_d[r * sb:(r + 1) * sb] * v_s[r:r + 1, :]
            o_parts.append(acc)
        o = jnp.concatenate(o_parts, axis=0)

        ref1 = c[2 * sb - 1:2 * sb]
        ref2a = c[sb - 1:sb]
        ref2b = c[3 * sb - 1:3 * sb]
        hi, lo = slice(2 * sb, 4 * sb), slice(0, 2 * sb)
        zeros_half = jnp.zeros((2 * sb, HGRN_W), jnp.float32)
        q1 = jnp.concatenate([zeros_half, q[hi] * jnp.exp(c[hi] - ref1)], axis=0)
        k1 = jnp.concatenate([k[lo] * jnp.exp(ref1 - c[lo]), zeros_half], axis=0)
        q2 = jnp.concatenate([zeros_sb, blk(q, 1) * jnp.exp(blk(c, 1) - ref2a),
                              zeros_sb, blk(q, 3) * jnp.exp(blk(c, 3) - ref2b)], axis=0)
        k2 = jnp.concatenate([blk(k, 0) * jnp.exp(ref2a - blk(c, 0)), zeros_sb,
                              blk(k, 2) * jnp.exp(ref2b - blk(c, 2)), zeros_sb], axis=0)

        c_last = c[ch - 1:ch, :]
        qe = q * jnp.exp(c)
        kd = k * jnp.exp(c_last - c)
        e_last = jnp.exp(c_last)
        nt = (((1,), (1,)), ((), ()))
        o_off = []
        for bb in range(HGRN_W // LANES):
            sl = slice(bb * LANES, (bb + 1) * LANES)
            st = state[bb]
            stack2 = lambda a: jnp.concatenate([jnp.where(lane_head0, a[:, sl], 0.0),
                                                jnp.where(lane_head0, 0.0, a[:, sl])], axis=0)
            a1 = lax.dot_general(_bf16(q1[:, sl]), _bf16(stack2(k1)), nt, preferred_element_type=jnp.float32)
            a2 = lax.dot_general(_bf16(q2[:, sl]), _bf16(stack2(k2)), nt, preferred_element_type=jnp.float32)
            a_off = a1 + jnp.where(level2, a2, 0.0)
            lhs = jnp.concatenate([_bf16(a_off), _bf16(qe[:, sl])], axis=1)
            rhs = jnp.concatenate([_bf16(stack2(v)), _bf16(st)], axis=0)
            o_off.append(_dot(lhs, rhs))
            ecol = jnp.broadcast_to(e_last[:, sl], (LANES, LANES)).T
            upd = lax.dot_general(_bf16(kd[:, sl]), _bf16(v[:, sl]), (((0,), (0,)), ((), ())),
                                  preferred_element_type=jnp.float32)
            state[bb] = jnp.where(same_head, ecol * st + upd, 0.0)
        o = o + jnp.concatenate(o_off, axis=1)

        ss = _group_sum(o * o, bd)
        y = o * lax.rsqrt(ss * (1.0 / HEAD_DIM) + EPS) * gain_ref[...] * g_ref[pl.ds(r0, ch), :]
        out_ref[pl.ds(r0, ch), :] = _bf16(y)

    for ci in range(n_chunks):
        chunk(ci)


def _hgrn(rq, rlf, rk, rv, rg, tri, bd, gain, batch, seq, rows):
    t = rq.shape[0]
    steps = seq // rows
    blk = pl.BlockSpec((rows, HGRN_W), lambda b, j: (b * steps + j, 0))
    full = lambda a: pl.BlockSpec(a.shape, lambda b, j: (0,) * a.ndim)
    return pl.pallas_call(
        functools.partial(_hgrn_kernel, n_chunks=rows // HGRN_CHUNK),
        grid=(batch, steps),
        in_specs=[blk, blk, blk, blk, blk, full(tri), full(bd), full(gain)],
        out_specs=blk,
        out_shape=jax.ShapeDtypeStruct((t, HGRN_W), jnp.bfloat16),
        scratch_shapes=[pltpu.VMEM((HGRN_W // LANES, LANES, LANES), jnp.float32)]
        + [pltpu.VMEM((2, HGRN_CHUNK, HGRN_W), jnp.float32) for _ in range(3)],
        compiler_params=pltpu.CompilerParams(dimension_semantics=("arbitrary", "arbitrary"),
                                             vmem_limit_bytes=VMEM_LIMIT_BYTES),
        name="hgrn2",
    )(rq, rlf, rk, rv, rg, tri, bd, gain)


INFO_E1, INFO_E2, INFO_R1, INFO_R2, INFO_W1, INFO_W2 = range(6)


def _mix_and_norm(x_ref, ya_ref, yc_ref, yr_ref, w_ref, g_ref):
    xn = (x_ref[...]
          + _dot(ya_ref[...], w_ref[0:ATTN_W, :])
          + _dot(yc_ref[...], w_ref[ATTN_W:ATTN_W + CONV_C, :])
          + _dot(yr_ref[...], w_ref[ATTN_W + CONV_C:, :]))
    h = xn * lax.rsqrt(jnp.mean(xn * xn, axis=-1, keepdims=True) + EPS) * g_ref[...]
    return xn, h


def _outproj_dense_kernel(x_ref, ya_ref, yc_ref, yr_ref, w_ref, g_ref, xo_ref, h_ref):
    xn, h = _mix_and_norm(x_ref, ya_ref, yc_ref, yr_ref, w_ref, g_ref)
    xo_ref[...] = xn
    h_ref[...] = _bf16(h)


def _outproj_routed_kernel(x_ref, ya_ref, yc_ref, yr_ref, w_ref, g_ref, wr_ref, br_ref, tri_ref,
                           xo_ref, h_ref, info_ref, info_t_ref, cnt_ref, cnt_s):
    @pl.when(pl.program_id(0) == 0)
    def _():
        cnt_s[...] = jnp.zeros_like(cnt_s)

    xn, h = _mix_and_norm(x_ref, ya_ref, yc_ref, yr_ref, w_ref, g_ref)
    xo_ref[...] = xn
    h_ref[...] = h
    h_hi = _bf16(h)
    h_lo = _bf16(h - h_hi.astype(jnp.float32))
    hw = _dot(h_hi, wr_ref[...])
    logits = hw[:, 0:LANES] + hw[:, LANES:2 * LANES] + _dot(h_lo, wr_ref[:, 0:LANES]) + br_ref[...]
    lane = lax.broadcasted_iota(jnp.int32, logits.shape, 1)
    logits = jnp.where(lane < N_EXPERTS, logits, MASK_VALUE)
    m1 = jnp.max(logits, axis=-1, keepdims=True)
    i1 = jnp.min(jnp.where(logits == m1, lane, LANES), axis=-1, keepdims=True)
    rest = jnp.where(lane == i1, MASK_VALUE, logits)
    m2 = jnp.max(rest, axis=-1, keepdims=True)
    i2 = jnp.min(jnp.where(rest == m2, lane, LANES), axis=-1, keepdims=True)
    e2 = jnp.exp(m2 - m1)
    w1 = 1.0 / (1.0 + e2)
    w2 = e2 * w1
    hit = (lane == i1) | (lane == i2)
    onehot = jnp.where(hit, 1.0, 0.0)
    incl = _dot(tri_ref[...], _bf16(onehot))
    rank = cnt_s[...] + incl - onehot
    r1 = jnp.sum(jnp.where(lane == i1, rank, 0.0), axis=-1, keepdims=True)
    r2 = jnp.sum(jnp.where(lane == i2, rank, 0.0), axis=-1, keepdims=True)
    cnt_new = cnt_s[...] + incl[incl.shape[0] - 1:, :]
    cnt_s[...] = cnt_new
    cnt_ref[...] = jnp.broadcast_to(cnt_new, cnt_ref.shape)
    rec = jnp.zeros(logits.shape, jnp.float32)
    for ln, val in ((INFO_E1, i1.astype(jnp.float32)), (INFO_E2, i2.astype(jnp.float32)),
                    (INFO_R1, r1), (INFO_R2, r2), (INFO_W1, w1), (INFO_W2, w2)):
        rec = jnp.where(lane == ln, val, rec)
    info_ref[...] = rec
    info_t_ref[...] = rec.T[0:SUBLANES, :]


def _outproj(x2, ya, yc, yr, wo, g, tm, router=None):
    t, dm = x2.shape
    row = lambda w: pl.BlockSpec((tm, w), lambda i: (i, 0))
    full = lambda a: pl.BlockSpec(a.shape, lambda i: (0,) * a.ndim)
    params = pltpu.CompilerParams(dimension_semantics=("arbitrary",), vmem_limit_bytes=VMEM_LIMIT_BYTES)
    base_specs = [row(dm), row(ATTN_W), row(CONV_C), row(HGRN_W), full(wo), full(g)]
    if router is None:
        return pl.pallas_call(
            _outproj_dense_kernel,
            grid=(t // tm,),
            in_specs=base_specs,
            out_specs=(row(dm), row(dm)),
            out_shape=(jax.ShapeDtypeStruct((t, dm), jnp.float32), jax.ShapeDtypeStruct((t, dm), jnp.bfloat16)),
            compiler_params=params,
            name="outproj_dense",
        )(x2, ya, yc, yr, wo, g)
    wr, br, tri = router
    return pl.pallas_call(
        _outproj_routed_kernel,
        grid=(t // tm,),
        in_specs=base_specs + [full(wr), full(br), full(tri)],
        out_specs=(row(dm), row(dm), row(LANES), pl.BlockSpec((SUBLANES, tm), lambda i: (0, i)),
                   pl.BlockSpec((SUBLANES, LANES), lambda i: (0, 0))),
        out_shape=(jax.ShapeDtypeStruct((t, dm), jnp.float32), jax.ShapeDtypeStruct((t, dm), jnp.float32),
                   jax.ShapeDtypeStruct((t, LANES), jnp.float32),
                   jax.ShapeDtypeStruct((SUBLANES, t), jnp.float32),
                   jax.ShapeDtypeStruct((SUBLANES, LANES), jnp.float32)),
        scratch_shapes=[pltpu.VMEM((1, LANES), jnp.float32)],
        compiler_params=params,
        name="outproj_routed",
    )(x2, ya, yc, yr, wo, g, wr, br, tri)


def _swiglu_step(h_ref, wg_ref, wu_ref, wd_ref):
    h = _bf16(h_ref[...])
    gt = _dot(h, _bf16(wg_ref[...]))
    up = _dot(h, _bf16(wu_ref[...]))
    act = _bf16(gt * (1.0 / (1.0 + jnp.exp(-gt))) * up)
    return _dot(act, _bf16(wd_ref[...]))


def _ffn_dense_kernel(h_ref, x_ref, wg_ref, wu_ref, wd_ref, out_ref, acc):
    f = pl.program_id(1)

    @pl.when(f == 0)
    def _():
        acc[...] = x_ref[...]

    acc[...] += _swiglu_step(h_ref, wg_ref, wu_ref, wd_ref)

    @pl.when(f == pl.num_programs(1) - 1)
    def _():
        out_ref[...] = acc[...]


def _ffn_dense(h, x2, wg, wu, wd, tm, fc):
    t, dm = x2.shape
    dff = wg.shape[1]
    return pl.pallas_call(
        _ffn_dense_kernel,
        grid=(t // tm, dff // fc),
        in_specs=[
            pl.BlockSpec((tm, dm), lambda i, f: (i, 0)),
            pl.BlockSpec((tm, dm), lambda i, f: (i, 0)),
            pl.BlockSpec((dm, fc), lambda i, f: (0, f)),
            pl.BlockSpec((dm, fc), lambda i, f: (0, f)),
            pl.BlockSpec((fc, dm), lambda i, f: (f, 0)),
        ],
        out_specs=pl.BlockSpec((tm, dm), lambda i, f: (i, 0)),
        out_shape=jax.ShapeDtypeStruct((t, dm), jnp.float32),
        scratch_shapes=[pltpu.VMEM((tm, dm), jnp.float32)],
        compiler_params=pltpu.CompilerParams(dimension_semantics=("arbitrary", "arbitrary"),
                                             vmem_limit_bytes=VMEM_LIMIT_BYTES),
        name="swiglu_dense",
    )(h, x2, wg, wu, wd)


def _ffn_grouped_kernel(te_ref, ta_ref, h_ref, wg_ref, wu_ref, wd_ref, out_ref, acc):
    i = pl.program_id(0)
    f = pl.program_id(1)

    @pl.when(ta_ref[i] == 1)
    def _():
        @pl.when(f == 0)
        def _():
            acc[...] = jnp.zeros_like(acc)

        acc[...] += _swiglu_step(h_ref, wg_ref, wu_ref, wd_ref)

        @pl.when(f == pl.num_programs(1) - 1)
        def _():
            out_ref[...] = acc[...]

    @pl.when((ta_ref[i] == 0) & (f == pl.num_programs(1) - 1))
    def _():
        out_ref[...] = jnp.zeros_like(out_ref)


def _ffn_grouped(tile_expert, tile_active, xs, wg, wu, wd, tm, fc):
    dm = xs.shape[1]
    n_tiles = tile_expert.shape[0]
    r = n_tiles * tm
    dff = wg.shape[2]
    nf = dff // fc
    fidx = lambda i, f, ta: f * ta[i] + (nf - 1) * (1 - ta[i])
    grid_spec = pltpu.PrefetchScalarGridSpec(
        num_scalar_prefetch=2,
        grid=(n_tiles, nf),
        in_specs=[
            pl.BlockSpec((tm, dm), lambda i, f, te, ta: (i * ta[i], 0)),
            pl.BlockSpec((None, dm, fc), lambda i, f, te, ta: (te[i], 0, fidx(i, f, ta))),
            pl.BlockSpec((None, dm, fc), lambda i, f, te, ta: (te[i], 0, fidx(i, f, ta))),
            pl.BlockSpec((None, fc, dm), lambda i, f, te, ta: (te[i], fidx(i, f, ta), 0)),
        ],
        out_specs=pl.BlockSpec((tm, dm), lambda i, f, te, ta: (i, 0)),
        scratch_shapes=[pltpu.VMEM((tm, dm), jnp.float32)],
    )
    return pl.pallas_call(
        _ffn_grouped_kernel,
        grid_spec=grid_spec,
        out_shape=jax.ShapeDtypeStruct((r, dm), jnp.float32),
        compiler_params=pltpu.CompilerParams(dimension_semantics=("arbitrary", "arbitrary"),
                                             vmem_limit_bytes=VMEM_LIMIT_BYTES),
        name="swiglu_grouped",
    )(tile_expert, tile_active, xs, wg, wu, wd)


def _row_copy(src, dst, sem):
    return pltpu.make_async_copy(src, dst, sem)


MOE_ISSUE_UNROLL = 8


def _wait_rows(ref, n_rows, sem):
    blk = ref.at[pl.ds(0, n_rows), :]
    pltpu.make_async_copy(blk, blk, sem).wait()


MOE_STAGE_BUFS = 3


def _dispatch_kernel(p1_ref, p2_ref, pad_ref, h_ref, xs_ref, stage, zrow, load_sems, scat_sems, zero_sem,
                     *, tb, n_chunks):
    def load(c, slot):
        return pltpu.make_async_copy(h_ref.at[pl.ds(c * tb, tb), :], stage.at[slot], load_sems.at[slot])

    load(0, 0).start()

    zrow[...] = jnp.zeros_like(zrow)
    for e in range(N_EXPERTS):
        def zero(r, _, e=e):
            _row_copy(zrow.at[pl.ds(0, 1), :], xs_ref.at[pl.ds(pad_ref[e] + r, 1), :], zero_sem).start()
            return 0

        lax.fori_loop(0, pad_ref[N_EXPERTS + e], zero, 0)
    for e in range(N_EXPERTS):
        def zero_done(r, _):
            _wait_rows(xs_ref, 1, zero_sem)
            return 0

        lax.fori_loop(0, pad_ref[N_EXPERTS + e], zero_done, 0)

    def chunk(c, _):
        slot = lax.rem(c, MOE_STAGE_BUFS)
        load(c, slot).wait()

        @pl.when(c >= 2)
        def _():
            _wait_rows(xs_ref, 2 * tb, scat_sems.at[lax.rem(c - 2, MOE_STAGE_BUFS)])

        @pl.when(c + 1 < n_chunks)
        def _():
            load(c + 1, lax.rem(c + 1, MOE_STAGE_BUFS)).start()

        sem = scat_sems.at[slot]

        def issue(rr, _):
            for u in range(MOE_ISSUE_UNROLL):
                r = rr * MOE_ISSUE_UNROLL + u
                tok = c * tb + r
                src = stage.at[slot, pl.ds(r, 1), :]
                _row_copy(src, xs_ref.at[pl.ds(p1_ref[tok], 1), :], sem).start(priority=0)
                _row_copy(src, xs_ref.at[pl.ds(p2_ref[tok], 1), :], sem).start(priority=1)
            return 0

        lax.fori_loop(0, tb // MOE_ISSUE_UNROLL, issue, 0)
        return 0

    lax.fori_loop(0, n_chunks, chunk, 0)
    for c in range(max(n_chunks - 2, 0), n_chunks):
        _wait_rows(xs_ref, 2 * tb, scat_sems.at[c % MOE_STAGE_BUFS])


def _dispatch(pos1, pos2, pad_start, h, n_tiles, tb, tm):
    t, dm = h.shape
    n_chunks = t // tb
    grid_spec = pltpu.PrefetchScalarGridSpec(
        num_scalar_prefetch=3,
        grid=(1,),
        in_specs=[pl.BlockSpec(memory_space=pl.ANY)],
        out_specs=pl.BlockSpec(memory_space=pl.ANY),
        scratch_shapes=[pltpu.VMEM((MOE_STAGE_BUFS, tb, dm), h.dtype),
                        pltpu.VMEM((SUBLANES, dm), h.dtype),
                        pltpu.SemaphoreType.DMA((MOE_STAGE_BUFS,)),
                        pltpu.SemaphoreType.DMA((MOE_STAGE_BUFS,)),
                        pltpu.SemaphoreType.DMA(())],
    )
    return pl.pallas_call(
        functools.partial(_dispatch_kernel, tb=tb, n_chunks=n_chunks),
        grid_spec=grid_spec,
        out_shape=jax.ShapeDtypeStruct((n_tiles * tm, dm), h.dtype),
        compiler_params=pltpu.CompilerParams(dimension_semantics=("arbitrary",), has_side_effects=True),
        name="moe_dispatch",
    )(pos1, pos2, pad_start, h)


def _combine_kernel(p1_ref, p2_ref, y_ref, x_ref, info_ref, out_ref, buf_a, buf_b, sems, *, tb):
    i = pl.program_id(0)
    n = pl.num_programs(0)

    def issue_step(step, buf, sem):
        def issue(rr, _):
            for u in range(MOE_ISSUE_UNROLL):
                r = rr * MOE_ISSUE_UNROLL + u
                tok = step * tb + r
                _row_copy(y_ref.at[pl.ds(p1_ref[tok], 1), :], buf.at[0, pl.ds(r, 1), :], sem).start(priority=0)
                _row_copy(y_ref.at[pl.ds(p2_ref[tok], 1), :], buf.at[1, pl.ds(r, 1), :], sem).start(priority=1)
            return 0

        lax.fori_loop(0, tb // MOE_ISSUE_UNROLL, issue, 0)

    def finish(buf, sem):
        _wait_rows(y_ref, 2 * tb, sem)
        info = info_ref[...]
        lane = lax.broadcasted_iota(jnp.int32, info.shape, 1)
        w1 = jnp.sum(jnp.where(lane == INFO_W1, info, 0.0), axis=-1, keepdims=True)
        w2 = jnp.sum(jnp.where(lane == INFO_W2, info, 0.0), axis=-1, keepdims=True)
        out_ref[...] = x_ref[...] + w1 * buf[0] + w2 * buf[1]

    @pl.when(i == 0)
    def _():
        issue_step(0, buf_a, sems.at[0])

    @pl.when(i % 2 == 0)
    def _():
        @pl.when(i + 1 < n)
        def _():
            issue_step(i + 1, buf_b, sems.at[1])

        finish(buf_a, sems.at[0])

    @pl.when(i % 2 == 1)
    def _():
        @pl.when(i + 1 < n)
        def _():
            issue_step(i + 1, buf_a, sems.at[0])

        finish(buf_b, sems.at[1])


def _combine(pos1, pos2, y, x2, info, tb):
    t, dm = x2.shape
    grid_spec = pltpu.PrefetchScalarGridSpec(
        num_scalar_prefetch=2,
        grid=(t // tb,),
        in_specs=[pl.BlockSpec(memory_space=pl.ANY),
                  pl.BlockSpec((tb, dm), lambda i, p1, p2: (i, 0)),
                  pl.BlockSpec((tb, LANES), lambda i, p1, p2: (i, 0))],
        out_specs=pl.BlockSpec((tb, dm), lambda i, p1, p2: (i, 0)),
        scratch_shapes=[pltpu.VMEM((2, tb, dm), jnp.float32), pltpu.VMEM((2, tb, dm), jnp.float32),
                        pltpu.SemaphoreType.DMA((2,))],
    )
    return pl.pallas_call(
        functools.partial(_combine_kernel, tb=tb),
        grid_spec=grid_spec,
        out_shape=jax.ShapeDtypeStruct((t, dm), jnp.float32),
        compiler_params=pltpu.CompilerParams(dimension_semantics=("arbitrary",)),
        name="moe_combine",
    )(pos1, pos2, y, x2, info)


def _moe_routing_tables(info_t, counts, tm, n_tiles):
    cnt = counts[0, :N_EXPERTS].astype(jnp.int32)
    padded = ((cnt + tm - 1) // tm) * tm
    ends = jnp.cumsum(padded)
    offsets = ends - padded
    e1 = info_t[INFO_E1].astype(jnp.int32)
    e2 = info_t[INFO_E2].astype(jnp.int32)
    pos1 = offsets[e1] + info_t[INFO_R1].astype(jnp.int32)
    pos2 = offsets[e2] + info_t[INFO_R2].astype(jnp.int32)
    start = jnp.arange(n_tiles, dtype=jnp.int32) * tm
    tile_expert = jnp.minimum(jnp.sum((start[:, None] >= ends[None, :]).astype(jnp.int32), axis=1), N_EXPERTS - 1)
    tile_active = (start < ends[-1]).astype(jnp.int32)
    pad_rows = jnp.concatenate([offsets + cnt, padded - cnt])
    return pos1, pos2, pad_rows, tile_expert, tile_active


def _pack_w_in(w):
    s = np.cumsum([0, ATTN_W, ATTN_W, ATTN_W, ATTN_HEADS, CONV_C, CONV_C, CONV_C, HGRN_W, HGRN_W, HGRN_W, HGRN_W])
    seg = [w[:, s[n]:s[n + 1]] for n in range(11)]
    a_q, a_k, a_v, a_f, c_x, c_b, c_c, r_q, r_f, r_i, r_g = seg
    a_f = jnp.pad(a_f, ((0, 0), (0, LANES - ATTN_HEADS)))
    return _bf16(jnp.concatenate([a_q, a_k, a_v, c_x, c_b, c_c, r_q, r_f, r_i, r_g, a_f], axis=1))


def _selection_constants():
    selq = np.zeros((LANES, ATTN_W), np.float32)
    selk = np.zeros((LANES, ATTN_W), np.float32)
    oneq = np.zeros((1, ATTN_W), np.float32)
    onek = np.zeros((1, ATTN_W), np.float32)
    for hd in range(ATTN_HEADS):
        base = hd * HEAD_DIM
        for piece in range(3):
            selq[piece * ATTN_HEADS + hd, base + piece] = 1.0
            selk[piece * ATTN_HEADS + hd, base + 3 + piece] = -1.0
            oneq[0, base + 3 + piece] = 1.0
            onek[0, base + piece] = 1.0
    return (jnp.asarray(selq, jnp.bfloat16), jnp.asarray(selk, jnp.bfloat16),
            jnp.asarray(oneq), jnp.asarray(onek))


def _tile_gain(gain, mult):
    return jnp.tile(gain.astype(jnp.float32) * mult, ATTN_HEADS).reshape(1, ATTN_W)


def kernel(x, norm_mix, w_in, attn_f_bias, q_norm_gain, k_norm_gain, conv_w, hgrn_lb_logits, mix_out_gain, w_out,
           norm_ffn, ffn_w_gate, ffn_w_up, ffn_w_down, moe_router_w, moe_router_b, moe_w_gate, moe_w_up, moe_w_down):
    batch, seq, dm = x.shape
    depth = w_in.shape[0]
    t = batch * seq
    f32 = jnp.float32
    tm = min(512, seq)
    tq = min(512, seq)
    hg_rows = min(512, seq)
    tm_ffn = min(1024, t)
    tb_moe = min(256, t)
    dff = ffn_w_gate.shape[-1]
    fc = dff // 4

    p_lb = jax.nn.softmax(hgrn_lb_logits.astype(f32), axis=0)
    lb_all = jnp.cumsum(p_lb, axis=0) - p_lb[0]

    tri_m = _bf16(jnp.tril(jnp.ones((tm, tm), f32)))
    tri_c = _bf16(jnp.tril(jnp.ones((HGRN_CHUNK, HGRN_CHUNK), f32)))
    grp = np.arange(HGRN_W) // HEAD_DIM
    bd = jnp.asarray(grp[:, None] == grp[None, :], jnp.bfloat16)
    selq, selk, oneq, onek = _selection_constants()
    scale = 1.0 / math.sqrt(HEAD_DIM)

    x2 = x.reshape(t, dm)
    for l in range(depth):
        wp = _pack_w_in(w_in[l])
        fb = jnp.pad(attn_f_bias[l].astype(f32), (0, LANES - ATTN_HEADS)).reshape(1, LANES)
        gq = _tile_gain(q_norm_gain[l], scale * LOG2E)
        gk = _tile_gain(k_norm_gain[l], 1.0)
        mog = mix_out_gain[l].astype(f32)
        ga = jnp.broadcast_to(mog[:ATTN_W].reshape(ATTN_HEADS, HEAD_DIM, 1), (ATTN_HEADS, HEAD_DIM, tq))
        gc = mog[ATTN_W:ATTN_W + CONV_C].reshape(1, CONV_C)
        gr = mog[ATTN_W + CONV_C:].reshape(1, HGRN_W)

        q, qaug, k, kaug, v, yc, rq, rlf, rk, rv, rg = _inproj(
            x2, seq, norm_mix[l].astype(f32).reshape(1, dm), wp, tri_m, fb, gq, gk, selq, selk, oneq, onek,
            conv_w[l].astype(f32), gc, bd, lb_all[l].reshape(1, HGRN_W), tm)

        vt = v.reshape(batch, ATTN_HEADS, HEAD_DIM, seq)
        seq3 = lambda a: a.reshape(batch, seq, ATTN_W)
        ya = _attention(seq3(q), seq3(qaug), seq3(k), seq3(kaug), vt, ga, tq)
        yr = _hgrn(rq, rlf, rk, rv, rg, tri_c, bd, gr, batch, seq, hg_rows)

        j = l // 2
        wo = _bf16(w_out[l])
        gf = norm_ffn[l].astype(f32).reshape(1, dm)
        if l % 2 == 0:
            xo, h2 = _outproj(x2, ya.reshape(t, ATTN_W), yc, yr, wo, gf, tm)
            x2 = _ffn_dense(h2, xo, ffn_w_gate[j], ffn_w_up[j], ffn_w_down[j], tm_ffn, fc)
        else:
            wr32 = jnp.pad(moe_router_w[j].astype(f32), ((0, 0), (0, LANES - N_EXPERTS)))
            wr_hi = _bf16(wr32)
            wr = jnp.concatenate([wr_hi, _bf16(wr32 - wr_hi.astype(f32))], axis=1)
            br = jnp.pad(moe_router_b[j].astype(f32), (0, LANES - N_EXPERTS)).reshape(1, LANES)
            xo, h2, info, info_t, counts = _outproj(x2, ya.reshape(t, ATTN_W), yc, yr, wo, gf, tm,
                                                    router=(wr, br, tri_m))
            n_tiles = (2 * t) // tm_ffn + N_EXPERTS
            pos1, pos2, pad_rows, tile_expert, tile_active = _moe_routing_tables(info_t, counts, tm_ffn, n_tiles)
            xs = _dispatch(pos1, pos2, pad_rows, h2, n_tiles, tb_moe, tm_ffn)
            ys = _ffn_grouped(tile_expert, tile_active, xs,
                              moe_w_gate[j], moe_w_up[j], moe_w_down[j], tm_ffn, fc)
            x2 = _combine(pos1, pos2, ys, xo, info, tb_moe)
    return x2.reshape(batch, seq, dm)
```

```python
import functools
import math

import jax
import jax.numpy as jnp
import numpy as np
from jax import lax
from jax.experimental import pallas as pl
from jax.experimental.pallas import tpu as pltpu

HEAD_DIM = 64
ATTN_HEADS = 8
CONV_C = 256
HGRN_W = 256
ATTN_W = ATTN_HEADS * HEAD_DIM
HGRN_CHUNK = 64
HGRN_SUB = 16
N_EXPERTS = 8
EPS = 1e-6
MASK_VALUE = -1e30
MASK_LOG_DECAY = -1e4
TINY = 1e-30
LOG2E = math.log2(math.e)

LANES = 128
SUBLANES = 8
VMEM_LIMIT_BYTES = 56 * 1024 * 1024

PAIR_W = 2 * HEAD_DIM
N_PAIRS = ATTN_HEADS // 2
OFF_Q = 0
OFF_K = OFF_Q + ATTN_W
OFF_V = OFF_K + ATTN_W
OFF_CX = OFF_V + ATTN_W
OFF_CB = OFF_CX + CONV_C
OFF_CC = OFF_CB + CONV_C
OFF_RQ = OFF_CC + CONV_C
OFF_RF = OFF_RQ + HGRN_W
OFF_RI = OFF_RF + HGRN_W
OFF_RG = OFF_RI + HGRN_W
OFF_AF = OFF_RG + HGRN_W
D_PACK = OFF_AF + LANES


def _bf16(x):
    return x.astype(jnp.bfloat16)


def _split3(x):
    p1 = _bf16(x)
    r1 = x - p1.astype(jnp.float32)
    p2 = _bf16(r1)
    r2 = r1 - p2.astype(jnp.float32)
    return p1, p2, _bf16(r2)


def _dot(a, b):
    return jnp.dot(a, b, preferred_element_type=jnp.float32)


def _group_sum(x, bd):
    hi = _bf16(x)
    lo = _bf16(x - hi.astype(jnp.float32))
    return _dot(hi, bd) + _dot(lo, bd)


def _silu(x):
    return x * (1.0 / (1.0 + jnp.exp(-x)))


def _sigmoid(x):
    return 1.0 / (1.0 + jnp.exp(-x))


def _inproj_kernel(x_ref, g_ref, w_ref, tri_ref, fb_ref, gq_ref, gk_ref, selq_ref, selk_ref,
                   oneq_ref, onek_ref, convw_ref, gc_ref, bd_ref, lb_ref,
                   q_ref, qaug_ref, k_ref, kaug_ref, v_ref, yc_ref, rq_ref, rlf_ref, rk_ref, rv_ref, rg_ref,
                   dcarry, ucarry, *, tiles_per_seq):
    i = pl.program_id(0)

    @pl.when(i % tiles_per_seq == 0)
    def _():
        dcarry[...] = jnp.zeros_like(dcarry)
        ucarry[...] = jnp.zeros_like(ucarry)

    x = x_ref[...]
    h = x * lax.rsqrt(jnp.mean(x * x, axis=-1, keepdims=True) + EPS) * g_ref[...]
    hb = _bf16(h)

    zf = _dot(hb, w_ref[:, OFF_AF:OFF_AF + LANES]) + fb_ref[...]
    ls = jnp.minimum(zf, 0.0) - jnp.log(1.0 + jnp.exp(-jnp.abs(zf)))
    p1, p2, p3 = _split3(ls)
    loc = _dot(tri_ref[...], jnp.concatenate([p1, p2, p3], axis=1))
    d = dcarry[...] + loc[:, 0:LANES] + loc[:, LANES:2 * LANES] + loc[:, 2 * LANES:3 * LANES]
    dcarry[...] = d[d.shape[0] - 1:, :]
    e1, e2, e3 = (e.astype(jnp.float32) for e in _split3(d * LOG2E))
    lane = lax.broadcasted_iota(jnp.int32, e1.shape, 1)
    epack = jnp.where(lane < ATTN_HEADS, e1,
                      jnp.where(lane < 2 * ATTN_HEADS, pltpu.roll(e2, ATTN_HEADS, 1),
                                jnp.where(lane < 3 * ATTN_HEADS, pltpu.roll(e3, 2 * ATTN_HEADS, 1), 0.0)))
    epack = _bf16(epack)
    qaug_ref[...] = _bf16(_dot(epack, selq_ref[...]) + oneq_ref[...]).T
    kaug_ref[...] = _bf16(_dot(epack, selk_ref[...]) + onek_ref[...])

    zq = _dot(hb, w_ref[:, OFF_Q:OFF_Q + ATTN_W])
    zk = _dot(hb, w_ref[:, OFF_K:OFF_K + ATTN_W])
    first = lax.broadcasted_iota(jnp.int32, (x.shape[0], PAIR_W), 1) < HEAD_DIM

    def head_norm(z):
        sq = z * z
        s0 = jnp.sum(jnp.where(first, sq, 0.0), axis=-1, keepdims=True)
        s1 = jnp.sum(jnp.where(first, 0.0, sq), axis=-1, keepdims=True)
        return z * lax.rsqrt(jnp.where(first, s0, s1) * (1.0 / HEAD_DIM) + EPS)

    qn = []
    for pb in range(N_PAIRS):
        sl = slice(pb * PAIR_W, (pb + 1) * PAIR_W)
        qn.append(_bf16(head_norm(zq[:, sl]) * gq_ref[:, sl]))
        k_ref[:, sl] = _bf16(head_norm(zk[:, sl]) * gk_ref[:, sl])
    q_ref[...] = jnp.concatenate(qn, axis=1).T
    v_ref[...] = _bf16(_dot(hb, w_ref[:, OFF_V:OFF_V + ATTN_W])).T

    cx = _dot(hb, w_ref[:, OFF_CX:OFF_CX + CONV_C])
    cb = _dot(hb, w_ref[:, OFF_CB:OFF_CB + CONV_C])
    cc = _dot(hb, w_ref[:, OFF_CC:OFF_CC + CONV_C])
    u = cc * cx
    uc = ucarry[...]
    row8 = lax.broadcasted_iota(jnp.int32, (SUBLANES, CONV_C), 0)
    r1 = pltpu.roll(u, 1, 0)
    r2 = pltpu.roll(u, 2, 0)
    top1 = jnp.where(row8 < 1, pltpu.roll(uc, 1, 0), r1[0:SUBLANES])
    top2 = jnp.where(row8 < 2, pltpu.roll(uc, 2, 0), r2[0:SUBLANES])
    u1 = jnp.concatenate([top1, r1[SUBLANES:]], axis=0)
    u2 = jnp.concatenate([top2, r2[SUBLANES:]], axis=0)
    ucarry[...] = u[u.shape[0] - SUBLANES:, :]
    yc = cb * (u2 * convw_ref[0:1, :] + u1 * convw_ref[1:2, :] + u * convw_ref[2:3, :])
    ssc = _group_sum(yc * yc, bd_ref[...])
    yc_ref[...] = _bf16(yc * lax.rsqrt(ssc * (1.0 / HEAD_DIM) + EPS) * gc_ref[...])

    lb = lb_ref[...]
    zr = _dot(hb, w_ref[:, OFF_RF:OFF_RF + HGRN_W])
    sg = _sigmoid(zr)
    f = lb + (1.0 - lb) * sg
    rlf_ref[...] = jnp.log(jnp.maximum(f, TINY))
    rk_ref[...] = (1.0 - lb) * _sigmoid(-zr)
    rq_ref[...] = _silu(_dot(hb, w_ref[:, OFF_RQ:OFF_RQ + HGRN_W]))
    rv_ref[...] = _dot(hb, w_ref[:, OFF_RI:OFF_RI + HGRN_W])
    rg_ref[...] = _silu(_dot(hb, w_ref[:, OFF_RG:OFF_RG + HGRN_W]))


def _inproj(x2, seq, g, wp, tri, fb, gq, gk, selq, selk, oneq, onek, convw, gc, bd, lb, tm):
    t, dm = x2.shape
    full = lambda a: pl.BlockSpec(a.shape, lambda i: (0,) * a.ndim)
    row = lambda w: pl.BlockSpec((tm, w), lambda i: (i, 0))
    consts = (g, wp, tri, fb, gq, gk, selq, selk, oneq, onek, convw, gc, bd, lb)
    tiles_per_seq = seq // tm
    rows_bf16 = jax.ShapeDtypeStruct((t, ATTN_W), jnp.bfloat16)
    cols_bf16 = jax.ShapeDtypeStruct((t // seq, ATTN_W, seq), jnp.bfloat16)
    out_shape = (cols_bf16, cols_bf16, rows_bf16, rows_bf16, cols_bf16,
                 jax.ShapeDtypeStruct((t, CONV_C), jnp.bfloat16),
                 ) + tuple(jax.ShapeDtypeStruct((t, HGRN_W), jnp.float32) for _ in range(5))
    t_spec = pl.BlockSpec((None, ATTN_W, tm), lambda i: (i // tiles_per_seq, 0, i % tiles_per_seq))
    out_specs = ((t_spec, t_spec, row(ATTN_W), row(ATTN_W), t_spec, row(CONV_C))
                 + tuple(row(HGRN_W) for _ in range(5)))
    return pl.pallas_call(
        functools.partial(_inproj_kernel, tiles_per_seq=seq // tm),
        grid=(t // tm,),
        in_specs=[row(dm)] + [full(a) for a in consts],
        out_specs=out_specs,
        out_shape=out_shape,
        scratch_shapes=[pltpu.VMEM((1, LANES), jnp.float32), pltpu.VMEM((SUBLANES, CONV_C), jnp.float32)],
        compiler_params=pltpu.CompilerParams(dimension_semantics=("arbitrary",),
                                             vmem_limit_bytes=VMEM_LIMIT_BYTES),
        name="inproj",
    )(x2, *consts)


def _attn_kernel(q_ref, qaug_ref, k_ref, kaug_ref, vt_ref, gain_ref, out_ref, m_s, acc_s, sa_ref, sb_ref,
                 mxa_ref, mxb_ref, qt_s, *, tq):
    s_a, s_b = (sa_ref, mxa_ref), (sb_ref, mxb_ref)
    i = pl.program_id(2)
    ones_rows = jnp.ones((2 * SUBLANES, tq), jnp.bfloat16)
    m_s[...] = jnp.full_like(m_s, MASK_VALUE)
    acc_s[...] = jnp.zeros_like(acc_s)

    chan = lax.broadcasted_iota(jnp.int32, (PAIR_W, tq), 0)
    zero = jnp.zeros((PAIR_W, tq), jnp.bfloat16)
    for hh in range(2):
        mine = (chan < HEAD_DIM) if hh == 0 else (chan >= HEAD_DIM)
        qt_s[hh] = jnp.concatenate([jnp.where(mine, q_ref[...], zero), jnp.where(mine, qaug_ref[...], zero)],
                                   axis=0)

    def scores(j, s_buf):
        k0 = pl.multiple_of(j * tq, tq)
        krows = jnp.concatenate([k_ref[pl.ds(k0, tq), :], kaug_ref[pl.ds(k0, tq), :]], axis=1)
        for hh in range(2):
            s = _dot(krows, qt_s[hh])
            s_buf[0][hh] = s
            s_buf[1][hh] = jnp.max(s, axis=0, keepdims=True)

    def consume(j, s_buf, masked):
        k0 = pl.multiple_of(j * tq, tq)
        for hh in range(2):
            s = s_buf[0][hh]
            if masked:
                kpos = lax.broadcasted_iota(jnp.int32, (tq, tq), 0)
                qpos = lax.broadcasted_iota(jnp.int32, (tq, tq), 1)
                s = jnp.where(kpos <= qpos, s, MASK_VALUE)
                blk_max = jnp.max(s, axis=0, keepdims=True)
            else:
                blk_max = s_buf[1][hh]
            m = m_s[hh]
            m_new = jnp.maximum(m, blk_max)
            p = jnp.exp2(s - m_new)
            alpha = jnp.exp2(m - m_new)
            vaug = jnp.concatenate([vt_ref[hh, :, pl.ds(k0, tq)], ones_rows], axis=0)
            acc_s[hh] = acc_s[hh] * alpha + _dot(vaug, _bf16(p))
            m_s[hh] = m_new

    scores(0, s_a)

    def pair(j):
        scores(j + 1, s_b)
        consume(j, s_a, False)
        scores(j + 2, s_a)
        consume(j + 1, s_b, False)

    def body(jj, _):
        pair(4 * jj)
        pair(4 * jj + 2)
        return 0

    lax.fori_loop(0, i // 4, body, 0)
    done = 4 * (i // 4)

    @pl.when(i - done >= 2)
    def _():
        pair(done)

    @pl.when(i % 2 == 1)
    def _():
        scores(i, s_b)
        consume(i - 1, s_a, False)
        consume(i, s_b, True)

    @pl.when(i % 2 == 0)
    def _():
        consume(i, s_a, True)

    ys = []
    for hh in range(2):
        acc = acc_s[hh]
        o = acc[0:HEAD_DIM] * (1.0 / acc[HEAD_DIM:HEAD_DIM + 1])
        ms = jnp.mean(o * o, axis=0, keepdims=True)
        ys.append(o * lax.rsqrt(ms + EPS) * gain_ref[hh])
    out_ref[...] = _bf16(jnp.concatenate(ys, axis=0).T)


def _attention(q, qaug, k, kaug, vt, gain, tq):
    b, s, _ = k.shape
    q_spec = pl.BlockSpec((None, PAIR_W, tq), lambda bi, hp, i: (bi, hp, i))
    k_spec = pl.BlockSpec((None, s, PAIR_W), lambda bi, hp, i: (bi, 0, hp))
    return pl.pallas_call(
        functools.partial(_attn_kernel, tq=tq),
        grid=(b, N_PAIRS, s // tq),
        in_specs=[
            q_spec, q_spec, k_spec, k_spec,
            pl.BlockSpec((None, 2, HEAD_DIM, s), lambda bi, hp, i: (bi, hp, 0, 0)),
            pl.BlockSpec((2, HEAD_DIM, tq), lambda bi, hp, i: (hp, 0, 0)),
        ],
        out_specs=pl.BlockSpec((None, tq, 2 * HEAD_DIM), lambda bi, hp, i: (bi, i, hp)),
        out_shape=jax.ShapeDtypeStruct((b, s, ATTN_W), jnp.bfloat16),
        scratch_shapes=[pltpu.VMEM((2, 1, tq), jnp.float32),
                        pltpu.VMEM((2, HEAD_DIM + 2 * SUBLANES, tq), jnp.float32),
                        pltpu.VMEM((2, tq, tq), jnp.float32),
                        pltpu.VMEM((2, tq, tq), jnp.float32),
                        pltpu.VMEM((2, 1, tq), jnp.float32),
                        pltpu.VMEM((2, 1, tq), jnp.float32),
                        pltpu.VMEM((2, 2 * PAIR_W, tq), jnp.bfloat16)],
        compiler_params=pltpu.CompilerParams(dimension_semantics=("arbitrary", "arbitrary", "arbitrary"),
                                             vmem_limit_bytes=VMEM_LIMIT_BYTES),
        name="fox_attention",
    )(q, qaug, k, kaug, vt, gain)


def _hgrn_kernel(q_ref, lf_ref, k_ref, v_ref, g_ref, tri_ref, bd_ref, gain_ref, out_ref,
                 state, c_all, k_all, v_all, *, n_chunks):
    @pl.when(pl.program_id(1) == 0)
    def _():
        state[...] = jnp.zeros_like(state)

    ch = HGRN_CHUNK
    sb = HGRN_SUB
    nsb = ch // sb
    row_sb = lax.broadcasted_iota(jnp.int32, (sb, HGRN_W), 0)
    r128 = lax.broadcasted_iota(jnp.int32, (LANES, LANES), 0)
    c128 = lax.broadcasted_iota(jnp.int32, (LANES, LANES), 1)
    same_head = (r128 < HEAD_DIM) == (c128 < HEAD_DIM)
    t64 = lax.broadcasted_iota(jnp.int32, (ch, LANES), 0) // sb
    s64 = (lax.broadcasted_iota(jnp.int32, (ch, LANES), 1) % HEAD_DIM) // sb
    level2 = ((t64 == 1) & (s64 == 0)) | ((t64 == 3) & (s64 == 2))
    lane_head0 = lax.broadcasted_iota(jnp.int32, (ch, LANES), 1) < HEAD_DIM
    bd = bd_ref[...]
    zeros_sb = jnp.zeros((sb, HGRN_W), jnp.float32)

    def chunk(ci):
        r0 = ci * ch
        c_s, k_s, v_s = c_all.at[ci % 2], k_all.at[ci % 2], v_all.at[ci % 2]
        q = q_ref[pl.ds(r0, ch), :]
        k = k_ref[pl.ds(r0, ch), :]
        v = v_ref[pl.ds(r0, ch), :]
        p1, p2, p3 = _split3(lf_ref[pl.ds(r0, ch), :])
        cc = _dot(tri_ref[...], jnp.concatenate([p1, p2, p3], axis=1))
        c = cc[:, 0:HGRN_W] + cc[:, HGRN_W:2 * HGRN_W] + cc[:, 2 * HGRN_W:3 * HGRN_W]
        c_s[...] = c
        k_s[...] = k
        v_s[...] = v
        blk = lambda a, n: a[n * sb:(n + 1) * sb]

        ps = []
        for n in range(nsb):
            cn, qn = blk(c, n), blk(q, n)
            for s in range(sb):
                r = n * sb + s
                dec = jnp.exp(jnp.where(row_sb >= s, cn - c_s[r:r + 1, :], MASK_LOG_DECAY))
                ps.append(_bf16(qn * k_s[r:r + 1, :] * dec))
        a_d = _dot(jnp.concatenate(ps, axis=0), bd)
        o_parts = []
        for n in range(nsb):
            acc = jnp.zeros((sb, HGRN_W), jnp.float32)
            for s in range(sb):
                r = n * sb + s
                acc = acc + a_d[r * sb:(r + 1) * sb] * v_s[r:r + 1, :]
            o_parts.append(acc)
        o = jnp.concatenate(o_parts, axis=0)

        ref1 = c[2 * sb - 1:2 * sb]
        ref2a = c[sb - 1:sb]
        ref2b = c[3 * sb - 1:3 * sb]
        hi, lo = slice(2 * sb, 4 * sb), slice(0, 2 * sb)
        zeros_half = jnp.zeros((2 * sb, HGRN_W), jnp.float32)
        q1 = jnp.concatenate([zeros_half, q[hi] * jnp.exp(c[hi] - ref1)], axis=0)
        k1 = jnp.concatenate([k[lo] * jnp.exp(ref1 - c[lo]), zeros_half], axis=0)
        q2 = jnp.concatenate([zeros_sb, blk(q, 1) * jnp.exp(blk(c, 1) - ref2a),
                              zeros_sb, blk(q, 3) * jnp.exp(blk(c, 3) - ref2b)], axis=0)
        k2 = jnp.concatenate([blk(k, 0) * jnp.exp(ref2a - blk(c, 0)), zeros_sb,
                              blk(k, 2) * jnp.exp(ref2b - blk(c, 2)), zeros_sb], axis=0)

        c_last = c[ch - 1:ch, :]
        qe = q * jnp.exp(c)
        kd = k * jnp.exp(c_last - c)
        e_last = jnp.exp(c_last)
        nt = (((1,), (1,)), ((), ()))
        o_off = []
        for bb in range(HGRN_W // LANES):
            sl = slice(bb * LANES, (bb + 1) * LANES)
            st = state[bb]
            stack2 = lambda a: jnp.concatenate([jnp.where(lane_head0, a[:, sl], 0.0),
                                                jnp.where(lane_head0, 0.0, a[:, sl])], axis=0)
            a1 = lax.dot_general(_bf16(q1[:, sl]), _bf16(stack2(k1)), nt, preferred_element_type=jnp.float32)
            a2 = lax.dot_general(_bf16(q2[:, sl]), _bf16(stack2(k2)), nt, preferred_element_type=jnp.float32)
            a_off = a1 + jnp.where(level2, a2, 0.0)
            lhs = jnp.concatenate([_bf16(a_off), _bf16(qe[:, sl])], axis=1)
            rhs = jnp.concatenate([_bf16(stack2(v)), _bf16(st)], axis=0)
            o_off.append(_dot(lhs, rhs))
            ecol = jnp.broadcast_to(e_last[:, sl], (LANES, LANES)).T
            upd = lax.dot_general(_bf16(kd[:, sl]), _bf16(v[:, sl]), (((0,), (0,)), ((), ())),
                                  preferred_element_type=jnp.float32)
            state[bb] = jnp.where(same_head, ecol * st + upd, 0.0)
        o = o + jnp.concatenate(o_off, axis=1)

        ss = _group_sum(o * o, bd)
        y = o * lax.rsqrt(ss * (1.0 / HEAD_DIM) + EPS) * gain_ref[...] * g_ref[pl.ds(r0, ch), :]
        out_ref[pl.ds(r0, ch), :] = _bf16(y)

    for ci in range(n_chunks):
        chunk(ci)


def _hgrn(rq, rlf, rk, rv, rg, tri, bd, gain, batch, seq, rows):
    t = rq.shape[0]
    steps = seq // rows
    blk = pl.BlockSpec((rows, HGRN_W), lambda b, j: (b * steps + j, 0))
    full = lambda a: pl.BlockSpec(a.shape, lambda b, j: (0,) * a.ndim)
    return pl.pallas_call(
        functools.partial(_hgrn_kernel, n_chunks=rows // HGRN_CHUNK),
        grid=(batch, steps),
        in_specs=[blk, blk, blk, blk, blk, full(tri), full(bd), full(gain)],
        out_specs=blk,
        out_shape=jax.ShapeDtypeStruct((t, HGRN_W), jnp.bfloat16),
        scratch_shapes=[pltpu.VMEM((HGRN_W // LANES, LANES, LANES), jnp.float32)]
        + [pltpu.VMEM((2, HGRN_CHUNK, HGRN_W), jnp.float32) for _ in range(3)],
        compiler_params=pltpu.CompilerParams(dimension_semantics=("arbitrary", "arbitrary"),
                                             vmem_limit_bytes=VMEM_LIMIT_BYTES),
        name="hgrn2",
    )(rq, rlf, rk, rv, rg, tri, bd, gain)


INFO_E1, INFO_E2, INFO_R1, INFO_R2, INFO_W1, INFO_W2 = range(6)


def _mix_and_norm(x_ref, ya_ref, yc_ref, yr_ref, w_ref, g_ref):
    xn = (x_ref[...]
          + _dot(ya_ref[...], w_ref[0:ATTN_W, :])
          + _dot(yc_ref[...], w_ref[ATTN_W:ATTN_W + CONV_C, :])
          + _dot(yr_ref[...], w_ref[ATTN_W + CONV_C:, :]))
    h = xn * lax.rsqrt(jnp.mean(xn * xn, axis=-1, keepdims=True) + EPS) * g_ref[...]
    return xn, h


def _outproj_dense_kernel(x_ref, ya_ref, yc_ref, yr_ref, w_ref, g_ref, xo_ref, h_ref):
    xn, h = _mix_and_norm(x_ref, ya_ref, yc_ref, yr_ref, w_ref, g_ref)
    xo_ref[...] = xn
    h_ref[...] = _bf16(h)


def _outproj_routed_kernel(x_ref, ya_ref, yc_ref, yr_ref, w_ref, g_ref, wr_ref, br_ref, tri_ref,
                           xo_ref, h_ref, info_ref, info_t_ref, cnt_ref, cnt_s):
    @pl.when(pl.program_id(0) == 0)
    def _():
        cnt_s[...] = jnp.zeros_like(cnt_s)

    xn, h = _mix_and_norm(x_ref, ya_ref, yc_ref, yr_ref, w_ref, g_ref)
    xo_ref[...] = xn
    h_ref[...] = h
    h_hi = _bf16(h)
    h_lo = _bf16(h - h_hi.astype(jnp.float32))
    hw = _dot(h_hi, wr_ref[...])
    logits = hw[:, 0:LANES] + hw[:, LANES:2 * LANES] + _dot(h_lo, wr_ref[:, 0:LANES]) + br_ref[...]
    lane = lax.broadcasted_iota(jnp.int32, logits.shape, 1)
    logits = jnp.where(lane < N_EXPERTS, logits, MASK_VALUE)
    m1 = jnp.max(logits, axis=-1, keepdims=True)
    i1 = jnp.min(jnp.where(logits == m1, lane, LANES), axis=-1, keepdims=True)
    rest = jnp.where(lane == i1, MASK_VALUE, logits)
    m2 = jnp.max(rest, axis=-1, keepdims=True)
    i2 = jnp.min(jnp.where(rest == m2, lane, LANES), axis=-1, keepdims=True)
    e2 = jnp.exp(m2 - m1)
    w1 = 1.0 / (1.0 + e2)
    w2 = e2 * w1
    hit = (lane == i1) | (lane == i2)
    onehot = jnp.where(hit, 1.0, 0.0)
    incl = _dot(tri_ref[...], _bf16(onehot))
    rank = cnt_s[...] + incl - onehot
    r1 = jnp.sum(jnp.where(lane == i1, rank, 0.0), axis=-1, keepdims=True)
    r2 = jnp.sum(jnp.where(lane == i2, rank, 0.0), axis=-1, keepdims=True)
    cnt_new = cnt_s[...] + incl[incl.shape[0] - 1:, :]
    cnt_s[...] = cnt_new
    cnt_ref[...] = jnp.broadcast_to(cnt_new, cnt_ref.shape)
    rec = jnp.zeros(logits.shape, jnp.float32)
    for ln, val in ((INFO_E1, i1.astype(jnp.float32)), (INFO_E2, i2.astype(jnp.float32)),
                    (INFO_R1, r1), (INFO_R2, r2), (INFO_W1, w1), (INFO_W2, w2)):
        rec = jnp.where(lane == ln, val, rec)
    info_ref[...] = rec
    info_t_ref[...] = rec.T[0:SUBLANES, :]


def _outproj(x2, ya, yc, yr, wo, g, tm, router=None):
    t, dm = x2.shape
    row = lambda w: pl.BlockSpec((tm, w), lambda i: (i, 0))
    full = lambda a: pl.BlockSpec(a.shape, lambda i: (0,) * a.ndim)
    params = pltpu.CompilerParams(dimension_semantics=("arbitrary",), vmem_limit_bytes=VMEM_LIMIT_BYTES)
    base_specs = [row(dm), row(ATTN_W), row(CONV_C), row(HGRN_W), full(wo), full(g)]
    if router is None:
        return pl.pallas_call(
            _outproj_dense_kernel,
            grid=(t // tm,),
            in_specs=base_specs,
            out_specs=(row(dm), row(dm)),
            out_shape=(jax.ShapeDtypeStruct((t, dm), jnp.float32), jax.ShapeDtypeStruct((t, dm), jnp.bfloat16)),
            compiler_params=params,
            name="outproj_dense",
        )(x2, ya, yc, yr, wo, g)
    wr, br, tri = router
    return pl.pallas_call(
        _outproj_routed_kernel,
        grid=(t // tm,),
        in_specs=base_specs + [full(wr), full(br), full(tri)],
        out_specs=(row(dm), row(dm), row(LANES), pl.BlockSpec((SUBLANES, tm), lambda i: (0, i)),
                   pl.BlockSpec((SUBLANES, LANES), lambda i: (0, 0))),
        out_shape=(jax.ShapeDtypeStruct((t, dm), jnp.float32), jax.ShapeDtypeStruct((t, dm), jnp.float32),
                   jax.ShapeDtypeStruct((t, LANES), jnp.float32),
                   jax.ShapeDtypeStruct((SUBLANES, t), jnp.float32),
                   jax.ShapeDtypeStruct((SUBLANES, LANES), jnp.float32)),
        scratch_shapes=[pltpu.VMEM((1, LANES), jnp.float32)],
        compiler_params=params,
        name="outproj_routed",
    )(x2, ya, yc, yr, wo, g, wr, br, tri)


def _swiglu_step(h_ref, wg_ref, wu_ref, wd_ref):
    h = _bf16(h_ref[...])
    gt = _dot(h, _bf16(wg_ref[...]))
    up = _dot(h, _bf16(wu_ref[...]))
    act = _bf16(gt * (1.0 / (1.0 + jnp.exp(-gt))) * up)
    return _dot(act, _bf16(wd_ref[...]))


def _ffn_dense_kernel(h_ref, x_ref, wg_ref, wu_ref, wd_ref, out_ref, acc):
    f = pl.program_id(1)

    @pl.when(f == 0)
    def _():
        acc[...] = x_ref[...]

    acc[...] += _swiglu_step(h_ref, wg_ref, wu_ref, wd_ref)

    @pl.when(f == pl.num_programs(1) - 1)
    def _():
        out_ref[...] = acc[...]


def _ffn_dense(h, x2, wg, wu, wd, tm, fc):
    t, dm = x2.shape
    dff = wg.shape[1]
    return pl.pallas_call(
        _ffn_dense_kernel,
        grid=(t // tm, dff // fc),
        in_specs=[
            pl.BlockSpec((tm, dm), lambda i, f: (i, 0)),
            pl.BlockSpec((tm, dm), lambda i, f: (i, 0)),
            pl.BlockSpec((dm, fc), lambda i, f: (0, f)),
            pl.BlockSpec((dm, fc), lambda i, f: (0, f)),
            pl.BlockSpec((fc, dm), lambda i, f: (f, 0)),
        ],
        out_specs=pl.BlockSpec((tm, dm), lambda i, f: (i, 0)),
        out_shape=jax.ShapeDtypeStruct((t, dm), jnp.float32),
        scratch_shapes=[pltpu.VMEM((tm, dm), jnp.float32)],
        compiler_params=pltpu.CompilerParams(dimension_semantics=("arbitrary", "arbitrary"),
                                             vmem_limit_bytes=VMEM_LIMIT_BYTES),
        name="swiglu_dense",
    )(h, x2, wg, wu, wd)


def _ffn_grouped_kernel(te_ref, ta_ref, h_ref, wg_ref, wu_ref, wd_ref, out_ref, acc):
    i = pl.program_id(0)
    f = pl.program_id(1)

    @pl.when(ta_ref[i] == 1)
    def _():
        @pl.when(f == 0)
        def _():
            acc[...] = jnp.zeros_like(acc)

        acc[...] += _swiglu_step(h_ref, wg_ref, wu_ref, wd_ref)

        @pl.when(f == pl.num_programs(1) - 1)
        def _():
            out_ref[...] = acc[...]

    @pl.when((ta_ref[i] == 0) & (f == pl.num_programs(1) - 1))
    def _():
        out_ref[...] = jnp.zeros_like(out_ref)


def _ffn_grouped(tile_expert, tile_active, xs, wg, wu, wd, tm, fc):
    dm = xs.shape[1]
    n_tiles = tile_expert.shape[0]
    r = n_tiles * tm
    dff = wg.shape[2]
    nf = dff // fc
    fidx = lambda i, f, ta: f * ta[i] + (nf - 1) * (1 - ta[i])
    grid_spec = pltpu.PrefetchScalarGridSpec(
        num_scalar_prefetch=2,
        grid=(n_tiles, nf),
        in_specs=[
            pl.BlockSpec((tm, dm), lambda i, f, te, ta: (i * ta[i], 0)),
            pl.BlockSpec((None, dm, fc), lambda i, f, te, ta: (te[i], 0, fidx(i, f, ta))),
            pl.BlockSpec((None, dm, fc), lambda i, f, te, ta: (te[i], 0, fidx(i, f, ta))),
            pl.BlockSpec((None, fc, dm), lambda i, f, te, ta: (te[i], fidx(i, f, ta), 0)),
        ],
        out_specs=pl.BlockSpec((tm, dm), lambda i, f, te, ta: (i, 0)),
        scratch_shapes=[pltpu.VMEM((tm, dm), jnp.float32)],
    )
    return pl.pallas_call(
        _ffn_grouped_kernel,
        grid_spec=grid_spec,
        out_shape=jax.ShapeDtypeStruct((r, dm), jnp.float32),
        compiler_params=pltpu.CompilerParams(dimension_semantics=("arbitrary", "arbitrary"),
                                             vmem_limit_bytes=VMEM_LIMIT_BYTES),
        name="swiglu_grouped",
    )(tile_expert, tile_active, xs, wg, wu, wd)


def _row_copy(src, dst, sem):
    return pltpu.make_async_copy(src, dst, sem)


MOE_ISSUE_UNROLL = 8


def _wait_rows(ref, n_rows, sem):
    blk = ref.at[pl.ds(0, n_rows), :]
    pltpu.make_async_copy(blk, blk, sem).wait()


MOE_STAGE_BUFS = 3


def _dispatch_kernel(p1_ref, p2_ref, pad_ref, h_ref, xs_ref, stage, zrow, load_sems, scat_sems, zero_sem,
                     *, tb, n_chunks):
    def load(c, slot):
        return pltpu.make_async_copy(h_ref.at[pl.ds(c * tb, tb), :], stage.at[slot], load_sems.at[slot])

    load(0, 0).start()

    zrow[...] = jnp.zeros_like(zrow)
    for e in range(N_EXPERTS):
        def zero(r, _, e=e):
            _row_copy(zrow.at[pl.ds(0, 1), :], xs_ref.at[pl.ds(pad_ref[e] + r, 1), :], zero_sem).start()
            return 0

        lax.fori_loop(0, pad_ref[N_EXPERTS + e], zero, 0)
    for e in range(N_EXPERTS):
        def zero_done(r, _):
            _wait_rows(xs_ref, 1, zero_sem)
            return 0

        lax.fori_loop(0, pad_ref[N_EXPERTS + e], zero_done, 0)

    def chunk(c, _):
        slot = lax.rem(c, MOE_STAGE_BUFS)
        load(c, slot).wait()

        @pl.when(c >= 2)
        def _():
            _wait_rows(xs_ref, 2 * tb, scat_sems.at[lax.rem(c - 2, MOE_STAGE_BUFS)])

        @pl.when(c + 1 < n_chunks)
        def _():
            load(c + 1, lax.rem(c + 1, MOE_STAGE_BUFS)).start()

        sem = scat_sems.at[slot]

        def issue(rr, _):
            for u in range(MOE_ISSUE_UNROLL):
                r = rr * MOE_ISSUE_UNROLL + u
                tok = c * tb + r
                src = stage.at[slot, pl.ds(r, 1), :]
                _row_copy(src, xs_ref.at[pl.ds(p1_ref[tok], 1), :], sem).start(priority=0)
                _row_copy(src, xs_ref.at[pl.ds(p2_ref[tok], 1), :], sem).start(priority=1)
            return 0

        lax.fori_loop(0, tb // MOE_ISSUE_UNROLL, issue, 0)
        return 0

    lax.fori_loop(0, n_chunks, chunk, 0)
    for c in range(max(n_chunks - 2, 0), n_chunks):
        _wait_rows(xs_ref, 2 * tb, scat_sems.at[c % MOE_STAGE_BUFS])


def _dispatch(pos1, pos2, pad_start, h, n_tiles, tb, tm):
    t, dm = h.shape
    n_chunks = t // tb
    grid_spec = pltpu.PrefetchScalarGridSpec(
        num_scalar_prefetch=3,
        grid=(1,),
        in_specs=[pl.BlockSpec(memory_space=pl.ANY)],
        out_specs=pl.BlockSpec(memory_space=pl.ANY),
        scratch_shapes=[pltpu.VMEM((MOE_STAGE_BUFS, tb, dm), h.dtype),
                        pltpu.VMEM((SUBLANES, dm), h.dtype),
                        pltpu.SemaphoreType.DMA((MOE_STAGE_BUFS,)),
                        pltpu.SemaphoreType.DMA((MOE_STAGE_BUFS,)),
                        pltpu.SemaphoreType.DMA(())],
    )
    return pl.pallas_call(
        functools.partial(_dispatch_kernel, tb=tb, n_chunks=n_chunks),
        grid_spec=grid_spec,
        out_shape=jax.ShapeDtypeStruct((n_tiles * tm, dm), h.dtype),
        compiler_params=pltpu.CompilerParams(dimension_semantics=("arbitrary",), has_side_effects=True),
        name="moe_dispatch",
    )(pos1, pos2, pad_start, h)


def _combine_kernel(p1_ref, p2_ref, y_ref, x_ref, info_ref, out_ref, buf_a, buf_b, sems, *, tb):
    i = pl.program_id(0)
    n = pl.num_programs(0)

    def issue_step(step, buf, sem):
        def issue(rr, _):
            for u in range(MOE_ISSUE_UNROLL):
                r = rr * MOE_ISSUE_UNROLL + u
                tok = step * tb + r
                _row_copy(y_ref.at[pl.ds(p1_ref[tok], 1), :], buf.at[0, pl.ds(r, 1), :], sem).start(priority=0)
                _row_copy(y_ref.at[pl.ds(p2_ref[tok], 1), :], buf.at[1, pl.ds(r, 1), :], sem).start(priority=1)
            return 0

        lax.fori_loop(0, tb // MOE_ISSUE_UNROLL, issue, 0)

    def finish(buf, sem):
        _wait_rows(y_ref, 2 * tb, sem)
        info = info_ref[...]
        lane = lax.broadcasted_iota(jnp.int32, info.shape, 1)
        w1 = jnp.sum(jnp.where(lane == INFO_W1, info, 0.0), axis=-1, keepdims=True)
        w2 = jnp.sum(jnp.where(lane == INFO_W2, info, 0.0), axis=-1, keepdims=True)
        out_ref[...] = x_ref[...] + w1 * buf[0] + w2 * buf[1]

    @pl.when(i == 0)
    def _():
        issue_step(0, buf_a, sems.at[0])

    @pl.when(i % 2 == 0)
    def _():
        @pl.when(i + 1 < n)
        def _():
            issue_step(i + 1, buf_b, sems.at[1])

        finish(buf_a, sems.at[0])

    @pl.when(i % 2 == 1)
    def _():
        @pl.when(i + 1 < n)
        def _():
            issue_step(i + 1, buf_a, sems.at[0])

        finish(buf_b, sems.at[1])


def _combine(pos1, pos2, y, x2, info, tb):
    t, dm = x2.shape
    grid_spec = pltpu.PrefetchScalarGridSpec(
        num_scalar_prefetch=2,
        grid=(t // tb,),
        in_specs=[pl.BlockSpec(memory_space=pl.ANY),
                  pl.BlockSpec((tb, dm), lambda i, p1, p2: (i, 0)),
                  pl.BlockSpec((tb, LANES), lambda i, p1, p2: (i, 0))],
        out_specs=pl.BlockSpec((tb, dm), lambda i, p1, p2: (i, 0)),
        scratch_shapes=[pltpu.VMEM((2, tb, dm), jnp.float32), pltpu.VMEM((2, tb, dm), jnp.float32),
                        pltpu.SemaphoreType.DMA((2,))],
    )
    return pl.pallas_call(
        functools.partial(_combine_kernel, tb=tb),
        grid_spec=grid_spec,
        out_shape=jax.ShapeDtypeStruct((t, dm), jnp.float32),
        compiler_params=pltpu.CompilerParams(dimension_semantics=("arbitrary",)),
        name="moe_combine",
    )(pos1, pos2, y, x2, info)


def _moe_routing_tables(info_t, counts, tm, n_tiles):
    cnt = counts[0, :N_EXPERTS].astype(jnp.int32)
    padded = ((cnt + tm - 1) // tm) * tm
    ends = jnp.cumsum(padded)
    offsets = ends - padded
    e1 = info_t[INFO_E1].astype(jnp.int32)
    e2 = info_t[INFO_E2].astype(jnp.int32)
    pos1 = offsets[e1] + info_t[INFO_R1].astype(jnp.int32)
    pos2 = offsets[e2] + info_t[INFO_R2].astype(jnp.int32)
    start = jnp.arange(n_tiles, dtype=jnp.int32) * tm
    tile_expert = jnp.minimum(jnp.sum((start[:, None] >= ends[None, :]).astype(jnp.int32), axis=1), N_EXPERTS - 1)
    tile_active = (start < ends[-1]).astype(jnp.int32)
    pad_rows = jnp.concatenate([offsets + cnt, padded - cnt])
    return pos1, pos2, pad_rows, tile_expert, tile_active


def _pack_w_in(w):
    s = np.cumsum([0, ATTN_W, ATTN_W, ATTN_W, ATTN_HEADS, CONV_C, CONV_C, CONV_C, HGRN_W, HGRN_W, HGRN_W, HGRN_W])
    seg = [w[:, s[n]:s[n + 1]] for n in range(11)]
    a_q, a_k, a_v, a_f, c_x, c_b, c_c, r_q, r_f, r_i, r_g = seg
    a_f = jnp.pad(a_f, ((0, 0), (0, LANES - ATTN_HEADS)))
    return _bf16(jnp.concatenate([a_q, a_k, a_v, c_x, c_b, c_c, r_q, r_f, r_i, r_g, a_f], axis=1))


def _selection_constants():
    selq = np.zeros((LANES, ATTN_W), np.float32)
    selk = np.zeros((LANES, ATTN_W), np.float32)
    oneq = np.zeros((1, ATTN_W), np.float32)
    onek = np.zeros((1, ATTN_W), np.float32)
    for hd in range(ATTN_HEADS):
        base = hd * HEAD_DIM
        for piece in range(3):
            selq[piece * ATTN_HEADS + hd, base + piece] = 1.0
            selk[piece * ATTN_HEADS + hd, base + 3 + piece] = -1.0
            oneq[0, base + 3 + piece] = 1.0
            onek[0, base + piece] = 1.0
    return (jnp.asarray(selq, jnp.bfloat16), jnp.asarray(selk, jnp.bfloat16),
            jnp.asarray(oneq), jnp.asarray(onek))


def _tile_gain(gain, mult):
    return jnp.tile(gain.astype(jnp.float32) * mult, ATTN_HEADS).reshape(1, ATTN_W)


def kernel(x, norm_mix, w_in, attn_f_bias, q_norm_gain, k_norm_gain, conv_w, hgrn_lb_logits, mix_out_gain, w_out,
           norm_ffn, ffn_w_gate, ffn_w_up, ffn_w_down, moe_router_w, moe_router_b, moe_w_gate, moe_w_up, moe_w_down):
    batch, seq, dm = x.shape
    depth = w_in.shape[0]
    t = batch * seq
    f32 = jnp.float32
    tm = min(512, seq)
    tq = min(512, seq)
    hg_rows = min(512, seq)
    tm_ffn = min(1024, t)
    tb_moe = min(256, t)
    dff = ffn_w_gate.shape[-1]
    fc = dff // 4

    p_lb = jax.nn.softmax(hgrn_lb_logits.astype(f32), axis=0)
    lb_all = jnp.cumsum(p_lb, axis=0) - p_lb[0]

    tri_m = _bf16(jnp.tril(jnp.ones((tm, tm), f32)))
    tri_c = _bf16(jnp.tril(jnp.ones((HGRN_CHUNK, HGRN_CHUNK), f32)))
    grp = np.arange(HGRN_W) // HEAD_DIM
    bd = jnp.asarray(grp[:, None] == grp[None, :], jnp.bfloat16)
    selq, selk, oneq, onek = _selection_constants()
    scale = 1.0 / math.sqrt(HEAD_DIM)

    x2 = x.reshape(t, dm)
    for l in range(depth):
        wp = _pack_w_in(w_in[l])
        fb = jnp.pad(attn_f_bias[l].astype(f32), (0, LANES - ATTN_HEADS)).reshape(1, LANES)
        gq = _tile_gain(q_norm_gain[l], scale * LOG2E)
        gk = _tile_gain(k_norm_gain[l], 1.0)
        mog = mix_out_gain[l].astype(f32)
        ga = jnp.broadcast_to(mog[:ATTN_W].reshape(ATTN_HEADS, HEAD_DIM, 1), (ATTN_HEADS, HEAD_DIM, tq))
        gc = mog[ATTN_W:ATTN_W + CONV_C].reshape(1, CONV_C)
        gr = mog[ATTN_W + CONV_C:].reshape(1, HGRN_W)

        q, qaug, k, kaug, v, yc, rq, rlf, rk, rv, rg = _inproj(
            x2, seq, norm_mix[l].astype(f32).reshape(1, dm), wp, tri_m, fb, gq, gk, selq, selk, oneq, onek,
            conv_w[l].astype(f32), gc, bd, lb_all[l].reshape(1, HGRN_W), tm)

        vt = v.reshape(batch, ATTN_HEADS, HEAD_DIM, seq)
        seq3 = lambda a: a.reshape(batch, seq, ATTN_W)
        ya = _attention(q, qaug, seq3(k), seq3(kaug), vt, ga, tq)
        yr = _hgrn(rq, rlf, rk, rv, rg, tri_c, bd, gr, batch, seq, hg_rows)

        j = l // 2
        wo = _bf16(w_out[l])
        gf = norm_ffn[l].astype(f32).reshape(1, dm)
        if l % 2 == 0:
            xo, h2 = _outproj(x2, ya.reshape(t, ATTN_W), yc, yr, wo, gf, tm)
            x2 = _ffn_dense(h2, xo, ffn_w_gate[j], ffn_w_up[j], ffn_w_down[j], tm_ffn, fc)
        else:
            wr32 = jnp.pad(moe_router_w[j].astype(f32), ((0, 0), (0, LANES - N_EXPERTS)))
            wr_hi = _bf16(wr32)
            wr = jnp.concatenate([wr_hi, _bf16(wr32 - wr_hi.astype(f32))], axis=1)
            br = jnp.pad(moe_router_b[j].astype(f32), (0, LANES - N_EXPERTS)).reshape(1, LANES)
            xo, h2, info, info_t, counts = _outproj(x2, ya.reshape(t, ATTN_W), yc, yr, wo, gf, tm,
                                                    router=(wr, br, tri_m))
            n_tiles = (2 * t) // tm_ffn + N_EXPERTS
            pos1, pos2, pad_rows, tile_expert, tile_active = _moe_routing_tables(info_t, counts, tm_ffn, n_tiles)
            xs = _dispatch(pos1, pos2, pad_rows, h2, n_tiles, tb_moe, tm_ffn)
            ys = _ffn_grouped(tile_expert, tile_active, xs,
                              moe_w_gate[j], moe_w_up[j], moe_w_down[j], tm_ffn, fc)
            x2 = _combine(pos1, pos2, ys, xo, info, tb_moe)
    return x2.reshape(batch, seq, dm)
```

```python
import functools
import math

import jax
import jax.numpy as jnp
import numpy as np
from jax import lax
from jax.experimental import pallas as pl
from jax.experimental.pallas import tpu as pltpu

HEAD_DIM = 64
ATTN_HEADS = 8
CONV_C = 256
HGRN_W = 256
ATTN_W = ATTN_HEADS * HEAD_DIM
HGRN_CHUNK = 64
HGRN_SUB = 16
N_EXPERTS = 8
EPS = 1e-6
MASK_VALUE = -1e30
MASK_LOG_DECAY = -1e4
TINY = 1e-30
LOG2E = math.log2(math.e)

LANES = 128
SUBLANES = 8
V7X_VMEM_BYTES = 64 * 1024 * 1024
VMEM_LIMIT_BYTES = V7X_VMEM_BYTES - 8 * 1024 * 1024

PAIR_W = 2 * HEAD_DIM
N_PAIRS = ATTN_HEADS // 2
OFF_Q = 0
OFF_K = OFF_Q + ATTN_W
OFF_V = OFF_K + ATTN_W
OFF_CX = OFF_V + ATTN_W
OFF_CB = OFF_CX + CONV_C
OFF_CC = OFF_CB + CONV_C
OFF_RQ = OFF_CC + CONV_C
OFF_RF = OFF_RQ + HGRN_W
OFF_RI = OFF_RF + HGRN_W
OFF_RG = OFF_RI + HGRN_W
OFF_AF = OFF_RG + HGRN_W
D_PACK = OFF_AF + LANES


def _bf16(x):
    return x.astype(jnp.bfloat16)


def _split3(x):
    p1 = _bf16(x)
    r1 = x - p1.astype(jnp.float32)
    p2 = _bf16(r1)
    r2 = r1 - p2.astype(jnp.float32)
    return p1, p2, _bf16(r2)


def _dot(a, b):
    return jnp.dot(a, b, preferred_element_type=jnp.float32)


def _group_sum(x, bd):
    hi = _bf16(x)
    lo = _bf16(x - hi.astype(jnp.float32))
    return _dot(hi, bd) + _dot(lo, bd)


def _silu(x):
    return x * (1.0 / (1.0 + jnp.exp(-x)))


def _sigmoid(x):
    return 1.0 / (1.0 + jnp.exp(-x))


def _inproj_kernel(x_ref, g_ref, w_ref, tri_ref, fb_ref, gq_ref, gk_ref, selq_ref, selk_ref,
                   oneq_ref, onek_ref, convw_ref, gc_ref, bd_ref, lb_ref,
                   q_ref, qaug_ref, k_ref, kaug_ref, v_ref, yc_ref, rq_ref, rlf_ref, rk_ref, rv_ref, rg_ref,
                   dcarry, ucarry, *, tiles_per_seq):
    i = pl.program_id(0)

    @pl.when(i % tiles_per_seq == 0)
    def _():
        dcarry[...] = jnp.zeros_like(dcarry)
        ucarry[...] = jnp.zeros_like(ucarry)

    x = x_ref[...]
    h = x * lax.rsqrt(jnp.mean(x * x, axis=-1, keepdims=True) + EPS) * g_ref[...]
    hb = _bf16(h)

    zf = _dot(hb, w_ref[:, OFF_AF:OFF_AF + LANES]) + fb_ref[...]
    ls = jnp.minimum(zf, 0.0) - jnp.log(1.0 + jnp.exp(-jnp.abs(zf)))
    p1, p2, p3 = _split3(ls)
    loc = _dot(tri_ref[...], jnp.concatenate([p1, p2, p3], axis=1))
    d = dcarry[...] + loc[:, 0:LANES] + loc[:, LANES:2 * LANES] + loc[:, 2 * LANES:3 * LANES]
    dcarry[...] = d[d.shape[0] - 1:, :]
    e1, e2, e3 = (e.astype(jnp.float32) for e in _split3(d * LOG2E))
    lane = lax.broadcasted_iota(jnp.int32, e1.shape, 1)
    epack = jnp.where(lane < ATTN_HEADS, e1,
                      jnp.where(lane < 2 * ATTN_HEADS, pltpu.roll(e2, ATTN_HEADS, 1),
                                jnp.where(lane < 3 * ATTN_HEADS, pltpu.roll(e3, 2 * ATTN_HEADS, 1), 0.0)))
    epack = _bf16(epack)
    qaug_ref[...] = _bf16(_dot(epack, selq_ref[...]) + oneq_ref[...]).T
    kaug_ref[...] = _bf16(_dot(epack, selk_ref[...]) + onek_ref[...])

    zq = _dot(hb, w_ref[:, OFF_Q:OFF_Q + ATTN_W])
    zk = _dot(hb, w_ref[:, OFF_K:OFF_K + ATTN_W])
    first = lax.broadcasted_iota(jnp.int32, (x.shape[0], PAIR_W), 1) < HEAD_DIM

    def head_norm(z):
        sq = z * z
        s0 = jnp.sum(jnp.where(first, sq, 0.0), axis=-1, keepdims=True)
        s1 = jnp.sum(jnp.where(first, 0.0, sq), axis=-1, keepdims=True)
        return z * lax.rsqrt(jnp.where(first, s0, s1) * (1.0 / HEAD_DIM) + EPS)

    qn = []
    for pb in range(N_PAIRS):
        sl = slice(pb * PAIR_W, (pb + 1) * PAIR_W)
        qn.append(_bf16(head_norm(zq[:, sl]) * gq_ref[:, sl]))
        k_ref[:, sl] = _bf16(head_norm(zk[:, sl]) * gk_ref[:, sl])
    q_ref[...] = jnp.concatenate(qn, axis=1).T
    v_ref[...] = _bf16(_dot(hb, w_ref[:, OFF_V:OFF_V + ATTN_W])).T

    cx = _dot(hb, w_ref[:, OFF_CX:OFF_CX + CONV_C])
    cb = _dot(hb, w_ref[:, OFF_CB:OFF_CB + CONV_C])
    cc = _dot(hb, w_ref[:, OFF_CC:OFF_CC + CONV_C])
    u = cc * cx
    uc = ucarry[...]
    row8 = lax.broadcasted_iota(jnp.int32, (SUBLANES, CONV_C), 0)
    r1 = pltpu.roll(u, 1, 0)
    r2 = pltpu.roll(u, 2, 0)
    top1 = jnp.where(row8 < 1, pltpu.roll(uc, 1, 0), r1[0:SUBLANES])
    top2 = jnp.where(row8 < 2, pltpu.roll(uc, 2, 0), r2[0:SUBLANES])
    u1 = jnp.concatenate([top1, r1[SUBLANES:]], axis=0)
    u2 = jnp.concatenate([top2, r2[SUBLANES:]], axis=0)
    ucarry[...] = u[u.shape[0] - SUBLANES:, :]
    yc = cb * (u2 * convw_ref[0:1, :] + u1 * convw_ref[1:2, :] + u * convw_ref[2:3, :])
    ssc = _group_sum(yc * yc, bd_ref[...])
    yc_ref[...] = _bf16(yc * lax.rsqrt(ssc * (1.0 / HEAD_DIM) + EPS) * gc_ref[...])

    lb = lb_ref[...]
    zr = _dot(hb, w_ref[:, OFF_RF:OFF_RF + HGRN_W])
    sg = _sigmoid(zr)
    f = lb + (1.0 - lb) * sg
    rlf_ref[...] = jnp.log(jnp.maximum(f, TINY))
    rk_ref[...] = (1.0 - lb) * _sigmoid(-zr)
    rq_ref[...] = _silu(_dot(hb, w_ref[:, OFF_RQ:OFF_RQ + HGRN_W]))
    rv_ref[...] = _dot(hb, w_ref[:, OFF_RI:OFF_RI + HGRN_W])
    rg_ref[...] = _silu(_dot(hb, w_ref[:, OFF_RG:OFF_RG + HGRN_W]))


def _inproj(x2, seq, g, wp, tri, fb, gq, gk, selq, selk, oneq, onek, convw, gc, bd, lb, tm):
    t, dm = x2.shape
    full = lambda a: pl.BlockSpec(a.shape, lambda i: (0,) * a.ndim)
    row = lambda w: pl.BlockSpec((tm, w), lambda i: (i, 0))
    consts = (g, wp, tri, fb, gq, gk, selq, selk, oneq, onek, convw, gc, bd, lb)
    tiles_per_seq = seq // tm
    rows_bf16 = jax.ShapeDtypeStruct((t, ATTN_W), jnp.bfloat16)
    cols_bf16 = jax.ShapeDtypeStruct((t // seq, ATTN_W, seq), jnp.bfloat16)
    out_shape = (cols_bf16, cols_bf16, rows_bf16, rows_bf16, cols_bf16,
                 jax.ShapeDtypeStruct((t, CONV_C), jnp.bfloat16),
                 ) + tuple(jax.ShapeDtypeStruct((t, HGRN_W), jnp.float32) for _ in range(5))
    t_spec = pl.BlockSpec((None, ATTN_W, tm), lambda i: (i // tiles_per_seq, 0, i % tiles_per_seq))
    out_specs = ((t_spec, t_spec, row(ATTN_W), row(ATTN_W), t_spec, row(CONV_C))
                 + tuple(row(HGRN_W) for _ in range(5)))
    return pl.pallas_call(
        functools.partial(_inproj_kernel, tiles_per_seq=seq // tm),
        grid=(t // tm,),
        in_specs=[row(dm)] + [full(a) for a in consts],
        out_specs=out_specs,
        out_shape=out_shape,
        scratch_shapes=[pltpu.VMEM((1, LANES), jnp.float32), pltpu.VMEM((SUBLANES, CONV_C), jnp.float32)],
        compiler_params=pltpu.CompilerParams(dimension_semantics=("arbitrary",),
                                             vmem_limit_bytes=VMEM_LIMIT_BYTES),
        name="inproj",
    )(x2, *consts)


def _attn_kernel(q_ref, qaug_ref, k_ref, kaug_ref, vt_ref, gain_ref, out_ref, m_s, acc_s, sa_ref, sb_ref,
                 mxa_ref, mxb_ref, qt_s, *, tq):
    s_a, s_b = (sa_ref, mxa_ref), (sb_ref, mxb_ref)
    i = pl.program_id(2)
    ones_rows = jnp.ones((2 * SUBLANES, tq), jnp.bfloat16)
    m_s[...] = jnp.full_like(m_s, MASK_VALUE)
    acc_s[...] = jnp.zeros_like(acc_s)

    chan = lax.broadcasted_iota(jnp.int32, (PAIR_W, tq), 0)
    zero = jnp.zeros((PAIR_W, tq), jnp.bfloat16)
    for hh in range(2):
        mine = (chan < HEAD_DIM) if hh == 0 else (chan >= HEAD_DIM)
        qt_s[hh] = jnp.concatenate([jnp.where(mine, q_ref[...], zero), jnp.where(mine, qaug_ref[...], zero)],
                                   axis=0)

    def scores(j, s_buf):
        k0 = pl.multiple_of(j * tq, tq)
        krows = jnp.concatenate([k_ref[pl.ds(k0, tq), :], kaug_ref[pl.ds(k0, tq), :]], axis=1)
        for hh in range(2):
            s = _dot(krows, qt_s[hh])
            s_buf[0][hh] = s
            s_buf[1][hh] = jnp.max(s, axis=0, keepdims=True)

    def consume(j, s_buf, masked):
        k0 = pl.multiple_of(j * tq, tq)
        for hh in range(2):
            s = s_buf[0][hh]
            if masked:
                kpos = lax.broadcasted_iota(jnp.int32, (tq, tq), 0)
                qpos = lax.broadcasted_iota(jnp.int32, (tq, tq), 1)
                s = jnp.where(kpos <= qpos, s, MASK_VALUE)
                blk_max = jnp.max(s, axis=0, keepdims=True)
            else:
                blk_max = s_buf[1][hh]
            m = m_s[hh]
            m_new = jnp.maximum(m, blk_max)
            p = jnp.exp2(s - m_new)
            alpha = jnp.exp2(m - m_new)
            vaug = jnp.concatenate([vt_ref[hh, :, pl.ds(k0, tq)], ones_rows], axis=0)
            acc_s[hh] = acc_s[hh] * alpha + _dot(vaug, _bf16(p))
            m_s[hh] = m_new

    scores(0, s_a)

    def pair(j):
        scores(j + 1, s_b)
        consume(j, s_a, False)
        scores(j + 2, s_a)
        consume(j + 1, s_b, False)

    def body(jj, _):
        pair(4 * jj)
        pair(4 * jj + 2)
        return 0

    lax.fori_loop(0, i // 4, body, 0)
    done = 4 * (i // 4)

    @pl.when(i - done >= 2)
    def _():
        pair(done)

    @pl.when(i % 2 == 1)
    def _():
        scores(i, s_b)
        consume(i - 1, s_a, False)
        consume(i, s_b, True)

    @pl.when(i % 2 == 0)
    def _():
        consume(i, s_a, True)

    ys = []
    for hh in range(2):
        acc = acc_s[hh]
        o = acc[0:HEAD_DIM] * (1.0 / acc[HEAD_DIM:HEAD_DIM + 1])
        ms = jnp.mean(o * o, axis=0, keepdims=True)
        ys.append(o * lax.rsqrt(ms + EPS) * gain_ref[hh])
    out_ref[...] = _bf16(jnp.concatenate(ys, axis=0).T)


def _attention(q, qaug, k, kaug, vt, gain, tq):
    b, s, _ = k.shape
    q_spec = pl.BlockSpec((None, PAIR_W, tq), lambda bi, hp, i: (bi, hp, i))
    k_spec = pl.BlockSpec((None, s, PAIR_W), lambda bi, hp, i: (bi, 0, hp))
    return pl.pallas_call(
        functools.partial(_attn_kernel, tq=tq),
        grid=(b, N_PAIRS, s // tq),
        in_specs=[
            q_spec, q_spec, k_spec, k_spec,
            pl.BlockSpec((None, 2, HEAD_DIM, s), lambda bi, hp, i: (bi, hp, 0, 0)),
            pl.BlockSpec((2, HEAD_DIM, tq), lambda bi, hp, i: (hp, 0, 0)),
        ],
        out_specs=pl.BlockSpec((None, tq, 2 * HEAD_DIM), lambda bi, hp, i: (bi, i, hp)),
        out_shape=jax.ShapeDtypeStruct((b, s, ATTN_W), jnp.bfloat16),
        scratch_shapes=[pltpu.VMEM((2, 1, tq), jnp.float32),
                        pltpu.VMEM((2, HEAD_DIM + 2 * SUBLANES, tq), jnp.float32),
                        pltpu.VMEM((2, tq, tq), jnp.float32),
                        pltpu.VMEM((2, tq, tq), jnp.float32),
                        pltpu.VMEM((2, 1, tq), jnp.float32),
                        pltpu.VMEM((2, 1, tq), jnp.float32),
                        pltpu.VMEM((2, 2 * PAIR_W, tq), jnp.bfloat16)],
        compiler_params=pltpu.CompilerParams(dimension_semantics=("arbitrary", "arbitrary", "arbitrary"),
                                             vmem_limit_bytes=VMEM_LIMIT_BYTES),
        name="fox_attention",
    )(q, qaug, k, kaug, vt, gain)


def _hgrn_kernel(q_ref, lf_ref, k_ref, v_ref, g_ref, tri_ref, bd_ref, gain_ref, out_ref,
                 state, c_all, k_all, v_all, *, n_chunks):
    @pl.when(pl.program_id(1) == 0)
    def _():
        state[...] = jnp.zeros_like(state)

    ch = HGRN_CHUNK
    sb = HGRN_SUB
    nsb = ch // sb
    row_sb = lax.broadcasted_iota(jnp.int32, (sb, HGRN_W), 0)
    r128 = lax.broadcasted_iota(jnp.int32, (LANES, LANES), 0)
    c128 = lax.broadcasted_iota(jnp.int32, (LANES, LANES), 1)
    same_head = (r128 < HEAD_DIM) == (c128 < HEAD_DIM)
    t64 = lax.broadcasted_iota(jnp.int32, (ch, LANES), 0) // sb
    s64 = (lax.broadcasted_iota(jnp.int32, (ch, LANES), 1) % HEAD_DIM) // sb
    level2 = ((t64 == 1) & (s64 == 0)) | ((t64 == 3) & (s64 == 2))
    lane_head0 = lax.broadcasted_iota(jnp.int32, (ch, LANES), 1) < HEAD_DIM
    bd = bd_ref[...]
    zeros_sb = jnp.zeros((sb, HGRN_W), jnp.float32)

    def chunk(ci):
        r0 = ci * ch
        c_s, k_s, v_s = c_all.at[ci % 2], k_all.at[ci % 2], v_all.at[ci % 2]
        q = q_ref[pl.ds(r0, ch), :]
        k = k_ref[pl.ds(r0, ch), :]
        v = v_ref[pl.ds(r0, ch), :]
        p1, p2, p3 = _split3(lf_ref[pl.ds(r0, ch), :])
        cc = _dot(tri_ref[...], jnp.concatenate([p1, p2, p3], axis=1))
        c = cc[:, 0:HGRN_W] + cc[:, HGRN_W:2 * HGRN_W] + cc[:, 2 * HGRN_W:3 * HGRN_W]
        c_s[...] = c
        k_s[...] = k
        v_s[...] = v
        blk = lambda a, n: a[n * sb:(n + 1) * sb]

        ps = []
        for n in range(nsb):
            cn, qn = blk(c, n), blk(q, n)
            for s in range(sb):
                r = n * sb + s
                dec = jnp.exp(jnp.where(row_sb >= s, cn - c_s[r:r + 1, :], MASK_LOG_DECAY))
                ps.append(_bf16(qn * k_s[r:r + 1, :] * dec))
        a_d = _dot(jnp.concatenate(ps, axis=0), bd)
        o_parts = []
        for n in range(nsb):
            acc = jnp.zeros((sb, HGRN_W), jnp.float32)
            for s in range(sb):
                r = n * sb + s
                acc = acc + a_d[r * sb:(r + 1) * sb] * v_s[r:r + 1, :]
            o_parts.append(acc)
        o = jnp.concatenate(o_parts, axis=0)

        ref1 = c[2 * sb - 1:2 * sb]
        ref2a = c[sb - 1:sb]
        ref2b = c[3 * sb - 1:3 * sb]
        hi, lo = slice(2 * sb, 4 * sb), slice(0, 2 * sb)
        zeros_half = jnp.zeros((2 * sb, HGRN_W), jnp.float32)
        q1 = jnp.concatenate([zeros_half, q[hi] * jnp.exp(c[hi] - ref1)], axis=0)
        k1 = jnp.concatenate([k[lo] * jnp.exp(ref1 - c[lo]), zeros_half], axis=0)
        q2 = jnp.concatenate([zeros_sb, blk(q, 1) * jnp.exp(blk(c, 1) - ref2a),
                              zeros_sb, blk(q, 3) * jnp.exp(blk(c, 3) - ref2b)], axis=0)
        k2 = jnp.concatenate([blk(k, 0) * jnp.exp(ref2a - blk(c, 0)), zeros_sb,
                              blk(k, 2) * jnp.exp(ref2b - blk(c, 2)), zeros_sb], axis=0)

        c_last = c[ch - 1:ch, :]
        qe = q * jnp.exp(c)
        kd = k * jnp.exp(c_last - c)
        e_last = jnp.exp(c_last)
        nt = (((1,), (1,)), ((), ()))
        o_off = []
        for bb in range(HGRN_W // LANES):
            sl = slice(bb * LANES, (bb + 1) * LANES)
            st = state[bb]
            stack2 = lambda a: jnp.concatenate([jnp.where(lane_head0, a[:, sl], 0.0),
                                                jnp.where(lane_head0, 0.0, a[:, sl])], axis=0)
            a1 = lax.dot_general(_bf16(q1[:, sl]), _bf16(stack2(k1)), nt, preferred_element_type=jnp.float32)
            a2 = lax.dot_general(_bf16(q2[:, sl]), _bf16(stack2(k2)), nt, preferred_element_type=jnp.float32)
            a_off = a1 + jnp.where(level2, a2, 0.0)
            lhs = jnp.concatenate([_bf16(a_off), _bf16(qe[:, sl])], axis=1)
            rhs = jnp.concatenate([_bf16(stack2(v)), _bf16(st)], axis=0)
            o_off.append(_dot(lhs, rhs))
            ecol = jnp.broadcast_to(e_last[:, sl], (LANES, LANES)).T
            upd = lax.dot_general(_bf16(kd[:, sl]), _bf16(v[:, sl]), (((0,), (0,)), ((), ())),
                                  preferred_element_type=jnp.float32)
            state[bb] = jnp.where(same_head, ecol * st + upd, 0.0)
        o = o + jnp.concatenate(o_off, axis=1)

        ss = _group_sum(o * o, bd)
        y = o * lax.rsqrt(ss * (1.0 / HEAD_DIM) + EPS) * gain_ref[...] * g_ref[pl.ds(r0, ch), :]
        out_ref[pl.ds(r0, ch), :] = _bf16(y)

    for ci in range(n_chunks):
        chunk(ci)


def _hgrn(rq, rlf, rk, rv, rg, tri, bd, gain, batch, seq, rows):
    t = rq.shape[0]
    steps = seq // rows
    blk = pl.BlockSpec((rows, HGRN_W), lambda b, j: (b * steps + j, 0))
    full = lambda a: pl.BlockSpec(a.shape, lambda b, j: (0,) * a.ndim)
    return pl.pallas_call(
        functools.partial(_hgrn_kernel, n_chunks=rows // HGRN_CHUNK),
        grid=(batch, steps),
        in_specs=[blk, blk, blk, blk, blk, full(tri), full(bd), full(gain)],
        out_specs=blk,
        out_shape=jax.ShapeDtypeStruct((t, HGRN_W), jnp.bfloat16),
        scratch_shapes=[pltpu.VMEM((HGRN_W // LANES, LANES, LANES), jnp.float32)]
        + [pltpu.VMEM((2, HGRN_CHUNK, HGRN_W), jnp.float32) for _ in range(3)],
        compiler_params=pltpu.CompilerParams(dimension_semantics=("arbitrary", "arbitrary"),
                                             vmem_limit_bytes=VMEM_LIMIT_BYTES),
        name="hgrn2",
    )(rq, rlf, rk, rv, rg, tri, bd, gain)


INFO_E1, INFO_E2, INFO_R1, INFO_R2, INFO_W1, INFO_W2 = range(6)


def _mix_and_norm(x_ref, ya_ref, yc_ref, yr_ref, w_ref, g_ref):
    xn = (x_ref[...]
          + _dot(ya_ref[...], w_ref[0:ATTN_W, :])
          + _dot(yc_ref[...], w_ref[ATTN_W:ATTN_W + CONV_C, :])
          + _dot(yr_ref[...], w_ref[ATTN_W + CONV_C:, :]))
    h = xn * lax.rsqrt(jnp.mean(xn * xn, axis=-1, keepdims=True) + EPS) * g_ref[...]
    return xn, h


def _outproj_dense_kernel(x_ref, ya_ref, yc_ref, yr_ref, w_ref, g_ref, xo_ref, h_ref):
    xn, h = _mix_and_norm(x_ref, ya_ref, yc_ref, yr_ref, w_ref, g_ref)
    xo_ref[...] = xn
    h_ref[...] = _bf16(h)


def _outproj_routed_kernel(x_ref, ya_ref, yc_ref, yr_ref, w_ref, g_ref, wr_ref, br_ref, tri_ref,
                           xo_ref, h_ref, info_ref, info_t_ref, cnt_ref, cnt_s):
    @pl.when(pl.program_id(0) == 0)
    def _():
        cnt_s[...] = jnp.zeros_like(cnt_s)

    xn, h = _mix_and_norm(x_ref, ya_ref, yc_ref, yr_ref, w_ref, g_ref)
    xo_ref[...] = xn
    h_ref[...] = h
    h_hi = _bf16(h)
    h_lo = _bf16(h - h_hi.astype(jnp.float32))
    hw = _dot(h_hi, wr_ref[...])
    logits = hw[:, 0:LANES] + hw[:, LANES:2 * LANES] + _dot(h_lo, wr_ref[:, 0:LANES]) + br_ref[...]
    lane = lax.broadcasted_iota(jnp.int32, logits.shape, 1)
    logits = jnp.where(lane < N_EXPERTS, logits, MASK_VALUE)
    m1 = jnp.max(logits, axis=-1, keepdims=True)
    i1 = jnp.min(jnp.where(logits == m1, lane, LANES), axis=-1, keepdims=True)
    rest = jnp.where(lane == i1, MASK_VALUE, logits)
    m2 = jnp.max(rest, axis=-1, keepdims=True)
    i2 = jnp.min(jnp.where(rest == m2, lane, LANES), axis=-1, keepdims=True)
    e2 = jnp.exp(m2 - m1)
    w1 = 1.0 / (1.0 + e2)
    w2 = e2 * w1
    hit = (lane == i1) | (lane == i2)
    onehot = jnp.where(hit, 1.0, 0.0)
    incl = _dot(tri_ref[...], _bf16(onehot))
    rank = cnt_s[...] + incl - onehot
    r1 = jnp.sum(jnp.where(lane == i1, rank, 0.0), axis=-1, keepdims=True)
    r2 = jnp.sum(jnp.where(lane == i2, rank, 0.0), axis=-1, keepdims=True)
    cnt_new = cnt_s[...] + incl[incl.shape[0] - 1:, :]
    cnt_s[...] = cnt_new
    cnt_ref[...] = jnp.broadcast_to(cnt_new, cnt_ref.shape)
    rec = jnp.zeros(logits.shape, jnp.float32)
    for ln, val in ((INFO_E1, i1.astype(jnp.float32)), (INFO_E2, i2.astype(jnp.float32)),
                    (INFO_R1, r1), (INFO_R2, r2), (INFO_W1, w1), (INFO_W2, w2)):
        rec = jnp.where(lane == ln, val, rec)
    info_ref[...] = rec
    info_t_ref[...] = rec.T[0:SUBLANES, :]


def _outproj(x2, ya, yc, yr, wo, g, tm, router=None):
    t, dm = x2.shape
    row = lambda w: pl.BlockSpec((tm, w), lambda i: (i, 0))
    full = lambda a: pl.BlockSpec(a.shape, lambda i: (0,) * a.ndim)
    params = pltpu.CompilerParams(dimension_semantics=("arbitrary",), vmem_limit_bytes=VMEM_LIMIT_BYTES)
    base_specs = [row(dm), row(ATTN_W), row(CONV_C), row(HGRN_W), full(wo), full(g)]
    if router is None:
        return pl.pallas_call(
            _outproj_dense_kernel,
            grid=(t // tm,),
            in_specs=base_specs,
            out_specs=(row(dm), row(dm)),
            out_shape=(jax.ShapeDtypeStruct((t, dm), jnp.float32), jax.ShapeDtypeStruct((t, dm), jnp.bfloat16)),
            compiler_params=params,
            name="outproj_dense",
        )(x2, ya, yc, yr, wo, g)
    wr, br, tri = router
    return pl.pallas_call(
        _outproj_routed_kernel,
        grid=(t // tm,),
        in_specs=base_specs + [full(wr), full(br), full(tri)],
        out_specs=(row(dm), row(dm), row(LANES), pl.BlockSpec((SUBLANES, tm), lambda i: (0, i)),
                   pl.BlockSpec((SUBLANES, LANES), lambda i: (0, 0))),
        out_shape=(jax.ShapeDtypeStruct((t, dm), jnp.float32), jax.ShapeDtypeStruct((t, dm), jnp.float32),
                   jax.ShapeDtypeStruct((t, LANES), jnp.float32),
                   jax.ShapeDtypeStruct((SUBLANES, t), jnp.float32),
                   jax.ShapeDtypeStruct((SUBLANES, LANES), jnp.float32)),
        scratch_shapes=[pltpu.VMEM((1, LANES), jnp.float32)],
        compiler_params=params,
        name="outproj_routed",
    )(x2, ya, yc, yr, wo, g, wr, br, tri)


def _swiglu_step(h_ref, wg_ref, wu_ref, wd_ref):
    h = _bf16(h_ref[...])
    gt = _dot(h, _bf16(wg_ref[...]))
    up = _dot(h, _bf16(wu_ref[...]))
    act = _bf16(gt * (1.0 / (1.0 + jnp.exp(-gt))) * up)
    return _dot(act, _bf16(wd_ref[...]))


def _ffn_dense_kernel(h_ref, x_ref, wg_ref, wu_ref, wd_ref, out_ref, acc):
    f = pl.program_id(1)

    @pl.when(f == 0)
    def _():
        acc[...] = x_ref[...]

    acc[...] += _swiglu_step(h_ref, wg_ref, wu_ref, wd_ref)

    @pl.when(f == pl.num_programs(1) - 1)
    def _():
        out_ref[...] = acc[...]


def _ffn_dense(h, x2, wg, wu, wd, tm, fc):
    t, dm = x2.shape
    dff = wg.shape[1]
    return pl.pallas_call(
        _ffn_dense_kernel,
        grid=(t // tm, dff // fc),
        in_specs=[
            pl.BlockSpec((tm, dm), lambda i, f: (i, 0)),
            pl.BlockSpec((tm, dm), lambda i, f: (i, 0)),
            pl.BlockSpec((dm, fc), lambda i, f: (0, f)),
            pl.BlockSpec((dm, fc), lambda i, f: (0, f)),
            pl.BlockSpec((fc, dm), lambda i, f: (f, 0)),
        ],
        out_specs=pl.BlockSpec((tm, dm), lambda i, f: (i, 0)),
        out_shape=jax.ShapeDtypeStruct((t, dm), jnp.float32),
        scratch_shapes=[pltpu.VMEM((tm, dm), jnp.float32)],
        compiler_params=pltpu.CompilerParams(dimension_semantics=("arbitrary", "arbitrary"),
                                             vmem_limit_bytes=VMEM_LIMIT_BYTES),
        name="swiglu_dense",
    )(h, x2, wg, wu, wd)


def _ffn_grouped_kernel(te_ref, ta_ref, h_ref, wg_ref, wu_ref, wd_ref, out_ref, acc):
    i = pl.program_id(0)
    f = pl.program_id(1)

    @pl.when(ta_ref[i] == 1)
    def _():
        @pl.when(f == 0)
        def _():
            acc[...] = jnp.zeros_like(acc)

        acc[...] += _swiglu_step(h_ref, wg_ref, wu_ref, wd_ref)

        @pl.when(f == pl.num_programs(1) - 1)
        def _():
            out_ref[...] = acc[...]

    @pl.when((ta_ref[i] == 0) & (f == pl.num_programs(1) - 1))
    def _():
        out_ref[...] = jnp.zeros_like(out_ref)


def _ffn_grouped(tile_expert, tile_active, xs, wg, wu, wd, tm, fc):
    dm = xs.shape[1]
    n_tiles = tile_expert.shape[0]
    r = n_tiles * tm
    dff = wg.shape[2]
    nf = dff // fc
    fidx = lambda i, f, ta: f * ta[i] + (nf - 1) * (1 - ta[i])
    grid_spec = pltpu.PrefetchScalarGridSpec(
        num_scalar_prefetch=2,
        grid=(n_tiles, nf),
        in_specs=[
            pl.BlockSpec((tm, dm), lambda i, f, te, ta: (i * ta[i], 0)),
            pl.BlockSpec((None, dm, fc), lambda i, f, te, ta: (te[i], 0, fidx(i, f, ta))),
            pl.BlockSpec((None, dm, fc), lambda i, f, te, ta: (te[i], 0, fidx(i, f, ta))),
            pl.BlockSpec((None, fc, dm), lambda i, f, te, ta: (te[i], fidx(i, f, ta), 0)),
        ],
        out_specs=pl.BlockSpec((tm, dm), lambda i, f, te, ta: (i, 0)),
        scratch_shapes=[pltpu.VMEM((tm, dm), jnp.float32)],
    )
    return pl.pallas_call(
        _ffn_grouped_kernel,
        grid_spec=grid_spec,
        out_shape=jax.ShapeDtypeStruct((r, dm), jnp.float32),
        compiler_params=pltpu.CompilerParams(dimension_semantics=("arbitrary", "arbitrary"),
                                             vmem_limit_bytes=VMEM_LIMIT_BYTES),
        name="swiglu_grouped",
    )(tile_expert, tile_active, xs, wg, wu, wd)


def _row_copy(src, dst, sem):
    return pltpu.make_async_copy(src, dst, sem)


MOE_ISSUE_UNROLL = 8


def _wait_rows(ref, n_rows, sem):
    blk = ref.at[pl.ds(0, n_rows), :]
    pltpu.make_async_copy(blk, blk, sem).wait()


MOE_STAGE_BUFS = 3


def _dispatch_kernel(p1_ref, p2_ref, pad_ref, h_ref, xs_ref, stage, zrow, load_sems, scat_sems, zero_sem,
                     *, tb, n_chunks):
    def load(c, slot):
        return pltpu.make_async_copy(h_ref.at[pl.ds(c * tb, tb), :], stage.at[slot], load_sems.at[slot])

    load(0, 0).start()

    zrow[...] = jnp.zeros_like(zrow)
    for e in range(N_EXPERTS):
        def zero(r, _, e=e):
            _row_copy(zrow.at[pl.ds(0, 1), :], xs_ref.at[pl.ds(pad_ref[e] + r, 1), :], zero_sem).start()
            return 0

        lax.fori_loop(0, pad_ref[N_EXPERTS + e], zero, 0)
    for e in range(N_EXPERTS):
        def zero_done(r, _):
            _wait_rows(xs_ref, 1, zero_sem)
            return 0

        lax.fori_loop(0, pad_ref[N_EXPERTS + e], zero_done, 0)

    def chunk(c, _):
        slot = lax.rem(c, MOE_STAGE_BUFS)
        load(c, slot).wait()

        @pl.when(c >= 2)
        def _():
            _wait_rows(xs_ref, 2 * tb, scat_sems.at[lax.rem(c - 2, MOE_STAGE_BUFS)])

        @pl.when(c + 1 < n_chunks)
        def _():
            load(c + 1, lax.rem(c + 1, MOE_STAGE_BUFS)).start()

        sem = scat_sems.at[slot]

        def issue(rr, _):
            for u in range(MOE_ISSUE_UNROLL):
                r = rr * MOE_ISSUE_UNROLL + u
                tok = c * tb + r
                src = stage.at[slot, pl.ds(r, 1), :]
                _row_copy(src, xs_ref.at[pl.ds(p1_ref[tok], 1), :], sem).start()
                _row_copy(src, xs_ref.at[pl.ds(p2_ref[tok], 1), :], sem).start()
            return 0

        lax.fori_loop(0, tb // MOE_ISSUE_UNROLL, issue, 0)
        return 0

    lax.fori_loop(0, n_chunks, chunk, 0)
    for c in range(max(n_chunks - 2, 0), n_chunks):
        _wait_rows(xs_ref, 2 * tb, scat_sems.at[c % MOE_STAGE_BUFS])


def _dispatch(pos1, pos2, pad_start, h, n_tiles, tb, tm):
    t, dm = h.shape
    n_chunks = t // tb
    grid_spec = pltpu.PrefetchScalarGridSpec(
        num_scalar_prefetch=3,
        grid=(1,),
        in_specs=[pl.BlockSpec(memory_space=pl.ANY)],
        out_specs=pl.BlockSpec(memory_space=pl.ANY),
        scratch_shapes=[pltpu.VMEM((MOE_STAGE_BUFS, tb, dm), h.dtype),
                        pltpu.VMEM((SUBLANES, dm), h.dtype),
                        pltpu.SemaphoreType.DMA((MOE_STAGE_BUFS,)),
                        pltpu.SemaphoreType.DMA((MOE_STAGE_BUFS,)),
                        pltpu.SemaphoreType.DMA(())],
    )
    return pl.pallas_call(
        functools.partial(_dispatch_kernel, tb=tb, n_chunks=n_chunks),
        grid_spec=grid_spec,
        out_shape=jax.ShapeDtypeStruct((n_tiles * tm, dm), h.dtype),
        compiler_params=pltpu.CompilerParams(dimension_semantics=("arbitrary",), has_side_effects=True),
        name="moe_dispatch",
    )(pos1, pos2, pad_start, h)


def _combine_kernel(p1_ref, p2_ref, y_ref, x_ref, info_ref, out_ref, buf_a, buf_b, sems, *, tb):
    i = pl.program_id(0)
    n = pl.num_programs(0)

    def issue_step(step, buf, sem):
        def issue(rr, _):
            for u in range(MOE_ISSUE_UNROLL):
                r = rr * MOE_ISSUE_UNROLL + u
                tok = step * tb + r
                _row_copy(y_ref.at[pl.ds(p1_ref[tok], 1), :], buf.at[0, pl.ds(r, 1), :], sem).start()
                _row_copy(y_ref.at[pl.ds(p2_ref[tok], 1), :], buf.at[1, pl.ds(r, 1), :], sem).start()
            return 0

        lax.fori_loop(0, tb // MOE_ISSUE_UNROLL, issue, 0)

    def finish(buf, sem):
        _wait_rows(y_ref, 2 * tb, sem)
        info = info_ref[...]
        lane = lax.broadcasted_iota(jnp.int32, info.shape, 1)
        w1 = jnp.sum(jnp.where(lane == INFO_W1, info, 0.0), axis=-1, keepdims=True)
        w2 = jnp.sum(jnp.where(lane == INFO_W2, info, 0.0), axis=-1, keepdims=True)
        out_ref[...] = x_ref[...] + w1 * buf[0] + w2 * buf[1]

    @pl.when(i == 0)
    def _():
        issue_step(0, buf_a, sems.at[0])

    @pl.when(i % 2 == 0)
    def _():
        @pl.when(i + 1 < n)
        def _():
            issue_step(i + 1, buf_b, sems.at[1])

        finish(buf_a, sems.at[0])

    @pl.when(i % 2 == 1)
    def _():
        @pl.when(i + 1 < n)
        def _():
            issue_step(i + 1, buf_a, sems.at[0])

        finish(buf_b, sems.at[1])


def _combine(pos1, pos2, y, x2, info, tb):
    t, dm = x2.shape
    grid_spec = pltpu.PrefetchScalarGridSpec(
        num_scalar_prefetch=2,
        grid=(t // tb,),
        in_specs=[pl.BlockSpec(memory_space=pl.ANY),
                  pl.BlockSpec((tb, dm), lambda i, p1, p2: (i, 0)),
                  pl.BlockSpec((tb, LANES), lambda i, p1, p2: (i, 0))],
        out_specs=pl.BlockSpec((tb, dm), lambda i, p1, p2: (i, 0)),
        scratch_shapes=[pltpu.VMEM((2, tb, dm), jnp.float32), pltpu.VMEM((2, tb, dm), jnp.float32),
                        pltpu.SemaphoreType.DMA((2,))],
    )
    return pl.pallas_call(
        functools.partial(_combine_kernel, tb=tb),
        grid_spec=grid_spec,
        out_shape=jax.ShapeDtypeStruct((t, dm), jnp.float32),
        compiler_params=pltpu.CompilerParams(dimension_semantics=("arbitrary",)),
        name="moe_combine",
    )(pos1, pos2, y, x2, info)


def _moe_routing_tables(info_t, counts, tm, n_tiles):
    cnt = counts[0, :N_EXPERTS].astype(jnp.int32)
    padded = ((cnt + tm - 1) // tm) * tm
    ends = jnp.cumsum(padded)
    offsets = ends - padded
    e1 = info_t[INFO_E1].astype(jnp.int32)
    e2 = info_t[INFO_E2].astype(jnp.int32)
    pos1 = offsets[e1] + info_t[INFO_R1].astype(jnp.int32)
    pos2 = offsets[e2] + info_t[INFO_R2].astype(jnp.int32)
    start = jnp.arange(n_tiles, dtype=jnp.int32) * tm
    tile_expert = jnp.minimum(jnp.sum((start[:, None] >= ends[None, :]).astype(jnp.int32), axis=1), N_EXPERTS - 1)
    tile_active = (start < ends[-1]).astype(jnp.int32)
    pad_rows = jnp.concatenate([offsets + cnt, padded - cnt])
    return pos1, pos2, pad_rows, tile_expert, tile_active


def _pack_w_in(w):
    s = np.cumsum([0, ATTN_W, ATTN_W, ATTN_W, ATTN_HEADS, CONV_C, CONV_C, CONV_C, HGRN_W, HGRN_W, HGRN_W, HGRN_W])
    seg = [w[:, s[n]:s[n + 1]] for n in range(11)]
    a_q, a_k, a_v, a_f, c_x, c_b, c_c, r_q, r_f, r_i, r_g = seg
    a_f = jnp.pad(a_f, ((0, 0), (0, LANES - ATTN_HEADS)))
    return _bf16(jnp.concatenate([a_q, a_k, a_v, c_x, c_b, c_c, r_q, r_f, r_i, r_g, a_f], axis=1))


def _selection_constants():
    selq = np.zeros((LANES, ATTN_W), np.float32)
    selk = np.zeros((LANES, ATTN_W), np.float32)
    oneq = np.zeros((1, ATTN_W), np.float32)
    onek = np.zeros((1, ATTN_W), np.float32)
    for hd in range(ATTN_HEADS):
        base = hd * HEAD_DIM
        for piece in range(3):
            selq[piece * ATTN_HEADS + hd, base + piece] = 1.0
            selk[piece * ATTN_HEADS + hd, base + 3 + piece] = -1.0
            oneq[0, base + 3 + piece] = 1.0
            onek[0, base + piece] = 1.0
    return (jnp.asarray(selq, jnp.bfloat16), jnp.asarray(selk, jnp.bfloat16),
            jnp.asarray(oneq), jnp.asarray(onek))


PROJ_ROWS = 512
ATTN_BLOCK = 512
HGRN_ROWS = 512
FFN_ROWS = 1024
FFN_COL_CHUNKS = 4
MOE_CHUNK = 512


def _plan_tiles(batch, seq, dff):
    t = batch * seq
    tiles = (min(PROJ_ROWS, seq), min(ATTN_BLOCK, seq), min(HGRN_ROWS, seq), min(FFN_ROWS, t),
             dff // FFN_COL_CHUNKS, min(MOE_CHUNK, t))
    tm, tq, hg_rows, tm_ffn, fc, tb_moe = tiles
    assert seq % tm == 0 and seq % tq == 0 and seq % hg_rows == 0 and hg_rows % HGRN_CHUNK == 0
    assert t % tm_ffn == 0 and t % tb_moe == 0 and tb_moe % MOE_ISSUE_UNROLL == 0
    assert dff % FFN_COL_CHUNKS == 0 and fc % LANES == 0
    return tiles


def _tile_gain(gain, mult):
    return jnp.tile(gain.astype(jnp.float32) * mult, ATTN_HEADS).reshape(1, ATTN_W)


def kernel(x, norm_mix, w_in, attn_f_bias, q_norm_gain, k_norm_gain, conv_w, hgrn_lb_logits, mix_out_gain, w_out,
           norm_ffn, ffn_w_gate, ffn_w_up, ffn_w_down, moe_router_w, moe_router_b, moe_w_gate, moe_w_up, moe_w_down):
    batch, seq, dm = x.shape
    depth = w_in.shape[0]
    t = batch * seq
    f32 = jnp.float32
    tm, tq, hg_rows, tm_ffn, fc, tb_moe = _plan_tiles(batch, seq, ffn_w_gate.shape[-1])

    p_lb = jax.nn.softmax(hgrn_lb_logits.astype(f32), axis=0)
    lb_all = jnp.cumsum(p_lb, axis=0) - p_lb[0]

    tri_m = _bf16(jnp.tril(jnp.ones((tm, tm), f32)))
    tri_c = _bf16(jnp.tril(jnp.ones((HGRN_CHUNK, HGRN_CHUNK), f32)))
    grp = np.arange(HGRN_W) // HEAD_DIM
    bd = jnp.asarray(grp[:, None] == grp[None, :], jnp.bfloat16)
    selq, selk, oneq, onek = _selection_constants()
    scale = 1.0 / math.sqrt(HEAD_DIM)

    x2 = x.reshape(t, dm)
    for l in range(depth):
        wp = _pack_w_in(w_in[l])
        fb = jnp.pad(attn_f_bias[l].astype(f32), (0, LANES - ATTN_HEADS)).reshape(1, LANES)
        gq = _tile_gain(q_norm_gain[l], scale * LOG2E)
        gk = _tile_gain(k_norm_gain[l], 1.0)
        mog = mix_out_gain[l].astype(f32)
        ga = jnp.broadcast_to(mog[:ATTN_W].reshape(ATTN_HEADS, HEAD_DIM, 1), (ATTN_HEADS, HEAD_DIM, tq))
        gc = mog[ATTN_W:ATTN_W + CONV_C].reshape(1, CONV_C)
        gr = mog[ATTN_W + CONV_C:].reshape(1, HGRN_W)

        q, qaug, k, kaug, v, yc, rq, rlf, rk, rv, rg = _inproj(
            x2, seq, norm_mix[l].astype(f32).reshape(1, dm), wp, tri_m, fb, gq, gk, selq, selk, oneq, onek,
            conv_w[l].astype(f32), gc, bd, lb_all[l].reshape(1, HGRN_W), tm)

        vt = v.reshape(batch, ATTN_HEADS, HEAD_DIM, seq)
        seq3 = lambda a: a.reshape(batch, seq, ATTN_W)
        ya = _attention(q, qaug, seq3(k), seq3(kaug), vt, ga, tq)
        yr = _hgrn(rq, rlf, rk, rv, rg, tri_c, bd, gr, batch, seq, hg_rows)

        j = l // 2
        wo = _bf16(w_out[l])
        gf = norm_ffn[l].astype(f32).reshape(1, dm)
        if l % 2 == 0:
            xo, h2 = _outproj(x2, ya.reshape(t, ATTN_W), yc, yr, wo, gf, tm)
            x2 = _ffn_dense(h2, xo, ffn_w_gate[j], ffn_w_up[j], ffn_w_down[j], tm_ffn, fc)
        else:
            wr32 = jnp.pad(moe_router_w[j].astype(f32), ((0, 0), (0, LANES - N_EXPERTS)))
            wr_hi = _bf16(wr32)
            wr = jnp.concatenate([wr_hi, _bf16(wr32 - wr_hi.astype(f32))], axis=1)
            br = jnp.pad(moe_router_b[j].astype(f32), (0, LANES - N_EXPERTS)).reshape(1, LANES)
            xo, h2, info, info_t, counts = _outproj(x2, ya.reshape(t, ATTN_W), yc, yr, wo, gf, tm,
                                                    router=(wr, br, tri_m))
            n_tiles = (2 * t) // tm_ffn + N_EXPERTS
            pos1, pos2, pad_rows, tile_expert, tile_active = _moe_routing_tables(info_t, counts, tm_ffn, n_tiles)
            xs = _dispatch(pos1, pos2, pad_rows, h2, n_tiles, tb_moe, tm_ffn)
            ys = _ffn_grouped(tile_expert, tile_active, xs,
                              moe_w_gate[j], moe_w_up[j], moe_w_down[j], tm_ffn, fc)
            x2 = _combine(pos1, pos2, ys, xo, info, tb_moe)
    return x2.reshape(batch, seq, dm)
```

```python
import functools
import math

import jax
import jax.numpy as jnp
import numpy as np
from jax import lax
from jax.experimental import pallas as pl
from jax.experimental.pallas import tpu as pltpu

HEAD_DIM = 64
ATTN_HEADS = 8
CONV_C = 256
HGRN_W = 256
ATTN_W = ATTN_HEADS * HEAD_DIM
HGRN_CHUNK = 64
HGRN_SUB = 16
N_EXPERTS = 8
EPS = 1e-6
MASK_VALUE = -1e30
MASK_LOG_DECAY = -1e4
TINY = 1e-30
LOG2E = math.log2(math.e)

LANES = 128
SUBLANES = 8
V7X_VMEM_BYTES = 64 * 1024 * 1024
VMEM_LIMIT_BYTES = V7X_VMEM_BYTES - 8 * 1024 * 1024

PAIR_W = 2 * HEAD_DIM
N_PAIRS = ATTN_HEADS // 2
OFF_Q = 0
OFF_K = OFF_Q + ATTN_W
OFF_V = OFF_K + ATTN_W
OFF_CX = OFF_V + ATTN_W
OFF_CB = OFF_CX + CONV_C
OFF_CC = OFF_CB + CONV_C
OFF_RQ = OFF_CC + CONV_C
OFF_RF = OFF_RQ + HGRN_W
OFF_RI = OFF_RF + HGRN_W
OFF_RG = OFF_RI + HGRN_W
OFF_AF = OFF_RG + HGRN_W
D_PACK = OFF_AF + LANES


def _bf16(x):
    return x.astype(jnp.bfloat16)


def _split3(x):
    p1 = _bf16(x)
    r1 = x - p1.astype(jnp.float32)
    p2 = _bf16(r1)
    r2 = r1 - p2.astype(jnp.float32)
    return p1, p2, _bf16(r2)


def _dot(a, b):
    return jnp.dot(a, b, preferred_element_type=jnp.float32)


def _group_sum(x, bd):
    hi = _bf16(x)
    lo = _bf16(x - hi.astype(jnp.float32))
    return _dot(hi, bd) + _dot(lo, bd)


def _silu(x):
    return x * (1.0 / (1.0 + jnp.exp(-x)))


def _sigmoid(x):
    return 1.0 / (1.0 + jnp.exp(-x))


def _inproj_kernel(x_ref, g_ref, w_ref, tri_ref, fb_ref, gq_ref, gk_ref, selq_ref, selk_ref,
                   oneq_ref, onek_ref, convw_ref, gc_ref, bd_ref, lb_ref,
                   q_ref, qaug_ref, k_ref, kaug_ref, v_ref, yc_ref, rq_ref, rlf_ref, rk_ref, rv_ref, rg_ref,
                   dcarry, ucarry, *, tiles_per_seq):
    i = pl.program_id(0)

    @pl.when(i % tiles_per_seq == 0)
    def _():
        dcarry[...] = jnp.zeros_like(dcarry)
        ucarry[...] = jnp.zeros_like(ucarry)

    x = x_ref[...]
    h = x * lax.rsqrt(jnp.mean(x * x, axis=-1, keepdims=True) + EPS) * g_ref[...]
    hb = _bf16(h)

    zf = _dot(hb, w_ref[:, OFF_AF:OFF_AF + LANES]) + fb_ref[...]
    ls = jnp.minimum(zf, 0.0) - jnp.log(1.0 + jnp.exp(-jnp.abs(zf)))
    p1, p2, p3 = _split3(ls)
    loc = _dot(tri_ref[...], jnp.concatenate([p1, p2, p3], axis=1))
    d = dcarry[...] + loc[:, 0:LANES] + loc[:, LANES:2 * LANES] + loc[:, 2 * LANES:3 * LANES]
    dcarry[...] = d[d.shape[0] - 1:, :]
    e1, e2, e3 = (e.astype(jnp.float32) for e in _split3(d * LOG2E))
    lane = lax.broadcasted_iota(jnp.int32, e1.shape, 1)
    epack = jnp.where(lane < ATTN_HEADS, e1,
                      jnp.where(lane < 2 * ATTN_HEADS, pltpu.roll(e2, ATTN_HEADS, 1),
                                jnp.where(lane < 3 * ATTN_HEADS, pltpu.roll(e3, 2 * ATTN_HEADS, 1), 0.0)))
    epack = _bf16(epack)
    qaug_ref[...] = _bf16(_dot(epack, selq_ref[...]) + oneq_ref[...]).T
    kaug_ref[...] = _bf16(_dot(epack, selk_ref[...]) + onek_ref[...])

    zq = _dot(hb, w_ref[:, OFF_Q:OFF_Q + ATTN_W])
    zk = _dot(hb, w_ref[:, OFF_K:OFF_K + ATTN_W])
    first = lax.broadcasted_iota(jnp.int32, (x.shape[0], PAIR_W), 1) < HEAD_DIM

    def head_norm(z):
        sq = z * z
        s0 = jnp.sum(jnp.where(first, sq, 0.0), axis=-1, keepdims=True)
        s1 = jnp.sum(jnp.where(first, 0.0, sq), axis=-1, keepdims=True)
        return z * lax.rsqrt(jnp.where(first, s0, s1) * (1.0 / HEAD_DIM) + EPS)

    qn = []
    for pb in range(N_PAIRS):
        sl = slice(pb * PAIR_W, (pb + 1) * PAIR_W)
        qn.append(_bf16(head_norm(zq[:, sl]) * gq_ref[:, sl]))
        k_ref[:, sl] = _bf16(head_norm(zk[:, sl]) * gk_ref[:, sl])
    q_ref[...] = jnp.concatenate(qn, axis=1).T
    v_ref[...] = _bf16(_dot(hb, w_ref[:, OFF_V:OFF_V + ATTN_W])).T

    cx = _dot(hb, w_ref[:, OFF_CX:OFF_CX + CONV_C])
    cb = _dot(hb, w_ref[:, OFF_CB:OFF_CB + CONV_C])
    cc = _dot(hb, w_ref[:, OFF_CC:OFF_CC + CONV_C])
    u = cc * cx
    uc = ucarry[...]
    row8 = lax.broadcasted_iota(jnp.int32, (SUBLANES, CONV_C), 0)
    r1 = pltpu.roll(u, 1, 0)
    r2 = pltpu.roll(u, 2, 0)
    top1 = jnp.where(row8 < 1, pltpu.roll(uc, 1, 0), r1[0:SUBLANES])
    top2 = jnp.where(row8 < 2, pltpu.roll(uc, 2, 0), r2[0:SUBLANES])
    u1 = jnp.concatenate([top1, r1[SUBLANES:]], axis=0)
    u2 = jnp.concatenate([top2, r2[SUBLANES:]], axis=0)
    ucarry[...] = u[u.shape[0] - SUBLANES:, :]
    yc = cb * (u2 * convw_ref[0:1, :] + u1 * convw_ref[1:2, :] + u * convw_ref[2:3, :])
    ssc = _group_sum(yc * yc, bd_ref[...])
    yc_ref[...] = _bf16(yc * lax.rsqrt(ssc * (1.0 / HEAD_DIM) + EPS) * gc_ref[...])

    lb = lb_ref[...]
    zr = _dot(hb, w_ref[:, OFF_RF:OFF_RF + HGRN_W])
    sg = _sigmoid(zr)
    f = lb + (1.0 - lb) * sg
    rlf_ref[...] = jnp.log(jnp.maximum(f, TINY))
    rk_ref[...] = (1.0 - lb) * _sigmoid(-zr)
    rq_ref[...] = _silu(_dot(hb, w_ref[:, OFF_RQ:OFF_RQ + HGRN_W]))
    rv_ref[...] = _dot(hb, w_ref[:, OFF_RI:OFF_RI + HGRN_W])
    rg_ref[...] = _silu(_dot(hb, w_ref[:, OFF_RG:OFF_RG + HGRN_W]))


def _inproj(x2, seq, g, wp, tri, fb, gq, gk, selq, selk, oneq, onek, convw, gc, bd, lb, tm):
    t, dm = x2.shape
    full = lambda a: pl.BlockSpec(a.shape, lambda i: (0,) * a.ndim)
    row = lambda w: pl.BlockSpec((tm, w), lambda i: (i, 0))
    consts = (g, wp, tri, fb, gq, gk, selq, selk, oneq, onek, convw, gc, bd, lb)
    tiles_per_seq = seq // tm
    rows_bf16 = jax.ShapeDtypeStruct((t, ATTN_W), jnp.bfloat16)
    cols_bf16 = jax.ShapeDtypeStruct((t // seq, ATTN_W, seq), jnp.bfloat16)
    out_shape = (cols_bf16, cols_bf16, rows_bf16, rows_bf16, cols_bf16,
                 jax.ShapeDtypeStruct((t, CONV_C), jnp.bfloat16),
                 ) + tuple(jax.ShapeDtypeStruct((t, HGRN_W), jnp.float32) for _ in range(5))
    t_spec = pl.BlockSpec((None, ATTN_W, tm), lambda i: (i // tiles_per_seq, 0, i % tiles_per_seq))
    out_specs = ((t_spec, t_spec, row(ATTN_W), row(ATTN_W), t_spec, row(CONV_C))
                 + tuple(row(HGRN_W) for _ in range(5)))
    return pl.pallas_call(
        functools.partial(_inproj_kernel, tiles_per_seq=seq // tm),
        grid=(t // tm,),
        in_specs=[row(dm)] + [full(a) for a in consts],
        out_specs=out_specs,
        out_shape=out_shape,
        scratch_shapes=[pltpu.VMEM((1, LANES), jnp.float32), pltpu.VMEM((SUBLANES, CONV_C), jnp.float32)],
        compiler_params=pltpu.CompilerParams(dimension_semantics=("arbitrary",),
                                             vmem_limit_bytes=VMEM_LIMIT_BYTES),
        name="inproj",
    )(x2, *consts)


def _attn_kernel(q_ref, qaug_ref, k_ref, kaug_ref, vt_ref, gain_ref, out_ref, m_s, acc_s, sa_ref, sb_ref,
                 mxa_ref, mxb_ref, qt_s, *, tq):
    s_a, s_b = (sa_ref, mxa_ref), (sb_ref, mxb_ref)
    i = pl.program_id(2)
    ones_rows = jnp.ones((2 * SUBLANES, tq), jnp.bfloat16)
    m_s[...] = jnp.full_like(m_s, MASK_VALUE)
    acc_s[...] = jnp.zeros_like(acc_s)

    chan = lax.broadcasted_iota(jnp.int32, (PAIR_W, tq), 0)
    zero = jnp.zeros((PAIR_W, tq), jnp.bfloat16)
    for hh in range(2):
        mine = (chan < HEAD_DIM) if hh == 0 else (chan >= HEAD_DIM)
        qt_s[hh] = jnp.concatenate([jnp.where(mine, q_ref[...], zero), jnp.where(mine, qaug_ref[...], zero)],
                                   axis=0)

    def scores(j, s_buf):
        k0 = pl.multiple_of(j * tq, tq)
        krows = jnp.concatenate([k_ref[pl.ds(k0, tq), :], kaug_ref[pl.ds(k0, tq), :]], axis=1)
        for hh in range(2):
            s = _dot(krows, qt_s[hh])
            s_buf[0][hh] = s
            s_buf[1][hh] = jnp.max(s, axis=0, keepdims=True)

    def consume(j, s_buf, masked):
        k0 = pl.multiple_of(j * tq, tq)
        for hh in range(2):
            s = s_buf[0][hh]
            if masked:
                kpos = lax.broadcasted_iota(jnp.int32, (tq, tq), 0)
                qpos = lax.broadcasted_iota(jnp.int32, (tq, tq), 1)
                s = jnp.where(kpos <= qpos, s, MASK_VALUE)
                blk_max = jnp.max(s, axis=0, keepdims=True)
            else:
                blk_max = s_buf[1][hh]
            m = m_s[hh]
            m_new = jnp.maximum(m, blk_max)
            p = jnp.exp2(s - m_new)
            alpha = jnp.exp2(m - m_new)
            vaug = jnp.concatenate([vt_ref[hh, :, pl.ds(k0, tq)], ones_rows], axis=0)
            acc_s[hh] = acc_s[hh] * alpha + _dot(vaug, _bf16(p))
            m_s[hh] = m_new

    scores(0, s_a)

    def pair(j):
        scores(j + 1, s_b)
        consume(j, s_a, False)
        scores(j + 2, s_a)
        consume(j + 1, s_b, False)

    def body(jj, _):
        pair(4 * jj)
        pair(4 * jj + 2)
        return 0

    lax.fori_loop(0, i // 4, body, 0)
    done = 4 * (i // 4)

    @pl.when(i - done >= 2)
    def _():
        pair(done)

    @pl.when(i % 2 == 1)
    def _():
        scores(i, s_b)
        consume(i - 1, s_a, False)
        consume(i, s_b, True)

    @pl.when(i % 2 == 0)
    def _():
        consume(i, s_a, True)

    ys = []
    for hh in range(2):
        acc = acc_s[hh]
        o = acc[0:HEAD_DIM] * (1.0 / acc[HEAD_DIM:HEAD_DIM + 1])
        ms = jnp.mean(o * o, axis=0, keepdims=True)
        ys.append(o * lax.rsqrt(ms + EPS) * gain_ref[hh])
    out_ref[...] = _bf16(jnp.concatenate(ys, axis=0).T)


def _attention(q, qaug, k, kaug, vt, gain, tq):
    b, s, _ = k.shape
    q_spec = pl.BlockSpec((None, PAIR_W, tq), lambda bi, hp, i: (bi, hp, i))
    k_spec = pl.BlockSpec((None, s, PAIR_W), lambda bi, hp, i: (bi, 0, hp))
    return pl.pallas_call(
        functools.partial(_attn_kernel, tq=tq),
        grid=(b, N_PAIRS, s // tq),
        in_specs=[
            q_spec, q_spec, k_spec, k_spec,
            pl.BlockSpec((None, 2, HEAD_DIM, s), lambda bi, hp, i: (bi, hp, 0, 0)),
            pl.BlockSpec((2, HEAD_DIM, tq), lambda bi, hp, i: (hp, 0, 0)),
        ],
        out_specs=pl.BlockSpec((None, tq, 2 * HEAD_DIM), lambda bi, hp, i: (bi, i, hp)),
        out_shape=jax.ShapeDtypeStruct((b, s, ATTN_W), jnp.bfloat16),
        scratch_shapes=[pltpu.VMEM((2, 1, tq), jnp.float32),
                        pltpu.VMEM((2, HEAD_DIM + 2 * SUBLANES, tq), jnp.float32),
                        pltpu.VMEM((2, tq, tq), jnp.float32),
                        pltpu.VMEM((2, tq, tq), jnp.float32),
                        pltpu.VMEM((2, 1, tq), jnp.float32),
                        pltpu.VMEM((2, 1, tq), jnp.float32),
                        pltpu.VMEM((2, 2 * PAIR_W, tq), jnp.bfloat16)],
        compiler_params=pltpu.CompilerParams(dimension_semantics=("arbitrary", "arbitrary", "arbitrary"),
                                             vmem_limit_bytes=VMEM_LIMIT_BYTES),
        name="fox_attention",
    )(q, qaug, k, kaug, vt, gain)


def _hgrn_kernel(q_ref, lf_ref, k_ref, v_ref, g_ref, tri_ref, bd_ref, gain_ref, out_ref,
                 state, c_all, k_all, v_all, *, n_chunks):
    @pl.when(pl.program_id(1) == 0)
    def _():
        state[...] = jnp.zeros_like(state)

    ch = HGRN_CHUNK
    sb = HGRN_SUB
    nsb = ch // sb
    row_sb = lax.broadcasted_iota(jnp.int32, (sb, HGRN_W), 0)
    r128 = lax.broadcasted_iota(jnp.int32, (LANES, LANES), 0)
    c128 = lax.broadcasted_iota(jnp.int32, (LANES, LANES), 1)
    same_head = (r128 < HEAD_DIM) == (c128 < HEAD_DIM)
    t64 = lax.broadcasted_iota(jnp.int32, (ch, LANES), 0) // sb
    s64 = (lax.broadcasted_iota(jnp.int32, (ch, LANES), 1) % HEAD_DIM) // sb
    level2 = ((t64 == 1) & (s64 == 0)) | ((t64 == 3) & (s64 == 2))
    lane_head0 = lax.broadcasted_iota(jnp.int32, (ch, LANES), 1) < HEAD_DIM
    bd = bd_ref[...]
    zeros_sb = jnp.zeros((sb, HGRN_W), jnp.float32)

    def chunk(ci):
        r0 = ci * ch
        c_s, k_s, v_s = c_all.at[ci % 2], k_all.at[ci % 2], v_all.at[ci % 2]
        q = q_ref[pl.ds(r0, ch), :]
        k = k_ref[pl.ds(r0, ch), :]
        v = v_ref[pl.ds(r0, ch), :]
        p1, p2, p3 = _split3(lf_ref[pl.ds(r0, ch), :])
        cc = _dot(tri_ref[...], jnp.concatenate([p1, p2, p3], axis=1))
        c = cc[:, 0:HGRN_W] + cc[:, HGRN_W:2 * HGRN_W] + cc[:, 2 * HGRN_W:3 * HGRN_W]
        c_s[...] = c
        k_s[...] = k
        v_s[...] = v
        blk = lambda a, n: a[n * sb:(n + 1) * sb]

        ps = []
        for n in range(nsb):
            cn, qn = blk(c, n), blk(q, n)
            for s in range(sb):
                r = n * sb + s
                dec = jnp.exp(jnp.where(row_sb >= s, cn - c_s[r:r + 1, :], MASK_LOG_DECAY))
                ps.append(_bf16(qn * k_s[r:r + 1, :] * dec))
        a_d = _dot(jnp.concatenate(ps, axis=0), bd)
        o_parts = []
        for n in range(nsb):
            acc = jnp.zeros((sb, HGRN_W), jnp.float32)
            for s in range(sb):
                r = n * sb + s
                acc = acc + a_d[r * sb:(r + 1) * sb] * v_s[r:r + 1, :]
            o_parts.append(acc)
        o = jnp.concatenate(o_parts, axis=0)

        ref1 = c[2 * sb - 1:2 * sb]
        ref2a = c[sb - 1:sb]
        ref2b = c[3 * sb - 1:3 * sb]
        hi, lo = slice(2 * sb, 4 * sb), slice(0, 2 * sb)
        zeros_half = jnp.zeros((2 * sb, HGRN_W), jnp.float32)
        q1 = jnp.concatenate([zeros_half, q[hi] * jnp.exp(c[hi] - ref1)], axis=0)
        k1 = jnp.concatenate([k[lo] * jnp.exp(ref1 - c[lo]), zeros_half], axis=0)
        q2 = jnp.concatenate([zeros_sb, blk(q, 1) * jnp.exp(blk(c, 1) - ref2a),
                              zeros_sb, blk(q, 3) * jnp.exp(blk(c, 3) - ref2b)], axis=0)
        k2 = jnp.concatenate([blk(k, 0) * jnp.exp(ref2a - blk(c, 0)), zeros_sb,
                              blk(k, 2) * jnp.exp(ref2b - blk(c, 2)), zeros_sb], axis=0)

        c_last = c[ch - 1:ch, :]
        qe = q * jnp.exp(c)
        kd = k * jnp.exp(c_last - c)
        e_last = jnp.exp(c_last)
        nt = (((1,), (1,)), ((), ()))
        o_off = []
        for bb in range(HGRN_W // LANES):
            sl = slice(bb * LANES, (bb + 1) * LANES)
            st = state[bb]
            stack2 = lambda a: jnp.concatenate([jnp.where(lane_head0, a[:, sl], 0.0),
                                                jnp.where(lane_head0, 0.0, a[:, sl])], axis=0)
            a1 = lax.dot_general(_bf16(q1[:, sl]), _bf16(stack2(k1)), nt, preferred_element_type=jnp.float32)
            a2 = lax.dot_general(_bf16(q2[:, sl]), _bf16(stack2(k2)), nt, preferred_element_type=jnp.float32)
            a_off = a1 + jnp.where(level2, a2, 0.0)
            lhs = jnp.concatenate([_bf16(a_off), _bf16(qe[:, sl])], axis=1)
            rhs = jnp.concatenate([_bf16(stack2(v)), _bf16(st)], axis=0)
            o_off.append(_dot(lhs, rhs))
            ecol = jnp.broadcast_to(e_last[:, sl], (LANES, LANES)).T
            upd = lax.dot_general(_bf16(kd[:, sl]), _bf16(v[:, sl]), (((0,), (0,)), ((), ())),
                                  preferred_element_type=jnp.float32)
            state[bb] = jnp.where(same_head, ecol * st + upd, 0.0)
        o = o + jnp.concatenate(o_off, axis=1)

        ss = _group_sum(o * o, bd)
        y = o * lax.rsqrt(ss * (1.0 / HEAD_DIM) + EPS) * gain_ref[...] * g_ref[pl.ds(r0, ch), :]
        out_ref[pl.ds(r0, ch), :] = _bf16(y)

    for ci in range(n_chunks):
        chunk(ci)


def _hgrn(rq, rlf, rk, rv, rg, tri, bd, gain, batch, seq, rows):
    t = rq.shape[0]
    steps = seq // rows
    blk = pl.BlockSpec((rows, HGRN_W), lambda b, j: (b * steps + j, 0))
    full = lambda a: pl.BlockSpec(a.shape, lambda b, j: (0,) * a.ndim)
    return pl.pallas_call(
        functools.partial(_hgrn_kernel, n_chunks=rows // HGRN_CHUNK),
        grid=(batch, steps),
        in_specs=[blk, blk, blk, blk, blk, full(tri), full(bd), full(gain)],
        out_specs=blk,
        out_shape=jax.ShapeDtypeStruct((t, HGRN_W), jnp.bfloat16),
        scratch_shapes=[pltpu.VMEM((HGRN_W // LANES, LANES, LANES), jnp.float32)]
        + [pltpu.VMEM((2, HGRN_CHUNK, HGRN_W), jnp.float32) for _ in range(3)],
        compiler_params=pltpu.CompilerParams(dimension_semantics=("arbitrary", "arbitrary"),
                                             vmem_limit_bytes=VMEM_LIMIT_BYTES),
        name="hgrn2",
    )(rq, rlf, rk, rv, rg, tri, bd, gain)


INFO_E1, INFO_E2, INFO_R1, INFO_R2, INFO_W1, INFO_W2 = range(6)


def _mix_and_norm(x_ref, ya_ref, yc_ref, yr_ref, w_ref, g_ref):
    xn = (x_ref[...]
          + _dot(ya_ref[...], w_ref[0:ATTN_W, :])
          + _dot(yc_ref[...], w_ref[ATTN_W:ATTN_W + CONV_C, :])
          + _dot(yr_ref[...], w_ref[ATTN_W + CONV_C:, :]))
    h = xn * lax.rsqrt(jnp.mean(xn * xn, axis=-1, keepdims=True) + EPS) * g_ref[...]
    return xn, h


def _outproj_dense_kernel(x_ref, ya_ref, yc_ref, yr_ref, w_ref, g_ref, xo_ref, h_ref):
    xn, h = _mix_and_norm(x_ref, ya_ref, yc_ref, yr_ref, w_ref, g_ref)
    xo_ref[...] = xn
    h_ref[...] = _bf16(h)


def _outproj_routed_kernel(x_ref, ya_ref, yc_ref, yr_ref, w_ref, g_ref, wr_ref, br_ref, tri_ref,
                           xo_ref, h_ref, info_ref, info_t_ref, cnt_ref, cnt_s):
    @pl.when(pl.program_id(0) == 0)
    def _():
        cnt_s[...] = jnp.zeros_like(cnt_s)

    xn, h = _mix_and_norm(x_ref, ya_ref, yc_ref, yr_ref, w_ref, g_ref)
    xo_ref[...] = xn
    h_ref[...] = h
    h_hi = _bf16(h)
    h_lo = _bf16(h - h_hi.astype(jnp.float32))
    hw = _dot(h_hi, wr_ref[...])
    logits = hw[:, 0:LANES] + hw[:, LANES:2 * LANES] + _dot(h_lo, wr_ref[:, 0:LANES]) + br_ref[...]
    lane = lax.broadcasted_iota(jnp.int32, logits.shape, 1)
    logits = jnp.where(lane < N_EXPERTS, logits, MASK_VALUE)
    m1 = jnp.max(logits, axis=-1, keepdims=True)
    i1 = jnp.min(jnp.where(logits == m1, lane, LANES), axis=-1, keepdims=True)
    rest = jnp.where(lane == i1, MASK_VALUE, logits)
    m2 = jnp.max(rest, axis=-1, keepdims=True)
    i2 = jnp.min(jnp.where(rest == m2, lane, LANES), axis=-1, keepdims=True)
    e2 = jnp.exp(m2 - m1)
    w1 = 1.0 / (1.0 + e2)
    w2 = e2 * w1
    hit = (lane == i1) | (lane == i2)
    onehot = jnp.where(hit, 1.0, 0.0)
    incl = _dot(tri_ref[...], _bf16(onehot))
    rank = cnt_s[...] + incl - onehot
    r1 = jnp.sum(jnp.where(lane == i1, rank, 0.0), axis=-1, keepdims=True)
    r2 = jnp.sum(jnp.where(lane == i2, rank, 0.0), axis=-1, keepdims=True)
    cnt_new = cnt_s[...] + incl[incl.shape[0] - 1:, :]
    cnt_s[...] = cnt_new
    cnt_ref[...] = jnp.broadcast_to(cnt_new, cnt_ref.shape)
    rec = jnp.zeros(logits.shape, jnp.float32)
    for ln, val in ((INFO_E1, i1.astype(jnp.float32)), (INFO_E2, i2.astype(jnp.float32)),
                    (INFO_R1, r1), (INFO_R2, r2), (INFO_W1, w1), (INFO_W2, w2)):
        rec = jnp.where(lane == ln, val, rec)
    info_ref[...] = rec
    info_t_ref[...] = rec.T[0:SUBLANES, :]


def _outproj(x2, ya, yc, yr, wo, g, tm, router=None):
    t, dm = x2.shape
    row = lambda w: pl.BlockSpec((tm, w), lambda i: (i, 0))
    full = lambda a: pl.BlockSpec(a.shape, lambda i: (0,) * a.ndim)
    params = pltpu.CompilerParams(dimension_semantics=("arbitrary",), vmem_limit_bytes=VMEM_LIMIT_BYTES)
    base_specs = [row(dm), row(ATTN_W), row(CONV_C), row(HGRN_W), full(wo), full(g)]
    if router is None:
        return pl.pallas_call(
            _outproj_dense_kernel,
            grid=(t // tm,),
            in_specs=base_specs,
            out_specs=(row(dm), row(dm)),
            out_shape=(jax.ShapeDtypeStruct((t, dm), jnp.float32), jax.ShapeDtypeStruct((t, dm), jnp.bfloat16)),
            compiler_params=params,
            name="outproj_dense",
        )(x2, ya, yc, yr, wo, g)
    wr, br, tri = router
    return pl.pallas_call(
        _outproj_routed_kernel,
        grid=(t // tm,),
        in_specs=base_specs + [full(wr), full(br), full(tri)],
        out_specs=(row(dm), row(dm), row(LANES), pl.BlockSpec((SUBLANES, tm), lambda i: (0, i)),
                   pl.BlockSpec((SUBLANES, LANES), lambda i: (0, 0))),
        out_shape=(jax.ShapeDtypeStruct((t, dm), jnp.float32), jax.ShapeDtypeStruct((t, dm), jnp.float32),
                   jax.ShapeDtypeStruct((t, LANES), jnp.float32),
                   jax.ShapeDtypeStruct((SUBLANES, t), jnp.float32),
                   jax.ShapeDtypeStruct((SUBLANES, LANES), jnp.float32)),
        scratch_shapes=[pltpu.VMEM((1, LANES), jnp.float32)],
        compiler_params=params,
        name="outproj_routed",
    )(x2, ya, yc, yr, wo, g, wr, br, tri)


def _swiglu_step(h_ref, wg_ref, wu_ref, wd_ref):
    h = _bf16(h_ref[...])
    gt = _dot(h, _bf16(wg_ref[...]))
    up = _dot(h, _bf16(wu_ref[...]))
    act = _bf16(gt * (1.0 / (1.0 + jnp.exp(-gt))) * up)
    return _dot(act, _bf16(wd_ref[...]))


def _ffn_dense_kernel(h_ref, x_ref, wg_ref, wu_ref, wd_ref, out_ref):
    @pl.when(pl.program_id(1) == 0)
    def _():
        out_ref[...] = x_ref[...]

    out_ref[...] += _swiglu_step(h_ref, wg_ref, wu_ref, wd_ref)


def _ffn_dense(h, x2, wg, wu, wd, tm, fc):
    t, dm = x2.shape
    dff = wg.shape[1]
    return pl.pallas_call(
        _ffn_dense_kernel,
        grid=(t // tm, dff // fc),
        in_specs=[
            pl.BlockSpec((tm, dm), lambda i, f: (i, 0)),
            pl.BlockSpec((tm, dm), lambda i, f: (i, 0)),
            pl.BlockSpec((dm, fc), lambda i, f: (0, f)),
            pl.BlockSpec((dm, fc), lambda i, f: (0, f)),
            pl.BlockSpec((fc, dm), lambda i, f: (f, 0)),
        ],
        out_specs=pl.BlockSpec((tm, dm), lambda i, f: (i, 0)),
        out_shape=jax.ShapeDtypeStruct((t, dm), jnp.float32),
        compiler_params=pltpu.CompilerParams(dimension_semantics=("arbitrary", "arbitrary"),
                                             vmem_limit_bytes=VMEM_LIMIT_BYTES),
        name="swiglu_dense",
    )(h, x2, wg, wu, wd)


def _ffn_grouped_kernel(te_ref, ta_ref, h_ref, wg_ref, wu_ref, wd_ref, out_ref):
    @pl.when(pl.program_id(1) == 0)
    def _():
        out_ref[...] = jnp.zeros_like(out_ref)

    @pl.when(ta_ref[pl.program_id(0)] == 1)
    def _():
        out_ref[...] += _swiglu_step(h_ref, wg_ref, wu_ref, wd_ref)


def _ffn_grouped(tile_expert, tile_active, xs, wg, wu, wd, tm, fc):
    dm = xs.shape[1]
    n_tiles = tile_expert.shape[0]
    r = n_tiles * tm
    dff = wg.shape[2]
    nf = dff // fc
    fidx = lambda i, f, ta: f * ta[i] + (nf - 1) * (1 - ta[i])
    grid_spec = pltpu.PrefetchScalarGridSpec(
        num_scalar_prefetch=2,
        grid=(n_tiles, nf),
        in_specs=[
            pl.BlockSpec((tm, dm), lambda i, f, te, ta: (i * ta[i], 0)),
            pl.BlockSpec((None, dm, fc), lambda i, f, te, ta: (te[i], 0, fidx(i, f, ta))),
            pl.BlockSpec((None, dm, fc), lambda i, f, te, ta: (te[i], 0, fidx(i, f, ta))),
            pl.BlockSpec((None, fc, dm), lambda i, f, te, ta: (te[i], fidx(i, f, ta), 0)),
        ],
        out_specs=pl.BlockSpec((tm, dm), lambda i, f, te, ta: (i, 0)),
    )
    return pl.pallas_call(
        _ffn_grouped_kernel,
        grid_spec=grid_spec,
        out_shape=jax.ShapeDtypeStruct((r, dm), jnp.float32),
        compiler_params=pltpu.CompilerParams(dimension_semantics=("arbitrary", "arbitrary"),
                                             vmem_limit_bytes=VMEM_LIMIT_BYTES),
        name="swiglu_grouped",
    )(tile_expert, tile_active, xs, wg, wu, wd)


def _row_copy(src, dst, sem):
    return pltpu.make_async_copy(src, dst, sem)


MOE_ISSUE_UNROLL = 8


def _wait_rows(ref, n_rows, sem):
    blk = ref.at[pl.ds(0, n_rows), :]
    pltpu.make_async_copy(blk, blk, sem).wait()


MOE_STAGE_BUFS = 3


def _dispatch_kernel(p1_ref, p2_ref, pad_ref, h_ref, xs_ref, stage, zrow, load_sems, scat_sems, zero_sem,
                     *, tb, n_chunks):
    def load(c, slot):
        return pltpu.make_async_copy(h_ref.at[pl.ds(c * tb, tb), :], stage.at[slot], load_sems.at[slot])

    load(0, 0).start()

    zrow[...] = jnp.zeros_like(zrow)
    for e in range(N_EXPERTS):
        def zero(r, _, e=e):
            _row_copy(zrow.at[pl.ds(0, 1), :], xs_ref.at[pl.ds(pad_ref[e] + r, 1), :], zero_sem).start()
            return 0

        lax.fori_loop(0, pad_ref[N_EXPERTS + e], zero, 0)
    for e in range(N_EXPERTS):
        def zero_done(r, _):
            _wait_rows(xs_ref, 1, zero_sem)
            return 0

        lax.fori_loop(0, pad_ref[N_EXPERTS + e], zero_done, 0)

    def chunk(c, _):
        slot = lax.rem(c, MOE_STAGE_BUFS)
        load(c, slot).wait()

        @pl.when(c >= 2)
        def _():
            _wait_rows(xs_ref, 2 * tb, scat_sems.at[lax.rem(c - 2, MOE_STAGE_BUFS)])

        @pl.when(c + 1 < n_chunks)
        def _():
            load(c + 1, lax.rem(c + 1, MOE_STAGE_BUFS)).start()

        sem = scat_sems.at[slot]

        def issue(rr, _):
            for u in range(MOE_ISSUE_UNROLL):
                r = rr * MOE_ISSUE_UNROLL + u
                tok = c * tb + r
                src = stage.at[slot, pl.ds(r, 1), :]
                _row_copy(src, xs_ref.at[pl.ds(p1_ref[tok], 1), :], sem).start()
                _row_copy(src, xs_ref.at[pl.ds(p2_ref[tok], 1), :], sem).start()
            return 0

        lax.fori_loop(0, tb // MOE_ISSUE_UNROLL, issue, 0)
        return 0

    lax.fori_loop(0, n_chunks, chunk, 0)
    for c in range(max(n_chunks - 2, 0), n_chunks):
        _wait_rows(xs_ref, 2 * tb, scat_sems.at[c % MOE_STAGE_BUFS])


def _dispatch(pos1, pos2, pad_start, h, n_tiles, tb, tm):
    t, dm = h.shape
    n_chunks = t // tb
    grid_spec = pltpu.PrefetchScalarGridSpec(
        num_scalar_prefetch=3,
        grid=(1,),
        in_specs=[pl.BlockSpec(memory_space=pl.ANY)],
        out_specs=pl.BlockSpec(memory_space=pl.ANY),
        scratch_shapes=[pltpu.VMEM((MOE_STAGE_BUFS, tb, dm), h.dtype),
                        pltpu.VMEM((SUBLANES, dm), h.dtype),
                        pltpu.SemaphoreType.DMA((MOE_STAGE_BUFS,)),
                        pltpu.SemaphoreType.DMA((MOE_STAGE_BUFS,)),
                        pltpu.SemaphoreType.DMA(())],
    )
    return pl.pallas_call(
        functools.partial(_dispatch_kernel, tb=tb, n_chunks=n_chunks),
        grid_spec=grid_spec,
        out_shape=jax.ShapeDtypeStruct((n_tiles * tm, dm), h.dtype),
        compiler_params=pltpu.CompilerParams(dimension_semantics=("arbitrary",), has_side_effects=True),
        name="moe_dispatch",
    )(pos1, pos2, pad_start, h)


def _combine_kernel(p1_ref, p2_ref, y_ref, x_ref, info_ref, out_ref, buf_a, buf_b, sems, *, tb):
    i = pl.program_id(0)
    n = pl.num_programs(0)

    def issue_step(step, buf, sem):
        def issue(rr, _):
            for u in range(MOE_ISSUE_UNROLL):
                r = rr * MOE_ISSUE_UNROLL + u
                tok = step * tb + r
                _row_copy(y_ref.at[pl.ds(p1_ref[tok], 1), :], buf.at[0, pl.ds(r, 1), :], sem).start()
                _row_copy(y_ref.at[pl.ds(p2_ref[tok], 1), :], buf.at[1, pl.ds(r, 1), :], sem).start()
            return 0

        lax.fori_loop(0, tb // MOE_ISSUE_UNROLL, issue, 0)

    def finish(buf, sem):
        _wait_rows(y_ref, 2 * tb, sem)
        info = info_ref[...]
        lane = lax.broadcasted_iota(jnp.int32, info.shape, 1)
        w1 = jnp.sum(jnp.where(lane == INFO_W1, info, 0.0), axis=-1, keepdims=True)
        w2 = jnp.sum(jnp.where(lane == INFO_W2, info, 0.0), axis=-1, keepdims=True)
        out_ref[...] = x_ref[...] + w1 * buf[0] + w2 * buf[1]

    @pl.when(i == 0)
    def _():
        issue_step(0, buf_a, sems.at[0])

    @pl.when(i % 2 == 0)
    def _():
        @pl.when(i + 1 < n)
        def _():
            issue_step(i + 1, buf_b, sems.at[1])

        finish(buf_a, sems.at[0])

    @pl.when(i % 2 == 1)
    def _():
        @pl.when(i + 1 < n)
        def _():
            issue_step(i + 1, buf_a, sems.at[0])

        finish(buf_b, sems.at[1])


def _combine(pos1, pos2, y, x2, info, tb):
    t, dm = x2.shape
    grid_spec = pltpu.PrefetchScalarGridSpec(
        num_scalar_prefetch=2,
        grid=(t // tb,),
        in_specs=[pl.BlockSpec(memory_space=pl.ANY),
                  pl.BlockSpec((tb, dm), lambda i, p1, p2: (i, 0)),
                  pl.BlockSpec((tb, LANES), lambda i, p1, p2: (i, 0))],
        out_specs=pl.BlockSpec((tb, dm), lambda i, p1, p2: (i, 0)),
        scratch_shapes=[pltpu.VMEM((2, tb, dm), jnp.float32), pltpu.VMEM((2, tb, dm), jnp.float32),
                        pltpu.SemaphoreType.DMA((2,))],
    )
    return pl.pallas_call(
        functools.partial(_combine_kernel, tb=tb),
        grid_spec=grid_spec,
        out_shape=jax.ShapeDtypeStruct((t, dm), jnp.float32),
        compiler_params=pltpu.CompilerParams(dimension_semantics=("arbitrary",)),
        name="moe_combine",
    )(pos1, pos2, y, x2, info)


def _moe_routing_tables(info_t, counts, tm, n_tiles):
    cnt = counts[0, :N_EXPERTS].astype(jnp.int32)
    padded = ((cnt + tm - 1) // tm) * tm
    ends = jnp.cumsum(padded)
    offsets = ends - padded
    e1 = info_t[INFO_E1].astype(jnp.int32)
    e2 = info_t[INFO_E2].astype(jnp.int32)
    pos1 = offsets[e1] + info_t[INFO_R1].astype(jnp.int32)
    pos2 = offsets[e2] + info_t[INFO_R2].astype(jnp.int32)
    start = jnp.arange(n_tiles, dtype=jnp.int32) * tm
    tile_expert = jnp.minimum(jnp.sum((start[:, None] >= ends[None, :]).astype(jnp.int32), axis=1), N_EXPERTS - 1)
    tile_active = (start < ends[-1]).astype(jnp.int32)
    pad_rows = jnp.concatenate([offsets + cnt, padded - cnt])
    return pos1, pos2, pad_rows, tile_expert, tile_active


def _pack_w_in(w):
    s = np.cumsum([0, ATTN_W, ATTN_W, ATTN_W, ATTN_HEADS, CONV_C, CONV_C, CONV_C, HGRN_W, HGRN_W, HGRN_W, HGRN_W])
    seg = [w[:, s[n]:s[n + 1]] for n in range(11)]
    a_q, a_k, a_v, a_f, c_x, c_b, c_c, r_q, r_f, r_i, r_g = seg
    a_f = jnp.pad(a_f, ((0, 0), (0, LANES - ATTN_HEADS)))
    return _bf16(jnp.concatenate([a_q, a_k, a_v, c_x, c_b, c_c, r_q, r_f, r_i, r_g, a_f], axis=1))


def _selection_constants():
    selq = np.zeros((LANES, ATTN_W), np.float32)
    selk = np.zeros((LANES, ATTN_W), np.float32)
    oneq = np.zeros((1, ATTN_W), np.float32)
    onek = np.zeros((1, ATTN_W), np.float32)
    for hd in range(ATTN_HEADS):
        base = hd * HEAD_DIM
        for piece in range(3):
            selq[piece * ATTN_HEADS + hd, base + piece] = 1.0
            selk[piece * ATTN_HEADS + hd, base + 3 + piece] = -1.0
            oneq[0, base + 3 + piece] = 1.0
            onek[0, base + piece] = 1.0
    return (jnp.asarray(selq, jnp.bfloat16), jnp.asarray(selk, jnp.bfloat16),
            jnp.asarray(oneq), jnp.asarray(onek))


PROJ_ROWS = 512
ATTN_BLOCK = 512
HGRN_ROWS = 512
FFN_ROWS = 1024
FFN_COL_CHUNKS = 4
MOE_CHUNK = 512


def _plan_tiles(batch, seq, dff):
    t = batch * seq
    tiles = (min(PROJ_ROWS, seq), min(ATTN_BLOCK, seq), min(HGRN_ROWS, seq), min(FFN_ROWS, t),
             dff // FFN_COL_CHUNKS, min(MOE_CHUNK, t))
    tm, tq, hg_rows, tm_ffn, fc, tb_moe = tiles
    assert seq % tm == 0 and seq % tq == 0 and seq % hg_rows == 0 and hg_rows % HGRN_CHUNK == 0
    assert t % tm_ffn == 0 and t % tb_moe == 0 and tb_moe % MOE_ISSUE_UNROLL == 0
    assert dff % FFN_COL_CHUNKS == 0 and fc % LANES == 0
    return tiles


def _tile_gain(gain, mult):
    return jnp.tile(gain.astype(jnp.float32) * mult, ATTN_HEADS).reshape(1, ATTN_W)


def kernel(x, norm_mix, w_in, attn_f_bias, q_norm_gain, k_norm_gain, conv_w, hgrn_lb_logits, mix_out_gain, w_out,
           norm_ffn, ffn_w_gate, ffn_w_up, ffn_w_down, moe_router_w, moe_router_b, moe_w_gate, moe_w_up, moe_w_down):
    batch, seq, dm = x.shape
    depth = w_in.shape[0]
    t = batch * seq
    f32 = jnp.float32
    tm, tq, hg_rows, tm_ffn, fc, tb_moe = _plan_tiles(batch, seq, ffn_w_gate.shape[-1])

    p_lb = jax.nn.softmax(hgrn_lb_logits.astype(f32), axis=0)
    lb_all = jnp.cumsum(p_lb, axis=0) - p_lb[0]

    tri_m = _bf16(jnp.tril(jnp.ones((tm, tm), f32)))
    tri_c = _bf16(jnp.tril(jnp.ones((HGRN_CHUNK, HGRN_CHUNK), f32)))
    grp = np.arange(HGRN_W) // HEAD_DIM
    bd = jnp.asarray(grp[:, None] == grp[None, :], jnp.bfloat16)
    selq, selk, oneq, onek = _selection_constants()
    scale = 1.0 / math.sqrt(HEAD_DIM)

    x2 = x.reshape(t, dm)
    for l in range(depth):
        wp = _pack_w_in(w_in[l])
        fb = jnp.pad(attn_f_bias[l].astype(f32), (0, LANES - ATTN_HEADS)).reshape(1, LANES)
        gq = _tile_gain(q_norm_gain[l], scale * LOG2E)
        gk = _tile_gain(k_norm_gain[l], 1.0)
        mog = mix_out_gain[l].astype(f32)
        ga = jnp.broadcast_to(mog[:ATTN_W].reshape(ATTN_HEADS, HEAD_DIM, 1), (ATTN_HEADS, HEAD_DIM, tq))
        gc = mog[ATTN_W:ATTN_W + CONV_C].reshape(1, CONV_C)
        gr = mog[ATTN_W + CONV_C:].reshape(1, HGRN_W)

        q, qaug, k, kaug, v, yc, rq, rlf, rk, rv, rg = _inproj(
            x2, seq, norm_mix[l].astype(f32).reshape(1, dm), wp, tri_m, fb, gq, gk, selq, selk, oneq, onek,
            conv_w[l].astype(f32), gc, bd, lb_all[l].reshape(1, HGRN_W), tm)

        vt = v.reshape(batch, ATTN_HEADS, HEAD_DIM, seq)
        seq3 = lambda a: a.reshape(batch, seq, ATTN_W)
        ya = _attention(q, qaug, seq3(k), seq3(kaug), vt, ga, tq)
        yr = _hgrn(rq, rlf, rk, rv, rg, tri_c, bd, gr, batch, seq, hg_rows)

        j = l // 2
        wo = _bf16(w_out[l])
        gf = norm_ffn[l].astype(f32).reshape(1, dm)
        if l % 2 == 0:
            xo, h2 = _outproj(x2, ya.reshape(t, ATTN_W), yc, yr, wo, gf, tm)
            x2 = _ffn_dense(h2, xo, ffn_w_gate[j], ffn_w_up[j], ffn_w_down[j], tm_ffn, fc)
        else:
            wr32 = jnp.pad(moe_router_w[j].astype(f32), ((0, 0), (0, LANES - N_EXPERTS)))
            wr_hi = _bf16(wr32)
            wr = jnp.concatenate([wr_hi, _bf16(wr32 - wr_hi.astype(f32))], axis=1)
            br = jnp.pad(moe_router_b[j].astype(f32), (0, LANES - N_EXPERTS)).reshape(1, LANES)
            xo, h2, info, info_t, counts = _outproj(x2, ya.reshape(t, ATTN_W), yc, yr, wo, gf, tm,
                                                    router=(wr, br, tri_m))
            n_tiles = (2 * t) // tm_ffn + N_EXPERTS
            pos1, pos2, pad_rows, tile_expert, tile_active = _moe_routing_tables(info_t, counts, tm_ffn, n_tiles)
            xs = _dispatch(pos1, pos2, pad_rows, h2, n_tiles, tb_moe, tm_ffn)
            ys = _ffn_grouped(tile_expert, tile_active, xs,
                              moe_w_gate[j], moe_w_up[j], moe_w_down[j], tm_ffn, fc)
            x2 = _combine(pos1, pos2, ys, xo, info, tb_moe)
    return x2.reshape(batch, seq, dm)
```

```python
import functools
import math

import jax
import jax.numpy as jnp
import numpy as np
from jax import lax
from jax.experimental import pallas as pl
from jax.experimental.pallas import tpu as pltpu

HEAD_DIM = 64
ATTN_HEADS = 8
CONV_C = 256
HGRN_W = 256
ATTN_W = ATTN_HEADS * HEAD_DIM
HGRN_CHUNK = 64
HGRN_SUB = 16
N_EXPERTS = 8
EPS = 1e-6
MASK_VALUE = -1e30
MASK_LOG_DECAY = -1e4
TINY = 1e-30
LOG2E = math.log2(math.e)

LANES = 128
SUBLANES = 8
V7X_VMEM_BYTES = 64 * 1024 * 1024
VMEM_LIMIT_BYTES = V7X_VMEM_BYTES - 8 * 1024 * 1024

PAIR_W = 2 * HEAD_DIM
N_PAIRS = ATTN_HEADS // 2
OFF_Q = 0
OFF_K = OFF_Q + ATTN_W
OFF_V = OFF_K + ATTN_W
OFF_CX = OFF_V + ATTN_W
OFF_CB = OFF_CX + CONV_C
OFF_CC = OFF_CB + CONV_C
OFF_RQ = OFF_CC + CONV_C
OFF_RF = OFF_RQ + HGRN_W
OFF_RI = OFF_RF + HGRN_W
OFF_RG = OFF_RI + HGRN_W
OFF_AF = OFF_RG + HGRN_W
D_PACK = OFF_AF + LANES


def _bf16(x):
    return x.astype(jnp.bfloat16)


def _split3(x):
    p1 = _bf16(x)
    r1 = x - p1.astype(jnp.float32)
    p2 = _bf16(r1)
    r2 = r1 - p2.astype(jnp.float32)
    return p1, p2, _bf16(r2)


def _dot(a, b):
    return jnp.dot(a, b, preferred_element_type=jnp.float32)


def _group_sum(x, bd):
    hi = _bf16(x)
    lo = _bf16(x - hi.astype(jnp.float32))
    return _dot(hi, bd) + _dot(lo, bd)


def _silu(x):
    return x * (1.0 / (1.0 + jnp.exp(-x)))


def _sigmoid(x):
    return 1.0 / (1.0 + jnp.exp(-x))


def _inproj_kernel(x_ref, g_ref, w_ref, tri_ref, fb_ref, gq_ref, gk_ref, selq_ref, selk_ref,
                   oneq_ref, onek_ref, convw_ref, gc_ref, bd_ref, lb_ref,
                   q_ref, qaug_ref, k_ref, kaug_ref, v_ref, yc_ref, rq_ref, rlf_ref, rk_ref, rv_ref, rg_ref,
                   dcarry, ucarry, *, tiles_per_seq):
    i = pl.program_id(0)

    @pl.when(i % tiles_per_seq == 0)
    def _():
        dcarry[...] = jnp.zeros_like(dcarry)
        ucarry[...] = jnp.zeros_like(ucarry)

    x = x_ref[...]
    h = x * lax.rsqrt(jnp.mean(x * x, axis=-1, keepdims=True) + EPS) * g_ref[...]
    hb = _bf16(h)

    zf = _dot(hb, w_ref[:, OFF_AF:OFF_AF + LANES]) + fb_ref[...]
    ls = jnp.minimum(zf, 0.0) - jnp.log(1.0 + jnp.exp(-jnp.abs(zf)))
    p1, p2, p3 = _split3(ls)
    loc = _dot(tri_ref[...], jnp.concatenate([p1, p2, p3], axis=1))
    d = dcarry[...] + loc[:, 0:LANES] + loc[:, LANES:2 * LANES] + loc[:, 2 * LANES:3 * LANES]
    dcarry[...] = d[d.shape[0] - 1:, :]
    e1, e2, e3 = (e.astype(jnp.float32) for e in _split3(d * LOG2E))
    lane = lax.broadcasted_iota(jnp.int32, e1.shape, 1)
    epack = jnp.where(lane < ATTN_HEADS, e1,
                      jnp.where(lane < 2 * ATTN_HEADS, pltpu.roll(e2, ATTN_HEADS, 1),
                                jnp.where(lane < 3 * ATTN_HEADS, pltpu.roll(e3, 2 * ATTN_HEADS, 1), 0.0)))
    epack = _bf16(epack)
    qaug_ref[...] = _bf16(_dot(epack, selq_ref[...]) + oneq_ref[...]).T
    kaug_ref[...] = _bf16(_dot(epack, selk_ref[...]) + onek_ref[...])

    zq = _dot(hb, w_ref[:, OFF_Q:OFF_Q + ATTN_W])
    zk = _dot(hb, w_ref[:, OFF_K:OFF_K + ATTN_W])
    first = lax.broadcasted_iota(jnp.int32, (x.shape[0], PAIR_W), 1) < HEAD_DIM

    def head_norm(z):
        sq = z * z
        s0 = jnp.sum(jnp.where(first, sq, 0.0), axis=-1, keepdims=True)
        s1 = jnp.sum(jnp.where(first, 0.0, sq), axis=-1, keepdims=True)
        return z * lax.rsqrt(jnp.where(first, s0, s1) * (1.0 / HEAD_DIM) + EPS)

    qn = []
    for pb in range(N_PAIRS):
        sl = slice(pb * PAIR_W, (pb + 1) * PAIR_W)
        qn.append(_bf16(head_norm(zq[:, sl]) * gq_ref[:, sl]))
        k_ref[:, sl] = _bf16(head_norm(zk[:, sl]) * gk_ref[:, sl])
    q_ref[...] = jnp.concatenate(qn, axis=1).T
    v_ref[...] = _bf16(_dot(hb, w_ref[:, OFF_V:OFF_V + ATTN_W])).T

    cx = _dot(hb, w_ref[:, OFF_CX:OFF_CX + CONV_C])
    cb = _dot(hb, w_ref[:, OFF_CB:OFF_CB + CONV_C])
    cc = _dot(hb, w_ref[:, OFF_CC:OFF_CC + CONV_C])
    u = cc * cx
    uc = ucarry[...]
    row8 = lax.broadcasted_iota(jnp.int32, (SUBLANES, CONV_C), 0)
    r1 = pltpu.roll(u, 1, 0)
    r2 = pltpu.roll(u, 2, 0)
    top1 = jnp.where(row8 < 1, pltpu.roll(uc, 1, 0), r1[0:SUBLANES])
    top2 = jnp.where(row8 < 2, pltpu.roll(uc, 2, 0), r2[0:SUBLANES])
    u1 = jnp.concatenate([top1, r1[SUBLANES:]], axis=0)
    u2 = jnp.concatenate([top2, r2[SUBLANES:]], axis=0)
    ucarry[...] = u[u.shape[0] - SUBLANES:, :]
    yc = cb * (u2 * convw_ref[0:1, :] + u1 * convw_ref[1:2, :] + u * convw_ref[2:3, :])
    ssc = _group_sum(yc * yc, bd_ref[...])
    yc_ref[...] = _bf16(yc * lax.rsqrt(ssc * (1.0 / HEAD_DIM) + EPS) * gc_ref[...])

    lb = lb_ref[...]
    zr = _dot(hb, w_ref[:, OFF_RF:OFF_RF + HGRN_W])
    sg = _sigmoid(zr)
    f = lb + (1.0 - lb) * sg
    rlf_ref[...] = jnp.log(jnp.maximum(f, TINY))
    rk_ref[...] = (1.0 - lb) * _sigmoid(-zr)
    rq_ref[...] = _silu(_dot(hb, w_ref[:, OFF_RQ:OFF_RQ + HGRN_W]))
    rv_ref[...] = _dot(hb, w_ref[:, OFF_RI:OFF_RI + HGRN_W])
    rg_ref[...] = _silu(_dot(hb, w_ref[:, OFF_RG:OFF_RG + HGRN_W]))


def _inproj(x2, seq, g, wp, tri, fb, gq, gk, selq, selk, oneq, onek, convw, gc, bd, lb, tm):
    t, dm = x2.shape
    full = lambda a: pl.BlockSpec(a.shape, lambda i: (0,) * a.ndim)
    row = lambda w: pl.BlockSpec((tm, w), lambda i: (i, 0))
    consts = (g, wp, tri, fb, gq, gk, selq, selk, oneq, onek, convw, gc, bd, lb)
    tiles_per_seq = seq // tm
    rows_bf16 = jax.ShapeDtypeStruct((t, ATTN_W), jnp.bfloat16)
    cols_bf16 = jax.ShapeDtypeStruct((t // seq, ATTN_W, seq), jnp.bfloat16)
    out_shape = (cols_bf16, cols_bf16, rows_bf16, rows_bf16, cols_bf16,
                 jax.ShapeDtypeStruct((t, CONV_C), jnp.bfloat16),
                 ) + tuple(jax.ShapeDtypeStruct((t, HGRN_W), jnp.float32) for _ in range(5))
    t_spec = pl.BlockSpec((None, ATTN_W, tm), lambda i: (i // tiles_per_seq, 0, i % tiles_per_seq))
    out_specs = ((t_spec, t_spec, row(ATTN_W), row(ATTN_W), t_spec, row(CONV_C))
                 + tuple(row(HGRN_W) for _ in range(5)))
    return pl.pallas_call(
        functools.partial(_inproj_kernel, tiles_per_seq=seq // tm),
        grid=(t // tm,),
        in_specs=[row(dm)] + [full(a) for a in consts],
        out_specs=out_specs,
        out_shape=out_shape,
        scratch_shapes=[pltpu.VMEM((1, LANES), jnp.float32), pltpu.VMEM((SUBLANES, CONV_C), jnp.float32)],
        compiler_params=pltpu.CompilerParams(dimension_semantics=("arbitrary",),
                                             vmem_limit_bytes=VMEM_LIMIT_BYTES),
        name="inproj",
    )(x2, *consts)


def _attn_kernel(q_ref, qaug_ref, k_ref, kaug_ref, vt_ref, gain_ref, out_ref, m_s, acc_s, sa_ref, sb_ref,
                 mxa_ref, mxb_ref, qt_s, *, tq):
    s_a, s_b = (sa_ref, mxa_ref), (sb_ref, mxb_ref)
    i = pl.program_id(2)
    ones_rows = jnp.ones((2 * SUBLANES, tq), jnp.bfloat16)
    m_s[...] = jnp.full_like(m_s, MASK_VALUE)
    acc_s[...] = jnp.zeros_like(acc_s)

    chan = lax.broadcasted_iota(jnp.int32, (PAIR_W, tq), 0)
    zero = jnp.zeros((PAIR_W, tq), jnp.bfloat16)
    for hh in range(2):
        mine = (chan < HEAD_DIM) if hh == 0 else (chan >= HEAD_DIM)
        qt_s[hh] = jnp.concatenate([jnp.where(mine, q_ref[...], zero), jnp.where(mine, qaug_ref[...], zero)],
                                   axis=0)

    def scores(j, s_buf):
        k0 = pl.multiple_of(j * tq, tq)
        krows = jnp.concatenate([k_ref[pl.ds(k0, tq), :], kaug_ref[pl.ds(k0, tq), :]], axis=1)
        for hh in range(2):
            s = _dot(krows, qt_s[hh])
            s_buf[0][hh] = s
            s_buf[1][hh] = jnp.max(s, axis=0, keepdims=True)

    def consume(j, s_buf, masked):
        k0 = pl.multiple_of(j * tq, tq)
        for hh in range(2):
            s = s_buf[0][hh]
            if masked:
                kpos = lax.broadcasted_iota(jnp.int32, (tq, tq), 0)
                qpos = lax.broadcasted_iota(jnp.int32, (tq, tq), 1)
                s = jnp.where(kpos <= qpos, s, MASK_VALUE)
                blk_max = jnp.max(s, axis=0, keepdims=True)
            else:
                blk_max = s_buf[1][hh]
            m = m_s[hh]
            m_new = jnp.maximum(m, blk_max)
            p = jnp.exp2(s - m_new)
            alpha = jnp.exp2(m - m_new)
            vaug = jnp.concatenate([vt_ref[hh, :, pl.ds(k0, tq)], ones_rows], axis=0)
            acc_s[hh] = acc_s[hh] * alpha + _dot(vaug, _bf16(p))
            m_s[hh] = m_new

    scores(0, s_a)

    def pair(j):
        scores(j + 1, s_b)
        consume(j, s_a, False)
        scores(j + 2, s_a)
        consume(j + 1, s_b, False)

    def body(jj, _):
        pair(4 * jj)
        pair(4 * jj + 2)
        return 0

    lax.fori_loop(0, i // 4, body, 0)
    done = 4 * (i // 4)

    @pl.when(i - done >= 2)
    def _():
        pair(done)

    @pl.when(i % 2 == 1)
    def _():
        scores(i, s_b)
        consume(i - 1, s_a, False)
        consume(i, s_b, True)

    @pl.when(i % 2 == 0)
    def _():
        consume(i, s_a, True)

    ys = []
    for hh in range(2):
        acc = acc_s[hh]
        o = acc[0:HEAD_DIM] * (1.0 / acc[HEAD_DIM:HEAD_DIM + 1])
        ms = jnp.mean(o * o, axis=0, keepdims=True)
        ys.append(o * lax.rsqrt(ms + EPS) * gain_ref[hh])
    out_ref[...] = _bf16(jnp.concatenate(ys, axis=0).T)


def _attention(q, qaug, k, kaug, vt, gain, tq):
    b, s, _ = k.shape
    q_spec = pl.BlockSpec((None, PAIR_W, tq), lambda bi, hp, i: (bi, hp, i))
    k_spec = pl.BlockSpec((None, s, PAIR_W), lambda bi, hp, i: (bi, 0, hp))
    return pl.pallas_call(
        functools.partial(_attn_kernel, tq=tq),
        grid=(b, N_PAIRS, s // tq),
        in_specs=[
            q_spec, q_spec, k_spec, k_spec,
            pl.BlockSpec((None, 2, HEAD_DIM, s), lambda bi, hp, i: (bi, hp, 0, 0)),
            pl.BlockSpec((2, HEAD_DIM, tq), lambda bi, hp, i: (hp, 0, 0)),
        ],
        out_specs=pl.BlockSpec((None, tq, 2 * HEAD_DIM), lambda bi, hp, i: (bi, i, hp)),
        out_shape=jax.ShapeDtypeStruct((b, s, ATTN_W), jnp.bfloat16),
        scratch_shapes=[pltpu.VMEM((2, 1, tq), jnp.float32),
                        pltpu.VMEM((2, HEAD_DIM + 2 * SUBLANES, tq), jnp.float32),
                        pltpu.VMEM((2, tq, tq), jnp.float32),
                        pltpu.VMEM((2, tq, tq), jnp.float32),
                        pltpu.VMEM((2, 1, tq), jnp.float32),
                        pltpu.VMEM((2, 1, tq), jnp.float32),
                        pltpu.VMEM((2, 2 * PAIR_W, tq), jnp.bfloat16)],
        compiler_params=pltpu.CompilerParams(dimension_semantics=("arbitrary", "arbitrary", "arbitrary"),
                                             vmem_limit_bytes=VMEM_LIMIT_BYTES),
        name="fox_attention",
    )(q, qaug, k, kaug, vt, gain)


def _hgrn_kernel(q_ref, lf_ref, k_ref, v_ref, g_ref, tri_ref, bd_ref, gain_ref, out_ref,
                 state, c_all, k_all, v_all, *, n_chunks):
    @pl.when(pl.program_id(1) == 0)
    def _():
        state[...] = jnp.zeros_like(state)

    ch = HGRN_CHUNK
    sb = HGRN_SUB
    nsb = ch // sb
    row_sb = lax.broadcasted_iota(jnp.int32, (sb, HGRN_W), 0)
    r128 = lax.broadcasted_iota(jnp.int32, (LANES, LANES), 0)
    c128 = lax.broadcasted_iota(jnp.int32, (LANES, LANES), 1)
    same_head = (r128 < HEAD_DIM) == (c128 < HEAD_DIM)
    t64 = lax.broadcasted_iota(jnp.int32, (ch, LANES), 0) // sb
    s64 = (lax.broadcasted_iota(jnp.int32, (ch, LANES), 1) % HEAD_DIM) // sb
    level2 = ((t64 == 1) & (s64 == 0)) | ((t64 == 3) & (s64 == 2))
    lane_head0 = lax.broadcasted_iota(jnp.int32, (ch, LANES), 1) < HEAD_DIM
    bd = bd_ref[...]
    zeros_sb = jnp.zeros((sb, HGRN_W), jnp.float32)

    def chunk(ci):
        r0 = ci * ch
        c_s, k_s, v_s = c_all.at[ci % 2], k_all.at[ci % 2], v_all.at[ci % 2]
        q = q_ref[pl.ds(r0, ch), :]
        k = k_ref[pl.ds(r0, ch), :]
        v = v_ref[pl.ds(r0, ch), :]
        p1, p2, p3 = _split3(lf_ref[pl.ds(r0, ch), :])
        cc = _dot(tri_ref[...], jnp.concatenate([p1, p2, p3], axis=1))
        c = cc[:, 0:HGRN_W] + cc[:, HGRN_W:2 * HGRN_W] + cc[:, 2 * HGRN_W:3 * HGRN_W]
        c_s[...] = c
        k_s[...] = k
        v_s[...] = v
        blk = lambda a, n: a[n * sb:(n + 1) * sb]

        ps = []
        for n in range(nsb):
            cn, qn = blk(c, n), blk(q, n)
            for s in range(sb):
                r = n * sb + s
                dec = jnp.exp(jnp.where(row_sb >= s, cn - c_s[r:r + 1, :], MASK_LOG_DECAY))
                ps.append(_bf16(qn * k_s[r:r + 1, :] * dec))
        a_d = _dot(jnp.concatenate(ps, axis=0), bd)
        o_parts = []
        for n in range(nsb):
            acc = jnp.zeros((sb, HGRN_W), jnp.float32)
            for s in range(sb):
                r = n * sb + s
                acc = acc + a_d[r * sb:(r + 1) * sb] * v_s[r:r + 1, :]
            o_parts.append(acc)
        o = jnp.concatenate(o_parts, axis=0)

        ref1 = c[2 * sb - 1:2 * sb]
        ref2a = c[sb - 1:sb]
        ref2b = c[3 * sb - 1:3 * sb]
        hi, lo = slice(2 * sb, 4 * sb), slice(0, 2 * sb)
        zeros_half = jnp.zeros((2 * sb, HGRN_W), jnp.float32)
        q1 = jnp.concatenate([zeros_half, q[hi] * jnp.exp(c[hi] - ref1)], axis=0)
        k1 = jnp.concatenate([k[lo] * jnp.exp(ref1 - c[lo]), zeros_half], axis=0)
        q2 = jnp.concatenate([zeros_sb, blk(q, 1) * jnp.exp(blk(c, 1) - ref2a),
                              zeros_sb, blk(q, 3) * jnp.exp(blk(c, 3) - ref2b)], axis=0)
        k2 = jnp.concatenate([blk(k, 0) * jnp.exp(ref2a - blk(c, 0)), zeros_sb,
                              blk(k, 2) * jnp.exp(ref2b - blk(c, 2)), zeros_sb], axis=0)

        c_last = c[ch - 1:ch, :]
        qe = q * jnp.exp(c)
        kd = k * jnp.exp(c_last - c)
        e_last = jnp.exp(c_last)
        nt = (((1,), (1,)), ((), ()))
        o_off = []
        for bb in range(HGRN_W // LANES):
            sl = slice(bb * LANES, (bb + 1) * LANES)
            st = state[bb]
            stack2 = lambda a: jnp.concatenate([jnp.where(lane_head0, a[:, sl], 0.0),
                                                jnp.where(lane_head0, 0.0, a[:, sl])], axis=0)
            a1 = lax.dot_general(_bf16(q1[:, sl]), _bf16(stack2(k1)), nt, preferred_element_type=jnp.float32)
            a2 = lax.dot_general(_bf16(q2[:, sl]), _bf16(stack2(k2)), nt, preferred_element_type=jnp.float32)
            a_off = a1 + jnp.where(level2, a2, 0.0)
            lhs = jnp.concatenate([_bf16(a_off), _bf16(qe[:, sl])], axis=1)
            rhs = jnp.concatenate([_bf16(stack2(v)), _bf16(st)], axis=0)
            o_off.append(_dot(lhs, rhs))
            ecol = jnp.broadcast_to(e_last[:, sl], (LANES, LANES)).T
            upd = lax.dot_general(_bf16(kd[:, sl]), _bf16(v[:, sl]), (((0,), (0,)), ((), ())),
                                  preferred_element_type=jnp.float32)
            state[bb] = jnp.where(same_head, ecol * st + upd, 0.0)
        o = o + jnp.concatenate(o_off, axis=1)

        ss = _group_sum(o * o, bd)
        y = o * lax.rsqrt(ss * (1.0 / HEAD_DIM) + EPS) * gain_ref[...] * g_ref[pl.ds(r0, ch), :]
        out_ref[pl.ds(r0, ch), :] = _bf16(y)

    for ci in range(n_chunks):
        chunk(ci)


def _hgrn(rq, rlf, rk, rv, rg, tri, bd, gain, batch, seq, rows):
    t = rq.shape[0]
    steps = seq // rows
    blk = pl.BlockSpec((rows, HGRN_W), lambda b, j: (b * steps + j, 0))
    full = lambda a: pl.BlockSpec(a.shape, lambda b, j: (0,) * a.ndim)
    return pl.pallas_call(
        functools.partial(_hgrn_kernel, n_chunks=rows // HGRN_CHUNK),
        grid=(batch, steps),
        in_specs=[blk, blk, blk, blk, blk, full(tri), full(bd), full(gain)],
        out_specs=blk,
        out_shape=jax.ShapeDtypeStruct((t, HGRN_W), jnp.bfloat16),
        scratch_shapes=[pltpu.VMEM((HGRN_W // LANES, LANES, LANES), jnp.float32)]
        + [pltpu.VMEM((2, HGRN_CHUNK, HGRN_W), jnp.float32) for _ in range(3)],
        compiler_params=pltpu.CompilerParams(dimension_semantics=("arbitrary", "arbitrary"),
                                             vmem_limit_bytes=VMEM_LIMIT_BYTES),
        name="hgrn2",
    )(rq, rlf, rk, rv, rg, tri, bd, gain)


INFO_E1, INFO_E2, INFO_R1, INFO_R2, INFO_W1, INFO_W2 = range(6)


def _mix_and_norm(x_ref, ya_ref, yc_ref, yr_ref, w_ref, g_ref):
    xn = (x_ref[...]
          + _dot(ya_ref[...], w_ref[0:ATTN_W, :])
          + _dot(yc_ref[...], w_ref[ATTN_W:ATTN_W + CONV_C, :])
          + _dot(yr_ref[...], w_ref[ATTN_W + CONV_C:, :]))
    h = xn * lax.rsqrt(jnp.mean(xn * xn, axis=-1, keepdims=True) + EPS) * g_ref[...]
    return xn, h


def _outproj_routed_kernel(x_ref, ya_ref, yc_ref, yr_ref, w_ref, g_ref, wr_ref, br_ref, tri_ref,
                           xo_ref, h_ref, info_ref, info_t_ref, cnt_ref, cnt_s):
    @pl.when(pl.program_id(0) == 0)
    def _():
        cnt_s[...] = jnp.zeros_like(cnt_s)

    xn, h = _mix_and_norm(x_ref, ya_ref, yc_ref, yr_ref, w_ref, g_ref)
    xo_ref[...] = xn
    h_ref[...] = h
    h_hi = _bf16(h)
    h_lo = _bf16(h - h_hi.astype(jnp.float32))
    hw = _dot(h_hi, wr_ref[...])
    logits = hw[:, 0:LANES] + hw[:, LANES:2 * LANES] + _dot(h_lo, wr_ref[:, 0:LANES]) + br_ref[...]
    lane = lax.broadcasted_iota(jnp.int32, logits.shape, 1)
    logits = jnp.where(lane < N_EXPERTS, logits, MASK_VALUE)
    m1 = jnp.max(logits, axis=-1, keepdims=True)
    i1 = jnp.min(jnp.where(logits == m1, lane, LANES), axis=-1, keepdims=True)
    rest = jnp.where(lane == i1, MASK_VALUE, logits)
    m2 = jnp.max(rest, axis=-1, keepdims=True)
    i2 = jnp.min(jnp.where(rest == m2, lane, LANES), axis=-1, keepdims=True)
    e2 = jnp.exp(m2 - m1)
    w1 = 1.0 / (1.0 + e2)
    w2 = e2 * w1
    hit = (lane == i1) | (lane == i2)
    onehot = jnp.where(hit, 1.0, 0.0)
    incl = _dot(tri_ref[...], _bf16(onehot))
    rank = cnt_s[...] + incl - onehot
    r1 = jnp.sum(jnp.where(lane == i1, rank, 0.0), axis=-1, keepdims=True)
    r2 = jnp.sum(jnp.where(lane == i2, rank, 0.0), axis=-1, keepdims=True)
    cnt_new = cnt_s[...] + incl[incl.shape[0] - 1:, :]
    cnt_s[...] = cnt_new
    cnt_ref[...] = jnp.broadcast_to(cnt_new, cnt_ref.shape)
    rec = jnp.zeros(logits.shape, jnp.float32)
    for ln, val in ((INFO_E1, i1.astype(jnp.float32)), (INFO_E2, i2.astype(jnp.float32)),
                    (INFO_R1, r1), (INFO_R2, r2), (INFO_W1, w1), (INFO_W2, w2)):
        rec = jnp.where(lane == ln, val, rec)
    info_ref[...] = rec
    info_t_ref[...] = rec.T[0:SUBLANES, :]


def _outproj_routed(x2, ya, yc, yr, wo, g, tm, router):
    t, dm = x2.shape
    row = lambda w: pl.BlockSpec((tm, w), lambda i: (i, 0))
    full = lambda a: pl.BlockSpec(a.shape, lambda i: (0,) * a.ndim)
    params = pltpu.CompilerParams(dimension_semantics=("arbitrary",), vmem_limit_bytes=VMEM_LIMIT_BYTES)
    base_specs = [row(dm), row(ATTN_W), row(CONV_C), row(HGRN_W), full(wo), full(g)]
    wr, br, tri = router
    return pl.pallas_call(
        _outproj_routed_kernel,
        grid=(t // tm,),
        in_specs=base_specs + [full(wr), full(br), full(tri)],
        out_specs=(row(dm), row(dm), row(LANES), pl.BlockSpec((SUBLANES, tm), lambda i: (0, i)),
                   pl.BlockSpec((SUBLANES, LANES), lambda i: (0, 0))),
        out_shape=(jax.ShapeDtypeStruct((t, dm), jnp.float32), jax.ShapeDtypeStruct((t, dm), jnp.float32),
                   jax.ShapeDtypeStruct((t, LANES), jnp.float32),
                   jax.ShapeDtypeStruct((SUBLANES, t), jnp.float32),
                   jax.ShapeDtypeStruct((SUBLANES, LANES), jnp.float32)),
        scratch_shapes=[pltpu.VMEM((1, LANES), jnp.float32)],
        compiler_params=params,
        name="outproj_routed",
    )(x2, ya, yc, yr, wo, g, wr, br, tri)


def _swiglu_step(h_ref, wg_ref, wu_ref, wd_ref):
    h = _bf16(h_ref[...])
    gt = _dot(h, _bf16(wg_ref[...]))
    up = _dot(h, _bf16(wu_ref[...]))
    act = _bf16(gt * (1.0 / (1.0 + jnp.exp(-gt))) * up)
    return _dot(act, _bf16(wd_ref[...]))


def _mix_ffn_dense_kernel(x_ref, ya_ref, yc_ref, yr_ref, wo_ref, g_ref, wg_ref, wu_ref, wd_ref, out_ref, h_s):
    @pl.when(pl.program_id(1) == 0)
    def _():
        xn, h = _mix_and_norm(x_ref, ya_ref, yc_ref, yr_ref, wo_ref, g_ref)
        out_ref[...] = xn
        h_s[...] = _bf16(h)

    out_ref[...] += _swiglu_step(h_s, wg_ref, wu_ref, wd_ref)


def _mix_ffn_dense(x2, ya, yc, yr, wo, g, wg, wu, wd, tm, fc):
    t, dm = x2.shape
    dff = wg.shape[1]
    row = lambda w: pl.BlockSpec((tm, w), lambda i, f: (i, 0))
    full = lambda a: pl.BlockSpec(a.shape, lambda i, f: (0,) * a.ndim)
    return pl.pallas_call(
        _mix_ffn_dense_kernel,
        grid=(t // tm, dff // fc),
        in_specs=[
            row(dm), row(ATTN_W), row(CONV_C), row(HGRN_W), full(wo), full(g),
            pl.BlockSpec((dm, fc), lambda i, f: (0, f)),
            pl.BlockSpec((dm, fc), lambda i, f: (0, f)),
            pl.BlockSpec((fc, dm), lambda i, f: (f, 0)),
        ],
        out_specs=row(dm),
        out_shape=jax.ShapeDtypeStruct((t, dm), jnp.float32),
        scratch_shapes=[pltpu.VMEM((tm, dm), jnp.bfloat16)],
        compiler_params=pltpu.CompilerParams(dimension_semantics=("arbitrary", "arbitrary"),
                                             vmem_limit_bytes=VMEM_LIMIT_BYTES),
        name="mix_swiglu_dense",
    )(x2, ya, yc, yr, wo, g, wg, wu, wd)


def _ffn_grouped_kernel(te_ref, ta_ref, h_ref, wg_ref, wu_ref, wd_ref, out_ref):
    @pl.when(pl.program_id(1) == 0)
    def _():
        out_ref[...] = jnp.zeros_like(out_ref)

    @pl.when(ta_ref[pl.program_id(0)] == 1)
    def _():
        out_ref[...] += _swiglu_step(h_ref, wg_ref, wu_ref, wd_ref)


def _ffn_grouped(tile_expert, tile_active, xs, wg, wu, wd, tm, fc):
    dm = xs.shape[1]
    n_tiles = tile_expert.shape[0]
    r = n_tiles * tm
    dff = wg.shape[2]
    nf = dff // fc
    fidx = lambda i, f, ta: f * ta[i] + (nf - 1) * (1 - ta[i])
    grid_spec = pltpu.PrefetchScalarGridSpec(
        num_scalar_prefetch=2,
        grid=(n_tiles, nf),
        in_specs=[
            pl.BlockSpec((tm, dm), lambda i, f, te, ta: (i * ta[i], 0)),
            pl.BlockSpec((None, dm, fc), lambda i, f, te, ta: (te[i], 0, fidx(i, f, ta))),
            pl.BlockSpec((None, dm, fc), lambda i, f, te, ta: (te[i], 0, fidx(i, f, ta))),
            pl.BlockSpec((None, fc, dm), lambda i, f, te, ta: (te[i], fidx(i, f, ta), 0)),
        ],
        out_specs=pl.BlockSpec((tm, dm), lambda i, f, te, ta: (i, 0)),
    )
    return pl.pallas_call(
        _ffn_grouped_kernel,
        grid_spec=grid_spec,
        out_shape=jax.ShapeDtypeStruct((r, dm), jnp.float32),
        compiler_params=pltpu.CompilerParams(dimension_semantics=("arbitrary", "arbitrary"),
                                             vmem_limit_bytes=VMEM_LIMIT_BYTES),
        name="swiglu_grouped",
    )(tile_expert, tile_active, xs, wg, wu, wd)


def _row_copy(src, dst, sem):
    return pltpu.make_async_copy(src, dst, sem)


MOE_ISSUE_UNROLL = 8


def _wait_rows(ref, n_rows, sem):
    blk = ref.at[pl.ds(0, n_rows), :]
    pltpu.make_async_copy(blk, blk, sem).wait()


MOE_STAGE_BUFS = 3


def _dispatch_kernel(p1_ref, p2_ref, pad_ref, h_ref, xs_ref, stage, zrow, load_sems, scat_sems, zero_sem,
                     *, tb, n_chunks):
    def load(c, slot):
        return pltpu.make_async_copy(h_ref.at[pl.ds(c * tb, tb), :], stage.at[slot], load_sems.at[slot])

    load(0, 0).start()

    zrow[...] = jnp.zeros_like(zrow)
    for e in range(N_EXPERTS):
        def zero(r, _, e=e):
            _row_copy(zrow.at[pl.ds(0, 1), :], xs_ref.at[pl.ds(pad_ref[e] + r, 1), :], zero_sem).start()
            return 0

        lax.fori_loop(0, pad_ref[N_EXPERTS + e], zero, 0)
    for e in range(N_EXPERTS):
        def zero_done(r, _):
            _wait_rows(xs_ref, 1, zero_sem)
            return 0

        lax.fori_loop(0, pad_ref[N_EXPERTS + e], zero_done, 0)

    def chunk(c, _):
        slot = lax.rem(c, MOE_STAGE_BUFS)
        load(c, slot).wait()

        @pl.when(c >= 2)
        def _():
            _wait_rows(xs_ref, 2 * tb, scat_sems.at[lax.rem(c - 2, MOE_STAGE_BUFS)])

        @pl.when(c + 1 < n_chunks)
        def _():
            load(c + 1, lax.rem(c + 1, MOE_STAGE_BUFS)).start()

        sem = scat_sems.at[slot]

        def issue(rr, _):
            for u in range(MOE_ISSUE_UNROLL):
                r = rr * MOE_ISSUE_UNROLL + u
                tok = c * tb + r
                src = stage.at[slot, pl.ds(r, 1), :]
                _row_copy(src, xs_ref.at[pl.ds(p1_ref[tok], 1), :], sem).start()
                _row_copy(src, xs_ref.at[pl.ds(p2_ref[tok], 1), :], sem).start()
            return 0

        lax.fori_loop(0, tb // MOE_ISSUE_UNROLL, issue, 0)
        return 0

    lax.fori_loop(0, n_chunks, chunk, 0)
    for c in range(max(n_chunks - 2, 0), n_chunks):
        _wait_rows(xs_ref, 2 * tb, scat_sems.at[c % MOE_STAGE_BUFS])


def _dispatch(pos1, pos2, pad_start, h, n_tiles, tb, tm):
    t, dm = h.shape
    n_chunks = t // tb
    grid_spec = pltpu.PrefetchScalarGridSpec(
        num_scalar_prefetch=3,
        grid=(1,),
        in_specs=[pl.BlockSpec(memory_space=pl.ANY)],
        out_specs=pl.BlockSpec(memory_space=pl.ANY),
        scratch_shapes=[pltpu.VMEM((MOE_STAGE_BUFS, tb, dm), h.dtype),
                        pltpu.VMEM((SUBLANES, dm), h.dtype),
                        pltpu.SemaphoreType.DMA((MOE_STAGE_BUFS,)),
                        pltpu.SemaphoreType.DMA((MOE_STAGE_BUFS,)),
                        pltpu.SemaphoreType.DMA(())],
    )
    return pl.pallas_call(
        functools.partial(_dispatch_kernel, tb=tb, n_chunks=n_chunks),
        grid_spec=grid_spec,
        out_shape=jax.ShapeDtypeStruct((n_tiles * tm, dm), h.dtype),
        compiler_params=pltpu.CompilerParams(dimension_semantics=("arbitrary",), has_side_effects=True),
        name="moe_dispatch",
    )(pos1, pos2, pad_start, h)


def _combine_kernel(p1_ref, p2_ref, y_ref, x_ref, info_ref, out_ref, buf_a, buf_b, sems, *, tb):
    i = pl.program_id(0)
    n = pl.num_programs(0)

    def issue_step(step, buf, sem):
        def issue(rr, _):
            for u in range(MOE_ISSUE_UNROLL):
                r = rr * MOE_ISSUE_UNROLL + u
                tok = step * tb + r
                _row_copy(y_ref.at[pl.ds(p1_ref[tok], 1), :], buf.at[0, pl.ds(r, 1), :], sem).start()
                _row_copy(y_ref.at[pl.ds(p2_ref[tok], 1), :], buf.at[1, pl.ds(r, 1), :], sem).start()
            return 0

        lax.fori_loop(0, tb // MOE_ISSUE_UNROLL, issue, 0)

    def finish(buf, sem):
        _wait_rows(y_ref, 2 * tb, sem)
        info = info_ref[...]
        lane = lax.broadcasted_iota(jnp.int32, info.shape, 1)
        w1 = jnp.sum(jnp.where(lane == INFO_W1, info, 0.0), axis=-1, keepdims=True)
        w2 = jnp.sum(jnp.where(lane == INFO_W2, info, 0.0), axis=-1, keepdims=True)
        out_ref[...] = x_ref[...] + w1 * buf[0] + w2 * buf[1]

    @pl.when(i == 0)
    def _():
        issue_step(0, buf_a, sems.at[0])

    @pl.when(i % 2 == 0)
    def _():
        @pl.when(i + 1 < n)
        def _():
            issue_step(i + 1, buf_b, sems.at[1])

        finish(buf_a, sems.at[0])

    @pl.when(i % 2 == 1)
    def _():
        @pl.when(i + 1 < n)
        def _():
            issue_step(i + 1, buf_a, sems.at[0])

        finish(buf_b, sems.at[1])


def _combine(pos1, pos2, y, x2, info, tb):
    t, dm = x2.shape
    grid_spec = pltpu.PrefetchScalarGridSpec(
        num_scalar_prefetch=2,
        grid=(t // tb,),
        in_specs=[pl.BlockSpec(memory_space=pl.ANY),
                  pl.BlockSpec((tb, dm), lambda i, p1, p2: (i, 0)),
                  pl.BlockSpec((tb, LANES), lambda i, p1, p2: (i, 0))],
        out_specs=pl.BlockSpec((tb, dm), lambda i, p1, p2: (i, 0)),
        scratch_shapes=[pltpu.VMEM((2, tb, dm), jnp.float32), pltpu.VMEM((2, tb, dm), jnp.float32),
                        pltpu.SemaphoreType.DMA((2,))],
    )
    return pl.pallas_call(
        functools.partial(_combine_kernel, tb=tb),
        grid_spec=grid_spec,
        out_shape=jax.ShapeDtypeStruct((t, dm), jnp.float32),
        compiler_params=pltpu.CompilerParams(dimension_semantics=("arbitrary",)),
        name="moe_combine",
    )(pos1, pos2, y, x2, info)


def _moe_routing_tables(info_t, counts, tm, n_tiles):
    cnt = counts[0, :N_EXPERTS].astype(jnp.int32)
    padded = ((cnt + tm - 1) // tm) * tm
    ends = jnp.cumsum(padded)
    offsets = ends - padded
    e1 = info_t[INFO_E1].astype(jnp.int32)
    e2 = info_t[INFO_E2].astype(jnp.int32)
    pos1 = offsets[e1] + info_t[INFO_R1].astype(jnp.int32)
    pos2 = offsets[e2] + info_t[INFO_R2].astype(jnp.int32)
    start = jnp.arange(n_tiles, dtype=jnp.int32) * tm
    tile_expert = jnp.minimum(jnp.sum((start[:, None] >= ends[None, :]).astype(jnp.int32), axis=1), N_EXPERTS - 1)
    tile_active = (start < ends[-1]).astype(jnp.int32)
    pad_rows = jnp.concatenate([offsets + cnt, padded - cnt])
    return pos1, pos2, pad_rows, tile_expert, tile_active


def _pack_w_in(w):
    s = np.cumsum([0, ATTN_W, ATTN_W, ATTN_W, ATTN_HEADS, CONV_C, CONV_C, CONV_C, HGRN_W, HGRN_W, HGRN_W, HGRN_W])
    seg = [w[:, s[n]:s[n + 1]] for n in range(11)]
    a_q, a_k, a_v, a_f, c_x, c_b, c_c, r_q, r_f, r_i, r_g = seg
    a_f = jnp.pad(a_f, ((0, 0), (0, LANES - ATTN_HEADS)))
    return _bf16(jnp.concatenate([a_q, a_k, a_v, c_x, c_b, c_c, r_q, r_f, r_i, r_g, a_f], axis=1))


def _selection_constants():
    selq = np.zeros((LANES, ATTN_W), np.float32)
    selk = np.zeros((LANES, ATTN_W), np.float32)
    oneq = np.zeros((1, ATTN_W), np.float32)
    onek = np.zeros((1, ATTN_W), np.float32)
    for hd in range(ATTN_HEADS):
        base = hd * HEAD_DIM
        for piece in range(3):
            selq[piece * ATTN_HEADS + hd, base + piece] = 1.0
            selk[piece * ATTN_HEADS + hd, base + 3 + piece] = -1.0
            oneq[0, base + 3 + piece] = 1.0
            onek[0, base + piece] = 1.0
    return (jnp.asarray(selq, jnp.bfloat16), jnp.asarray(selk, jnp.bfloat16),
            jnp.asarray(oneq), jnp.asarray(onek))


PROJ_ROWS = 512
ATTN_BLOCK = 512
HGRN_ROWS = 512
FFN_ROWS = 1024
FFN_COL_CHUNKS = 4
MOE_CHUNK = 512


def _plan_tiles(batch, seq, dff):
    t = batch * seq
    tiles = (min(PROJ_ROWS, seq), min(ATTN_BLOCK, seq), min(HGRN_ROWS, seq), min(FFN_ROWS, t),
             dff // FFN_COL_CHUNKS, min(MOE_CHUNK, t))
    tm, tq, hg_rows, tm_ffn, fc, tb_moe = tiles
    assert seq % tm == 0 and seq % tq == 0 and seq % hg_rows == 0 and hg_rows % HGRN_CHUNK == 0
    assert t % tm_ffn == 0 and t % tb_moe == 0 and tb_moe % MOE_ISSUE_UNROLL == 0
    assert dff % FFN_COL_CHUNKS == 0 and fc % LANES == 0
    return tiles


def _tile_gain(gain, mult):
    return jnp.tile(gain.astype(jnp.float32) * mult, ATTN_HEADS).reshape(1, ATTN_W)


def kernel(x, norm_mix, w_in, attn_f_bias, q_norm_gain, k_norm_gain, conv_w, hgrn_lb_logits, mix_out_gain, w_out,
           norm_ffn, ffn_w_gate, ffn_w_up, ffn_w_down, moe_router_w, moe_router_b, moe_w_gate, moe_w_up, moe_w_down):
    batch, seq, dm = x.shape
    depth = w_in.shape[0]
    t = batch * seq
    f32 = jnp.float32
    tm, tq, hg_rows, tm_ffn, fc, tb_moe = _plan_tiles(batch, seq, ffn_w_gate.shape[-1])

    p_lb = jax.nn.softmax(hgrn_lb_logits.astype(f32), axis=0)
    lb_all = jnp.cumsum(p_lb, axis=0) - p_lb[0]

    tri_m = _bf16(jnp.tril(jnp.ones((tm, tm), f32)))
    tri_c = _bf16(jnp.tril(jnp.ones((HGRN_CHUNK, HGRN_CHUNK), f32)))
    grp = np.arange(HGRN_W) // HEAD_DIM
    bd = jnp.asarray(grp[:, None] == grp[None, :], jnp.bfloat16)
    selq, selk, oneq, onek = _selection_constants()
    scale = 1.0 / math.sqrt(HEAD_DIM)

    x2 = x.reshape(t, dm)
    for l in range(depth):
        wp = _pack_w_in(w_in[l])
        fb = jnp.pad(attn_f_bias[l].astype(f32), (0, LANES - ATTN_HEADS)).reshape(1, LANES)
        gq = _tile_gain(q_norm_gain[l], scale * LOG2E)
        gk = _tile_gain(k_norm_gain[l], 1.0)
        mog = mix_out_gain[l].astype(f32)
        ga = jnp.broadcast_to(mog[:ATTN_W].reshape(ATTN_HEADS, HEAD_DIM, 1), (ATTN_HEADS, HEAD_DIM, tq))
        gc = mog[ATTN_W:ATTN_W + CONV_C].reshape(1, CONV_C)
        gr = mog[ATTN_W + CONV_C:].reshape(1, HGRN_W)

        q, qaug, k, kaug, v, yc, rq, rlf, rk, rv, rg = _inproj(
            x2, seq, norm_mix[l].astype(f32).reshape(1, dm), wp, tri_m, fb, gq, gk, selq, selk, oneq, onek,
            conv_w[l].astype(f32), gc, bd, lb_all[l].reshape(1, HGRN_W), tm)

        vt = v.reshape(batch, ATTN_HEADS, HEAD_DIM, seq)
        seq3 = lambda a: a.reshape(batch, seq, ATTN_W)
        ya = _attention(q, qaug, seq3(k), seq3(kaug), vt, ga, tq)
        yr = _hgrn(rq, rlf, rk, rv, rg, tri_c, bd, gr, batch, seq, hg_rows)

        j = l // 2
        wo = _bf16(w_out[l])
        gf = norm_ffn[l].astype(f32).reshape(1, dm)
        if l % 2 == 0:
            x2 = _mix_ffn_dense(x2, ya.reshape(t, ATTN_W), yc, yr, wo, gf,
                                ffn_w_gate[j], ffn_w_up[j], ffn_w_down[j], tm_ffn, fc)
        else:
            wr32 = jnp.pad(moe_router_w[j].astype(f32), ((0, 0), (0, LANES - N_EXPERTS)))
            wr_hi = _bf16(wr32)
            wr = jnp.concatenate([wr_hi, _bf16(wr32 - wr_hi.astype(f32))], axis=1)
            br = jnp.pad(moe_router_b[j].astype(f32), (0, LANES - N_EXPERTS)).reshape(1, LANES)
            xo, h2, info, info_t, counts = _outproj_routed(x2, ya.reshape(t, ATTN_W), yc, yr, wo, gf, tm,
                                                           (wr, br, tri_m))
            n_tiles = (2 * t) // tm_ffn + N_EXPERTS
            pos1, pos2, pad_rows, tile_expert, tile_active = _moe_routing_tables(info_t, counts, tm_ffn, n_tiles)
            xs = _dispatch(pos1, pos2, pad_rows, h2, n_tiles, tb_moe, tm_ffn)
            ys = _ffn_grouped(tile_expert, tile_active, xs,
                              moe_w_gate[j], moe_w_up[j], moe_w_down[j], tm_ffn, fc)
            x2 = _combine(pos1, pos2, ys, xo, info, tb_moe)
    return x2.reshape(batch, seq, dm)
```
